```python
import jax, jax.numpy as jnp
from jax import lax
import numpy as np

D_MODEL = 1024
BATCH = 8
SEQ = 8192
DEPTH = 4

CHUNK = 64
Q_BLOCK = 128
N_BRANCHES = 3
BRANCH_WIDTH = 512
SB_HEADS = 8
SB_HEAD_DIM = BRANCH_WIDTH // SB_HEADS
CONV_CHANNELS = BRANCH_WIDTH
CONV_WIDTH = 31
GLA_HEADS = 4
GLA_KEY_DIM = BRANCH_WIDTH // 2
GLA_VALUE_DIM = BRANCH_WIDTH
GLA_HEAD_K = GLA_KEY_DIM // GLA_HEADS
GLA_HEAD_V = GLA_VALUE_DIM // GLA_HEADS
GLA_GATE_RANK = 16
GLA_GATE_TAU = 16.0
D_FF = 2816
NORM_EPS = 1e-6

IN_WIDTHS = (
    BRANCH_WIDTH, BRANCH_WIDTH, BRANCH_WIDTH,
    2 * CONV_CHANNELS,
    GLA_KEY_DIM, GLA_KEY_DIM, GLA_VALUE_DIM,
    GLA_VALUE_DIM,
    GLA_GATE_RANK,
    N_BRANCHES * D_MODEL,
)
IN_WIDTH = sum(IN_WIDTHS)

kernel_name = "hybrid_stickbreak_conformer_gla_trunk"


def rms_norm(x, g):
    xf = x.astype(jnp.float32)
    y = xf * lax.rsqrt(jnp.mean(xf * xf, axis=-1, keepdims=True) + NORM_EPS)
    return (y * g.astype(jnp.float32)).astype(x.dtype)


def swiglu_ffn(h, w_gate, w_up, w_down):
    return (jax.nn.silu(h @ w_gate) * (h @ w_up)) @ w_down


def stick_breaking_attention(q, k, v):
    b, nh, s, dh = q.shape
    nb = s // Q_BLOCK
    q_blocks = q.reshape(b, nh, nb, Q_BLOCK, dh).transpose(2, 0, 1, 3, 4)
    key_pos = jnp.arange(s)
    scale = dh ** -0.5

    def one_block(args):
        q_blk, blk = args
        z = jnp.einsum('bhqd,bhkd->bhqk', q_blk, k).astype(jnp.float32) * scale
        query_pos = blk * Q_BLOCK + jnp.arange(Q_BLOCK)
        earlier = key_pos[None, :] < query_pos[:, None]
        log_keep = jnp.where(earlier, jax.nn.log_sigmoid(-z), 0.0)
        log_stick = lax.cumsum(log_keep, axis=3, reverse=True) - log_keep
        w = jnp.where(earlier, jnp.exp(jax.nn.log_sigmoid(z) + log_stick), 0.0)
        return jnp.einsum('bhqk,bhkd->bhqd', w.astype(v.dtype), v)

    out = lax.map(one_block, (q_blocks, jnp.arange(nb)))
    return out.transpose(1, 2, 0, 3, 4).reshape(b, nh, s, dh)


def conformer_conv(u_glu, conv_w, conv_b, ln_g, ln_b):
    a, g = jnp.split(u_glu, 2, axis=-1)
    u = a * jax.nn.sigmoid(g)
    y = lax.conv_general_dilated(
        u, conv_w[:, None, :], window_strides=(1,),
        padding=[(CONV_WIDTH - 1, 0)],
        dimension_numbers=('NWC', 'WIO', 'NWC'),
        feature_group_count=CONV_CHANNELS) + conv_b
    yf = y.astype(jnp.float32)
    mu = jnp.mean(yf, axis=-1, keepdims=True)
    var = jnp.mean(jnp.square(yf - mu), axis=-1, keepdims=True)
    yn = (yf - mu) * lax.rsqrt(var + NORM_EPS) * ln_g.astype(jnp.float32) + ln_b.astype(jnp.float32)
    return jax.nn.silu(yn).astype(u.dtype)


def gla_chunked(q, k, v, log_alpha):
    b, s, nh, dk = q.shape
    dv = v.shape[-1]
    n = s // CHUNK

    def to_chunks(t):
        return t.reshape(b, n, CHUNK, nh, t.shape[-1]).transpose(1, 0, 2, 3, 4)

    la_c = to_chunks(log_alpha.astype(jnp.float32))
    decay_to_end = lax.cumsum(la_c, axis=2, reverse=True) - la_c
    chunk_decay = jnp.exp(jnp.sum(la_c, axis=2))
    k_dec = to_chunks(k).astype(jnp.float32) * jnp.exp(decay_to_end)
    q_c = to_chunks(q).astype(jnp.float32) * (dk ** -0.5)
    v_c = to_chunks(v).astype(jnp.float32)

    def step(state, xs):
        q_n, k_n, v_n, lam = xs
        state = lam[..., None] * state + jnp.einsum('bchk,bchv->bhkv', k_n, v_n)
        return state, jnp.einsum('bchk,bhkv->bchv', q_n, state)

    state0 = jnp.zeros((b, nh, dk, dv), jnp.float32)
    _, o = lax.scan(step, state0, (q_c, k_dec, v_c, chunk_decay))
    return o.transpose(1, 0, 2, 3, 4).reshape(b, s, nh, dv).astype(v.dtype)


def hybrid_mixer(h, w_in, conv_w, conv_b, conv_ln_g, conv_ln_b,
                 gla_w_alpha, gla_b_alpha, gla_norm_g, w_branch, w_out):
    b, s, _ = h.shape
    proj = h @ w_in
    offsets = [int(o) for o in np.cumsum(IN_WIDTHS)[:-1]]
    (sb_q, sb_k, sb_v, conv_in, gla_q, gla_k, gla_v, gla_r, gla_lr,
     gate_logits) = jnp.split(proj, offsets, axis=-1)

    def heads(t, nh):
        return t.reshape(b, s, nh, -1)

    sb_out = stick_breaking_attention(
        heads(sb_q, SB_HEADS).transpose(0, 2, 1, 3),
        heads(sb_k, SB_HEADS).transpose(0, 2, 1, 3),
        heads(sb_v, SB_HEADS).transpose(0, 2, 1, 3))
    sb_out = sb_out.transpose(0, 2, 1, 3).reshape(b, s, BRANCH_WIDTH)

    conv_out = conformer_conv(conv_in, conv_w, conv_b, conv_ln_g, conv_ln_b)

    log_alpha = jax.nn.log_sigmoid((gla_lr @ gla_w_alpha + gla_b_alpha).astype(jnp.float32)) / GLA_GATE_TAU
    gla_o = gla_chunked(heads(gla_q, GLA_HEADS), heads(gla_k, GLA_HEADS),
                        heads(gla_v, GLA_HEADS), heads(log_alpha, GLA_HEADS))
    gla_o = rms_norm(gla_o, gla_norm_g.reshape(GLA_HEADS, GLA_HEAD_V))
    gla_out = gla_o.reshape(b, s, GLA_VALUE_DIM) * jax.nn.silu(gla_r)

    branches = jnp.stack([sb_out, conv_out, gla_out], axis=2)
    branch_d = jnp.einsum('bsgw,gwd->bsgd', branches, w_branch)
    gates = jax.nn.sigmoid(gate_logits.reshape(b, s, N_BRANCHES, D_MODEL))
    merged = jnp.sum(gates * branch_d, axis=2)
    return merged @ w_out


def _fwd_setup_inputs(seed: int = 0) -> dict:
    key = jax.random.key(seed)
    ks = jax.random.split(key, 20)

    def nrm(k, shape, scale):
        return jax.random.normal(k, shape, jnp.float32) * scale

    L, D = DEPTH, D_MODEL
    return {
        "x": nrm(ks[0], (BATCH, SEQ, D), 1.0),
        "norm_pre": 1.0 + nrm(ks[1], (L, 3, D), 0.05),
        "norm_post": 1.0 + nrm(ks[2], (L, 3, D), 0.05),
        "ffn1_w_gate": nrm(ks[3], (L, D, D_FF), D ** -0.5),
        "ffn1_w_up": nrm(ks[4], (L, D, D_FF), D ** -0.5),
        "ffn1_w_down": nrm(ks[5], (L, D_FF, D), D_FF ** -0.5),
        "ffn2_w_gate": nrm(ks[6], (L, D, D_FF), D ** -0.5),
        "ffn2_w_up": nrm(ks[7], (L, D, D_FF), D ** -0.5),
        "ffn2_w_down": nrm(ks[8], (L, D_FF, D), D_FF ** -0.5),
        "w_in": nrm(ks[9], (L, D, IN_WIDTH), D ** -0.5),
        "conv_w": nrm(ks[10], (L, CONV_WIDTH, CONV_CHANNELS), CONV_WIDTH ** -0.5),
        "conv_b": nrm(ks[11], (L, CONV_CHANNELS), 0.02),
        "conv_ln_g": 1.0 + nrm(ks[12], (L, CONV_CHANNELS), 0.05),
        "conv_ln_b": nrm(ks[13], (L, CONV_CHANNELS), 0.02),
        "gla_w_alpha": nrm(ks[14], (L, GLA_GATE_RANK, GLA_KEY_DIM), GLA_GATE_RANK ** -0.5),
        "gla_b_alpha": nrm(ks[15], (L, GLA_KEY_DIM), 0.02),
        "gla_norm_g": 1.0 + nrm(ks[16], (L, GLA_VALUE_DIM), 0.05),
        "w_branch": nrm(ks[17], (L, N_BRANCHES, BRANCH_WIDTH, D), BRANCH_WIDTH ** -0.5),
        "w_out": nrm(ks[18], (L, D, D), D ** -0.5),
    }


def _fwd_reference(x, norm_pre, norm_post, ffn1_w_gate, ffn1_w_up, ffn1_w_down,
              ffn2_w_gate, ffn2_w_up, ffn2_w_down, w_in, conv_w, conv_b,
              conv_ln_g, conv_ln_b, gla_w_alpha, gla_b_alpha, gla_norm_g,
              w_branch, w_out):
    for l in range(DEPTH):
        h = rms_norm(x, norm_pre[l, 0])
        x = x + 0.5 * rms_norm(swiglu_ffn(h, ffn1_w_gate[l], ffn1_w_up[l], ffn1_w_down[l]), norm_post[l, 0])
        h = rms_norm(x, norm_pre[l, 1])
        m = hybrid_mixer(h, w_in[l], conv_w[l], conv_b[l], conv_ln_g[l], conv_ln_b[l],
                         gla_w_alpha[l], gla_b_alpha[l], gla_norm_g[l], w_branch[l], w_out[l])
        x = x + rms_norm(m, norm_post[l, 1])
        h = rms_norm(x, norm_pre[l, 2])
        x = x + 0.5 * rms_norm(swiglu_ffn(h, ffn2_w_gate[l], ffn2_w_up[l], ffn2_w_down[l]), norm_post[l, 2])
    return x


import jax as _jax
import jax.numpy as _jnp

TWIN_FORMAT = 'train_step'
FWD_PARAMS = ['x', 'norm_pre', 'norm_post', 'ffn1_w_gate', 'ffn1_w_up', 'ffn1_w_down', 'ffn2_w_gate', 'ffn2_w_up', 'ffn2_w_down', 'w_in', 'conv_w', 'conv_b', 'conv_ln_g', 'conv_ln_b', 'gla_w_alpha', 'gla_b_alpha', 'gla_norm_g', 'w_branch', 'w_out']
TWIN_WEIGHTS = ['norm_pre', 'norm_post', 'ffn1_w_gate', 'ffn1_w_up', 'ffn1_w_down', 'ffn2_w_gate', 'ffn2_w_up', 'ffn2_w_down', 'w_in', 'conv_w', 'conv_b', 'conv_ln_g', 'conv_ln_b', 'gla_w_alpha', 'gla_b_alpha', 'gla_norm_g', 'w_branch', 'w_out']
TWIN_DIFF_INPUT = 'x'
TWIN_INPUTS = ['x', 'norm_pre', 'norm_post', 'ffn1_w_gate', 'ffn1_w_up', 'ffn1_w_down', 'ffn2_w_gate', 'ffn2_w_up', 'ffn2_w_down', 'w_in', 'conv_w', 'conv_b', 'conv_ln_g', 'conv_ln_b', 'gla_w_alpha', 'gla_b_alpha', 'gla_norm_g', 'w_branch', 'w_out', 'loss_target', 'm_norm_pre', 'm_norm_post', 'm_ffn1_w_gate', 'm_ffn1_w_up', 'm_ffn1_w_down', 'm_ffn2_w_gate', 'm_ffn2_w_up', 'm_ffn2_w_down', 'm_w_in', 'm_conv_w', 'm_conv_b', 'm_conv_ln_g', 'm_conv_ln_b', 'm_gla_w_alpha', 'm_gla_b_alpha', 'm_gla_norm_g', 'm_w_branch', 'm_w_out', 'v_norm_pre', 'v_norm_post', 'v_ffn1_w_gate', 'v_ffn1_w_up', 'v_ffn1_w_down', 'v_ffn2_w_gate', 'v_ffn2_w_up', 'v_ffn2_w_down', 'v_w_in', 'v_conv_w', 'v_conv_b', 'v_conv_ln_g', 'v_conv_ln_b', 'v_gla_w_alpha', 'v_gla_b_alpha', 'v_gla_norm_g', 'v_w_branch', 'v_w_out']
TWIN_OUTPUTS = ['loss', 'grad_x', 'grad_norm_pre', 'grad_norm_post', 'grad_ffn1_w_gate', 'grad_ffn1_w_up', 'grad_ffn1_w_down', 'grad_ffn2_w_gate', 'grad_ffn2_w_up', 'grad_ffn2_w_down', 'grad_w_in', 'grad_conv_w', 'grad_conv_b', 'grad_conv_ln_g', 'grad_conv_ln_b', 'grad_gla_w_alpha', 'grad_gla_b_alpha', 'grad_gla_norm_g', 'grad_w_branch', 'grad_w_out', 'delta_norm_pre', 'delta_norm_post', 'delta_ffn1_w_gate', 'delta_ffn1_w_up', 'delta_ffn1_w_down', 'delta_ffn2_w_gate', 'delta_ffn2_w_up', 'delta_ffn2_w_down', 'delta_w_in', 'delta_conv_w', 'delta_conv_b', 'delta_conv_ln_g', 'delta_conv_ln_b', 'delta_gla_w_alpha', 'delta_gla_b_alpha', 'delta_gla_norm_g', 'delta_w_branch', 'delta_w_out', 'new_m_norm_pre', 'new_m_norm_post', 'new_m_ffn1_w_gate', 'new_m_ffn1_w_up', 'new_m_ffn1_w_down', 'new_m_ffn2_w_gate', 'new_m_ffn2_w_up', 'new_m_ffn2_w_down', 'new_m_w_in', 'new_m_conv_w', 'new_m_conv_b', 'new_m_conv_ln_g', 'new_m_conv_ln_b', 'new_m_gla_w_alpha', 'new_m_gla_b_alpha', 'new_m_gla_norm_g', 'new_m_w_branch', 'new_m_w_out', 'new_v_norm_pre', 'new_v_norm_post', 'new_v_ffn1_w_gate', 'new_v_ffn1_w_up', 'new_v_ffn1_w_down', 'new_v_ffn2_w_gate', 'new_v_ffn2_w_up', 'new_v_ffn2_w_down', 'new_v_w_in', 'new_v_conv_w', 'new_v_conv_b', 'new_v_conv_ln_g', 'new_v_conv_ln_b', 'new_v_gla_w_alpha', 'new_v_gla_b_alpha', 'new_v_gla_norm_g', 'new_v_w_branch', 'new_v_w_out']
TWIN_LEAF_KINDS = {'loss': 'loss', 'grad_x': 'grad_x', 'grad_norm_pre': 'grad_w', 'grad_norm_post': 'grad_w', 'grad_ffn1_w_gate': 'grad_w', 'grad_ffn1_w_up': 'grad_w', 'grad_ffn1_w_down': 'grad_w', 'grad_ffn2_w_gate': 'grad_w', 'grad_ffn2_w_up': 'grad_w', 'grad_ffn2_w_down': 'grad_w', 'grad_w_in': 'grad_w', 'grad_conv_w': 'grad_w', 'grad_conv_b': 'grad_w', 'grad_conv_ln_g': 'grad_w', 'grad_conv_ln_b': 'grad_w', 'grad_gla_w_alpha': 'grad_w', 'grad_gla_b_alpha': 'grad_w', 'grad_gla_norm_g': 'grad_w', 'grad_w_branch': 'grad_w', 'grad_w_out': 'grad_w', 'delta_norm_pre': 'delta_w', 'delta_norm_post': 'delta_w', 'delta_ffn1_w_gate': 'delta_w', 'delta_ffn1_w_up': 'delta_w', 'delta_ffn1_w_down': 'delta_w', 'delta_ffn2_w_gate': 'delta_w', 'delta_ffn2_w_up': 'delta_w', 'delta_ffn2_w_down': 'delta_w', 'delta_w_in': 'delta_w', 'delta_conv_w': 'delta_w', 'delta_conv_b': 'delta_w', 'delta_conv_ln_g': 'delta_w', 'delta_conv_ln_b': 'delta_w', 'delta_gla_w_alpha': 'delta_w', 'delta_gla_b_alpha': 'delta_w', 'delta_gla_norm_g': 'delta_w', 'delta_w_branch': 'delta_w', 'delta_w_out': 'delta_w', 'new_m_norm_pre': 'new_m', 'new_m_norm_post': 'new_m', 'new_m_ffn1_w_gate': 'new_m', 'new_m_ffn1_w_up': 'new_m', 'new_m_ffn1_w_down': 'new_m', 'new_m_ffn2_w_gate': 'new_m', 'new_m_ffn2_w_up': 'new_m', 'new_m_ffn2_w_down': 'new_m', 'new_m_w_in': 'new_m', 'new_m_conv_w': 'new_m', 'new_m_conv_b': 'new_m', 'new_m_conv_ln_g': 'new_m', 'new_m_conv_ln_b': 'new_m', 'new_m_gla_w_alpha': 'new_m', 'new_m_gla_b_alpha': 'new_m', 'new_m_gla_norm_g': 'new_m', 'new_m_w_branch': 'new_m', 'new_m_w_out': 'new_m', 'new_v_norm_pre': 'new_v', 'new_v_norm_post': 'new_v', 'new_v_ffn1_w_gate': 'new_v', 'new_v_ffn1_w_up': 'new_v', 'new_v_ffn1_w_down': 'new_v', 'new_v_ffn2_w_gate': 'new_v', 'new_v_ffn2_w_up': 'new_v', 'new_v_ffn2_w_down': 'new_v', 'new_v_w_in': 'new_v', 'new_v_conv_w': 'new_v', 'new_v_conv_b': 'new_v', 'new_v_conv_ln_g': 'new_v', 'new_v_conv_ln_b': 'new_v', 'new_v_gla_w_alpha': 'new_v', 'new_v_gla_b_alpha': 'new_v', 'new_v_gla_norm_g': 'new_v', 'new_v_w_branch': 'new_v', 'new_v_w_out': 'new_v'}


def _forward(args):
    return _fwd_reference(*[args[k] for k in FWD_PARAMS])


def _output_shape():
    def fwd():
        inp = _fwd_setup_inputs(0)
        return _fwd_reference(*[inp[k] for k in FWD_PARAMS])
    out = _jax.eval_shape(fwd)
    return out.shape, out.dtype

N_MICROBATCH = 1
ADAM_LR = 0.001
ADAM_B1 = 0.9
ADAM_B2 = 0.999
ADAM_EPS = 1e-08
ADAM_WD = 0.01
ADAM_STEP = 10
PER_EXAMPLE_BATCH_AXIS = {'x': 0, 'loss_target': 0}
SHARED_INPUTS = []
_WEIGHT_DTYPES = {'norm_pre': _jnp.float32, 'norm_post': _jnp.float32, 'ffn1_w_gate': _jnp.float32, 'ffn1_w_up': _jnp.float32, 'ffn1_w_down': _jnp.float32, 'ffn2_w_gate': _jnp.float32, 'ffn2_w_up': _jnp.float32, 'ffn2_w_down': _jnp.float32, 'w_in': _jnp.float32, 'conv_w': _jnp.float32, 'conv_b': _jnp.float32, 'conv_ln_g': _jnp.float32, 'conv_ln_b': _jnp.float32, 'gla_w_alpha': _jnp.float32, 'gla_b_alpha': _jnp.float32, 'gla_norm_g': _jnp.float32, 'w_branch': _jnp.float32, 'w_out': _jnp.float32}
MOMENT_SCALE = {'norm_pre': 2.032398e+00, 'norm_post': 3.924825e+01, 'ffn1_w_gate': 7.038260e-01, 'ffn1_w_up': 7.558845e-01, 'ffn1_w_down': 1.253905e+00, 'ffn2_w_gate': 4.652841e-01, 'ffn2_w_up': 6.267250e-01, 'ffn2_w_down': 1.042512e+00, 'w_in': 1.044443e+00, 'conv_w': 2.037444e+00, 'conv_b': 2.847926e+01, 'conv_ln_g': 1.096556e+01, 'conv_ln_b': 1.684016e+01, 'gla_w_alpha': 1.956025e-01, 'gla_b_alpha': 8.456500e-01, 'gla_norm_g': 1.148398e+00, 'w_branch': 2.690307e+00, 'w_out': 4.742949e+00}


def _to_microbatches(a, axis):
    t = _jnp.moveaxis(a, axis, 0)
    t = t.reshape((N_MICROBATCH, t.shape[0] // N_MICROBATCH) + t.shape[1:])
    return _jnp.moveaxis(t, 1, axis + 1)


def setup_inputs(seed: int = 0) -> dict:
    inp = _fwd_setup_inputs(seed)
    key = _jax.random.fold_in(_jax.random.key(seed), 7919)
    shape, _ = _output_shape()
    out = dict(inp)
    out["loss_target"] = _jax.random.normal(_jax.random.fold_in(key, 0), shape, _jnp.float32)
    for i, name in enumerate(TWIN_WEIGHTS):
        w = inp[name].astype(_jnp.float32)
        if MOMENT_SCALE is None:
            s = _jnp.sqrt(_jnp.mean(_jnp.square(w)) + 1e-30)
        else:
            s = MOMENT_SCALE[name]
        km, kv = _jax.random.split(_jax.random.fold_in(key, i + 1))
        out[name] = w
        out["m_" + name] = s * _jax.random.normal(km, w.shape, _jnp.float32)
        out["v_" + name] = (s * s) * _jax.random.uniform(kv, w.shape, _jnp.float32, 0.5, 1.5)
    if N_MICROBATCH > 1:
        for name, axis in PER_EXAMPLE_BATCH_AXIS.items():
            out[name] = _to_microbatches(out[name], axis)
    return {'x': out['x'], 'norm_pre': out['norm_pre'], 'norm_post': out['norm_post'], 'ffn1_w_gate': out['ffn1_w_gate'], 'ffn1_w_up': out['ffn1_w_up'], 'ffn1_w_down': out['ffn1_w_down'], 'ffn2_w_gate': out['ffn2_w_gate'], 'ffn2_w_up': out['ffn2_w_up'], 'ffn2_w_down': out['ffn2_w_down'], 'w_in': out['w_in'], 'conv_w': out['conv_w'], 'conv_b': out['conv_b'], 'conv_ln_g': out['conv_ln_g'], 'conv_ln_b': out['conv_ln_b'], 'gla_w_alpha': out['gla_w_alpha'], 'gla_b_alpha': out['gla_b_alpha'], 'gla_norm_g': out['gla_norm_g'], 'w_branch': out['w_branch'], 'w_out': out['w_out'], 'loss_target': out['loss_target'], 'm_norm_pre': out['m_norm_pre'], 'm_norm_post': out['m_norm_post'], 'm_ffn1_w_gate': out['m_ffn1_w_gate'], 'm_ffn1_w_up': out['m_ffn1_w_up'], 'm_ffn1_w_down': out['m_ffn1_w_down'], 'm_ffn2_w_gate': out['m_ffn2_w_gate'], 'm_ffn2_w_up': out['m_ffn2_w_up'], 'm_ffn2_w_down': out['m_ffn2_w_down'], 'm_w_in': out['m_w_in'], 'm_conv_w': out['m_conv_w'], 'm_conv_b': out['m_conv_b'], 'm_conv_ln_g': out['m_conv_ln_g'], 'm_conv_ln_b': out['m_conv_ln_b'], 'm_gla_w_alpha': out['m_gla_w_alpha'], 'm_gla_b_alpha': out['m_gla_b_alpha'], 'm_gla_norm_g': out['m_gla_norm_g'], 'm_w_branch': out['m_w_branch'], 'm_w_out': out['m_w_out'], 'v_norm_pre': out['v_norm_pre'], 'v_norm_post': out['v_norm_post'], 'v_ffn1_w_gate': out['v_ffn1_w_gate'], 'v_ffn1_w_up': out['v_ffn1_w_up'], 'v_ffn1_w_down': out['v_ffn1_w_down'], 'v_ffn2_w_gate': out['v_ffn2_w_gate'], 'v_ffn2_w_up': out['v_ffn2_w_up'], 'v_ffn2_w_down': out['v_ffn2_w_down'], 'v_w_in': out['v_w_in'], 'v_conv_w': out['v_conv_w'], 'v_conv_b': out['v_conv_b'], 'v_conv_ln_g': out['v_conv_ln_g'], 'v_conv_ln_b': out['v_conv_ln_b'], 'v_gla_w_alpha': out['v_gla_w_alpha'], 'v_gla_b_alpha': out['v_gla_b_alpha'], 'v_gla_norm_g': out['v_gla_norm_g'], 'v_w_branch': out['v_w_branch'], 'v_w_out': out['v_w_out']}


def _loss(weights, diff, rest, loss_target):
    with _jax.named_scope("forward"):
        args = {**rest, TWIN_DIFF_INPUT: diff, **{k: w.astype(_WEIGHT_DTYPES[k]) for k, w in weights.items()}}
        y = _forward(args)
    with _jax.named_scope("loss_head"):
        err = _jnp.square(y.astype(_jnp.float32) - loss_target)
        return 0.5 * _jnp.sum(_jnp.mean(err, axis=-1)) if err.ndim else 0.5 * err


def _adamw(w, g, m, v):
    m = ADAM_B1 * m + (1.0 - ADAM_B1) * g
    v = ADAM_B2 * v + (1.0 - ADAM_B2) * _jnp.square(g)
    m_hat = m / (1.0 - ADAM_B1 ** ADAM_STEP)
    v_hat = v / (1.0 - ADAM_B2 ** ADAM_STEP)
    delta = -ADAM_LR * (m_hat / (_jnp.sqrt(v_hat) + ADAM_EPS) + ADAM_WD * w)
    return delta, m, v


def reference(x, norm_pre, norm_post, ffn1_w_gate, ffn1_w_up, ffn1_w_down, ffn2_w_gate, ffn2_w_up, ffn2_w_down, w_in, conv_w, conv_b, conv_ln_g, conv_ln_b, gla_w_alpha, gla_b_alpha, gla_norm_g, w_branch, w_out, loss_target, m_norm_pre, m_norm_post, m_ffn1_w_gate, m_ffn1_w_up, m_ffn1_w_down, m_ffn2_w_gate, m_ffn2_w_up, m_ffn2_w_down, m_w_in, m_conv_w, m_conv_b, m_conv_ln_g, m_conv_ln_b, m_gla_w_alpha, m_gla_b_alpha, m_gla_norm_g, m_w_branch, m_w_out, v_norm_pre, v_norm_post, v_ffn1_w_gate, v_ffn1_w_up, v_ffn1_w_down, v_ffn2_w_gate, v_ffn2_w_up, v_ffn2_w_down, v_w_in, v_conv_w, v_conv_b, v_conv_ln_g, v_conv_ln_b, v_gla_w_alpha, v_gla_b_alpha, v_gla_norm_g, v_w_branch, v_w_out):
    given = dict(x=x, norm_pre=norm_pre, norm_post=norm_post, ffn1_w_gate=ffn1_w_gate, ffn1_w_up=ffn1_w_up, ffn1_w_down=ffn1_w_down, ffn2_w_gate=ffn2_w_gate, ffn2_w_up=ffn2_w_up, ffn2_w_down=ffn2_w_down, w_in=w_in, conv_w=conv_w, conv_b=conv_b, conv_ln_g=conv_ln_g, conv_ln_b=conv_ln_b, gla_w_alpha=gla_w_alpha, gla_b_alpha=gla_b_alpha, gla_norm_g=gla_norm_g, w_branch=w_branch, w_out=w_out, loss_target=loss_target, m_norm_pre=m_norm_pre, m_norm_post=m_norm_post, m_ffn1_w_gate=m_ffn1_w_gate, m_ffn1_w_up=m_ffn1_w_up, m_ffn1_w_down=m_ffn1_w_down, m_ffn2_w_gate=m_ffn2_w_gate, m_ffn2_w_up=m_ffn2_w_up, m_ffn2_w_down=m_ffn2_w_down, m_w_in=m_w_in, m_conv_w=m_conv_w, m_conv_b=m_conv_b, m_conv_ln_g=m_conv_ln_g, m_conv_ln_b=m_conv_ln_b, m_gla_w_alpha=m_gla_w_alpha, m_gla_b_alpha=m_gla_b_alpha, m_gla_norm_g=m_gla_norm_g, m_w_branch=m_w_branch, m_w_out=m_w_out, v_norm_pre=v_norm_pre, v_norm_post=v_norm_post, v_ffn1_w_gate=v_ffn1_w_gate, v_ffn1_w_up=v_ffn1_w_up, v_ffn1_w_down=v_ffn1_w_down, v_ffn2_w_gate=v_ffn2_w_gate, v_ffn2_w_up=v_ffn2_w_up, v_ffn2_w_down=v_ffn2_w_down, v_w_in=v_w_in, v_conv_w=v_conv_w, v_conv_b=v_conv_b, v_conv_ln_g=v_conv_ln_g, v_conv_ln_b=v_conv_ln_b, v_gla_w_alpha=v_gla_w_alpha, v_gla_b_alpha=v_gla_b_alpha, v_gla_norm_g=v_gla_norm_g, v_w_branch=v_w_branch, v_w_out=v_w_out)
    weights = {n: given[n] for n in TWIN_WEIGHTS}
    shared = {n: given[n] for n in SHARED_INPUTS}
    per_example = {n: given[n] for n in ['x']}
    grad_fn = _jax.value_and_grad(_loss, argnums=(0, 1))

    def one_microbatch(ex, loss_target):
        ex = dict(ex)
        diff = ex.pop(TWIN_DIFF_INPUT)
        return grad_fn(weights, diff, {**shared, **ex}, loss_target)

    if N_MICROBATCH == 1:
        loss, (grad_w, grad_x) = one_microbatch(per_example, given["loss_target"])
    else:
        def body(carry, xs):
            loss_sum, grad_sum = carry
            l_k, (gw_k, gx_k) = one_microbatch(xs[0], xs[1])
            with _jax.named_scope("update"):
                return (loss_sum + l_k, _jax.tree.map(_jnp.add, grad_sum, gw_k)), gx_k

        init = (_jnp.zeros((), _jnp.float32), _jax.tree.map(_jnp.zeros_like, weights))
        (loss, grad_w), grad_x = _jax.lax.scan(body, init, (per_example, given["loss_target"]))
    with _jax.named_scope("update"):
        delta_w, new_m, new_v = {}, {}, {}
        for n in TWIN_WEIGHTS:
            delta_w[n], new_m[n], new_v[n] = _adamw(weights[n], grad_w[n], given["m_" + n], given["v_" + n])
    return (loss, grad_x, *[grad_w[n] for n in TWIN_WEIGHTS], *[delta_w[n] for n in TWIN_WEIGHTS],
            *[new_m[n] for n in TWIN_WEIGHTS], *[new_v[n] for n in TWIN_WEIGHTS])
```

```python
import functools

import jax
import jax.numpy as jnp
from jax import lax
from jax.experimental import pallas as pl
from jax.experimental.pallas import tpu as pltpu

F32 = jnp.float32
MXU_DTYPE = jnp.bfloat16
HIGHEST = lax.Precision.HIGHEST
MESH = pl.DeviceIdType.MESH

NORM_EPS = 1e-6
D_MODEL = 1024
D_FF = 2816
BRANCH = 512
CHUNK = 64
CONV_WIDTH = 31
CONV_PAD = 32
GLA_RANK = 16
GLA_RANK_PAD = 128
GLA_TAU = 16.0
SB_SCALE = 0.125
GLA_SCALE = 0.125
PACK_COLS = 1024
VMEM_LIMIT = 56 * 1024 * 1024

ADAM_LR, ADAM_B1, ADAM_B2, ADAM_EPS, ADAM_WD, ADAM_STEP = 0.001, 0.9, 0.999, 1e-08, 0.01, 10

NT = (((1,), (1,)), ((), ()))
TN = (((0,), (0,)), ((), ()))
NN = (((1,), (0,)), ((), ()))


def _params(sem=None, vmem=None):
    return pltpu.CompilerParams(dimension_semantics=sem, vmem_limit_bytes=vmem)


def _mx(v):
    return v.astype(MXU_DTYPE)


def _mx_round(v):
    return v.astype(MXU_DTYPE).astype(F32)


def _fit(dim, want):
    if dim <= want:
        return dim
    for d in range(want - want % 128, 0, -128):
        if dim % d == 0:
            return d
    raise ValueError((dim, want))


def mm(a, b, c=None, *, ta=False, tb=False, out_dtype=F32, tm=1024, tn=1024, tk=1024, name):
    K, M = a.shape if ta else a.shape[::-1]
    N = b.shape[0] if tb else b.shape[1]
    assert (b.shape[1] if tb else b.shape[0]) == K, (a.shape, b.shape, ta, tb)
    tm, tn, tk = _fit(M, tm), _fit(N, tn), _fit(K, tk)
    nk = K // tk
    dn = (((0 if ta else 1,), (1 if tb else 0,)), ((), ()))

    def body(*refs):
        if c is None:
            a_ref, b_ref, o_ref, acc = refs
            c_ref = None
        else:
            a_ref, b_ref, c_ref, o_ref, acc = refs
        k = pl.program_id(2)
        p = lax.dot_general(_mx(a_ref[...]), _mx(b_ref[...]), dn, preferred_element_type=F32)

        @pl.when(k == 0)
        def _():
            acc[...] = p

        @pl.when(k > 0)
        def _():
            acc[...] += p

        @pl.when(k == nk - 1)
        def _():
            r = acc[...]
            if c_ref is not None:
                r = r + c_ref[...].astype(F32)
            o_ref[...] = r.astype(o_ref.dtype)

    a_spec = pl.BlockSpec((tk, tm), lambda i, j, k: (k, i)) if ta else pl.BlockSpec((tm, tk), lambda i, j, k: (i, k))
    b_spec = pl.BlockSpec((tn, tk), lambda i, j, k: (j, k)) if tb else pl.BlockSpec((tk, tn), lambda i, j, k: (k, j))
    o_spec = pl.BlockSpec((tm, tn), lambda i, j, k: (i, j))
    ins, in_specs = [a, b], [a_spec, b_spec]
    if c is not None:
        ins.append(c)
        in_specs.append(o_spec)
    return pl.pallas_call(
        body, name=name, grid=(M // tm, N // tn, nk), in_specs=in_specs, out_specs=o_spec,
        out_shape=jax.ShapeDtypeStruct((M, N), out_dtype), scratch_shapes=[pltpu.VMEM((tm, tn), F32)],
        compiler_params=_params(("parallel", "parallel", "arbitrary"), VMEM_LIMIT),
    )(*ins)


def _row_arg(arg):
    if isinstance(arg, tuple):
        return arg
    return arg, arg.shape[1], 0


def _row_specs(rows, n):
    arrs, specs, avals = [], [], []
    for arg in rows:
        arr, width, cb = _row_arg(arg)
        rb = arr.shape[0] // n
        arrs.append(arr)
        specs.append(pl.BlockSpec((rb, width), lambda i, cb=cb: (i, cb)))
        avals.append(jax.ShapeDtypeStruct((rb, width), F32))
    return arrs, specs, avals


def _par_specs(pars):
    specs = [pl.BlockSpec(p.shape, lambda i, nd=p.ndim: (0,) * nd) for p in pars]
    avals = [jax.ShapeDtypeStruct(p.shape, F32) for p in pars]
    return specs, avals


def rowwise(f, rows, pars, out_dtypes, *, tm, name):
    n = _row_arg(rows[0])[0].shape[0] // tm
    arrs, rspecs, ravals = _row_specs(rows, n)
    pspecs, pavals = _par_specs(pars)
    oavals = jax.eval_shape(f, *ravals, *pavals)
    nin = len(arrs) + len(pars)

    def body(*refs):
        outs = f(*[r[...].astype(F32) for r in refs[:nin]])
        for o_ref, o in zip(refs[nin:], outs):
            o_ref[...] = o.astype(o_ref.dtype)

    return pl.pallas_call(
        body, name=name, grid=(n,), in_specs=rspecs + pspecs,
        out_specs=[pl.BlockSpec(o.shape, lambda i: (i, 0)) for o in oavals],
        out_shape=[jax.ShapeDtypeStruct((n * o.shape[0], o.shape[1]), dt) for o, dt in zip(oavals, out_dtypes)],
        compiler_params=_params(("parallel",), VMEM_LIMIT),
    )(*arrs, *pars)


def rowwise_vjp(f, rows, pars, cots, row_grad, par_grad, d_dtypes, adds=None, *, tm, name):
    n = _row_arg(rows[0])[0].shape[0] // tm
    arrs, rspecs, ravals = _row_specs(rows, n)
    pspecs, pavals = _par_specs(pars)
    carrs, cspecs, _ = _row_specs(cots, n)
    adds = adds or [None] * len(row_grad)
    add_arrs = [a for a in adds if a is not None]
    _, aspecs, _ = _row_specs(add_arrs, n)
    nr, npar, nc, na = len(arrs), len(pars), len(carrs), len(add_arrs)
    diff = list(row_grad) + [nr + j for j in par_grad]
    ngr = len(row_grad)

    def body(*refs):
        ins = [r[...].astype(F32) for r in refs[:nr + npar]]
        cs = tuple(r[...].astype(F32) for r in refs[nr + npar:nr + npar + nc])
        add_refs = list(refs[nr + npar + nc:nr + npar + nc + na])
        outs = refs[nr + npar + nc + na:]

        def g(*d):
            full = list(ins)
            for idx, val in zip(diff, d):
                full[idx] = val
            return f(*full)

        _, pullback = jax.vjp(g, *[ins[idx] for idx in diff])
        ds = pullback(cs)
        for k in range(ngr):
            d = ds[k]
            if adds[k] is not None:
                d = d + add_refs.pop(0)[...].astype(F32)
            outs[k][...] = d.astype(outs[k].dtype)
        i = pl.program_id(0)
        for k in range(ngr, len(diff)):
            @pl.when(i == 0)
            def _(k=k):
                outs[k][...] = ds[k]

            @pl.when(i > 0)
            def _(k=k):
                outs[k][...] += ds[k]

    out_specs = [pl.BlockSpec(ravals[i].shape, lambda i: (i, 0)) for i in row_grad] + [pspecs[j] for j in par_grad]
    out_shape = [jax.ShapeDtypeStruct((arrs[i].shape[0], ravals[i].shape[1]), dt) for i, dt in zip(row_grad, d_dtypes)]
    out_shape += [jax.ShapeDtypeStruct(pars[j].shape, F32) for j in par_grad]
    return pl.pallas_call(
        body, name=name, grid=(n,), in_specs=rspecs + pspecs + cspecs + aspecs, out_specs=out_specs, out_shape=out_shape,
        compiler_params=_params(("arbitrary",), VMEM_LIMIT),
    )(*arrs, *pars, *carrs, *add_arrs)


def _logsig(x):
    return jnp.minimum(x, 0.0) - jnp.log(1.0 + jnp.exp(-jnp.abs(x)))


def _sigmoid(x):
    return 1.0 / (1.0 + jnp.exp(-x))


def _silu(x):
    return x * _sigmoid(x)


def _rms(x, g):
    return x * lax.rsqrt(jnp.mean(x * x, axis=-1, keepdims=True) + NORM_EPS) * g


def f_rms(x, g):
    return (_rms(x, g),)


def f_swiglu(a, b):
    return (_silu(a) * b,)


def f_half_post(x, f, g):
    return (x + 0.5 * _rms(f, g),)


def f_post(x, m, g):
    return (x + _rms(m, g),)


def f_glu(a, g):
    return (a * _sigmoid(g),)


def f_conv_ln(y, lg, lb):
    mu = jnp.mean(y, axis=-1, keepdims=True)
    var = jnp.mean(jnp.square(y - mu), axis=-1, keepdims=True)
    return (_silu((y - mu) * lax.rsqrt(var + NORM_EPS) * lg + lb),)


def f_merge(g0, g1, g2, b0, b1, b2):
    return (_sigmoid(g0) * b0 + _sigmoid(g1) * b1 + _sigmoid(g2) * b2,)


def f_gla_post(o, r, g):
    w = o.shape[1]
    hv = w // 4
    i = lax.broadcasted_iota(jnp.int32, (w, w), 0) // hv
    j = lax.broadcasted_iota(jnp.int32, (w, w), 1) // hv
    avg = jnp.where(i == j, 1.0 / hv, 0.0).astype(F32)
    ms = jnp.dot(o * o, avg, precision=HIGHEST, preferred_element_type=F32)
    return (o * lax.rsqrt(ms + NORM_EPS) * g * _silu(r),)


def f_gla_pre(q, k, lr, wa, ba):
    tm = q.shape[0]
    pre = jnp.dot(_mx_round(lr), _mx_round(wa), precision=HIGHEST, preferred_element_type=F32) + ba
    la = _logsig(pre) / GLA_TAU
    i = lax.broadcasted_iota(jnp.int32, (tm, tm), 0)
    j = lax.broadcasted_iota(jnp.int32, (tm, tm), 1)
    later = jnp.where((i // CHUNK == j // CHUNK) & (j > i), 1.0, 0.0).astype(F32)
    ci = lax.broadcasted_iota(jnp.int32, (tm // CHUNK, tm), 0)
    cj = lax.broadcasted_iota(jnp.int32, (tm // CHUNK, tm), 1) // CHUNK
    chunk_sum = jnp.where(ci == cj, 1.0, 0.0).astype(F32)
    to_end = jnp.dot(later, la, precision=HIGHEST, preferred_element_type=F32)
    lam = jnp.exp(jnp.dot(chunk_sum, la, precision=HIGHEST, preferred_element_type=F32))
    return q * GLA_SCALE, k * jnp.exp(to_end), lam


def _split_dot(x, u):
    hi = _mx(x)
    lo = _mx(x - hi.astype(F32))
    return jnp.dot(hi, u, preferred_element_type=F32) + jnp.dot(lo, u, preferred_element_type=F32)


def _tri(tq, tk):
    row = lax.broadcasted_iota(jnp.int32, (tq, tk), 0)
    col = lax.broadcasted_iota(jnp.int32, (tq, tk), 1)
    return row, col


def _sb_tile(qh, kh, valid, suf, run):
    z = lax.dot_general(qh, kh, NT, preferred_element_type=F32) * SB_SCALE
    lp = jnp.minimum(z, 0.0) - jnp.log(1.0 + jnp.exp(-jnp.abs(z)))
    lk = jnp.where(valid, lp - z, 0.0)
    inc = _split_dot(lk, suf)
    a = jnp.where(valid, jnp.exp(lp + (inc - lk + run)), 0.0)
    return lp, a, run + inc[:, 0:1]


def sb_attn_fwd(qkv, *, tq, name):
    S = qkv.shape[0]
    tk = tq
    pairs = BRANCH // 128

    def body(q_ref, k_ref, v_ref, o_ref):
        i = pl.program_id(1)
        row, col = _tri(tq, tk)
        suf = _mx(row >= col)
        q = q_ref[...]

        def step(j, carry):
            ks = pl.multiple_of((i - j) * tk, tk)
            kb = k_ref[pl.ds(ks, tk), :]
            vb = v_ref[pl.ds(ks, tk), :]
            valid = (col - row) < j * tq
            new = []
            for h in range(2):
                acc, run = carry[h]
                sl = slice(64 * h, 64 * h + 64)
                _, a, run = _sb_tile(q[:, sl], kb[:, sl], valid, suf, run)
                acc = acc + jnp.dot(_mx(a), vb[:, sl], preferred_element_type=F32)
                new.append((acc, run))
            return tuple(new)

        zero = (jnp.zeros((tq, 64), F32), jnp.zeros((tq, 1), F32))
        res = lax.fori_loop(0, i + 1, step, (zero, zero))
        o_ref[...] = jnp.concatenate([res[0][0], res[1][0]], axis=1).astype(o_ref.dtype)

    return pl.pallas_call(
        body, name=name, grid=(pairs, S // tq),
        in_specs=[pl.BlockSpec((tq, 128), lambda p, i: (i, p)),
                  pl.BlockSpec((S, 128), lambda p, i: (0, pairs + p)),
                  pl.BlockSpec((S, 128), lambda p, i: (0, 2 * pairs + p))],
        out_specs=pl.BlockSpec((tq, 128), lambda p, i: (i, p)),
        out_shape=jax.ShapeDtypeStruct((S, BRANCH), MXU_DTYPE),
        compiler_params=_params(("parallel", "parallel"), VMEM_LIMIT),
    )(qkv, qkv, qkv)


def sb_attn_bwd(qkv, do, *, tq, name):
    S = qkv.shape[0]
    tk = tq
    nq = S // tq
    pairs = BRANCH // 128

    def body(q_ref, k_ref, v_ref, do_ref, dq_ref, dk_ref, dv_ref, g_sc, b_sc):
        i = pl.program_id(1)

        @pl.when(i == 0)
        def _():
            dk_ref[...] = jnp.zeros_like(dk_ref)
            dv_ref[...] = jnp.zeros_like(dv_ref)

        row, col = _tri(tq, tk)
        suf = _mx(row >= col)
        pre = _mx(row <= col)
        q = q_ref[...]
        dout = do_ref[...]

        def sweep1(j, carry):
            kblk = i - j
            ks = pl.multiple_of(kblk * tk, tk)
            kb = k_ref[pl.ds(ks, tk), :]
            vb = v_ref[pl.ds(ks, tk), :]
            valid = (col - row) < j * tq
            runs, dvs = [], []
            for h in range(2):
                sl = slice(64 * h, 64 * h + 64)
                lp, a, run = _sb_tile(q[:, sl], kb[:, sl], valid, suf, carry[h])
                da = lax.dot_general(dout[:, sl], vb[:, sl], NT, preferred_element_type=F32)
                g_sc[h, kblk] = (a * da).astype(g_sc.dtype)
                b_sc[h, kblk] = jnp.where(valid, jnp.exp(lp), 0.0).astype(b_sc.dtype)
                dvs.append(lax.dot_general(_mx(a), dout[:, sl], TN, preferred_element_type=F32))
                runs.append(run)
            dv_ref[pl.ds(ks, tk), :] += jnp.concatenate(dvs, axis=1)
            return tuple(runs)

        lax.fori_loop(0, i + 1, sweep1, (jnp.zeros((tq, 1), F32), jnp.zeros((tq, 1), F32)))

        def sweep2(kblk, carry):
            ks = pl.multiple_of(kblk * tk, tk)
            kb = k_ref[pl.ds(ks, tk), :]
            new, dks = [], []
            for h in range(2):
                dq, run = carry[h]
                sl = slice(64 * h, 64 * h + 64)
                g = g_sc[h, kblk]
                beta = b_sc[h, kblk].astype(F32)
                inc = jnp.dot(g, pre, preferred_element_type=F32)
                g = g.astype(F32)
                dz = _mx((g - beta * (inc + run)) * SB_SCALE)
                dq = dq + jnp.dot(dz, kb[:, sl], preferred_element_type=F32)
                dks.append(lax.dot_general(dz, q[:, sl], TN, preferred_element_type=F32))
                new.append((dq, run + inc[:, tk - 1:tk]))
            dk_ref[pl.ds(ks, tk), :] += jnp.concatenate(dks, axis=1)
            return tuple(new)

        zero = (jnp.zeros((tq, 64), F32), jnp.zeros((tq, 1), F32))
        res = lax.fori_loop(0, i + 1, sweep2, (zero, zero))
        dq_ref[...] = jnp.concatenate([res[0][0], res[1][0]], axis=1).astype(dq_ref.dtype)

    return pl.pallas_call(
        body, name=name, grid=(pairs, nq),
        in_specs=[pl.BlockSpec((tq, 128), lambda p, i: (i, p)),
                  pl.BlockSpec((S, 128), lambda p, i: (0, pairs + p)),
                  pl.BlockSpec((S, 128), lambda p, i: (0, 2 * pairs + p)),
                  pl.BlockSpec((tq, 128), lambda p, i: (i, p))],
        out_specs=[pl.BlockSpec((tq, 128), lambda p, i: (i, p)),
                   pl.BlockSpec((S, 128), lambda p, i: (0, p)),
                   pl.BlockSpec((S, 128), lambda p, i: (0, p))],
        out_shape=[jax.ShapeDtypeStruct((S, BRANCH), MXU_DTYPE), jax.ShapeDtypeStruct((S, BRANCH), F32),
                   jax.ShapeDtypeStruct((S, BRANCH), F32)],
        scratch_shapes=[pltpu.VMEM((2, nq, tq, tk), MXU_DTYPE), pltpu.VMEM((2, nq, tq, tk), MXU_DTYPE)],
        compiler_params=_params(("parallel", "arbitrary"), VMEM_LIMIT),
    )(qkv, qkv, qkv, do)


def conv_fwd(u, w, b, *, tm, name):
    S, C = u.shape
    hb = tm // CONV_PAD

    def body(u_ref, halo_ref, w_ref, b_ref, y_ref, buf):
        i = pl.program_id(0)
        buf[pl.ds(CONV_PAD, tm), :] = u_ref[...]
        buf[pl.ds(0, CONV_PAD), :] = jnp.where(i > 0, halo_ref[...], 0.0)
        acc = jnp.broadcast_to(b_ref[...], (tm, 128))
        for j in range(CONV_WIDTH):
            acc = acc + buf[pl.ds(CONV_PAD - (CONV_WIDTH - 1) + j, tm), :] * w_ref[pl.ds(j, 1), :]
        y_ref[...] = acc

    return pl.pallas_call(
        body, name=name, grid=(S // tm, C // 128),
        in_specs=[pl.BlockSpec((tm, 128), lambda i, c: (i, c)),
                  pl.BlockSpec((CONV_PAD, 128), lambda i, c: (jnp.maximum(i * hb - 1, 0), c)),
                  pl.BlockSpec((CONV_PAD, 128), lambda i, c: (0, c)),
                  pl.BlockSpec((1, 128), lambda i, c: (0, c))],
        out_specs=pl.BlockSpec((tm, 128), lambda i, c: (i, c)),
        out_shape=jax.ShapeDtypeStruct((S, C), F32),
        scratch_shapes=[pltpu.VMEM((tm + CONV_PAD, 128), F32)],
        compiler_params=_params(("parallel", "parallel")),
    )(u, u, w, b)


def conv_bwd(dy, u, w, *, tm, name):
    S, C = u.shape
    hb = tm // CONV_PAD
    n = S // tm

    def body(dy_ref, dyn_ref, u_ref, up_ref, w_ref, du_ref, dw_ref, bufy, bufu):
        i = pl.program_id(1)
        dyv = dy_ref[...]
        bufy[pl.ds(0, tm), :] = dyv
        bufy[pl.ds(tm, CONV_PAD), :] = jnp.where(i < n - 1, dyn_ref[...], 0.0)
        bufu[pl.ds(CONV_PAD, tm), :] = u_ref[...]
        bufu[pl.ds(0, CONV_PAD), :] = jnp.where(i > 0, up_ref[...], 0.0)

        @pl.when(i == 0)
        def _():
            dw_ref[...] = jnp.zeros_like(dw_ref)

        acc = jnp.zeros((tm, 128), F32)
        for j in range(CONV_WIDTH):
            acc = acc + bufy[pl.ds(CONV_WIDTH - 1 - j, tm), :] * w_ref[pl.ds(j, 1), :]
            shifted = bufu[pl.ds(CONV_PAD - (CONV_WIDTH - 1) + j, tm), :]
            dw_ref[pl.ds(j, 1), :] += jnp.sum(dyv * shifted, axis=0, keepdims=True)
        du_ref[...] = acc
        dw_ref[pl.ds(CONV_WIDTH, 1), :] += jnp.sum(dyv, axis=0, keepdims=True)

    return pl.pallas_call(
        body, name=name, grid=(C // 128, n),
        in_specs=[pl.BlockSpec((tm, 128), lambda c, i: (i, c)),
                  pl.BlockSpec((CONV_PAD, 128), lambda c, i: (jnp.minimum((i + 1) * hb, n * hb - 1), c)),
                  pl.BlockSpec((tm, 128), lambda c, i: (i, c)),
                  pl.BlockSpec((CONV_PAD, 128), lambda c, i: (jnp.maximum(i * hb - 1, 0), c)),
                  pl.BlockSpec((CONV_PAD, 128), lambda c, i: (0, c))],
        out_specs=[pl.BlockSpec((tm, 128), lambda c, i: (i, c)), pl.BlockSpec((CONV_PAD, 128), lambda c, i: (0, c))],
        out_shape=[jax.ShapeDtypeStruct((S, C), F32), jax.ShapeDtypeStruct((CONV_PAD, C), F32)],
        scratch_shapes=[pltpu.VMEM((tm + CONV_PAD, 128), F32), pltpu.VMEM((tm + CONV_PAD, 128), F32)],
        compiler_params=_params(("parallel", "arbitrary")),
    )(dy, dy, u, u, w)


GLA_HEADS, GLA_DK, GLA_DV = 4, 64, 128


def gla_scan_fwd(qs, kd, gl, lam, *, tm, name):
    S = qs.shape[0]
    nc = tm // CHUNK

    def body(qs_ref, kd_ref, v_ref, lam_ref, o_ref, st_ref, state):
        @pl.when(pl.program_id(0) == 0)
        def _():
            state[...] = jnp.zeros_like(state)

        for c in range(nc):
            rows = pl.ds(c * CHUNK, CHUNK)
            q, k, v = _mx(qs_ref[rows, :]), _mx(kd_ref[rows, :]), _mx(v_ref[rows, :])
            upd = [lax.dot_general(v[:, h * GLA_DV:(h + 1) * GLA_DV], k[:, h * GLA_DK:(h + 1) * GLA_DK], TN,
                                   preferred_element_type=F32) for h in range(GLA_HEADS)]
            st = state[...] * lam_ref[pl.ds(c, 1), :] + jnp.concatenate(upd, axis=1)
            state[...] = st
            st_ref[c] = st
            stm = _mx(st)
            o = [lax.dot_general(q[:, h * GLA_DK:(h + 1) * GLA_DK], stm[:, h * GLA_DK:(h + 1) * GLA_DK], NT,
                                 preferred_element_type=F32) for h in range(GLA_HEADS)]
            o_ref[rows, :] = jnp.concatenate(o, axis=1)

    dk_all = GLA_HEADS * GLA_DK
    dv_all = GLA_HEADS * GLA_DV
    return pl.pallas_call(
        body, name=name, grid=(S // tm,),
        in_specs=[pl.BlockSpec((tm, dk_all), lambda i: (i, 0)), pl.BlockSpec((tm, dk_all), lambda i: (i, 0)),
                  pl.BlockSpec((tm, dv_all), lambda i: (i, 1)), pl.BlockSpec((nc, dk_all), lambda i: (i, 0))],
        out_specs=[pl.BlockSpec((tm, dv_all), lambda i: (i, 0)), pl.BlockSpec((nc, GLA_DV, dk_all), lambda i: (i, 0, 0))],
        out_shape=[jax.ShapeDtypeStruct((S, dv_all), F32), jax.ShapeDtypeStruct((S // CHUNK, GLA_DV, dk_all), F32)],
        scratch_shapes=[pltpu.VMEM((GLA_DV, dk_all), F32)],
        compiler_params=_params(("arbitrary",)),
    )(qs, kd, gl, lam)


def gla_scan_bwd(do, qs, kd, gl, lam, states, *, tm, name):
    S = qs.shape[0]
    nc = tm // CHUNK
    n = S // tm
    dk_all = GLA_HEADS * GLA_DK
    dv_all = GLA_HEADS * GLA_DV

    def body(do_ref, qs_ref, kd_ref, v_ref, lam_ref, st_ref, prev_ref, dqs_ref, dkd_ref, dv_ref, dlam_ref, carry):
        i = pl.program_id(0)

        @pl.when(i == 0)
        def _():
            carry[...] = jnp.zeros_like(carry)

        for c in reversed(range(nc)):
            rows = pl.ds(c * CHUNK, CHUNK)
            q, k, v, d = _mx(qs_ref[rows, :]), _mx(kd_ref[rows, :]), _mx(v_ref[rows, :]), _mx(do_ref[rows, :])
            st = _mx(st_ref[c])
            before = st_ref[c - 1] if c > 0 else jnp.where(i < n - 1, prev_ref[0], 0.0)
            outer = [lax.dot_general(d[:, h * GLA_DV:(h + 1) * GLA_DV], q[:, h * GLA_DK:(h + 1) * GLA_DK], TN,
                                     preferred_element_type=F32) for h in range(GLA_HEADS)]
            dst = carry[...] + jnp.concatenate(outer, axis=1)
            dstm = _mx(dst)
            dq, dkk, dvv = [], [], []
            for h in range(GLA_HEADS):
                ksl = slice(h * GLA_DK, (h + 1) * GLA_DK)
                vsl = slice(h * GLA_DV, (h + 1) * GLA_DV)
                dq.append(jnp.dot(d[:, vsl], st[:, ksl], preferred_element_type=F32))
                dkk.append(jnp.dot(v[:, vsl], dstm[:, ksl], preferred_element_type=F32))
                dvv.append(lax.dot_general(k[:, ksl], dstm[:, ksl], NT, preferred_element_type=F32))
            dqs_ref[rows, :] = jnp.concatenate(dq, axis=1)
            dkd_ref[rows, :] = jnp.concatenate(dkk, axis=1)
            dv_ref[rows, :] = jnp.concatenate(dvv, axis=1)
            dlam_ref[pl.ds(c, 1), :] = jnp.sum(dst * before, axis=0, keepdims=True)
            carry[...] = dst * lam_ref[pl.ds(c, 1), :]

    rev = lambda i: n - 1 - i
    return pl.pallas_call(
        body, name=name, grid=(n,),
        in_specs=[pl.BlockSpec((tm, dv_all), lambda i: (rev(i), 0)), pl.BlockSpec((tm, dk_all), lambda i: (rev(i), 0)),
                  pl.BlockSpec((tm, dk_all), lambda i: (rev(i), 0)), pl.BlockSpec((tm, dv_all), lambda i: (rev(i), 1)),
                  pl.BlockSpec((nc, dk_all), lambda i: (rev(i), 0)),
                  pl.BlockSpec((nc, GLA_DV, dk_all), lambda i: (rev(i), 0, 0)),
                  pl.BlockSpec((1, GLA_DV, dk_all), lambda i: (jnp.maximum(rev(i) * nc - 1, 0), 0, 0))],
        out_specs=[pl.BlockSpec((tm, dk_all), lambda i: (rev(i), 0)), pl.BlockSpec((tm, dk_all), lambda i: (rev(i), 0)),
                   pl.BlockSpec((tm, dv_all), lambda i: (rev(i), 0)), pl.BlockSpec((nc, dk_all), lambda i: (rev(i), 0))],
        out_shape=[jax.ShapeDtypeStruct((S, dk_all), F32), jax.ShapeDtypeStruct((S, dk_all), F32),
                   jax.ShapeDtypeStruct((S, dv_all), F32), jax.ShapeDtypeStruct((S // CHUNK, dk_all), F32)],
        scratch_shapes=[pltpu.VMEM((GLA_DV, dk_all), F32)],
        compiler_params=_params(("arbitrary",)),
    )(do, qs, kd, gl, lam, states, states)


def loss_head(y, target, *, tm, name):
    S, D = y.shape

    def body(y_ref, t_ref, dy_ref, sq_ref):
        err = y_ref[...] - t_ref[...]
        dy_ref[...] = err * (1.0 / D)
        part = jnp.sum(err * err, axis=0, keepdims=True)

        @pl.when(pl.program_id(0) == 0)
        def _():
            sq_ref[...] = part

        @pl.when(pl.program_id(0) > 0)
        def _():
            sq_ref[...] += part

    return pl.pallas_call(
        body, name=name, grid=(S // tm,),
        in_specs=[pl.BlockSpec((tm, D), lambda i: (i, 0)), pl.BlockSpec((tm, D), lambda i: (i, 0))],
        out_specs=[pl.BlockSpec((tm, D), lambda i: (i, 0)), pl.BlockSpec((1, D), lambda i: (0, 0))],
        out_shape=[jax.ShapeDtypeStruct((S, D), F32), jax.ShapeDtypeStruct((1, D), F32)],
        compiler_params=_params(("arbitrary",)),
    )(y, target)


def f_adamw(w, g, m, v):
    m = ADAM_B1 * m + (1.0 - ADAM_B1) * g
    v = ADAM_B2 * v + (1.0 - ADAM_B2) * jnp.square(g)
    m_hat = m / (1.0 - ADAM_B1 ** ADAM_STEP)
    v_hat = v / (1.0 - ADAM_B2 ** ADAM_STEP)
    return -ADAM_LR * (m_hat / (jnp.sqrt(v_hat) + ADAM_EPS) + ADAM_WD * w), m, v


def adamw(w, g, m, v, *, name):
    shape = w.shape
    cols = shape[-1]
    rows = w.size // cols
    tm = rows
    while tm % 16 == 0 and tm * cols * 4 > (1 << 20):
        tm //= 2
    flat = [t.reshape(rows, cols) for t in (w, g, m, v)]
    outs = rowwise(f_adamw, flat, [], [F32, F32, F32], tm=tm, name=name)
    return [o.reshape(shape) for o in outs]


def _place():
    x, y, c = lax.axis_index("x"), lax.axis_index("y"), lax.axis_index("c")
    return x, y, c, [(1 - x, y), (x, 1 - y), (1 - x, 1 - y)]


def _any():
    return pl.BlockSpec(memory_space=pl.ANY)


def all_gather_chips(v, *, name):
    R, C = v.shape
    H = R // 2

    def body(v_ref, o_ref, send, recv, fsend, frecv, local):
        x, y, c, chips = _place()
        me = 2 * x + y
        mine = pl.ds(pl.multiple_of(c * H, 16), H)
        other = pl.ds(pl.multiple_of((1 - c) * H, 16), H)
        own = pltpu.make_async_copy(v_ref, o_ref.at[me], local)
        own.start()

        def cross(j, src_chip):
            return pltpu.make_async_remote_copy(
                src_ref=v_ref.at[mine], dst_ref=o_ref.at[src_chip, mine], send_sem=send.at[j], recv_sem=recv.at[j],
                device_id=(*chips[j], c), device_id_type=MESH)

        def handed(j, half):
            src_chip = 2 * chips[j][0] + chips[j][1]
            return pltpu.make_async_remote_copy(
                src_ref=o_ref.at[src_chip, half], dst_ref=o_ref.at[src_chip, half], send_sem=fsend.at[j],
                recv_sem=frecv.at[j], device_id=(x, y, 1 - c), device_id_type=MESH)

        for j in range(3):
            cross(j, me).start()
        for j in range(3):
            cross(j, 2 * chips[j][0] + chips[j][1]).wait_recv()
            handed(j, mine).start()
        for j in range(3):
            handed(j, other).wait_recv()
        for j in range(3):
            cross(j, me).wait_send()
            handed(j, mine).wait_send()
        own.wait()

    return pl.pallas_call(
        body, name=name, in_specs=[_any()], out_specs=_any(), out_shape=jax.ShapeDtypeStruct((4, R, C), v.dtype),
        scratch_shapes=[pltpu.SemaphoreType.DMA((3,)), pltpu.SemaphoreType.DMA((3,)), pltpu.SemaphoreType.DMA((3,)),
                        pltpu.SemaphoreType.DMA((3,)), pltpu.SemaphoreType.DMA],
    )(v)


def swap_halves(g, *, name):
    _, R, C = g.shape
    H = R // 2

    def body(g_ref, mine_ref, theirs_ref, send, recv, local):
        x, y, c, _ = _place()
        keep = pltpu.make_async_copy(g_ref.at[:, pl.ds(pl.multiple_of(c * H, 8), H)], mine_ref, local)
        keep.start()
        give = pltpu.make_async_remote_copy(
            src_ref=g_ref.at[:, pl.ds(pl.multiple_of((1 - c) * H, 8), H)], dst_ref=theirs_ref, send_sem=send,
            recv_sem=recv, device_id=(x, y, 1 - c), device_id_type=MESH)
        give.start()
        give.wait()
        keep.wait()

    half = jax.ShapeDtypeStruct((4, H, C), g.dtype)
    return pl.pallas_call(
        body, name=name, in_specs=[_any()], out_specs=[_any(), _any()], out_shape=[half, half],
        scratch_shapes=[pltpu.SemaphoreType.DMA, pltpu.SemaphoreType.DMA, pltpu.SemaphoreType.DMA],
    )(g)


def exchange_pieces(p, *, name):
    def body(p_ref, b_ref, send, recv, local):
        x, y, c, chips = _place()
        me = 2 * x + y
        own = pltpu.make_async_copy(p_ref.at[me], b_ref.at[me], local)
        own.start()

        def cross(j, piece, slot):
            return pltpu.make_async_remote_copy(
                src_ref=p_ref.at[piece], dst_ref=b_ref.at[slot], send_sem=send.at[j], recv_sem=recv.at[j],
                device_id=(*chips[j], c), device_id_type=MESH)

        for j in range(3):
            cross(j, 2 * chips[j][0] + chips[j][1], me).start()
        for j in range(3):
            cross(j, me, 2 * chips[j][0] + chips[j][1]).wait_recv()
        for j in range(3):
            cross(j, 2 * chips[j][0] + chips[j][1], me).wait_send()
        own.wait()

    return pl.pallas_call(
        body, name=name, in_specs=[_any()], out_specs=_any(), out_shape=jax.ShapeDtypeStruct(p.shape, p.dtype),
        scratch_shapes=[pltpu.SemaphoreType.DMA((3,)), pltpu.SemaphoreType.DMA((3,)), pltpu.SemaphoreType.DMA],
    )(p)


def join_halves(f, *, name):
    H, C = f.shape

    def body(f_ref, o_ref, send, recv, local):
        x, y, c, _ = _place()
        keep = pltpu.make_async_copy(f_ref, o_ref.at[c], local)
        keep.start()
        give = pltpu.make_async_remote_copy(src_ref=f_ref, dst_ref=o_ref.at[c], send_sem=send, recv_sem=recv,
                                            device_id=(x, y, 1 - c), device_id_type=MESH)
        give.start()
        give.wait()
        keep.wait()

    return pl.pallas_call(
        body, name=name, in_specs=[_any()], out_specs=_any(), out_shape=jax.ShapeDtypeStruct((2, H, C), f.dtype),
        scratch_shapes=[pltpu.SemaphoreType.DMA, pltpu.SemaphoreType.DMA, pltpu.SemaphoreType.DMA],
    )(f)


def _add_rows(terms, *, name):
    def f(*t):
        s = t[0]
        for u in t[1:]:
            s = s + u
        return (s,)

    R = terms[0].shape[0]
    tm = 512
    while R % tm:
        tm //= 2
    return rowwise(f, terms, [], [F32], tm=tm, name=name)[0]


def reduce_scatter(g):
    _, R, C = g.shape
    H = R // 2
    mine, theirs = swap_halves(g, name="rs_swap_halves")
    pair = _add_rows([mine.reshape(4 * H, C), theirs.reshape(4 * H, C)], name="rs_add_pair").reshape(4, H, C)
    got = exchange_pieces(pair, name="rs_exchange")
    total = _add_rows([got[k] for k in range(4)], name="rs_add_chips")
    return join_halves(total, name="rs_join").reshape(R, C)


BIG = ("ffn1_w_gate", "ffn1_w_up", "ffn1_w_down", "ffn2_w_gate", "ffn2_w_up", "ffn2_w_down", "w_in", "w_branch", "w_out")
SMALL = ("norm_pre", "norm_post", "conv_w", "gla_w_alpha")
REPLICATED = ("conv_b", "conv_ln_g", "conv_ln_b", "gla_b_alpha", "gla_norm_g")
WEIGHTS = ("norm_pre", "norm_post", "ffn1_w_gate", "ffn1_w_up", "ffn1_w_down", "ffn2_w_gate", "ffn2_w_up", "ffn2_w_down",
           "w_in", "conv_w", "conv_b", "conv_ln_g", "conv_ln_b", "gla_w_alpha", "gla_b_alpha", "gla_norm_g", "w_branch",
           "w_out")
SHARD_AXIS = {"ffn1_w_gate": 2, "ffn1_w_up": 2, "ffn1_w_down": 1, "ffn2_w_gate": 2, "ffn2_w_up": 2, "ffn2_w_down": 1,
              "w_in": 2, "w_branch": 3, "w_out": 1, "norm_pre": 2, "norm_post": 2, "conv_w": 2, "gla_w_alpha": 2}


def _pack(arrs, dtype, row_mult):
    flat = jnp.concatenate([a.astype(dtype).reshape(-1) for a in arrs])
    rows = -(-flat.shape[0] // PACK_COLS)
    rows = -(-rows // row_mult) * row_mult
    return jnp.pad(flat, (0, rows * PACK_COLS - flat.shape[0])).reshape(rows, PACK_COLS)


def _unpack(flat, shapes):
    out, off = [], 0
    for s in shapes:
        n = 1
        for d in s:
            n *= d
        out.append(lax.slice_in_dim(flat, off, off + n, axis=flat.ndim - 1).reshape(flat.shape[:-1] + tuple(s)))
        off += n
    return out


def gather_weights(shards, names, dtype, row_mult, name):
    packed = _pack([shards[k] for k in names], dtype, row_mult)
    got = all_gather_chips(packed, name=name).reshape(4, -1)
    parts = _unpack(got, [shards[k].shape for k in names])
    return {k: jnp.concatenate([p[q] for q in range(4)], axis=SHARD_AXIS[k]) for k, p in zip(names, parts)}


def scatter_grads(grads, shards):
    sharded = BIG + SMALL
    pieces = []
    for q in range(4):
        part = []
        for k in sharded:
            w = shards[k].shape[SHARD_AXIS[k]]
            part.append(lax.slice_in_dim(grads[k], q * w, (q + 1) * w, axis=SHARD_AXIS[k]))
        part += [grads[k] for k in REPLICATED]
        pieces.append(_pack(part, F32, 512))
    total = reduce_scatter(jnp.stack(pieces)).reshape(-1)
    names = sharded + REPLICATED
    return dict(zip(names, _unpack(total, [shards[k].shape for k in names])))


TM = 256
TM_GLA = 512
TQ = 256

IN_SB, IN_CONV, IN_GLA, IN_LR, IN_GATE = 0, 1536, 2560, 4096, 4112
IN_END = 7184


def layer_weights(full, l):
    w = {}
    for f in ("ffn1", "ffn2"):
        w[f + "_gu"] = jnp.concatenate([full[f + "_w_gate"][l], full[f + "_w_up"][l]], axis=1)
        w[f + "_d"] = full[f + "_w_down"][l]
    win = full["w_in"][l]
    w["in_sb"] = win[:, IN_SB:IN_CONV]
    w["in_conv"] = win[:, IN_CONV:IN_GLA]
    w["in_gla"] = win[:, IN_GLA:IN_LR]
    w["in_lr"] = jnp.pad(win[:, IN_LR:IN_GATE], ((0, 0), (0, GLA_RANK_PAD - GLA_RANK)))
    w["in_gate"] = win[:, IN_GATE:IN_END]
    w["branch"] = [full["w_branch"][l, g] for g in range(3)]
    w["out"] = full["w_out"][l]
    return w


def ffn_fwd(x, gpre, gpost, wgu, wd, tag):
    h = rowwise(f_rms, [x], [gpre], [MXU_DTYPE], tm=TM, name=tag + "_pre")[0]
    ab = mm(h, wgu, tn=512, name=tag + "_gu")
    z = rowwise(f_swiglu, [(ab, D_FF, 0), (ab, D_FF, 1)], [], [MXU_DTYPE], tm=TM, name=tag + "_act")[0]
    f = mm(z, wd, tk=1408, name=tag + "_down")
    x2 = rowwise(f_half_post, [x, f], [gpost], [F32], tm=TM, name=tag + "_post")[0]
    return x2, (x, h, ab, z, f)


def ffn_bwd(dx2, res, gpre, gpost, wgu, wd, tag):
    x, h, ab, z, f = res
    df, dgpost = rowwise_vjp(f_half_post, [x, f], [gpost], [dx2], [1], [0], [MXU_DTYPE], tm=TM, name=tag + "_post_b")
    dz = mm(df, wd, tb=True, tn=1408, name=tag + "_down_dx")
    dwd = mm(z, df, ta=True, tm=1408, name=tag + "_down_dw")
    da, db = rowwise_vjp(f_swiglu, [(ab, D_FF, 0), (ab, D_FF, 1)], [], [dz], [0, 1], [], [MXU_DTYPE, MXU_DTYPE], tm=TM,
                         name=tag + "_act_b")
    dab = jnp.concatenate([da, db], axis=1)
    dh = mm(dab, wgu, tb=True, tk=1408, name=tag + "_gu_dx")
    dwgu = mm(h, dab, ta=True, tn=1408, name=tag + "_gu_dw")
    dx, dgpre = rowwise_vjp(f_rms, [x], [gpre], [dh], [0], [0], [F32], [dx2], tm=TM, name=tag + "_pre_b")
    return dx, dgpre, dgpost, dwgu[:, :D_FF], dwgu[:, D_FF:], dwd


def mixer_fwd(x, p, w, tag):
    h = rowwise(f_rms, [x], [p["gpre"]], [MXU_DTYPE], tm=TM, name=tag + "_pre")[0]
    qkv = mm(h, w["in_sb"], out_dtype=MXU_DTYPE, tn=512, name=tag + "_in_sb")
    cv = mm(h, w["in_conv"], name=tag + "_in_conv")
    gl = mm(h, w["in_gla"], tn=512, name=tag + "_in_gla")
    lr = mm(h, w["in_lr"], out_dtype=MXU_DTYPE, name=tag + "_in_lr")
    gt = mm(h, w["in_gate"], name=tag + "_in_gate")
    sb = sb_attn_fwd(qkv, tq=TQ, name=tag + "_sb")
    u = rowwise(f_glu, [(cv, BRANCH, 0), (cv, BRANCH, 1)], [], [F32], tm=TM, name=tag + "_glu")[0]
    y = conv_fwd(u, p["conv_w"], p["conv_b"], tm=TM, name=tag + "_conv")
    cb = rowwise(f_conv_ln, [y], [p["ln_g"], p["ln_b"]], [MXU_DTYPE], tm=TM, name=tag + "_ln")[0]
    qs, kd, lam = rowwise(f_gla_pre, [(gl, 256, 0), (gl, 256, 1), lr], [p["wa"], p["ba"]], [MXU_DTYPE, MXU_DTYPE, F32],
                          tm=TM_GLA, name=tag + "_gla_pre")
    o, states = gla_scan_fwd(qs, kd, gl, lam, tm=TM_GLA, name=tag + "_gla_scan")
    gb = rowwise(f_gla_post, [o, (gl, BRANCH, 2)], [p["gn"]], [MXU_DTYPE], tm=TM, name=tag + "_gla_post")[0]
    branches = [sb, cb, gb]
    bd = [mm(branches[g], w["branch"][g], name=tag + f"_branch{g}") for g in range(3)]
    merged = rowwise(f_merge, [(gt, D_MODEL, 0), (gt, D_MODEL, 1), (gt, D_MODEL, 2)] + bd, [], [MXU_DTYPE], tm=TM,
                     name=tag + "_merge")[0]
    m = mm(merged, w["out"], name=tag + "_out")
    x2 = rowwise(f_post, [x, m], [p["gpost"]], [F32], tm=TM, name=tag + "_post")[0]
    return x2, (x, h, qkv, cv, gl, lr, gt, u, y, qs, kd, lam, o, states, branches, bd, merged, m)


def mixer_bwd(dx2, res, p, w, tag):
    x, h, qkv, cv, gl, lr, gt, u, y, qs, kd, lam, o, states, branches, bd, merged, m = res
    g = {}
    dm, g["gpost"] = rowwise_vjp(f_post, [x, m], [p["gpost"]], [dx2], [1], [0], [MXU_DTYPE], tm=TM, name=tag + "_post_b")
    dmerged = mm(dm, w["out"], tb=True, name=tag + "_out_dx")
    g["out"] = mm(merged, dm, ta=True, name=tag + "_out_dw")
    gts = [(gt, D_MODEL, 0), (gt, D_MODEL, 1), (gt, D_MODEL, 2)]
    dgate = rowwise_vjp(f_merge, gts + bd, [], [dmerged], [0, 1, 2, 3, 4, 5], [], [MXU_DTYPE] * 6, tm=TM,
                        name=tag + "_merge_b")
    dgt = jnp.concatenate(dgate[:3], axis=1)
    dbd = dgate[3:]
    g["branch"] = [mm(branches[k], dbd[k], ta=True, name=tag + f"_branch{k}_dw") for k in range(3)]
    dsb = mm(dbd[0], w["branch"][0], tb=True, out_dtype=MXU_DTYPE, name=tag + "_branch0_dx")
    dq, dk, dv = sb_attn_bwd(qkv, dsb, tq=TQ, name=tag + "_sb_b")
    dqkv = jnp.concatenate([dq, _mx(dk), _mx(dv)], axis=1)
    dcb = mm(dbd[1], w["branch"][1], tb=True, name=tag + "_branch1_dx")
    dy, g["ln_g"], g["ln_b"] = rowwise_vjp(f_conv_ln, [y], [p["ln_g"], p["ln_b"]], [dcb], [0], [0, 1], [F32], tm=TM,
                                           name=tag + "_ln_b")
    du, dwb = conv_bwd(dy, u, p["conv_w"], tm=TM, name=tag + "_conv_b")
    g["conv_w"], g["conv_b"] = dwb[:CONV_WIDTH], dwb[CONV_WIDTH:CONV_WIDTH + 1]
    dca, dcg = rowwise_vjp(f_glu, [(cv, BRANCH, 0), (cv, BRANCH, 1)], [], [du], [0, 1], [], [MXU_DTYPE, MXU_DTYPE], tm=TM,
                           name=tag + "_glu_b")
    dcv = jnp.concatenate([dca, dcg], axis=1)
    dgb = mm(dbd[2], w["branch"][2], tb=True, name=tag + "_branch2_dx")
    do, dr, g["gn"] = rowwise_vjp(f_gla_post, [o, (gl, BRANCH, 2)], [p["gn"]], [dgb], [0, 1], [0], [F32, MXU_DTYPE], tm=TM,
                                  name=tag + "_gla_post_b")
    dqs, dkd, dgv, dlam = gla_scan_bwd(do, qs, kd, gl, lam, states, tm=TM_GLA, name=tag + "_gla_scan_b")
    dgq, dgk, dlr, g["wa"], g["ba"] = rowwise_vjp(
        f_gla_pre, [(gl, 256, 0), (gl, 256, 1), lr], [p["wa"], p["ba"]], [dqs, dkd, dlam], [0, 1, 2], [0, 1],
        [MXU_DTYPE, MXU_DTYPE, MXU_DTYPE], tm=TM_GLA, name=tag + "_gla_pre_b")
    dgl = jnp.concatenate([dgq, dgk, _mx(dgv), dr], axis=1)
    secs = [("in_sb", dqkv), ("in_conv", dcv), ("in_gla", dgl), ("in_lr", dlr), ("in_gate", dgt)]
    dh = None
    for k, d in secs:
        dh = mm(d, w[k], dh, tb=True, name=tag + "_" + k + "_dx")
        g[k] = mm(h, d, ta=True, tn=1536, name=tag + "_" + k + "_dw")
    dx, g["gpre"] = rowwise_vjp(f_rms, [x], [p["gpre"]], [dh], [0], [0], [F32], [dx2], tm=TM, name=tag + "_pre_b")
    return dx, g


def kernel(x, norm_pre, norm_post, ffn1_w_gate, ffn1_w_up, ffn1_w_down, ffn2_w_gate, ffn2_w_up, ffn2_w_down, w_in, conv_w, conv_b, conv_ln_g, conv_ln_b, gla_w_alpha, gla_b_alpha, gla_norm_g, w_branch, w_out, loss_target, m_norm_pre, m_norm_post, m_ffn1_w_gate, m_ffn1_w_up, m_ffn1_w_down, m_ffn2_w_gate, m_ffn2_w_up, m_ffn2_w_down, m_w_in, m_conv_w, m_conv_b, m_conv_ln_g, m_conv_ln_b, m_gla_w_alpha, m_gla_b_alpha, m_gla_norm_g, m_w_branch, m_w_out, v_norm_pre, v_norm_post, v_ffn1_w_gate, v_ffn1_w_up, v_ffn1_w_down, v_ffn2_w_gate, v_ffn2_w_up, v_ffn2_w_down, v_w_in, v_conv_w, v_conv_b, v_conv_ln_g, v_conv_ln_b, v_gla_w_alpha, v_gla_b_alpha, v_gla_norm_g, v_w_branch, v_w_out):
    shards = dict(norm_pre=norm_pre, norm_post=norm_post, ffn1_w_gate=ffn1_w_gate, ffn1_w_up=ffn1_w_up,
                  ffn1_w_down=ffn1_w_down, ffn2_w_gate=ffn2_w_gate, ffn2_w_up=ffn2_w_up, ffn2_w_down=ffn2_w_down, w_in=w_in,
                  conv_w=conv_w, conv_b=conv_b, conv_ln_g=conv_ln_g, conv_ln_b=conv_ln_b, gla_w_alpha=gla_w_alpha,
                  gla_b_alpha=gla_b_alpha, gla_norm_g=gla_norm_g, w_branch=w_branch, w_out=w_out)
    mom_m = dict(zip(WEIGHTS, (m_norm_pre, m_norm_post, m_ffn1_w_gate, m_ffn1_w_up, m_ffn1_w_down, m_ffn2_w_gate,
                               m_ffn2_w_up, m_ffn2_w_down, m_w_in, m_conv_w, m_conv_b, m_conv_ln_g, m_conv_ln_b,
                               m_gla_w_alpha, m_gla_b_alpha, m_gla_norm_g, m_w_branch, m_w_out)))
    mom_v = dict(zip(WEIGHTS, (v_norm_pre, v_norm_post, v_ffn1_w_gate, v_ffn1_w_up, v_ffn1_w_down, v_ffn2_w_gate,
                               v_ffn2_w_up, v_ffn2_w_down, v_w_in, v_conv_w, v_conv_b, v_conv_ln_g, v_conv_ln_b,
                               v_gla_w_alpha, v_gla_b_alpha, v_gla_norm_g, v_w_branch, v_w_out)))
    depth = norm_pre.shape[0]
    full = gather_weights(shards, BIG, MXU_DTYPE, 32, "gather_big")
    full.update(gather_weights(shards, SMALL, F32, 16, "gather_small"))

    def layer_params(l):
        ffn = [dict(gpre=full["norm_pre"][l, k:k + 1], gpost=full["norm_post"][l, k:k + 1]) for k in (0, 2)]
        mix = dict(gpre=full["norm_pre"][l, 1:2], gpost=full["norm_post"][l, 1:2],
                   conv_w=jnp.pad(full["conv_w"][l], ((0, CONV_PAD - CONV_WIDTH), (0, 0))), conv_b=conv_b[l:l + 1],
                   ln_g=conv_ln_g[l:l + 1], ln_b=conv_ln_b[l:l + 1],
                   wa=jnp.pad(full["gla_w_alpha"][l], ((0, GLA_RANK_PAD - GLA_RANK), (0, 0))), ba=gla_b_alpha[l:l + 1],
                   gn=gla_norm_g[l:l + 1])
        return ffn, mix

    xs = x[0]
    saved = []
    for l in range(depth):
        w = layer_weights(full, l)
        ffn, mix = layer_params(l)
        xs, r1 = ffn_fwd(xs, ffn[0]["gpre"], ffn[0]["gpost"], w["ffn1_gu"], w["ffn1_d"], f"l{l}_ffn1")
        xs, r2 = mixer_fwd(xs, mix, w, f"l{l}_mix")
        xs, r3 = ffn_fwd(xs, ffn[1]["gpre"], ffn[1]["gpost"], w["ffn2_gu"], w["ffn2_d"], f"l{l}_ffn2")
        saved.append((w, ffn, mix, r1, r2, r3))
    dx, sq = loss_head(xs, loss_target[0], tm=TM, name="loss_head")
    loss = lax.psum(0.5 * jnp.sum(sq) / D_MODEL, ("x", "y", "c"))

    per_layer = []
    for l in reversed(range(depth)):
        w, ffn, mix, r1, r2, r3 = saved[l]
        g = {}
        dx, gpre2, gpost2, g["ffn2_w_gate"], g["ffn2_w_up"], g["ffn2_w_down"] = ffn_bwd(
            dx, r3, ffn[1]["gpre"], ffn[1]["gpost"], w["ffn2_gu"], w["ffn2_d"], f"l{l}_ffn2")
        dx, gm = mixer_bwd(dx, r2, mix, w, f"l{l}_mix")
        dx, gpre0, gpost0, g["ffn1_w_gate"], g["ffn1_w_up"], g["ffn1_w_down"] = ffn_bwd(
            dx, r1, ffn[0]["gpre"], ffn[0]["gpost"], w["ffn1_gu"], w["ffn1_d"], f"l{l}_ffn1")
        g["norm_pre"] = jnp.concatenate([gpre0, gm["gpre"], gpre2], axis=0)
        g["norm_post"] = jnp.concatenate([gpost0, gm["gpost"], gpost2], axis=0)
        g["w_in"] = jnp.concatenate([gm["in_sb"], gm["in_conv"], gm["in_gla"], gm["in_lr"][:, :GLA_RANK], gm["in_gate"]],
                                    axis=1)
        g["conv_w"], g["conv_b"] = gm["conv_w"], gm["conv_b"][0]
        g["conv_ln_g"], g["conv_ln_b"] = gm["ln_g"][0], gm["ln_b"][0]
        g["gla_w_alpha"], g["gla_b_alpha"], g["gla_norm_g"] = gm["wa"][:GLA_RANK], gm["ba"][0], gm["gn"][0]
        g["w_branch"] = jnp.stack(gm["branch"])
        g["w_out"] = gm["out"]
        per_layer.append(g)
    per_layer.reverse()
    grads = {k: jnp.stack([g[k] for g in per_layer]) for k in WEIGHTS}

    grad_w = scatter_grads(grads, shards)
    delta, new_m, new_v = {}, {}, {}
    for k in WEIGHTS:
        delta[k], new_m[k], new_v[k] = adamw(shards[k], grad_w[k], mom_m[k], mom_v[k], name="adamw_" + k)
    return (loss, dx[None], *[grad_w[k] for k in WEIGHTS], *[delta[k] for k in WEIGHTS], *[new_m[k] for k in WEIGHTS],
            *[new_v[k] for k in WEIGHTS])
```

```python
import functools

import jax
import jax.numpy as jnp
from jax import lax
from jax.experimental import pallas as pl
from jax.experimental.pallas import tpu as pltpu

F32 = jnp.float32
MXU_DTYPE = jnp.bfloat16
HIGHEST = lax.Precision.HIGHEST
MESH = pl.DeviceIdType.MESH

NORM_EPS = 1e-6
D_MODEL = 1024
D_FF = 2816
BRANCH = 512
CHUNK = 64
CONV_WIDTH = 31
CONV_PAD = 32
GLA_RANK = 16
GLA_RANK_PAD = 128
GLA_TAU = 16.0
SB_SCALE = 0.125
SB_CUTOFF = 60.0
GLA_SCALE = 0.125
PACK_COLS = 1024
VMEM_LIMIT = 56 * 1024 * 1024

ADAM_LR, ADAM_B1, ADAM_B2, ADAM_EPS, ADAM_WD, ADAM_STEP = 0.001, 0.9, 0.999, 1e-08, 0.01, 10

NT = (((1,), (1,)), ((), ()))
TN = (((0,), (0,)), ((), ()))
NN = (((1,), (0,)), ((), ()))


def _params(sem=None, vmem=None):
    return pltpu.CompilerParams(dimension_semantics=sem, vmem_limit_bytes=vmem)


def _mx(v):
    return v.astype(MXU_DTYPE)


def _mx_round(v):
    return v.astype(MXU_DTYPE).astype(F32)


def _fit(dim, want):
    if dim <= want:
        return dim
    for d in range(want - want % 128, 0, -128):
        if dim % d == 0:
            return d
    raise ValueError((dim, want))


def mm(a, b, c=None, *, ta=False, tb=False, out_dtype=F32, tm=1024, tn=1024, tk=1024, name):
    K, M = a.shape if ta else a.shape[::-1]
    N = b.shape[0] if tb else b.shape[1]
    assert (b.shape[1] if tb else b.shape[0]) == K, (a.shape, b.shape, ta, tb)
    tm, tn, tk = _fit(M, tm), _fit(N, tn), _fit(K, tk)
    nk = K // tk
    dn = (((0 if ta else 1,), (1 if tb else 0,)), ((), ()))

    def body(*refs):
        if c is None:
            a_ref, b_ref, o_ref, acc = refs
            c_ref = None
        else:
            a_ref, b_ref, c_ref, o_ref, acc = refs
        k = pl.program_id(2)
        p = lax.dot_general(_mx(a_ref[...]), _mx(b_ref[...]), dn, preferred_element_type=F32)

        @pl.when(k == 0)
        def _():
            acc[...] = p

        @pl.when(k > 0)
        def _():
            acc[...] += p

        @pl.when(k == nk - 1)
        def _():
            r = acc[...]
            if c_ref is not None:
                r = r + c_ref[...].astype(F32)
            o_ref[...] = r.astype(o_ref.dtype)

    a_spec = pl.BlockSpec((tk, tm), lambda i, j, k: (k, i)) if ta else pl.BlockSpec((tm, tk), lambda i, j, k: (i, k))
    b_spec = pl.BlockSpec((tn, tk), lambda i, j, k: (j, k)) if tb else pl.BlockSpec((tk, tn), lambda i, j, k: (k, j))
    o_spec = pl.BlockSpec((tm, tn), lambda i, j, k: (i, j))
    ins, in_specs = [a, b], [a_spec, b_spec]
    if c is not None:
        ins.append(c)
        in_specs.append(o_spec)
    return pl.pallas_call(
        body, name=name, grid=(M // tm, N // tn, nk), in_specs=in_specs, out_specs=o_spec,
        out_shape=jax.ShapeDtypeStruct((M, N), out_dtype), scratch_shapes=[pltpu.VMEM((tm, tn), F32)],
        compiler_params=_params(("parallel", "parallel", "arbitrary"), VMEM_LIMIT),
    )(*ins)


def _row_arg(arg):
    if isinstance(arg, tuple):
        return arg
    return arg, arg.shape[1], 0


def _row_specs(rows, n):
    arrs, specs, avals = [], [], []
    for arg in rows:
        arr, width, cb = _row_arg(arg)
        rb = arr.shape[0] // n
        arrs.append(arr)
        specs.append(pl.BlockSpec((rb, width), lambda i, cb=cb: (i, cb)))
        avals.append(jax.ShapeDtypeStruct((rb, width), F32))
    return arrs, specs, avals


def _par_specs(pars):
    specs = [pl.BlockSpec(p.shape, lambda i, nd=p.ndim: (0,) * nd) for p in pars]
    avals = [jax.ShapeDtypeStruct(p.shape, F32) for p in pars]
    return specs, avals


def rowwise(f, rows, pars, out_dtypes, *, tm, name):
    n = _row_arg(rows[0])[0].shape[0] // tm
    arrs, rspecs, ravals = _row_specs(rows, n)
    pspecs, pavals = _par_specs(pars)
    oavals = jax.eval_shape(f, *ravals, *pavals)
    nin = len(arrs) + len(pars)

    def body(*refs):
        outs = f(*[r[...].astype(F32) for r in refs[:nin]])
        for o_ref, o in zip(refs[nin:], outs):
            o_ref[...] = o.astype(o_ref.dtype)

    return pl.pallas_call(
        body, name=name, grid=(n,), in_specs=rspecs + pspecs,
        out_specs=[pl.BlockSpec(o.shape, lambda i: (i, 0)) for o in oavals],
        out_shape=[jax.ShapeDtypeStruct((n * o.shape[0], o.shape[1]), dt) for o, dt in zip(oavals, out_dtypes)],
        compiler_params=_params(("parallel",), VMEM_LIMIT),
    )(*arrs, *pars)


def rowwise_vjp(f, rows, pars, cots, row_grad, par_grad, d_dtypes, adds=None, *, tm, name):
    n = _row_arg(rows[0])[0].shape[0] // tm
    arrs, rspecs, ravals = _row_specs(rows, n)
    pspecs, pavals = _par_specs(pars)
    carrs, cspecs, _ = _row_specs(cots, n)
    adds = adds or [None] * len(row_grad)
    add_arrs = [a for a in adds if a is not None]
    _, aspecs, _ = _row_specs(add_arrs, n)
    nr, npar, nc, na = len(arrs), len(pars), len(carrs), len(add_arrs)
    diff = list(row_grad) + [nr + j for j in par_grad]
    ngr = len(row_grad)

    def body(*refs):
        ins = [r[...].astype(F32) for r in refs[:nr + npar]]
        cs = tuple(r[...].astype(F32) for r in refs[nr + npar:nr + npar + nc])
        add_refs = list(refs[nr + npar + nc:nr + npar + nc + na])
        outs = refs[nr + npar + nc + na:]

        def g(*d):
            full = list(ins)
            for idx, val in zip(diff, d):
                full[idx] = val
            return f(*full)

        _, pullback = jax.vjp(g, *[ins[idx] for idx in diff])
        ds = pullback(cs)
        for k in range(ngr):
            d = ds[k]
            if adds[k] is not None:
                d = d + add_refs.pop(0)[...].astype(F32)
            outs[k][...] = d.astype(outs[k].dtype)
        i = pl.program_id(0)
        for k in range(ngr, len(diff)):
            @pl.when(i == 0)
            def _(k=k):
                outs[k][...] = ds[k]

            @pl.when(i > 0)
            def _(k=k):
                outs[k][...] += ds[k]

    out_specs = [pl.BlockSpec(ravals[i].shape, lambda i: (i, 0)) for i in row_grad] + [pspecs[j] for j in par_grad]
    out_shape = [jax.ShapeDtypeStruct((arrs[i].shape[0], ravals[i].shape[1]), dt) for i, dt in zip(row_grad, d_dtypes)]
    out_shape += [jax.ShapeDtypeStruct(pars[j].shape, F32) for j in par_grad]
    return pl.pallas_call(
        body, name=name, grid=(n,), in_specs=rspecs + pspecs + cspecs + aspecs, out_specs=out_specs, out_shape=out_shape,
        compiler_params=_params(("arbitrary",), VMEM_LIMIT),
    )(*arrs, *pars, *carrs, *add_arrs)


def _logsig(x):
    return jnp.minimum(x, 0.0) - jnp.log(1.0 + jnp.exp(-jnp.abs(x)))


def _sigmoid(x):
    return 1.0 / (1.0 + jnp.exp(-x))


def _silu(x):
    return x * _sigmoid(x)


def _rms(x, g):
    return x * lax.rsqrt(jnp.mean(x * x, axis=-1, keepdims=True) + NORM_EPS) * g


def f_rms(x, g):
    return (_rms(x, g),)


def f_swiglu(a, b):
    return (_silu(a) * b,)


def f_half_post(x, f, g):
    return (x + 0.5 * _rms(f, g),)


def f_post(x, m, g):
    return (x + _rms(m, g),)


def f_glu(a, g):
    return (a * _sigmoid(g),)


def f_conv_ln(y, lg, lb):
    mu = jnp.mean(y, axis=-1, keepdims=True)
    var = jnp.mean(jnp.square(y - mu), axis=-1, keepdims=True)
    return (_silu((y - mu) * lax.rsqrt(var + NORM_EPS) * lg + lb),)


def f_merge(g0, g1, g2, b0, b1, b2):
    return (_sigmoid(g0) * b0 + _sigmoid(g1) * b1 + _sigmoid(g2) * b2,)


def f_gla_post(o, r, g):
    w = o.shape[1]
    hv = w // 4
    i = lax.broadcasted_iota(jnp.int32, (w, w), 0) // hv
    j = lax.broadcasted_iota(jnp.int32, (w, w), 1) // hv
    avg = jnp.where(i == j, 1.0 / hv, 0.0).astype(F32)
    ms = jnp.dot(o * o, avg, precision=HIGHEST, preferred_element_type=F32)
    return (o * lax.rsqrt(ms + NORM_EPS) * g * _silu(r),)


def f_gla_pre(q, k, lr, wa, ba):
    tm = q.shape[0]
    pre = jnp.dot(_mx_round(lr), _mx_round(wa), precision=HIGHEST, preferred_element_type=F32) + ba
    la = _logsig(pre) / GLA_TAU
    i = lax.broadcasted_iota(jnp.int32, (tm, tm), 0)
    j = lax.broadcasted_iota(jnp.int32, (tm, tm), 1)
    later = jnp.where((i // CHUNK == j // CHUNK) & (j > i), 1.0, 0.0).astype(F32)
    ci = lax.broadcasted_iota(jnp.int32, (tm // CHUNK, tm), 0)
    cj = lax.broadcasted_iota(jnp.int32, (tm // CHUNK, tm), 1) // CHUNK
    chunk_sum = jnp.where(ci == cj, 1.0, 0.0).astype(F32)
    to_end = jnp.dot(later, la, precision=HIGHEST, preferred_element_type=F32)
    lam = jnp.exp(jnp.dot(chunk_sum, la, precision=HIGHEST, preferred_element_type=F32))
    return q * GLA_SCALE, k * jnp.exp(to_end), lam


def _split_dot(x, u):
    hi = _mx(x)
    lo = _mx(x - hi.astype(F32))
    return jnp.dot(hi, u, preferred_element_type=F32) + jnp.dot(lo, u, preferred_element_type=F32)


def _tri(tq, tk):
    row = lax.broadcasted_iota(jnp.int32, (tq, tk), 0)
    col = lax.broadcasted_iota(jnp.int32, (tq, tk), 1)
    return row, col


def _sb_tile(qh, kh, valid, suf, run):
    z = lax.dot_general(qh, kh, NT, preferred_element_type=F32) * SB_SCALE
    lp = jnp.minimum(z, 0.0) - jnp.log(1.0 + jnp.exp(-jnp.abs(z)))
    lk = jnp.where(valid, lp - z, 0.0)
    inc = _split_dot(lk, suf)
    a = jnp.where(valid, jnp.exp(lp + (inc - lk + run)), 0.0)
    return lp, a, run + inc[:, 0:1]


def _sticks_left(run0, run1):
    return (jnp.maximum(jnp.max(run0), jnp.max(run1)) > -SB_CUTOFF).astype(jnp.int32)


def sb_attn_fwd(qkv, *, tq, name):
    S = qkv.shape[0]
    tk = tq
    pairs = BRANCH // 128

    def body(q_ref, k_ref, v_ref, o_ref):
        i = pl.program_id(1)
        row, col = _tri(tq, tk)
        suf = _mx(row >= col)
        q = q_ref[...]

        def step(state):
            j, _, carry = state
            ks = pl.multiple_of((i - j) * tk, tk)
            kb = k_ref[pl.ds(ks, tk), :]
            vb = v_ref[pl.ds(ks, tk), :]
            valid = (col - row) < j * tq
            new = []
            for h in range(2):
                acc, run = carry[h]
                sl = slice(64 * h, 64 * h + 64)
                _, a, run = _sb_tile(q[:, sl], kb[:, sl], valid, suf, run)
                acc = acc + jnp.dot(_mx(a), vb[:, sl], preferred_element_type=F32)
                new.append((acc, run))
            return j + 1, _sticks_left(new[0][1], new[1][1]), tuple(new)

        zero = (jnp.zeros((tq, 64), F32), jnp.zeros((tq, 1), F32))
        _, _, res = lax.while_loop(lambda s: (s[0] <= i) & (s[1] > 0), step, (jnp.int32(0), jnp.int32(1), (zero, zero)))
        o_ref[...] = jnp.concatenate([res[0][0], res[1][0]], axis=1).astype(o_ref.dtype)

    return pl.pallas_call(
        body, name=name, grid=(pairs, S // tq),
        in_specs=[pl.BlockSpec((tq, 128), lambda p, i: (i, p)),
                  pl.BlockSpec((S, 128), lambda p, i: (0, pairs + p)),
                  pl.BlockSpec((S, 128), lambda p, i: (0, 2 * pairs + p))],
        out_specs=pl.BlockSpec((tq, 128), lambda p, i: (i, p)),
        out_shape=jax.ShapeDtypeStruct((S, BRANCH), MXU_DTYPE),
        compiler_params=_params(("parallel", "parallel"), VMEM_LIMIT),
    )(qkv, qkv, qkv)


def sb_attn_bwd(qkv, do, *, tq, name):
    S = qkv.shape[0]
    tk = tq
    nq = S // tq
    pairs = BRANCH // 128

    def body(q_ref, k_ref, v_ref, do_ref, dq_ref, dk_ref, dv_ref, g_sc, b_sc):
        i = pl.program_id(1)

        @pl.when(i == 0)
        def _():
            dk_ref[...] = jnp.zeros_like(dk_ref)
            dv_ref[...] = jnp.zeros_like(dv_ref)

        row, col = _tri(tq, tk)
        suf = _mx(row >= col)
        pre = _mx(row <= col)
        q = q_ref[...]
        dout = do_ref[...]

        def sweep1(state):
            j, _, carry = state
            kblk = i - j
            ks = pl.multiple_of(kblk * tk, tk)
            kb = k_ref[pl.ds(ks, tk), :]
            vb = v_ref[pl.ds(ks, tk), :]
            valid = (col - row) < j * tq
            runs, dvs = [], []
            for h in range(2):
                sl = slice(64 * h, 64 * h + 64)
                lp, a, run = _sb_tile(q[:, sl], kb[:, sl], valid, suf, carry[h])
                da = lax.dot_general(dout[:, sl], vb[:, sl], NT, preferred_element_type=F32)
                g_sc[h, kblk] = (a * da).astype(g_sc.dtype)
                b_sc[h, kblk] = jnp.where(valid, jnp.exp(lp), 0.0).astype(b_sc.dtype)
                dvs.append(lax.dot_general(_mx(a), dout[:, sl], TN, preferred_element_type=F32))
                runs.append(run)
            dv_ref[pl.ds(ks, tk), :] += jnp.concatenate(dvs, axis=1)
            return j + 1, _sticks_left(runs[0], runs[1]), tuple(runs)

        start = (jnp.int32(0), jnp.int32(1), (jnp.zeros((tq, 1), F32), jnp.zeros((tq, 1), F32)))
        tiles, _, _ = lax.while_loop(lambda s: (s[0] <= i) & (s[1] > 0), sweep1, start)

        def sweep2(kblk, carry):
            ks = pl.multiple_of(kblk * tk, tk)
            kb = k_ref[pl.ds(ks, tk), :]
            new, dks = [], []
            for h in range(2):
                dq, run = carry[h]
                sl = slice(64 * h, 64 * h + 64)
                g = g_sc[h, kblk]
                beta = b_sc[h, kblk].astype(F32)
                inc = jnp.dot(g, pre, preferred_element_type=F32)
                g = g.astype(F32)
                dz = _mx((g - beta * (inc + run)) * SB_SCALE)
                dq = dq + jnp.dot(dz, kb[:, sl], preferred_element_type=F32)
                dks.append(lax.dot_general(dz, q[:, sl], TN, preferred_element_type=F32))
                new.append((dq, run + inc[:, tk - 1:tk]))
            dk_ref[pl.ds(ks, tk), :] += jnp.concatenate(dks, axis=1)
            return tuple(new)

        zero = (jnp.zeros((tq, 64), F32), jnp.zeros((tq, 1), F32))
        res = lax.fori_loop(i + 1 - tiles, i + 1, sweep2, (zero, zero))
        dq_ref[...] = jnp.concatenate([res[0][0], res[1][0]], axis=1).astype(dq_ref.dtype)

    return pl.pallas_call(
        body, name=name, grid=(pairs, nq),
        in_specs=[pl.BlockSpec((tq, 128), lambda p, i: (i, p)),
                  pl.BlockSpec((S, 128), lambda p, i: (0, pairs + p)),
                  pl.BlockSpec((S, 128), lambda p, i: (0, 2 * pairs + p)),
                  pl.BlockSpec((tq, 128), lambda p, i: (i, p))],
        out_specs=[pl.BlockSpec((tq, 128), lambda p, i: (i, p)),
                   pl.BlockSpec((S, 128), lambda p, i: (0, p)),
                   pl.BlockSpec((S, 128), lambda p, i: (0, p))],
        out_shape=[jax.ShapeDtypeStruct((S, BRANCH), MXU_DTYPE), jax.ShapeDtypeStruct((S, BRANCH), F32),
                   jax.ShapeDtypeStruct((S, BRANCH), F32)],
        scratch_shapes=[pltpu.VMEM((2, nq, tq, tk), MXU_DTYPE), pltpu.VMEM((2, nq, tq, tk), MXU_DTYPE)],
        compiler_params=_params(("parallel", "arbitrary"), VMEM_LIMIT),
    )(qkv, qkv, qkv, do)


def conv_fwd(u, w, b, *, tm, name):
    S, C = u.shape
    hb = tm // CONV_PAD

    def body(u_ref, halo_ref, w_ref, b_ref, y_ref, buf):
        i = pl.program_id(0)
        buf[pl.ds(CONV_PAD, tm), :] = u_ref[...]
        buf[pl.ds(0, CONV_PAD), :] = jnp.where(i > 0, halo_ref[...], 0.0)
        acc = jnp.broadcast_to(b_ref[...], (tm, 128))
        for j in range(CONV_WIDTH):
            acc = acc + buf[pl.ds(CONV_PAD - (CONV_WIDTH - 1) + j, tm), :] * w_ref[pl.ds(j, 1), :]
        y_ref[...] = acc

    return pl.pallas_call(
        body, name=name, grid=(S // tm, C // 128),
        in_specs=[pl.BlockSpec((tm, 128), lambda i, c: (i, c)),
                  pl.BlockSpec((CONV_PAD, 128), lambda i, c: (jnp.maximum(i * hb - 1, 0), c)),
                  pl.BlockSpec((CONV_PAD, 128), lambda i, c: (0, c)),
                  pl.BlockSpec((1, 128), lambda i, c: (0, c))],
        out_specs=pl.BlockSpec((tm, 128), lambda i, c: (i, c)),
        out_shape=jax.ShapeDtypeStruct((S, C), F32),
        scratch_shapes=[pltpu.VMEM((tm + CONV_PAD, 128), F32)],
        compiler_params=_params(("parallel", "parallel")),
    )(u, u, w, b)


def conv_bwd(dy, u, w, *, tm, name):
    S, C = u.shape
    hb = tm // CONV_PAD
    n = S // tm

    def body(dy_ref, dyn_ref, u_ref, up_ref, w_ref, du_ref, dw_ref, bufy, bufu):
        i = pl.program_id(1)
        dyv = dy_ref[...]
        bufy[pl.ds(0, tm), :] = dyv
        bufy[pl.ds(tm, CONV_PAD), :] = jnp.where(i < n - 1, dyn_ref[...], 0.0)
        bufu[pl.ds(CONV_PAD, tm), :] = u_ref[...]
        bufu[pl.ds(0, CONV_PAD), :] = jnp.where(i > 0, up_ref[...], 0.0)

        @pl.when(i == 0)
        def _():
            dw_ref[...] = jnp.zeros_like(dw_ref)

        acc = jnp.zeros((tm, 128), F32)
        for j in range(CONV_WIDTH):
            acc = acc + bufy[pl.ds(CONV_WIDTH - 1 - j, tm), :] * w_ref[pl.ds(j, 1), :]
            shifted = bufu[pl.ds(CONV_PAD - (CONV_WIDTH - 1) + j, tm), :]
            dw_ref[pl.ds(j, 1), :] += jnp.sum(dyv * shifted, axis=0, keepdims=True)
        du_ref[...] = acc
        dw_ref[pl.ds(CONV_WIDTH, 1), :] += jnp.sum(dyv, axis=0, keepdims=True)

    return pl.pallas_call(
        body, name=name, grid=(C // 128, n),
        in_specs=[pl.BlockSpec((tm, 128), lambda c, i: (i, c)),
                  pl.BlockSpec((CONV_PAD, 128), lambda c, i: (jnp.minimum((i + 1) * hb, n * hb - 1), c)),
                  pl.BlockSpec((tm, 128), lambda c, i: (i, c)),
                  pl.BlockSpec((CONV_PAD, 128), lambda c, i: (jnp.maximum(i * hb - 1, 0), c)),
                  pl.BlockSpec((CONV_PAD, 128), lambda c, i: (0, c))],
        out_specs=[pl.BlockSpec((tm, 128), lambda c, i: (i, c)), pl.BlockSpec((CONV_PAD, 128), lambda c, i: (0, c))],
        out_shape=[jax.ShapeDtypeStruct((S, C), F32), jax.ShapeDtypeStruct((CONV_PAD, C), F32)],
        scratch_shapes=[pltpu.VMEM((tm + CONV_PAD, 128), F32), pltpu.VMEM((tm + CONV_PAD, 128), F32)],
        compiler_params=_params(("parallel", "arbitrary")),
    )(dy, dy, u, u, w)


GLA_HEADS, GLA_DK, GLA_DV = 4, 64, 128


def gla_scan_fwd(qs, kd, gl, lam, *, tm, name):
    S = qs.shape[0]
    nc = tm // CHUNK

    def body(qs_ref, kd_ref, v_ref, lam_ref, o_ref, st_ref, state):
        @pl.when(pl.program_id(0) == 0)
        def _():
            state[...] = jnp.zeros_like(state)

        for c in range(nc):
            rows = pl.ds(c * CHUNK, CHUNK)
            q, k, v = _mx(qs_ref[rows, :]), _mx(kd_ref[rows, :]), _mx(v_ref[rows, :])
            upd = [lax.dot_general(v[:, h * GLA_DV:(h + 1) * GLA_DV], k[:, h * GLA_DK:(h + 1) * GLA_DK], TN,
                                   preferred_element_type=F32) for h in range(GLA_HEADS)]
            st = state[...] * lam_ref[pl.ds(c, 1), :] + jnp.concatenate(upd, axis=1)
            state[...] = st
            st_ref[c] = st
            stm = _mx(st)
            o = [lax.dot_general(q[:, h * GLA_DK:(h + 1) * GLA_DK], stm[:, h * GLA_DK:(h + 1) * GLA_DK], NT,
                                 preferred_element_type=F32) for h in range(GLA_HEADS)]
            o_ref[rows, :] = jnp.concatenate(o, axis=1)

    dk_all = GLA_HEADS * GLA_DK
    dv_all = GLA_HEADS * GLA_DV
    return pl.pallas_call(
        body, name=name, grid=(S // tm,),
        in_specs=[pl.BlockSpec((tm, dk_all), lambda i: (i, 0)), pl.BlockSpec((tm, dk_all), lambda i: (i, 0)),
                  pl.BlockSpec((tm, dv_all), lambda i: (i, 1)), pl.BlockSpec((nc, dk_all), lambda i: (i, 0))],
        out_specs=[pl.BlockSpec((tm, dv_all), lambda i: (i, 0)), pl.BlockSpec((nc, GLA_DV, dk_all), lambda i: (i, 0, 0))],
        out_shape=[jax.ShapeDtypeStruct((S, dv_all), F32), jax.ShapeDtypeStruct((S // CHUNK, GLA_DV, dk_all), F32)],
        scratch_shapes=[pltpu.VMEM((GLA_DV, dk_all), F32)],
        compiler_params=_params(("arbitrary",)),
    )(qs, kd, gl, lam)


def gla_scan_bwd(do, qs, kd, gl, lam, states, *, tm, name):
    S = qs.shape[0]
    nc = tm // CHUNK
    n = S // tm
    dk_all = GLA_HEADS * GLA_DK
    dv_all = GLA_HEADS * GLA_DV

    def body(do_ref, qs_ref, kd_ref, v_ref, lam_ref, st_ref, prev_ref, dqs_ref, dkd_ref, dv_ref, dlam_ref, carry):
        i = pl.program_id(0)

        @pl.when(i == 0)
        def _():
            carry[...] = jnp.zeros_like(carry)

        for c in reversed(range(nc)):
            rows = pl.ds(c * CHUNK, CHUNK)
            q, k, v, d = _mx(qs_ref[rows, :]), _mx(kd_ref[rows, :]), _mx(v_ref[rows, :]), _mx(do_ref[rows, :])
            st = _mx(st_ref[c])
            before = st_ref[c - 1] if c > 0 else jnp.where(i < n - 1, prev_ref[0], 0.0)
            outer = [lax.dot_general(d[:, h * GLA_DV:(h + 1) * GLA_DV], q[:, h * GLA_DK:(h + 1) * GLA_DK], TN,
                                     preferred_element_type=F32) for h in range(GLA_HEADS)]
            dst = carry[...] + jnp.concatenate(outer, axis=1)
            dstm = _mx(dst)
            dq, dkk, dvv = [], [], []
            for h in range(GLA_HEADS):
                ksl = slice(h * GLA_DK, (h + 1) * GLA_DK)
                vsl = slice(h * GLA_DV, (h + 1) * GLA_DV)
                dq.append(jnp.dot(d[:, vsl], st[:, ksl], preferred_element_type=F32))
                dkk.append(jnp.dot(v[:, vsl], dstm[:, ksl], preferred_element_type=F32))
                dvv.append(lax.dot_general(k[:, ksl], dstm[:, ksl], NT, preferred_element_type=F32))
            dqs_ref[rows, :] = jnp.concatenate(dq, axis=1)
            dkd_ref[rows, :] = jnp.concatenate(dkk, axis=1)
            dv_ref[rows, :] = jnp.concatenate(dvv, axis=1)
            dlam_ref[pl.ds(c, 1), :] = jnp.sum(dst * before, axis=0, keepdims=True)
            carry[...] = dst * lam_ref[pl.ds(c, 1), :]

    rev = lambda i: n - 1 - i
    return pl.pallas_call(
        body, name=name, grid=(n,),
        in_specs=[pl.BlockSpec((tm, dv_all), lambda i: (rev(i), 0)), pl.BlockSpec((tm, dk_all), lambda i: (rev(i), 0)),
                  pl.BlockSpec((tm, dk_all), lambda i: (rev(i), 0)), pl.BlockSpec((tm, dv_all), lambda i: (rev(i), 1)),
                  pl.BlockSpec((nc, dk_all), lambda i: (rev(i), 0)),
                  pl.BlockSpec((nc, GLA_DV, dk_all), lambda i: (rev(i), 0, 0)),
                  pl.BlockSpec((1, GLA_DV, dk_all), lambda i: (jnp.maximum(rev(i) * nc - 1, 0), 0, 0))],
        out_specs=[pl.BlockSpec((tm, dk_all), lambda i: (rev(i), 0)), pl.BlockSpec((tm, dk_all), lambda i: (rev(i), 0)),
                   pl.BlockSpec((tm, dv_all), lambda i: (rev(i), 0)), pl.BlockSpec((nc, dk_all), lambda i: (rev(i), 0))],
        out_shape=[jax.ShapeDtypeStruct((S, dk_all), F32), jax.ShapeDtypeStruct((S, dk_all), F32),
                   jax.ShapeDtypeStruct((S, dv_all), F32), jax.ShapeDtypeStruct((S // CHUNK, dk_all), F32)],
        scratch_shapes=[pltpu.VMEM((GLA_DV, dk_all), F32)],
        compiler_params=_params(("arbitrary",)),
    )(do, qs, kd, gl, lam, states, states)


def loss_head(y, target, *, tm, name):
    S, D = y.shape

    def body(y_ref, t_ref, dy_ref, sq_ref):
        err = y_ref[...] - t_ref[...]
        dy_ref[...] = err * (1.0 / D)
        part = jnp.sum(err * err, axis=0, keepdims=True)

        @pl.when(pl.program_id(0) == 0)
        def _():
            sq_ref[...] = part

        @pl.when(pl.program_id(0) > 0)
        def _():
            sq_ref[...] += part

    return pl.pallas_call(
        body, name=name, grid=(S // tm,),
        in_specs=[pl.BlockSpec((tm, D), lambda i: (i, 0)), pl.BlockSpec((tm, D), lambda i: (i, 0))],
        out_specs=[pl.BlockSpec((tm, D), lambda i: (i, 0)), pl.BlockSpec((1, D), lambda i: (0, 0))],
        out_shape=[jax.ShapeDtypeStruct((S, D), F32), jax.ShapeDtypeStruct((1, D), F32)],
        compiler_params=_params(("arbitrary",)),
    )(y, target)


def f_adamw(w, g, m, v):
    m = ADAM_B1 * m + (1.0 - ADAM_B1) * g
    v = ADAM_B2 * v + (1.0 - ADAM_B2) * jnp.square(g)
    m_hat = m / (1.0 - ADAM_B1 ** ADAM_STEP)
    v_hat = v / (1.0 - ADAM_B2 ** ADAM_STEP)
    return -ADAM_LR * (m_hat / (jnp.sqrt(v_hat) + ADAM_EPS) + ADAM_WD * w), m, v


def adamw(w, g, m, v, *, name):
    shape = w.shape
    cols = shape[-1]
    rows = w.size // cols
    tm = rows
    while tm % 16 == 0 and tm * cols * 4 > (1 << 20):
        tm //= 2
    flat = [t.reshape(rows, cols) for t in (w, g, m, v)]
    outs = rowwise(f_adamw, flat, [], [F32, F32, F32], tm=tm, name=name)
    return [o.reshape(shape) for o in outs]


def _place():
    x, y, c = lax.axis_index("x"), lax.axis_index("y"), lax.axis_index("c")
    return x, y, c, [(1 - x, y), (x, 1 - y), (1 - x, 1 - y)]


def _any():
    return pl.BlockSpec(memory_space=pl.ANY)


DMA_CHUNKS = 8


def _chunks(rows):
    n = DMA_CHUNKS
    while n > 1 and rows % (n * 16):
        n //= 2
    return [(k * (rows // n), rows // n) for k in range(n)]


def _start_chunked(make, rows):
    for off, size in _chunks(rows):
        make(off, size).start()


def all_gather_chips(v, *, name):
    R, C = v.shape
    H = R // 2

    def body(v_ref, o_ref, send, recv, fsend, frecv, local):
        x, y, c, chips = _place()
        me = 2 * x + y
        ids = [2 * px + py for px, py in chips]
        mine = pl.ds(pl.multiple_of(c * H, 8), H)
        other = pl.ds(pl.multiple_of((1 - c) * H, 8), H)

        def own(rows):
            return pltpu.make_async_copy(v_ref.at[rows], o_ref.at[me, rows], local)

        def cross(j, src_chip, rows):
            return pltpu.make_async_remote_copy(
                src_ref=v_ref.at[rows], dst_ref=o_ref.at[src_chip, rows], send_sem=send.at[j], recv_sem=recv.at[j],
                device_id=(*chips[j], c), device_id_type=MESH)

        def handed(j, rows):
            return pltpu.make_async_remote_copy(
                src_ref=o_ref.at[ids[j], rows], dst_ref=o_ref.at[ids[j], rows], send_sem=fsend.at[j],
                recv_sem=frecv.at[j], device_id=(x, y, 1 - c), device_id_type=MESH)

        def my_rows(off, size):
            return pl.ds(pl.multiple_of(c * H + off, 8), size)

        _start_chunked(lambda off, size: own(pl.ds(off, size)), R)
        for j in range(3):
            _start_chunked(lambda off, size, j=j: cross(j, me, my_rows(off, size)), H)
        for j in range(3):
            cross(j, ids[j], mine).wait_recv()
            _start_chunked(lambda off, size, j=j: handed(j, my_rows(off, size)), H)
        for j in range(3):
            handed(j, other).wait_recv()
        for j in range(3):
            cross(j, me, mine).wait_send()
            handed(j, mine).wait_send()
        own(pl.ds(0, R)).wait()

    return pl.pallas_call(
        body, name=name, in_specs=[_any()], out_specs=_any(), out_shape=jax.ShapeDtypeStruct((4, R, C), v.dtype),
        scratch_shapes=[pltpu.SemaphoreType.DMA((3,)), pltpu.SemaphoreType.DMA((3,)), pltpu.SemaphoreType.DMA((3,)),
                        pltpu.SemaphoreType.DMA((3,)), pltpu.SemaphoreType.DMA],
    )(v)


def swap_halves(g, *, name):
    n, R, C = g.shape
    H = R // 2

    def body(g_ref, theirs_ref, send, recv):
        x, y, c, _ = _place()

        def give(q, off, size):
            return pltpu.make_async_remote_copy(
                src_ref=g_ref.at[q, pl.ds(pl.multiple_of((1 - c) * H + off, 8), size)],
                dst_ref=theirs_ref.at[q, pl.ds(off, size)], send_sem=send.at[q], recv_sem=recv.at[q],
                device_id=(x, y, 1 - c), device_id_type=MESH)

        for q in range(n):
            _start_chunked(functools.partial(give, q), H)
        for q in range(n):
            give(q, 0, H).wait()

    return pl.pallas_call(
        body, name=name, in_specs=[_any()], out_specs=_any(), out_shape=jax.ShapeDtypeStruct((n, H, C), g.dtype),
        scratch_shapes=[pltpu.SemaphoreType.DMA((n,)), pltpu.SemaphoreType.DMA((n,))],
    )(g)


def exchange_pieces(p, *, name):
    H = p.shape[1]

    def body(p_ref, b_ref, send, recv, local):
        x, y, c, chips = _place()
        me = 2 * x + y
        ids = [2 * px + py for px, py in chips]

        def own(off, size):
            return pltpu.make_async_copy(p_ref.at[me, pl.ds(off, size)], b_ref.at[me, pl.ds(off, size)], local)

        def cross(j, piece, slot, off, size):
            return pltpu.make_async_remote_copy(
                src_ref=p_ref.at[piece, pl.ds(off, size)], dst_ref=b_ref.at[slot, pl.ds(off, size)], send_sem=send.at[j],
                recv_sem=recv.at[j], device_id=(*chips[j], c), device_id_type=MESH)

        _start_chunked(own, H)
        for j in range(3):
            _start_chunked(functools.partial(cross, j, ids[j], me), H)
        for j in range(3):
            cross(j, me, ids[j], 0, H).wait_recv()
        for j in range(3):
            cross(j, ids[j], me, 0, H).wait_send()
        own(0, H).wait()

    return pl.pallas_call(
        body, name=name, in_specs=[_any()], out_specs=_any(), out_shape=jax.ShapeDtypeStruct(p.shape, p.dtype),
        scratch_shapes=[pltpu.SemaphoreType.DMA((3,)), pltpu.SemaphoreType.DMA((3,)), pltpu.SemaphoreType.DMA],
    )(p)


def join_halves(f, *, name):
    H, C = f.shape

    def body(f_ref, o_ref, send, recv, local):
        x, y, c, _ = _place()

        def keep(off, size):
            return pltpu.make_async_copy(f_ref.at[pl.ds(off, size)], o_ref.at[c, pl.ds(off, size)], local)

        def give(off, size):
            return pltpu.make_async_remote_copy(
                src_ref=f_ref.at[pl.ds(off, size)], dst_ref=o_ref.at[c, pl.ds(off, size)], send_sem=send, recv_sem=recv,
                device_id=(x, y, 1 - c), device_id_type=MESH)

        _start_chunked(keep, H)
        _start_chunked(give, H)
        give(0, H).wait()
        keep(0, H).wait()

    return pl.pallas_call(
        body, name=name, in_specs=[_any()], out_specs=_any(), out_shape=jax.ShapeDtypeStruct((2, H, C), f.dtype),
        scratch_shapes=[pltpu.SemaphoreType.DMA, pltpu.SemaphoreType.DMA, pltpu.SemaphoreType.DMA],
    )(f)


def _row_tile(rows, want):
    tm = want
    while rows % tm:
        tm //= 2
    return tm


def _add_pair(g, theirs, *, name):
    n, R, C = g.shape
    H = R // 2
    tm = _row_tile(H, 512)
    nb = H // tm

    def body(c_ref, g_ref, t_ref, o_ref):
        o_ref[...] = g_ref[...] + t_ref[...]

    grid_spec = pltpu.PrefetchScalarGridSpec(
        num_scalar_prefetch=1, grid=(n, nb),
        in_specs=[pl.BlockSpec((None, tm, C), lambda q, i, c: (q, c[0] * nb + i, 0)),
                  pl.BlockSpec((None, tm, C), lambda q, i, c: (q, i, 0))],
        out_specs=pl.BlockSpec((None, tm, C), lambda q, i, c: (q, i, 0)))
    return pl.pallas_call(
        body, name=name, grid_spec=grid_spec, out_shape=jax.ShapeDtypeStruct((n, H, C), g.dtype),
        compiler_params=_params(("parallel", "parallel")),
    )(lax.axis_index("c").astype(jnp.int32).reshape(1), g, theirs)


def _add_stack(b, *, name):
    n, H, C = b.shape
    tm = _row_tile(H, 256)

    def body(b_ref, o_ref):
        s = b_ref[0]
        for k in range(1, n):
            s = s + b_ref[k]
        o_ref[...] = s

    return pl.pallas_call(
        body, name=name, grid=(H // tm,), in_specs=[pl.BlockSpec((n, tm, C), lambda i: (0, i, 0))],
        out_specs=pl.BlockSpec((tm, C), lambda i: (i, 0)), out_shape=jax.ShapeDtypeStruct((H, C), b.dtype),
        compiler_params=_params(("parallel",)),
    )(b)


def reduce_scatter(g):
    _, R, C = g.shape
    pair = _add_pair(g, swap_halves(g, name="rs_swap_halves"), name="rs_add_pair")
    total = _add_stack(exchange_pieces(pair, name="rs_exchange"), name="rs_add_chips")
    return join_halves(total, name="rs_join").reshape(R, C)


BIG = ("ffn1_w_gate", "ffn1_w_up", "ffn1_w_down", "ffn2_w_gate", "ffn2_w_up", "ffn2_w_down", "w_in", "w_branch", "w_out")
SMALL = ("norm_pre", "norm_post", "conv_w", "gla_w_alpha")
REPLICATED = ("conv_b", "conv_ln_g", "conv_ln_b", "gla_b_alpha", "gla_norm_g")
WEIGHTS = ("norm_pre", "norm_post", "ffn1_w_gate", "ffn1_w_up", "ffn1_w_down", "ffn2_w_gate", "ffn2_w_up", "ffn2_w_down",
           "w_in", "conv_w", "conv_b", "conv_ln_g", "conv_ln_b", "gla_w_alpha", "gla_b_alpha", "gla_norm_g", "w_branch",
           "w_out")
SHARD_AXIS = {"ffn1_w_gate": 2, "ffn1_w_up": 2, "ffn1_w_down": 1, "ffn2_w_gate": 2, "ffn2_w_up": 2, "ffn2_w_down": 1,
              "w_in": 2, "w_branch": 3, "w_out": 1, "norm_pre": 2, "norm_post": 2, "conv_w": 2, "gla_w_alpha": 2}


def _pack(arrs, dtype, row_mult):
    flat = jnp.concatenate([a.astype(dtype).reshape(-1) for a in arrs])
    rows = -(-flat.shape[0] // PACK_COLS)
    rows = -(-rows // row_mult) * row_mult
    return jnp.pad(flat, (0, rows * PACK_COLS - flat.shape[0])).reshape(rows, PACK_COLS)


def _unpack(flat, shapes):
    out, off = [], 0
    for s in shapes:
        n = 1
        for d in s:
            n *= d
        out.append(lax.slice_in_dim(flat, off, off + n, axis=flat.ndim - 1).reshape(flat.shape[:-1] + tuple(s)))
        off += n
    return out


def gather_weights(shards, names, dtype, row_mult, name):
    packed = _pack([shards[k] for k in names], dtype, row_mult)
    got = all_gather_chips(packed, name=name).reshape(4, -1)
    parts = _unpack(got, [shards[k].shape for k in names])
    return {k: jnp.concatenate([p[q] for q in range(4)], axis=SHARD_AXIS[k]) for k, p in zip(names, parts)}


def scatter_grads(grads, shards):
    sharded = BIG + SMALL
    pieces = []
    for q in range(4):
        part = []
        for k in sharded:
            w = shards[k].shape[SHARD_AXIS[k]]
            part.append(lax.slice_in_dim(grads[k], q * w, (q + 1) * w, axis=SHARD_AXIS[k]))
        part += [grads[k] for k in REPLICATED]
        pieces.append(_pack(part, F32, 512))
    total = reduce_scatter(jnp.stack(pieces)).reshape(-1)
    names = sharded + REPLICATED
    return dict(zip(names, _unpack(total, [shards[k].shape for k in names])))


TM = 256
TM_GLA = 512
TQ = 256

IN_SB, IN_CONV, IN_GLA, IN_LR, IN_GATE = 0, 1536, 2560, 4096, 4112
IN_END = 7184


def layer_weights(full, l):
    w = {}
    for f in ("ffn1", "ffn2"):
        w[f + "_gu"] = jnp.concatenate([full[f + "_w_gate"][l], full[f + "_w_up"][l]], axis=1)
        w[f + "_d"] = full[f + "_w_down"][l]
    win = full["w_in"][l]
    w["in_sb"] = win[:, IN_SB:IN_CONV]
    w["in_conv"] = win[:, IN_CONV:IN_GLA]
    w["in_gla"] = win[:, IN_GLA:IN_LR]
    w["in_lr"] = jnp.pad(win[:, IN_LR:IN_GATE], ((0, 0), (0, GLA_RANK_PAD - GLA_RANK)))
    w["in_gate"] = win[:, IN_GATE:IN_END]
    w["branch"] = [full["w_branch"][l, g] for g in range(3)]
    w["out"] = full["w_out"][l]
    return w


def ffn_fwd(x, gpre, gpost, wgu, wd, tag):
    h = rowwise(f_rms, [x], [gpre], [MXU_DTYPE], tm=TM, name=tag + "_pre")[0]
    ab = mm(h, wgu, tn=512, name=tag + "_gu")
    z = rowwise(f_swiglu, [(ab, D_FF, 0), (ab, D_FF, 1)], [], [MXU_DTYPE], tm=TM, name=tag + "_act")[0]
    f = mm(z, wd, tk=1408, name=tag + "_down")
    x2 = rowwise(f_half_post, [x, f], [gpost], [F32], tm=TM, name=tag + "_post")[0]
    return x2, (x, h, ab, z, f)


def ffn_bwd(dx2, res, gpre, gpost, wgu, wd, tag):
    x, h, ab, z, f = res
    df, dgpost = rowwise_vjp(f_half_post, [x, f], [gpost], [dx2], [1], [0], [MXU_DTYPE], tm=TM, name=tag + "_post_b")
    dz = mm(df, wd, tb=True, tn=1408, name=tag + "_down_dx")
    dwd = mm(z, df, ta=True, tm=1408, name=tag + "_down_dw")
    da, db = rowwise_vjp(f_swiglu, [(ab, D_FF, 0), (ab, D_FF, 1)], [], [dz], [0, 1], [], [MXU_DTYPE, MXU_DTYPE], tm=TM,
                         name=tag + "_act_b")
    dab = jnp.concatenate([da, db], axis=1)
    dh = mm(dab, wgu, tb=True, tk=1408, name=tag + "_gu_dx")
    dwgu = mm(h, dab, ta=True, tn=1408, name=tag + "_gu_dw")
    dx, dgpre = rowwise_vjp(f_rms, [x], [gpre], [dh], [0], [0], [F32], [dx2], tm=TM, name=tag + "_pre_b")
    return dx, dgpre, dgpost, dwgu[:, :D_FF], dwgu[:, D_FF:], dwd


def mixer_fwd(x, p, w, tag):
    h = rowwise(f_rms, [x], [p["gpre"]], [MXU_DTYPE], tm=TM, name=tag + "_pre")[0]
    qkv = mm(h, w["in_sb"], out_dtype=MXU_DTYPE, tn=512, name=tag + "_in_sb")
    cv = mm(h, w["in_conv"], name=tag + "_in_conv")
    gl = mm(h, w["in_gla"], tn=512, name=tag + "_in_gla")
    lr = mm(h, w["in_lr"], out_dtype=MXU_DTYPE, name=tag + "_in_lr")
    gt = mm(h, w["in_gate"], name=tag + "_in_gate")
    sb = sb_attn_fwd(qkv, tq=TQ, name=tag + "_sb")
    u = rowwise(f_glu, [(cv, BRANCH, 0), (cv, BRANCH, 1)], [], [F32], tm=TM, name=tag + "_glu")[0]
    y = conv_fwd(u, p["conv_w"], p["conv_b"], tm=TM, name=tag + "_conv")
    cb = rowwise(f_conv_ln, [y], [p["ln_g"], p["ln_b"]], [MXU_DTYPE], tm=TM, name=tag + "_ln")[0]
    qs, kd, lam = rowwise(f_gla_pre, [(gl, 256, 0), (gl, 256, 1), lr], [p["wa"], p["ba"]], [MXU_DTYPE, MXU_DTYPE, F32],
                          tm=TM_GLA, name=tag + "_gla_pre")
    o, states = gla_scan_fwd(qs, kd, gl, lam, tm=TM_GLA, name=tag + "_gla_scan")
    gb = rowwise(f_gla_post, [o, (gl, BRANCH, 2)], [p["gn"]], [MXU_DTYPE], tm=TM, name=tag + "_gla_post")[0]
    branches = [sb, cb, gb]
    bd = [mm(branches[g], w["branch"][g], name=tag + f"_branch{g}") for g in range(3)]
    merged = rowwise(f_merge, [(gt, D_MODEL, 0), (gt, D_MODEL, 1), (gt, D_MODEL, 2)] + bd, [], [MXU_DTYPE], tm=TM,
                     name=tag + "_merge")[0]
    m = mm(merged, w["out"], name=tag + "_out")
    x2 = rowwise(f_post, [x, m], [p["gpost"]], [F32], tm=TM, name=tag + "_post")[0]
    return x2, (x, h, qkv, cv, gl, lr, gt, u, y, qs, kd, lam, o, states, branches, bd, merged, m)


def mixer_bwd(dx2, res, p, w, tag):
    x, h, qkv, cv, gl, lr, gt, u, y, qs, kd, lam, o, states, branches, bd, merged, m = res
    g = {}
    dm, g["gpost"] = rowwise_vjp(f_post, [x, m], [p["gpost"]], [dx2], [1], [0], [MXU_DTYPE], tm=TM, name=tag + "_post_b")
    dmerged = mm(dm, w["out"], tb=True, name=tag + "_out_dx")
    g["out"] = mm(merged, dm, ta=True, name=tag + "_out_dw")
    gts = [(gt, D_MODEL, 0), (gt, D_MODEL, 1), (gt, D_MODEL, 2)]
    dgate = rowwise_vjp(f_merge, gts + bd, [], [dmerged], [0, 1, 2, 3, 4, 5], [], [MXU_DTYPE] * 6, tm=TM,
                        name=tag + "_merge_b")
    dgt = jnp.concatenate(dgate[:3], axis=1)
    dbd = dgate[3:]
    g["branch"] = [mm(branches[k], dbd[k], ta=True, name=tag + f"_branch{k}_dw") for k in range(3)]
    dsb = mm(dbd[0], w["branch"][0], tb=True, out_dtype=MXU_DTYPE, name=tag + "_branch0_dx")
    dq, dk, dv = sb_attn_bwd(qkv, dsb, tq=TQ, name=tag + "_sb_b")
    dqkv = jnp.concatenate([dq, _mx(dk), _mx(dv)], axis=1)
    dcb = mm(dbd[1], w["branch"][1], tb=True, name=tag + "_branch1_dx")
    dy, g["ln_g"], g["ln_b"] = rowwise_vjp(f_conv_ln, [y], [p["ln_g"], p["ln_b"]], [dcb], [0], [0, 1], [F32], tm=TM,
                                           name=tag + "_ln_b")
    du, dwb = conv_bwd(dy, u, p["conv_w"], tm=TM, name=tag + "_conv_b")
    g["conv_w"], g["conv_b"] = dwb[:CONV_WIDTH], dwb[CONV_WIDTH:CONV_WIDTH + 1]
    dca, dcg = rowwise_vjp(f_glu, [(cv, BRANCH, 0), (cv, BRANCH, 1)], [], [du], [0, 1], [], [MXU_DTYPE, MXU_DTYPE], tm=TM,
                           name=tag + "_glu_b")
    dcv = jnp.concatenate([dca, dcg], axis=1)
    dgb = mm(dbd[2], w["branch"][2], tb=True, name=tag + "_branch2_dx")
    do, dr, g["gn"] = rowwise_vjp(f_gla_post, [o, (gl, BRANCH, 2)], [p["gn"]], [dgb], [0, 1], [0], [F32, MXU_DTYPE], tm=TM,
                                  name=tag + "_gla_post_b")
    dqs, dkd, dgv, dlam = gla_scan_bwd(do, qs, kd, gl, lam, states, tm=TM_GLA, name=tag + "_gla_scan_b")
    dgq, dgk, dlr, g["wa"], g["ba"] = rowwise_vjp(
        f_gla_pre, [(gl, 256, 0), (gl, 256, 1), lr], [p["wa"], p["ba"]], [dqs, dkd, dlam], [0, 1, 2], [0, 1],
        [MXU_DTYPE, MXU_DTYPE, MXU_DTYPE], tm=TM_GLA, name=tag + "_gla_pre_b")
    dgl = jnp.concatenate([dgq, dgk, _mx(dgv), dr], axis=1)
    secs = [("in_sb", dqkv), ("in_conv", dcv), ("in_gla", dgl), ("in_lr", dlr), ("in_gate", dgt)]
    dh = None
    for k, d in secs:
        dh = mm(d, w[k], dh, tb=True, name=tag + "_" + k + "_dx")
        g[k] = mm(h, d, ta=True, tn=1536, name=tag + "_" + k + "_dw")
    dx, g["gpre"] = rowwise_vjp(f_rms, [x], [p["gpre"]], [dh], [0], [0], [F32], [dx2], tm=TM, name=tag + "_pre_b")
    return dx, g


def kernel(x, norm_pre, norm_post, ffn1_w_gate, ffn1_w_up, ffn1_w_down, ffn2_w_gate, ffn2_w_up, ffn2_w_down, w_in, conv_w, conv_b, conv_ln_g, conv_ln_b, gla_w_alpha, gla_b_alpha, gla_norm_g, w_branch, w_out, loss_target, m_norm_pre, m_norm_post, m_ffn1_w_gate, m_ffn1_w_up, m_ffn1_w_down, m_ffn2_w_gate, m_ffn2_w_up, m_ffn2_w_down, m_w_in, m_conv_w, m_conv_b, m_conv_ln_g, m_conv_ln_b, m_gla_w_alpha, m_gla_b_alpha, m_gla_norm_g, m_w_branch, m_w_out, v_norm_pre, v_norm_post, v_ffn1_w_gate, v_ffn1_w_up, v_ffn1_w_down, v_ffn2_w_gate, v_ffn2_w_up, v_ffn2_w_down, v_w_in, v_conv_w, v_conv_b, v_conv_ln_g, v_conv_ln_b, v_gla_w_alpha, v_gla_b_alpha, v_gla_norm_g, v_w_branch, v_w_out):
    shards = dict(norm_pre=norm_pre, norm_post=norm_post, ffn1_w_gate=ffn1_w_gate, ffn1_w_up=ffn1_w_up,
                  ffn1_w_down=ffn1_w_down, ffn2_w_gate=ffn2_w_gate, ffn2_w_up=ffn2_w_up, ffn2_w_down=ffn2_w_down, w_in=w_in,
                  conv_w=conv_w, conv_b=conv_b, conv_ln_g=conv_ln_g, conv_ln_b=conv_ln_b, gla_w_alpha=gla_w_alpha,
                  gla_b_alpha=gla_b_alpha, gla_norm_g=gla_norm_g, w_branch=w_branch, w_out=w_out)
    mom_m = dict(zip(WEIGHTS, (m_norm_pre, m_norm_post, m_ffn1_w_gate, m_ffn1_w_up, m_ffn1_w_down, m_ffn2_w_gate,
                               m_ffn2_w_up, m_ffn2_w_down, m_w_in, m_conv_w, m_conv_b, m_conv_ln_g, m_conv_ln_b,
                               m_gla_w_alpha, m_gla_b_alpha, m_gla_norm_g, m_w_branch, m_w_out)))
    mom_v = dict(zip(WEIGHTS, (v_norm_pre, v_norm_post, v_ffn1_w_gate, v_ffn1_w_up, v_ffn1_w_down, v_ffn2_w_gate,
                               v_ffn2_w_up, v_ffn2_w_down, v_w_in, v_conv_w, v_conv_b, v_conv_ln_g, v_conv_ln_b,
                               v_gla_w_alpha, v_gla_b_alpha, v_gla_norm_g, v_w_branch, v_w_out)))
    depth = norm_pre.shape[0]
    full = gather_weights(shards, BIG, MXU_DTYPE, 256, "gather_big")
    full.update(gather_weights(shards, SMALL, F32, 16, "gather_small"))

    def layer_params(l):
        ffn = [dict(gpre=full["norm_pre"][l, k:k + 1], gpost=full["norm_post"][l, k:k + 1]) for k in (0, 2)]
        mix = dict(gpre=full["norm_pre"][l, 1:2], gpost=full["norm_post"][l, 1:2],
                   conv_w=jnp.pad(full["conv_w"][l], ((0, CONV_PAD - CONV_WIDTH), (0, 0))), conv_b=conv_b[l:l + 1],
                   ln_g=conv_ln_g[l:l + 1], ln_b=conv_ln_b[l:l + 1],
                   wa=jnp.pad(full["gla_w_alpha"][l], ((0, GLA_RANK_PAD - GLA_RANK), (0, 0))), ba=gla_b_alpha[l:l + 1],
                   gn=gla_norm_g[l:l + 1])
        return ffn, mix

    xs = x[0]
    saved = []
    for l in range(depth):
        w = layer_weights(full, l)
        ffn, mix = layer_params(l)
        xs, r1 = ffn_fwd(xs, ffn[0]["gpre"], ffn[0]["gpost"], w["ffn1_gu"], w["ffn1_d"], f"l{l}_ffn1")
        xs, r2 = mixer_fwd(xs, mix, w, f"l{l}_mix")
        xs, r3 = ffn_fwd(xs, ffn[1]["gpre"], ffn[1]["gpost"], w["ffn2_gu"], w["ffn2_d"], f"l{l}_ffn2")
        saved.append((w, ffn, mix, r1, r2, r3))
    dx, sq = loss_head(xs, loss_target[0], tm=TM, name="loss_head")
    loss = lax.psum(0.5 * jnp.sum(sq) / D_MODEL, ("x", "y", "c"))

    per_layer = []
    for l in reversed(range(depth)):
        w, ffn, mix, r1, r2, r3 = saved[l]
        g = {}
        dx, gpre2, gpost2, g["ffn2_w_gate"], g["ffn2_w_up"], g["ffn2_w_down"] = ffn_bwd(
            dx, r3, ffn[1]["gpre"], ffn[1]["gpost"], w["ffn2_gu"], w["ffn2_d"], f"l{l}_ffn2")
        dx, gm = mixer_bwd(dx, r2, mix, w, f"l{l}_mix")
        dx, gpre0, gpost0, g["ffn1_w_gate"], g["ffn1_w_up"], g["ffn1_w_down"] = ffn_bwd(
            dx, r1, ffn[0]["gpre"], ffn[0]["gpost"], w["ffn1_gu"], w["ffn1_d"], f"l{l}_ffn1")
        g["norm_pre"] = jnp.concatenate([gpre0, gm["gpre"], gpre2], axis=0)
        g["norm_post"] = jnp.concatenate([gpost0, gm["gpost"], gpost2], axis=0)
        g["w_in"] = jnp.concatenate([gm["in_sb"], gm["in_conv"], gm["in_gla"], gm["in_lr"][:, :GLA_RANK], gm["in_gate"]],
                                    axis=1)
        g["conv_w"], g["conv_b"] = gm["conv_w"], gm["conv_b"][0]
        g["conv_ln_g"], g["conv_ln_b"] = gm["ln_g"][0], gm["ln_b"][0]
        g["gla_w_alpha"], g["gla_b_alpha"], g["gla_norm_g"] = gm["wa"][:GLA_RANK], gm["ba"][0], gm["gn"][0]
        g["w_branch"] = jnp.stack(gm["branch"])
        g["w_out"] = gm["out"]
        per_layer.append(g)
    per_layer.reverse()
    grads = {k: jnp.stack([g[k] for g in per_layer]) for k in WEIGHTS}

    grad_w = scatter_grads(grads, shards)
    delta, new_m, new_v = {}, {}, {}
    for k in WEIGHTS:
        delta[k], new_m[k], new_v[k] = adamw(shards[k], grad_w[k], mom_m[k], mom_v[k], name="adamw_" + k)
    return (loss, dx[None], *[grad_w[k] for k in WEIGHTS], *[delta[k] for k in WEIGHTS], *[new_m[k] for k in WEIGHTS],
            *[new_v[k] for k in WEIGHTS])
```

```python
import functools

import jax
import jax.numpy as jnp
from jax import lax
from jax.experimental import pallas as pl
from jax.experimental.pallas import tpu as pltpu

F32 = jnp.float32
MXU_DTYPE = jnp.bfloat16
HIGHEST = lax.Precision.HIGHEST
MESH = pl.DeviceIdType.MESH

NORM_EPS = 1e-6
D_MODEL = 1024
D_FF = 2816
BRANCH = 512
CHUNK = 64
CONV_WIDTH = 31
CONV_PAD = 32
GLA_RANK = 16
GLA_RANK_PAD = 128
GLA_TAU = 16.0
SB_SCALE = 0.125
SB_CUTOFF = 60.0
GLA_SCALE = 0.125
PACK_COLS = 1024
VMEM_LIMIT = 56 * 1024 * 1024

ADAM_LR, ADAM_B1, ADAM_B2, ADAM_EPS, ADAM_WD, ADAM_STEP = 0.001, 0.9, 0.999, 1e-08, 0.01, 10

NT = (((1,), (1,)), ((), ()))
TN = (((0,), (0,)), ((), ()))
NN = (((1,), (0,)), ((), ()))


def _params(sem=None, vmem=None):
    return pltpu.CompilerParams(dimension_semantics=sem, vmem_limit_bytes=vmem)


def _mx(v):
    return v.astype(MXU_DTYPE)


def _mx_round(v):
    return v.astype(MXU_DTYPE).astype(F32)


def _fit(dim, want):
    if dim <= want:
        return dim
    for d in range(want - want % 128, 0, -128):
        if dim % d == 0:
            return d
    raise ValueError((dim, want))


def mm(a, b, c=None, *, ta=False, tb=False, out_dtype=F32, tm=1024, tn=1024, tk=1024, name):
    K, M = a.shape if ta else a.shape[::-1]
    N = b.shape[0] if tb else b.shape[1]
    assert (b.shape[1] if tb else b.shape[0]) == K, (a.shape, b.shape, ta, tb)
    tm, tn, tk = _fit(M, tm), _fit(N, tn), _fit(K, tk)
    nk = K // tk
    dn = (((0 if ta else 1,), (1 if tb else 0,)), ((), ()))

    def body(*refs):
        if c is None:
            a_ref, b_ref, o_ref, acc = refs
            c_ref = None
        else:
            a_ref, b_ref, c_ref, o_ref, acc = refs
        k = pl.program_id(2)
        p = lax.dot_general(_mx(a_ref[...]), _mx(b_ref[...]), dn, preferred_element_type=F32)

        @pl.when(k == 0)
        def _():
            acc[...] = p

        @pl.when(k > 0)
        def _():
            acc[...] += p

        @pl.when(k == nk - 1)
        def _():
            r = acc[...]
            if c_ref is not None:
                r = r + c_ref[...].astype(F32)
            o_ref[...] = r.astype(o_ref.dtype)

    a_spec = pl.BlockSpec((tk, tm), lambda i, j, k: (k, i)) if ta else pl.BlockSpec((tm, tk), lambda i, j, k: (i, k))
    b_spec = pl.BlockSpec((tn, tk), lambda i, j, k: (j, k)) if tb else pl.BlockSpec((tk, tn), lambda i, j, k: (k, j))
    o_spec = pl.BlockSpec((tm, tn), lambda i, j, k: (i, j))
    ins, in_specs = [a, b], [a_spec, b_spec]
    if c is not None:
        ins.append(c)
        in_specs.append(o_spec)
    return pl.pallas_call(
        body, name=name, grid=(M // tm, N // tn, nk), in_specs=in_specs, out_specs=o_spec,
        out_shape=jax.ShapeDtypeStruct((M, N), out_dtype), scratch_shapes=[pltpu.VMEM((tm, tn), F32)],
        compiler_params=_params(("parallel", "parallel", "arbitrary"), VMEM_LIMIT),
    )(*ins)


def _row_arg(arg):
    if isinstance(arg, tuple):
        return arg
    return arg, arg.shape[1], 0


def _row_specs(rows, n):
    arrs, specs, avals = [], [], []
    for arg in rows:
        arr, width, cb = _row_arg(arg)
        rb = arr.shape[0] // n
        arrs.append(arr)
        specs.append(pl.BlockSpec((rb, width), lambda i, cb=cb: (i, cb)))
        avals.append(jax.ShapeDtypeStruct((rb, width), F32))
    return arrs, specs, avals


def _par_specs(pars):
    specs = [pl.BlockSpec(p.shape, lambda i, nd=p.ndim: (0,) * nd) for p in pars]
    avals = [jax.ShapeDtypeStruct(p.shape, F32) for p in pars]
    return specs, avals


def rowwise(f, rows, pars, out_dtypes, *, tm, name):
    n = _row_arg(rows[0])[0].shape[0] // tm
    arrs, rspecs, ravals = _row_specs(rows, n)
    pspecs, pavals = _par_specs(pars)
    oavals = jax.eval_shape(f, *ravals, *pavals)
    nin = len(arrs) + len(pars)

    def body(*refs):
        outs = f(*[r[...].astype(F32) for r in refs[:nin]])
        for o_ref, o in zip(refs[nin:], outs):
            o_ref[...] = o.astype(o_ref.dtype)

    return pl.pallas_call(
        body, name=name, grid=(n,), in_specs=rspecs + pspecs,
        out_specs=[pl.BlockSpec(o.shape, lambda i: (i, 0)) for o in oavals],
        out_shape=[jax.ShapeDtypeStruct((n * o.shape[0], o.shape[1]), dt) for o, dt in zip(oavals, out_dtypes)],
        compiler_params=_params(("parallel",), VMEM_LIMIT),
    )(*arrs, *pars)


def rowwise_vjp(f, rows, pars, cots, row_grad, par_grad, d_dtypes, adds=None, *, tm, name):
    n = _row_arg(rows[0])[0].shape[0] // tm
    arrs, rspecs, ravals = _row_specs(rows, n)
    pspecs, pavals = _par_specs(pars)
    carrs, cspecs, _ = _row_specs(cots, n)
    adds = adds or [None] * len(row_grad)
    add_arrs = [a for a in adds if a is not None]
    _, aspecs, _ = _row_specs(add_arrs, n)
    nr, npar, nc, na = len(arrs), len(pars), len(carrs), len(add_arrs)
    diff = list(row_grad) + [nr + j for j in par_grad]
    ngr = len(row_grad)

    def body(*refs):
        ins = [r[...].astype(F32) for r in refs[:nr + npar]]
        cs = tuple(r[...].astype(F32) for r in refs[nr + npar:nr + npar + nc])
        add_refs = list(refs[nr + npar + nc:nr + npar + nc + na])
        outs = refs[nr + npar + nc + na:]

        def g(*d):
            full = list(ins)
            for idx, val in zip(diff, d):
                full[idx] = val
            return f(*full)

        _, pullback = jax.vjp(g, *[ins[idx] for idx in diff])
        ds = pullback(cs)
        for k in range(ngr):
            d = ds[k]
            if adds[k] is not None:
                d = d + add_refs.pop(0)[...].astype(F32)
            outs[k][...] = d.astype(outs[k].dtype)
        i = pl.program_id(0)
        for k in range(ngr, len(diff)):
            @pl.when(i == 0)
            def _(k=k):
                outs[k][...] = ds[k]

            @pl.when(i > 0)
            def _(k=k):
                outs[k][...] += ds[k]

    out_specs = [pl.BlockSpec(ravals[i].shape, lambda i: (i, 0)) for i in row_grad] + [pspecs[j] for j in par_grad]
    out_shape = [jax.ShapeDtypeStruct((arrs[i].shape[0], ravals[i].shape[1]), dt) for i, dt in zip(row_grad, d_dtypes)]
    out_shape += [jax.ShapeDtypeStruct(pars[j].shape, F32) for j in par_grad]
    return pl.pallas_call(
        body, name=name, grid=(n,), in_specs=rspecs + pspecs + cspecs + aspecs, out_specs=out_specs, out_shape=out_shape,
        compiler_params=_params(("arbitrary",), VMEM_LIMIT),
    )(*arrs, *pars, *carrs, *add_arrs)


def _logsig(x):
    return jnp.minimum(x, 0.0) - jnp.log(1.0 + jnp.exp(-jnp.abs(x)))


def _sigmoid(x):
    return 1.0 / (1.0 + jnp.exp(-x))


def _silu(x):
    return x * _sigmoid(x)


def _rms(x, g):
    return x * lax.rsqrt(jnp.mean(x * x, axis=-1, keepdims=True) + NORM_EPS) * g


def f_rms(x, g):
    return (_rms(x, g),)


def f_swiglu(a, b):
    return (_silu(a) * b,)


def f_half_post(x, f, g):
    return (x + 0.5 * _rms(f, g),)


def f_post(x, m, g):
    return (x + _rms(m, g),)


def f_glu(a, g):
    return (a * _sigmoid(g),)


def f_conv_ln(y, lg, lb):
    mu = jnp.mean(y, axis=-1, keepdims=True)
    var = jnp.mean(jnp.square(y - mu), axis=-1, keepdims=True)
    return (_silu((y - mu) * lax.rsqrt(var + NORM_EPS) * lg + lb),)


def f_merge(g0, g1, g2, b0, b1, b2):
    return (_sigmoid(g0) * b0 + _sigmoid(g1) * b1 + _sigmoid(g2) * b2,)


def f_gla_post(o, r, g):
    w = o.shape[1]
    hv = w // 4
    i = lax.broadcasted_iota(jnp.int32, (w, w), 0) // hv
    j = lax.broadcasted_iota(jnp.int32, (w, w), 1) // hv
    avg = jnp.where(i == j, 1.0 / hv, 0.0).astype(F32)
    ms = jnp.dot(o * o, avg, precision=HIGHEST, preferred_element_type=F32)
    return (o * lax.rsqrt(ms + NORM_EPS) * g * _silu(r),)


def f_gla_pre(q, k, lr, wa, ba):
    tm = q.shape[0]
    pre = jnp.dot(_mx_round(lr), _mx_round(wa), precision=HIGHEST, preferred_element_type=F32) + ba
    la = _logsig(pre) / GLA_TAU
    i = lax.broadcasted_iota(jnp.int32, (tm, tm), 0)
    j = lax.broadcasted_iota(jnp.int32, (tm, tm), 1)
    later = jnp.where((i // CHUNK == j // CHUNK) & (j > i), 1.0, 0.0).astype(F32)
    ci = lax.broadcasted_iota(jnp.int32, (tm // CHUNK, tm), 0)
    cj = lax.broadcasted_iota(jnp.int32, (tm // CHUNK, tm), 1) // CHUNK
    chunk_sum = jnp.where(ci == cj, 1.0, 0.0).astype(F32)
    to_end = jnp.dot(later, la, precision=HIGHEST, preferred_element_type=F32)
    lam = jnp.exp(jnp.dot(chunk_sum, la, precision=HIGHEST, preferred_element_type=F32))
    return q * GLA_SCALE, k * jnp.exp(to_end), lam


def _split_dot(x, u):
    hi = _mx(x)
    lo = _mx(x - hi.astype(F32))
    return jnp.dot(hi, u, preferred_element_type=F32) + jnp.dot(lo, u, preferred_element_type=F32)


def _tri(tq, tk):
    row = lax.broadcasted_iota(jnp.int32, (tq, tk), 0)
    col = lax.broadcasted_iota(jnp.int32, (tq, tk), 1)
    return row, col


def _sb_tile(qh, kh, valid, suf, run):
    z = lax.dot_general(qh, kh, NT, preferred_element_type=F32) * SB_SCALE
    lp = jnp.minimum(z, 0.0) - jnp.log(1.0 + jnp.exp(-jnp.abs(z)))
    lk = jnp.where(valid, lp - z, 0.0)
    inc = _split_dot(lk, suf)
    a = jnp.where(valid, jnp.exp(lp + (inc - lk + run)), 0.0)
    return lp, a, run + inc[:, 0:1]


def _sticks_left(run0, run1):
    return (jnp.maximum(jnp.max(run0), jnp.max(run1)) > -SB_CUTOFF).astype(jnp.int32)


def sb_attn_fwd(qkv, *, tq, name):
    S = qkv.shape[0]
    tk = tq
    pairs = BRANCH // 128

    def body(q_ref, k_ref, v_ref, o_ref):
        i = pl.program_id(1)
        row, col = _tri(tq, tk)
        suf = _mx(row >= col)
        q = q_ref[...]

        def step(state):
            j, _, carry = state
            ks = pl.multiple_of((i - j) * tk, tk)
            kb = k_ref[pl.ds(ks, tk), :]
            vb = v_ref[pl.ds(ks, tk), :]
            valid = (col - row) < j * tq
            new = []
            for h in range(2):
                acc, run = carry[h]
                sl = slice(64 * h, 64 * h + 64)
                _, a, run = _sb_tile(q[:, sl], kb[:, sl], valid, suf, run)
                acc = acc + jnp.dot(_mx(a), vb[:, sl], preferred_element_type=F32)
                new.append((acc, run))
            return j + 1, _sticks_left(new[0][1], new[1][1]), tuple(new)

        zero = (jnp.zeros((tq, 64), F32), jnp.zeros((tq, 1), F32))
        _, _, res = lax.while_loop(lambda s: (s[0] <= i) & (s[1] > 0), step, (jnp.int32(0), jnp.int32(1), (zero, zero)))
        o_ref[...] = jnp.concatenate([res[0][0], res[1][0]], axis=1).astype(o_ref.dtype)

    return pl.pallas_call(
        body, name=name, grid=(pairs, S // tq),
        in_specs=[pl.BlockSpec((tq, 128), lambda p, i: (i, p)),
                  pl.BlockSpec((S, 128), lambda p, i: (0, pairs + p)),
                  pl.BlockSpec((S, 128), lambda p, i: (0, 2 * pairs + p))],
        out_specs=pl.BlockSpec((tq, 128), lambda p, i: (i, p)),
        out_shape=jax.ShapeDtypeStruct((S, BRANCH), MXU_DTYPE),
        compiler_params=_params(("parallel", "parallel"), VMEM_LIMIT),
    )(qkv, qkv, qkv)


def sb_attn_bwd(qkv, do, *, tq, name):
    S = qkv.shape[0]
    tk = tq
    nq = S // tq
    pairs = BRANCH // 128

    def body(q_ref, k_ref, v_ref, do_ref, dq_ref, dk_ref, dv_ref, g_sc, b_sc):
        i = pl.program_id(1)

        @pl.when(i == 0)
        def _():
            dk_ref[...] = jnp.zeros_like(dk_ref)
            dv_ref[...] = jnp.zeros_like(dv_ref)

        row, col = _tri(tq, tk)
        suf = _mx(row >= col)
        pre = _mx(row <= col)
        q = q_ref[...]
        dout = do_ref[...]

        def sweep1(state):
            j, _, carry = state
            kblk = i - j
            ks = pl.multiple_of(kblk * tk, tk)
            kb = k_ref[pl.ds(ks, tk), :]
            vb = v_ref[pl.ds(ks, tk), :]
            valid = (col - row) < j * tq
            runs, dvs = [], []
            for h in range(2):
                sl = slice(64 * h, 64 * h + 64)
                lp, a, run = _sb_tile(q[:, sl], kb[:, sl], valid, suf, carry[h])
                da = lax.dot_general(dout[:, sl], vb[:, sl], NT, preferred_element_type=F32)
                g_sc[h, kblk] = (a * da).astype(g_sc.dtype)
                b_sc[h, kblk] = jnp.where(valid, jnp.exp(lp), 0.0).astype(b_sc.dtype)
                dvs.append(lax.dot_general(_mx(a), dout[:, sl], TN, preferred_element_type=F32))
                runs.append(run)
            dv_ref[pl.ds(ks, tk), :] += jnp.concatenate(dvs, axis=1)
            return j + 1, _sticks_left(runs[0], runs[1]), tuple(runs)

        start = (jnp.int32(0), jnp.int32(1), (jnp.zeros((tq, 1), F32), jnp.zeros((tq, 1), F32)))
        tiles, _, _ = lax.while_loop(lambda s: (s[0] <= i) & (s[1] > 0), sweep1, start)

        def sweep2(kblk, carry):
            ks = pl.multiple_of(kblk * tk, tk)
            kb = k_ref[pl.ds(ks, tk), :]
            new, dks = [], []
            for h in range(2):
                dq, run = carry[h]
                sl = slice(64 * h, 64 * h + 64)
                g = g_sc[h, kblk]
                beta = b_sc[h, kblk].astype(F32)
                inc = jnp.dot(g, pre, preferred_element_type=F32)
                g = g.astype(F32)
                dz = _mx((g - beta * (inc + run)) * SB_SCALE)
                dq = dq + jnp.dot(dz, kb[:, sl], preferred_element_type=F32)
                dks.append(lax.dot_general(dz, q[:, sl], TN, preferred_element_type=F32))
                new.append((dq, run + inc[:, tk - 1:tk]))
            dk_ref[pl.ds(ks, tk), :] += jnp.concatenate(dks, axis=1)
            return tuple(new)

        zero = (jnp.zeros((tq, 64), F32), jnp.zeros((tq, 1), F32))
        res = lax.fori_loop(i + 1 - tiles, i + 1, sweep2, (zero, zero))
        dq_ref[...] = jnp.concatenate([res[0][0], res[1][0]], axis=1).astype(dq_ref.dtype)

    return pl.pallas_call(
        body, name=name, grid=(pairs, nq),
        in_specs=[pl.BlockSpec((tq, 128), lambda p, i: (i, p)),
                  pl.BlockSpec((S, 128), lambda p, i: (0, pairs + p)),
                  pl.BlockSpec((S, 128), lambda p, i: (0, 2 * pairs + p)),
                  pl.BlockSpec((tq, 128), lambda p, i: (i, p))],
        out_specs=[pl.BlockSpec((tq, 128), lambda p, i: (i, p)),
                   pl.BlockSpec((S, 128), lambda p, i: (0, p)),
                   pl.BlockSpec((S, 128), lambda p, i: (0, p))],
        out_shape=[jax.ShapeDtypeStruct((S, BRANCH), MXU_DTYPE), jax.ShapeDtypeStruct((S, BRANCH), F32),
                   jax.ShapeDtypeStruct((S, BRANCH), F32)],
        scratch_shapes=[pltpu.VMEM((2, nq, tq, tk), MXU_DTYPE), pltpu.VMEM((2, nq, tq, tk), MXU_DTYPE)],
        compiler_params=_params(("parallel", "arbitrary"), VMEM_LIMIT),
    )(qkv, qkv, qkv, do)


def conv_fwd(u, w, b, *, tm, name):
    S, C = u.shape
    hb = tm // CONV_PAD

    def body(u_ref, halo_ref, w_ref, b_ref, y_ref, buf):
        i = pl.program_id(0)
        buf[pl.ds(CONV_PAD, tm), :] = u_ref[...]
        buf[pl.ds(0, CONV_PAD), :] = jnp.where(i > 0, halo_ref[...], 0.0)
        acc = jnp.broadcast_to(b_ref[...], (tm, 128))
        for j in range(CONV_WIDTH):
            acc = acc + buf[pl.ds(CONV_PAD - (CONV_WIDTH - 1) + j, tm), :] * w_ref[pl.ds(j, 1), :]
        y_ref[...] = acc

    return pl.pallas_call(
        body, name=name, grid=(S // tm, C // 128),
        in_specs=[pl.BlockSpec((tm, 128), lambda i, c: (i, c)),
                  pl.BlockSpec((CONV_PAD, 128), lambda i, c: (jnp.maximum(i * hb - 1, 0), c)),
                  pl.BlockSpec((CONV_PAD, 128), lambda i, c: (0, c)),
                  pl.BlockSpec((1, 128), lambda i, c: (0, c))],
        out_specs=pl.BlockSpec((tm, 128), lambda i, c: (i, c)),
        out_shape=jax.ShapeDtypeStruct((S, C), F32),
        scratch_shapes=[pltpu.VMEM((tm + CONV_PAD, 128), F32)],
        compiler_params=_params(("parallel", "parallel")),
    )(u, u, w, b)


def conv_bwd(dy, u, w, *, tm, name):
    S, C = u.shape
    hb = tm // CONV_PAD
    n = S // tm

    def body(dy_ref, dyn_ref, u_ref, up_ref, w_ref, du_ref, dw_ref, bufy, bufu):
        i = pl.program_id(1)
        dyv = dy_ref[...]
        bufy[pl.ds(0, tm), :] = dyv
        bufy[pl.ds(tm, CONV_PAD), :] = jnp.where(i < n - 1, dyn_ref[...], 0.0)
        bufu[pl.ds(CONV_PAD, tm), :] = u_ref[...]
        bufu[pl.ds(0, CONV_PAD), :] = jnp.where(i > 0, up_ref[...], 0.0)

        @pl.when(i == 0)
        def _():
            dw_ref[...] = jnp.zeros_like(dw_ref)

        acc = jnp.zeros((tm, 128), F32)
        for j in range(CONV_WIDTH):
            acc = acc + bufy[pl.ds(CONV_WIDTH - 1 - j, tm), :] * w_ref[pl.ds(j, 1), :]
            shifted = bufu[pl.ds(CONV_PAD - (CONV_WIDTH - 1) + j, tm), :]
            dw_ref[pl.ds(j, 1), :] += jnp.sum(dyv * shifted, axis=0, keepdims=True)
        du_ref[...] = acc
        dw_ref[pl.ds(CONV_WIDTH, 1), :] += jnp.sum(dyv, axis=0, keepdims=True)

    return pl.pallas_call(
        body, name=name, grid=(C // 128, n),
        in_specs=[pl.BlockSpec((tm, 128), lambda c, i: (i, c)),
                  pl.BlockSpec((CONV_PAD, 128), lambda c, i: (jnp.minimum((i + 1) * hb, n * hb - 1), c)),
                  pl.BlockSpec((tm, 128), lambda c, i: (i, c)),
                  pl.BlockSpec((CONV_PAD, 128), lambda c, i: (jnp.maximum(i * hb - 1, 0), c)),
                  pl.BlockSpec((CONV_PAD, 128), lambda c, i: (0, c))],
        out_specs=[pl.BlockSpec((tm, 128), lambda c, i: (i, c)), pl.BlockSpec((CONV_PAD, 128), lambda c, i: (0, c))],
        out_shape=[jax.ShapeDtypeStruct((S, C), F32), jax.ShapeDtypeStruct((CONV_PAD, C), F32)],
        scratch_shapes=[pltpu.VMEM((tm + CONV_PAD, 128), F32), pltpu.VMEM((tm + CONV_PAD, 128), F32)],
        compiler_params=_params(("parallel", "arbitrary")),
    )(dy, dy, u, u, w)


GLA_HEADS, GLA_DK, GLA_DV = 4, 64, 128


def gla_scan_fwd(qs, kd, gl, lam, *, tm, name):
    S = qs.shape[0]
    nc = tm // CHUNK

    def body(qs_ref, kd_ref, v_ref, lam_ref, o_ref, st_ref, state):
        @pl.when(pl.program_id(0) == 0)
        def _():
            state[...] = jnp.zeros_like(state)

        for c in range(nc):
            rows = pl.ds(c * CHUNK, CHUNK)
            q, k, v = _mx(qs_ref[rows, :]), _mx(kd_ref[rows, :]), _mx(v_ref[rows, :])
            upd = [lax.dot_general(v[:, h * GLA_DV:(h + 1) * GLA_DV], k[:, h * GLA_DK:(h + 1) * GLA_DK], TN,
                                   preferred_element_type=F32) for h in range(GLA_HEADS)]
            st = state[...] * lam_ref[pl.ds(c, 1), :] + jnp.concatenate(upd, axis=1)
            state[...] = st
            st_ref[c] = st
            stm = _mx(st)
            o = [lax.dot_general(q[:, h * GLA_DK:(h + 1) * GLA_DK], stm[:, h * GLA_DK:(h + 1) * GLA_DK], NT,
                                 preferred_element_type=F32) for h in range(GLA_HEADS)]
            o_ref[rows, :] = jnp.concatenate(o, axis=1)

    dk_all = GLA_HEADS * GLA_DK
    dv_all = GLA_HEADS * GLA_DV
    return pl.pallas_call(
        body, name=name, grid=(S // tm,),
        in_specs=[pl.BlockSpec((tm, dk_all), lambda i: (i, 0)), pl.BlockSpec((tm, dk_all), lambda i: (i, 0)),
                  pl.BlockSpec((tm, dv_all), lambda i: (i, 1)), pl.BlockSpec((nc, dk_all), lambda i: (i, 0))],
        out_specs=[pl.BlockSpec((tm, dv_all), lambda i: (i, 0)), pl.BlockSpec((nc, GLA_DV, dk_all), lambda i: (i, 0, 0))],
        out_shape=[jax.ShapeDtypeStruct((S, dv_all), F32), jax.ShapeDtypeStruct((S // CHUNK, GLA_DV, dk_all), F32)],
        scratch_shapes=[pltpu.VMEM((GLA_DV, dk_all), F32)],
        compiler_params=_params(("arbitrary",)),
    )(qs, kd, gl, lam)


def gla_scan_bwd(do, qs, kd, gl, lam, states, *, tm, name):
    S = qs.shape[0]
    nc = tm // CHUNK
    n = S // tm
    dk_all = GLA_HEADS * GLA_DK
    dv_all = GLA_HEADS * GLA_DV

    def body(do_ref, qs_ref, kd_ref, v_ref, lam_ref, st_ref, prev_ref, dqs_ref, dkd_ref, dv_ref, dlam_ref, carry):
        i = pl.program_id(0)

        @pl.when(i == 0)
        def _():
            carry[...] = jnp.zeros_like(carry)

        for c in reversed(range(nc)):
            rows = pl.ds(c * CHUNK, CHUNK)
            q, k, v, d = _mx(qs_ref[rows, :]), _mx(kd_ref[rows, :]), _mx(v_ref[rows, :]), _mx(do_ref[rows, :])
            st = _mx(st_ref[c])
            before = st_ref[c - 1] if c > 0 else jnp.where(i < n - 1, prev_ref[0], 0.0)
            outer = [lax.dot_general(d[:, h * GLA_DV:(h + 1) * GLA_DV], q[:, h * GLA_DK:(h + 1) * GLA_DK], TN,
                                     preferred_element_type=F32) for h in range(GLA_HEADS)]
            dst = carry[...] + jnp.concatenate(outer, axis=1)
            dstm = _mx(dst)
            dq, dkk, dvv = [], [], []
            for h in range(GLA_HEADS):
                ksl = slice(h * GLA_DK, (h + 1) * GLA_DK)
                vsl = slice(h * GLA_DV, (h + 1) * GLA_DV)
                dq.append(jnp.dot(d[:, vsl], st[:, ksl], preferred_element_type=F32))
                dkk.append(jnp.dot(v[:, vsl], dstm[:, ksl], preferred_element_type=F32))
                dvv.append(lax.dot_general(k[:, ksl], dstm[:, ksl], NT, preferred_element_type=F32))
            dqs_ref[rows, :] = jnp.concatenate(dq, axis=1)
            dkd_ref[rows, :] = jnp.concatenate(dkk, axis=1)
            dv_ref[rows, :] = jnp.concatenate(dvv, axis=1)
            dlam_ref[pl.ds(c, 1), :] = jnp.sum(dst * before, axis=0, keepdims=True)
            carry[...] = dst * lam_ref[pl.ds(c, 1), :]

    rev = lambda i: n - 1 - i
    return pl.pallas_call(
        body, name=name, grid=(n,),
        in_specs=[pl.BlockSpec((tm, dv_all), lambda i: (rev(i), 0)), pl.BlockSpec((tm, dk_all), lambda i: (rev(i), 0)),
                  pl.BlockSpec((tm, dk_all), lambda i: (rev(i), 0)), pl.BlockSpec((tm, dv_all), lambda i: (rev(i), 1)),
                  pl.BlockSpec((nc, dk_all), lambda i: (rev(i), 0)),
                  pl.BlockSpec((nc, GLA_DV, dk_all), lambda i: (rev(i), 0, 0)),
                  pl.BlockSpec((1, GLA_DV, dk_all), lambda i: (jnp.maximum(rev(i) * nc - 1, 0), 0, 0))],
        out_specs=[pl.BlockSpec((tm, dk_all), lambda i: (rev(i), 0)), pl.BlockSpec((tm, dk_all), lambda i: (rev(i), 0)),
                   pl.BlockSpec((tm, dv_all), lambda i: (rev(i), 0)), pl.BlockSpec((nc, dk_all), lambda i: (rev(i), 0))],
        out_shape=[jax.ShapeDtypeStruct((S, dk_all), F32), jax.ShapeDtypeStruct((S, dk_all), F32),
                   jax.ShapeDtypeStruct((S, dv_all), F32), jax.ShapeDtypeStruct((S // CHUNK, dk_all), F32)],
        scratch_shapes=[pltpu.VMEM((GLA_DV, dk_all), F32)],
        compiler_params=_params(("arbitrary",)),
    )(do, qs, kd, gl, lam, states, states)


def loss_head(y, target, *, tm, name):
    S, D = y.shape

    def body(y_ref, t_ref, dy_ref, sq_ref):
        err = y_ref[...] - t_ref[...]
        dy_ref[...] = err * (1.0 / D)
        part = jnp.sum(err * err, axis=0, keepdims=True)

        @pl.when(pl.program_id(0) == 0)
        def _():
            sq_ref[...] = part

        @pl.when(pl.program_id(0) > 0)
        def _():
            sq_ref[...] += part

    return pl.pallas_call(
        body, name=name, grid=(S // tm,),
        in_specs=[pl.BlockSpec((tm, D), lambda i: (i, 0)), pl.BlockSpec((tm, D), lambda i: (i, 0))],
        out_specs=[pl.BlockSpec((tm, D), lambda i: (i, 0)), pl.BlockSpec((1, D), lambda i: (0, 0))],
        out_shape=[jax.ShapeDtypeStruct((S, D), F32), jax.ShapeDtypeStruct((1, D), F32)],
        compiler_params=_params(("arbitrary",)),
    )(y, target)


def f_adamw(w, g, m, v):
    m = ADAM_B1 * m + (1.0 - ADAM_B1) * g
    v = ADAM_B2 * v + (1.0 - ADAM_B2) * jnp.square(g)
    m_hat = m / (1.0 - ADAM_B1 ** ADAM_STEP)
    v_hat = v / (1.0 - ADAM_B2 ** ADAM_STEP)
    return -ADAM_LR * (m_hat / (jnp.sqrt(v_hat) + ADAM_EPS) + ADAM_WD * w), m, v


def adamw(w, g, m, v, *, name):
    shape = w.shape
    cols = shape[-1]
    rows = w.size // cols
    tm = rows
    while tm % 16 == 0 and tm * cols * 4 > (1 << 20):
        tm //= 2
    flat = [t.reshape(rows, cols) for t in (w, g, m, v)]
    outs = rowwise(f_adamw, flat, [], [F32, F32, F32], tm=tm, name=name)
    return [o.reshape(shape) for o in outs]


def _place():
    x, y, c = lax.axis_index("x"), lax.axis_index("y"), lax.axis_index("c")
    return x, y, c, [(1 - x, y), (x, 1 - y), (1 - x, 1 - y)]


def _my_chip():
    return 2 * lax.axis_index("x") + lax.axis_index("y")


def _any():
    return pl.BlockSpec(memory_space=pl.ANY)


DMA_CHUNKS = 8


def _chunks(rows):
    n = DMA_CHUNKS
    while n > 1 and rows % (n * 16):
        n //= 2
    return [(k * (rows // n), rows // n) for k in range(n)]


def _start_chunked(make, rows):
    for off, size in _chunks(rows):
        make(off, size).start()


def all_gather_chips(v, *, name):
    R, C = v.shape
    H = R // 2

    def body(v_ref, o_ref, send, recv, fsend, frecv):
        x, y, c, chips = _place()
        me = 2 * x + y
        ids = [2 * px + py for px, py in chips]
        mine = pl.ds(pl.multiple_of(c * H, 8), H)
        other = pl.ds(pl.multiple_of((1 - c) * H, 8), H)

        def cross(j, src_chip, rows):
            return pltpu.make_async_remote_copy(
                src_ref=v_ref.at[rows], dst_ref=o_ref.at[src_chip, rows], send_sem=send.at[j], recv_sem=recv.at[j],
                device_id=(*chips[j], c), device_id_type=MESH)

        def handed(j, rows):
            return pltpu.make_async_remote_copy(
                src_ref=o_ref.at[ids[j], rows], dst_ref=o_ref.at[ids[j], rows], send_sem=fsend.at[j],
                recv_sem=frecv.at[j], device_id=(x, y, 1 - c), device_id_type=MESH)

        def my_rows(off, size):
            return pl.ds(pl.multiple_of(c * H + off, 8), size)

        for j in range(3):
            _start_chunked(lambda off, size, j=j: cross(j, me, my_rows(off, size)), H)
        for j in range(3):
            cross(j, ids[j], mine).wait_recv()
            _start_chunked(lambda off, size, j=j: handed(j, my_rows(off, size)), H)
        for j in range(3):
            handed(j, other).wait_recv()
        for j in range(3):
            cross(j, me, mine).wait_send()
            handed(j, mine).wait_send()

    got = pl.pallas_call(
        body, name=name, in_specs=[_any()], out_specs=_any(), out_shape=jax.ShapeDtypeStruct((4, R, C), v.dtype),
        scratch_shapes=[pltpu.SemaphoreType.DMA((3,)), pltpu.SemaphoreType.DMA((3,)), pltpu.SemaphoreType.DMA((3,)),
                        pltpu.SemaphoreType.DMA((3,))],
    )(v)
    return lax.dynamic_update_slice(got, v[None], (_my_chip(), 0, 0))


def swap_halves(g, *, name):
    n, R, C = g.shape
    H = R // 2

    def body(g_ref, theirs_ref, send, recv):
        x, y, c, _ = _place()

        def give(q, off, size):
            return pltpu.make_async_remote_copy(
                src_ref=g_ref.at[q, pl.ds(pl.multiple_of((1 - c) * H + off, 8), size)],
                dst_ref=theirs_ref.at[q, pl.ds(off, size)], send_sem=send.at[q], recv_sem=recv.at[q],
                device_id=(x, y, 1 - c), device_id_type=MESH)

        for q in range(n):
            _start_chunked(functools.partial(give, q), H)
        for q in range(n):
            give(q, 0, H).wait()

    return pl.pallas_call(
        body, name=name, in_specs=[_any()], out_specs=_any(), out_shape=jax.ShapeDtypeStruct((n, H, C), g.dtype),
        scratch_shapes=[pltpu.SemaphoreType.DMA((n,)), pltpu.SemaphoreType.DMA((n,))],
    )(g)


def exchange_pieces(p, *, name):
    H = p.shape[1]

    def body(p_ref, b_ref, send, recv):
        x, y, c, chips = _place()
        me = 2 * x + y
        ids = [2 * px + py for px, py in chips]

        def cross(j, piece, slot, off, size):
            return pltpu.make_async_remote_copy(
                src_ref=p_ref.at[piece, pl.ds(off, size)], dst_ref=b_ref.at[slot, pl.ds(off, size)], send_sem=send.at[j],
                recv_sem=recv.at[j], device_id=(*chips[j], c), device_id_type=MESH)

        for j in range(3):
            _start_chunked(functools.partial(cross, j, ids[j], me), H)
        for j in range(3):
            cross(j, me, ids[j], 0, H).wait_recv()
        for j in range(3):
            cross(j, ids[j], me, 0, H).wait_send()

    got = pl.pallas_call(
        body, name=name, in_specs=[_any()], out_specs=_any(), out_shape=jax.ShapeDtypeStruct(p.shape, p.dtype),
        scratch_shapes=[pltpu.SemaphoreType.DMA((3,)), pltpu.SemaphoreType.DMA((3,))],
    )(p)
    me = _my_chip()
    return lax.dynamic_update_slice(got, lax.dynamic_slice_in_dim(p, me, 1, axis=0), (me, 0, 0))


def join_halves(f, *, name):
    H, C = f.shape

    def body(f_ref, o_ref, send, recv):
        x, y, c, _ = _place()

        def give(off, size):
            return pltpu.make_async_remote_copy(
                src_ref=f_ref.at[pl.ds(off, size)], dst_ref=o_ref.at[pl.ds(off, size)], send_sem=send, recv_sem=recv,
                device_id=(x, y, 1 - c), device_id_type=MESH)

        _start_chunked(give, H)
        give(0, H).wait()

    theirs = pl.pallas_call(
        body, name=name, in_specs=[_any()], out_specs=_any(), out_shape=jax.ShapeDtypeStruct((H, C), f.dtype),
        scratch_shapes=[pltpu.SemaphoreType.DMA, pltpu.SemaphoreType.DMA],
    )(f)
    south = lax.axis_index("c") == 0
    return jnp.concatenate([jnp.where(south, f, theirs), jnp.where(south, theirs, f)], axis=0)


def _row_tile(rows, want):
    tm = want
    while rows % tm:
        tm //= 2
    return tm


def _add_pair(g, theirs, *, name):
    n, R, C = g.shape
    H = R // 2
    tm = _row_tile(H, 512)
    nb = H // tm

    def body(c_ref, g_ref, t_ref, o_ref):
        o_ref[...] = g_ref[...] + t_ref[...]

    grid_spec = pltpu.PrefetchScalarGridSpec(
        num_scalar_prefetch=1, grid=(n, nb),
        in_specs=[pl.BlockSpec((None, tm, C), lambda q, i, c: (q, c[0] * nb + i, 0)),
                  pl.BlockSpec((None, tm, C), lambda q, i, c: (q, i, 0))],
        out_specs=pl.BlockSpec((None, tm, C), lambda q, i, c: (q, i, 0)))
    return pl.pallas_call(
        body, name=name, grid_spec=grid_spec, out_shape=jax.ShapeDtypeStruct((n, H, C), g.dtype),
        compiler_params=_params(("parallel", "parallel")),
    )(lax.axis_index("c").astype(jnp.int32).reshape(1), g, theirs)


def _add_stack(b, *, name):
    n, H, C = b.shape
    tm = _row_tile(H, 256)

    def body(b_ref, o_ref):
        s = b_ref[0]
        for k in range(1, n):
            s = s + b_ref[k]
        o_ref[...] = s

    return pl.pallas_call(
        body, name=name, grid=(H // tm,), in_specs=[pl.BlockSpec((n, tm, C), lambda i: (0, i, 0))],
        out_specs=pl.BlockSpec((tm, C), lambda i: (i, 0)), out_shape=jax.ShapeDtypeStruct((H, C), b.dtype),
        compiler_params=_params(("parallel",)),
    )(b)


def reduce_scatter(g):
    pair = _add_pair(g, swap_halves(g, name="rs_swap_halves"), name="rs_add_pair")
    total = _add_stack(exchange_pieces(pair, name="rs_exchange"), name="rs_add_chips")
    return join_halves(total, name="rs_join")


BIG = ("ffn1_w_gate", "ffn1_w_up", "ffn1_w_down", "ffn2_w_gate", "ffn2_w_up", "ffn2_w_down", "w_in", "w_branch", "w_out")
SMALL = ("norm_pre", "norm_post", "conv_w", "gla_w_alpha")
REPLICATED = ("conv_b", "conv_ln_g", "conv_ln_b", "gla_b_alpha", "gla_norm_g")
WEIGHTS = ("norm_pre", "norm_post", "ffn1_w_gate", "ffn1_w_up", "ffn1_w_down", "ffn2_w_gate", "ffn2_w_up", "ffn2_w_down",
           "w_in", "conv_w", "conv_b", "conv_ln_g", "conv_ln_b", "gla_w_alpha", "gla_b_alpha", "gla_norm_g", "w_branch",
           "w_out")
SHARD_AXIS = {"ffn1_w_gate": 2, "ffn1_w_up": 2, "ffn1_w_down": 1, "ffn2_w_gate": 2, "ffn2_w_up": 2, "ffn2_w_down": 1,
              "w_in": 2, "w_branch": 3, "w_out": 1, "norm_pre": 2, "norm_post": 2, "conv_w": 2, "gla_w_alpha": 2}


def _size(shape):
    n = 1
    for d in shape:
        n *= d
    return n


def _pack_rows(shape):
    return -(-_size(shape) // (16 * PACK_COLS)) * 16


def _pack(arrs, dtype, row_mult):
    parts, rows = [], 0
    for a in arrs:
        r = _pack_rows(a.shape)
        flat = a.astype(dtype).reshape(-1)
        if r * PACK_COLS != flat.shape[0]:
            flat = jnp.pad(flat, (0, r * PACK_COLS - flat.shape[0]))
        parts.append(flat.reshape(r, PACK_COLS))
        rows += r
    if rows % row_mult:
        parts.append(jnp.zeros((row_mult - rows % row_mult, PACK_COLS), dtype))
    return jnp.concatenate(parts, axis=0)


def _unpack(buf, shapes):
    lead = buf.shape[:-2]
    out, off = [], 0
    for s in shapes:
        r = _pack_rows(s)
        part = lax.slice_in_dim(buf, off, off + r, axis=buf.ndim - 2).reshape(lead + (r * PACK_COLS,))
        if r * PACK_COLS != _size(s):
            part = lax.slice_in_dim(part, 0, _size(s), axis=part.ndim - 1)
        out.append(part.reshape(lead + tuple(s)))
        off += r
    return out


def gather_weights(shards, names, dtype, row_mult, name):
    packed = _pack([shards[k] for k in names], dtype, row_mult)
    parts = _unpack(all_gather_chips(packed, name=name), [shards[k].shape for k in names])
    return {k: jnp.concatenate([p[q] for q in range(4)], axis=SHARD_AXIS[k]) for k, p in zip(names, parts)}


def scatter_grads(grads, shards):
    sharded = BIG + SMALL
    pieces = []
    for q in range(4):
        part = []
        for k in sharded:
            w = shards[k].shape[SHARD_AXIS[k]]
            part.append(lax.slice_in_dim(grads[k], q * w, (q + 1) * w, axis=SHARD_AXIS[k]))
        part += [grads[k] for k in REPLICATED]
        pieces.append(_pack(part, F32, 512))
    total = reduce_scatter(jnp.stack(pieces))
    names = sharded + REPLICATED
    return dict(zip(names, _unpack(total, [shards[k].shape for k in names])))


TM = 256
TM_GLA = 512
TQ = 256

IN_SB, IN_CONV, IN_GLA, IN_LR, IN_GATE = 0, 1536, 2560, 4096, 4112
IN_END = 7184


def layer_weights(full, l):
    w = {}
    for f in ("ffn1", "ffn2"):
        w[f + "_gu"] = jnp.concatenate([full[f + "_w_gate"][l], full[f + "_w_up"][l]], axis=1)
        w[f + "_d"] = full[f + "_w_down"][l]
    win = full["w_in"][l]
    w["in_sb"] = win[:, IN_SB:IN_CONV]
    w["in_conv"] = win[:, IN_CONV:IN_GLA]
    w["in_gla"] = win[:, IN_GLA:IN_LR]
    w["in_lr"] = jnp.pad(win[:, IN_LR:IN_GATE], ((0, 0), (0, GLA_RANK_PAD - GLA_RANK)))
    w["in_gate"] = win[:, IN_GATE:IN_END]
    w["branch"] = [full["w_branch"][l, g] for g in range(3)]
    w["out"] = full["w_out"][l]
    return w


def ffn_fwd(x, gpre, gpost, wgu, wd, tag):
    h = rowwise(f_rms, [x], [gpre], [MXU_DTYPE], tm=TM, name=tag + "_pre")[0]
    ab = mm(h, wgu, tn=512, name=tag + "_gu")
    z = rowwise(f_swiglu, [(ab, D_FF, 0), (ab, D_FF, 1)], [], [MXU_DTYPE], tm=TM, name=tag + "_act")[0]
    f = mm(z, wd, tk=1408, name=tag + "_down")
    x2 = rowwise(f_half_post, [x, f], [gpost], [F32], tm=TM, name=tag + "_post")[0]
    return x2, (x, h, ab, z, f)


def ffn_bwd(dx2, res, gpre, gpost, wgu, wd, tag):
    x, h, ab, z, f = res
    df, dgpost = rowwise_vjp(f_half_post, [x, f], [gpost], [dx2], [1], [0], [MXU_DTYPE], tm=TM, name=tag + "_post_b")
    dz = mm(df, wd, tb=True, tn=1408, name=tag + "_down_dx")
    dwd = mm(z, df, ta=True, tm=1408, name=tag + "_down_dw")
    da, db = rowwise_vjp(f_swiglu, [(ab, D_FF, 0), (ab, D_FF, 1)], [], [dz], [0, 1], [], [MXU_DTYPE, MXU_DTYPE], tm=TM,
                         name=tag + "_act_b")
    dab = jnp.concatenate([da, db], axis=1)
    dh = mm(dab, wgu, tb=True, tk=1408, name=tag + "_gu_dx")
    dwgu = mm(h, dab, ta=True, tn=1408, name=tag + "_gu_dw")
    dx, dgpre = rowwise_vjp(f_rms, [x], [gpre], [dh], [0], [0], [F32], [dx2], tm=TM, name=tag + "_pre_b")
    return dx, dgpre, dgpost, dwgu[:, :D_FF], dwgu[:, D_FF:], dwd


def mixer_fwd(x, p, w, tag):
    h = rowwise(f_rms, [x], [p["gpre"]], [MXU_DTYPE], tm=TM, name=tag + "_pre")[0]
    qkv = mm(h, w["in_sb"], out_dtype=MXU_DTYPE, tn=512, name=tag + "_in_sb")
    cv = mm(h, w["in_conv"], name=tag + "_in_conv")
    gl = mm(h, w["in_gla"], tn=512, name=tag + "_in_gla")
    lr = mm(h, w["in_lr"], out_dtype=MXU_DTYPE, name=tag + "_in_lr")
    gt = mm(h, w["in_gate"], name=tag + "_in_gate")
    sb = sb_attn_fwd(qkv, tq=TQ, name=tag + "_sb")
    u = rowwise(f_glu, [(cv, BRANCH, 0), (cv, BRANCH, 1)], [], [F32], tm=TM, name=tag + "_glu")[0]
    y = conv_fwd(u, p["conv_w"], p["conv_b"], tm=TM, name=tag + "_conv")
    cb = rowwise(f_conv_ln, [y], [p["ln_g"], p["ln_b"]], [MXU_DTYPE], tm=TM, name=tag + "_ln")[0]
    qs, kd, lam = rowwise(f_gla_pre, [(gl, 256, 0), (gl, 256, 1), lr], [p["wa"], p["ba"]], [MXU_DTYPE, MXU_DTYPE, F32],
                          tm=TM_GLA, name=tag + "_gla_pre")
    o, states = gla_scan_fwd(qs, kd, gl, lam, tm=TM_GLA, name=tag + "_gla_scan")
    gb = rowwise(f_gla_post, [o, (gl, BRANCH, 2)], [p["gn"]], [MXU_DTYPE], tm=TM, name=tag + "_gla_post")[0]
    branches = [sb, cb, gb]
    bd = [mm(branches[g], w["branch"][g], name=tag + f"_branch{g}") for g in range(3)]
    merged = rowwise(f_merge, [(gt, D_MODEL, 0), (gt, D_MODEL, 1), (gt, D_MODEL, 2)] + bd, [], [MXU_DTYPE], tm=TM,
                     name=tag + "_merge")[0]
    m = mm(merged, w["out"], name=tag + "_out")
    x2 = rowwise(f_post, [x, m], [p["gpost"]], [F32], tm=TM, name=tag + "_post")[0]
    return x2, (x, h, qkv, cv, gl, lr, gt, u, y, qs, kd, lam, o, states, branches, bd, merged, m)


def mixer_bwd(dx2, res, p, w, tag):
    x, h, qkv, cv, gl, lr, gt, u, y, qs, kd, lam, o, states, branches, bd, merged, m = res
    g = {}
    dm, g["gpost"] = rowwise_vjp(f_post, [x, m], [p["gpost"]], [dx2], [1], [0], [MXU_DTYPE], tm=TM, name=tag + "_post_b")
    dmerged = mm(dm, w["out"], tb=True, name=tag + "_out_dx")
    g["out"] = mm(merged, dm, ta=True, name=tag + "_out_dw")
    gts = [(gt, D_MODEL, 0), (gt, D_MODEL, 1), (gt, D_MODEL, 2)]
    dgate = rowwise_vjp(f_merge, gts + bd, [], [dmerged], [0, 1, 2, 3, 4, 5], [], [MXU_DTYPE] * 6, tm=TM,
                        name=tag + "_merge_b")
    dgt = jnp.concatenate(dgate[:3], axis=1)
    dbd = dgate[3:]
    g["branch"] = [mm(branches[k], dbd[k], ta=True, name=tag + f"_branch{k}_dw") for k in range(3)]
    dsb = mm(dbd[0], w["branch"][0], tb=True, out_dtype=MXU_DTYPE, name=tag + "_branch0_dx")
    dq, dk, dv = sb_attn_bwd(qkv, dsb, tq=TQ, name=tag + "_sb_b")
    dqkv = jnp.concatenate([dq, _mx(dk), _mx(dv)], axis=1)
    dcb = mm(dbd[1], w["branch"][1], tb=True, name=tag + "_branch1_dx")
    dy, g["ln_g"], g["ln_b"] = rowwise_vjp(f_conv_ln, [y], [p["ln_g"], p["ln_b"]], [dcb], [0], [0, 1], [F32], tm=TM,
                                           name=tag + "_ln_b")
    du, dwb = conv_bwd(dy, u, p["conv_w"], tm=TM, name=tag + "_conv_b")
    g["conv_w"], g["conv_b"] = dwb[:CONV_WIDTH], dwb[CONV_WIDTH:CONV_WIDTH + 1]
    dca, dcg = rowwise_vjp(f_glu, [(cv, BRANCH, 0), (cv, BRANCH, 1)], [], [du], [0, 1], [], [MXU_DTYPE, MXU_DTYPE], tm=TM,
                           name=tag + "_glu_b")
    dcv = jnp.concatenate([dca, dcg], axis=1)
    dgb = mm(dbd[2], w["branch"][2], tb=True, name=tag + "_branch2_dx")
    do, dr, g["gn"] = rowwise_vjp(f_gla_post, [o, (gl, BRANCH, 2)], [p["gn"]], [dgb], [0, 1], [0], [F32, MXU_DTYPE], tm=TM,
                                  name=tag + "_gla_post_b")
    dqs, dkd, dgv, dlam = gla_scan_bwd(do, qs, kd, gl, lam, states, tm=TM_GLA, name=tag + "_gla_scan_b")
    dgq, dgk, dlr, g["wa"], g["ba"] = rowwise_vjp(
        f_gla_pre, [(gl, 256, 0), (gl, 256, 1), lr], [p["wa"], p["ba"]], [dqs, dkd, dlam], [0, 1, 2], [0, 1],
        [MXU_DTYPE, MXU_DTYPE, MXU_DTYPE], tm=TM_GLA, name=tag + "_gla_pre_b")
    dgl = jnp.concatenate([dgq, dgk, _mx(dgv), dr], axis=1)
    secs = [("in_sb", dqkv), ("in_conv", dcv), ("in_gla", dgl), ("in_lr", dlr), ("in_gate", dgt)]
    dh = None
    for k, d in secs:
        dh = mm(d, w[k], dh, tb=True, name=tag + "_" + k + "_dx")
        g[k] = mm(h, d, ta=True, tn=1536, name=tag + "_" + k + "_dw")
    dx, g["gpre"] = rowwise_vjp(f_rms, [x], [p["gpre"]], [dh], [0], [0], [F32], [dx2], tm=TM, name=tag + "_pre_b")
    return dx, g


def kernel(x, norm_pre, norm_post, ffn1_w_gate, ffn1_w_up, ffn1_w_down, ffn2_w_gate, ffn2_w_up, ffn2_w_down, w_in, conv_w, conv_b, conv_ln_g, conv_ln_b, gla_w_alpha, gla_b_alpha, gla_norm_g, w_branch, w_out, loss_target, m_norm_pre, m_norm_post, m_ffn1_w_gate, m_ffn1_w_up, m_ffn1_w_down, m_ffn2_w_gate, m_ffn2_w_up, m_ffn2_w_down, m_w_in, m_conv_w, m_conv_b, m_conv_ln_g, m_conv_ln_b, m_gla_w_alpha, m_gla_b_alpha, m_gla_norm_g, m_w_branch, m_w_out, v_norm_pre, v_norm_post, v_ffn1_w_gate, v_ffn1_w_up, v_ffn1_w_down, v_ffn2_w_gate, v_ffn2_w_up, v_ffn2_w_down, v_w_in, v_conv_w, v_conv_b, v_conv_ln_g, v_conv_ln_b, v_gla_w_alpha, v_gla_b_alpha, v_gla_norm_g, v_w_branch, v_w_out):
    shards = dict(norm_pre=norm_pre, norm_post=norm_post, ffn1_w_gate=ffn1_w_gate, ffn1_w_up=ffn1_w_up,
                  ffn1_w_down=ffn1_w_down, ffn2_w_gate=ffn2_w_gate, ffn2_w_up=ffn2_w_up, ffn2_w_down=ffn2_w_down, w_in=w_in,
                  conv_w=conv_w, conv_b=conv_b, conv_ln_g=conv_ln_g, conv_ln_b=conv_ln_b, gla_w_alpha=gla_w_alpha,
                  gla_b_alpha=gla_b_alpha, gla_norm_g=gla_norm_g, w_branch=w_branch, w_out=w_out)
    mom_m = dict(zip(WEIGHTS, (m_norm_pre, m_norm_post, m_ffn1_w_gate, m_ffn1_w_up, m_ffn1_w_down, m_ffn2_w_gate,
                               m_ffn2_w_up, m_ffn2_w_down, m_w_in, m_conv_w, m_conv_b, m_conv_ln_g, m_conv_ln_b,
                               m_gla_w_alpha, m_gla_b_alpha, m_gla_norm_g, m_w_branch, m_w_out)))
    mom_v = dict(zip(WEIGHTS, (v_norm_pre, v_norm_post, v_ffn1_w_gate, v_ffn1_w_up, v_ffn1_w_down, v_ffn2_w_gate,
                               v_ffn2_w_up, v_ffn2_w_down, v_w_in, v_conv_w, v_conv_b, v_conv_ln_g, v_conv_ln_b,
                               v_gla_w_alpha, v_gla_b_alpha, v_gla_norm_g, v_w_branch, v_w_out)))
    depth = norm_pre.shape[0]
    full = gather_weights(shards, BIG, MXU_DTYPE, 256, "gather_big")
    full.update(gather_weights(shards, SMALL, F32, 16, "gather_small"))

    def layer_params(l):
        ffn = [dict(gpre=full["norm_pre"][l, k:k + 1], gpost=full["norm_post"][l, k:k + 1]) for k in (0, 2)]
        mix = dict(gpre=full["norm_pre"][l, 1:2], gpost=full["norm_post"][l, 1:2],
                   conv_w=jnp.pad(full["conv_w"][l], ((0, CONV_PAD - CONV_WIDTH), (0, 0))), conv_b=conv_b[l:l + 1],
                   ln_g=conv_ln_g[l:l + 1], ln_b=conv_ln_b[l:l + 1],
                   wa=jnp.pad(full["gla_w_alpha"][l], ((0, GLA_RANK_PAD - GLA_RANK), (0, 0))), ba=gla_b_alpha[l:l + 1],
                   gn=gla_norm_g[l:l + 1])
        return ffn, mix

    xs = x[0]
    saved = []
    for l in range(depth):
        w = layer_weights(full, l)
        ffn, mix = layer_params(l)
        xs, r1 = ffn_fwd(xs, ffn[0]["gpre"], ffn[0]["gpost"], w["ffn1_gu"], w["ffn1_d"], f"l{l}_ffn1")
        xs, r2 = mixer_fwd(xs, mix, w, f"l{l}_mix")
        xs, r3 = ffn_fwd(xs, ffn[1]["gpre"], ffn[1]["gpost"], w["ffn2_gu"], w["ffn2_d"], f"l{l}_ffn2")
        saved.append((w, ffn, mix, r1, r2, r3))
    dx, sq = loss_head(xs, loss_target[0], tm=TM, name="loss_head")
    loss = lax.psum(0.5 * jnp.sum(sq) / D_MODEL, ("x", "y", "c"))

    per_layer = []
    for l in reversed(range(depth)):
        w, ffn, mix, r1, r2, r3 = saved[l]
        g = {}
        dx, gpre2, gpost2, g["ffn2_w_gate"], g["ffn2_w_up"], g["ffn2_w_down"] = ffn_bwd(
            dx, r3, ffn[1]["gpre"], ffn[1]["gpost"], w["ffn2_gu"], w["ffn2_d"], f"l{l}_ffn2")
        dx, gm = mixer_bwd(dx, r2, mix, w, f"l{l}_mix")
        dx, gpre0, gpost0, g["ffn1_w_gate"], g["ffn1_w_up"], g["ffn1_w_down"] = ffn_bwd(
            dx, r1, ffn[0]["gpre"], ffn[0]["gpost"], w["ffn1_gu"], w["ffn1_d"], f"l{l}_ffn1")
        g["norm_pre"] = jnp.concatenate([gpre0, gm["gpre"], gpre2], axis=0)
        g["norm_post"] = jnp.concatenate([gpost0, gm["gpost"], gpost2], axis=0)
        g["w_in"] = jnp.concatenate([gm["in_sb"], gm["in_conv"], gm["in_gla"], gm["in_lr"][:, :GLA_RANK], gm["in_gate"]],
                                    axis=1)
        g["conv_w"], g["conv_b"] = gm["conv_w"], gm["conv_b"][0]
        g["conv_ln_g"], g["conv_ln_b"] = gm["ln_g"][0], gm["ln_b"][0]
        g["gla_w_alpha"], g["gla_b_alpha"], g["gla_norm_g"] = gm["wa"][:GLA_RANK], gm["ba"][0], gm["gn"][0]
        g["w_branch"] = jnp.stack(gm["branch"])
        g["w_out"] = gm["out"]
        per_layer.append(g)
    per_layer.reverse()
    grads = {k: jnp.stack([g[k] for g in per_layer]) for k in WEIGHTS}

    grad_w = scatter_grads(grads, shards)
    delta, new_m, new_v = {}, {}, {}
    for k in WEIGHTS:
        delta[k], new_m[k], new_v[k] = adamw(shards[k], grad_w[k], mom_m[k], mom_v[k], name="adamw_" + k)
    return (loss, dx[None], *[grad_w[k] for k in WEIGHTS], *[delta[k] for k in WEIGHTS], *[new_m[k] for k in WEIGHTS],
            *[new_v[k] for k in WEIGHTS])
```

```python
import functools

import jax
import jax.numpy as jnp
from jax import lax
from jax.experimental import pallas as pl
from jax.experimental.pallas import tpu as pltpu

F32 = jnp.float32
MXU_DTYPE = jnp.bfloat16
GRAD_WIRE_DTYPE = jnp.bfloat16
HIGHEST = lax.Precision.HIGHEST
MESH = pl.DeviceIdType.MESH

NORM_EPS = 1e-6
D_MODEL = 1024
D_FF = 2816
BRANCH = 512
CHUNK = 64
CONV_WIDTH = 31
CONV_PAD = 32
GLA_RANK = 16
GLA_RANK_PAD = 128
GLA_TAU = 16.0
SB_SCALE = 0.125
SB_CUTOFF = 60.0
GLA_SCALE = 0.125
PACK_COLS = 1024
VMEM_LIMIT = 56 * 1024 * 1024

ADAM_LR, ADAM_B1, ADAM_B2, ADAM_EPS, ADAM_WD, ADAM_STEP = 0.001, 0.9, 0.999, 1e-08, 0.01, 10

NT = (((1,), (1,)), ((), ()))
TN = (((0,), (0,)), ((), ()))
NN = (((1,), (0,)), ((), ()))


def _params(sem=None, vmem=None):
    return pltpu.CompilerParams(dimension_semantics=sem, vmem_limit_bytes=vmem)


def _mx(v):
    return v.astype(MXU_DTYPE)


def _mx_round(v):
    return v.astype(MXU_DTYPE).astype(F32)


def _fit(dim, want):
    if dim <= want:
        return dim
    for d in range(want - want % 128, 0, -128):
        if dim % d == 0:
            return d
    raise ValueError((dim, want))


def mm(a, b, c=None, *, ta=False, tb=False, out_dtype=F32, tm=1024, tn=1024, tk=1024, name):
    K, M = a.shape if ta else a.shape[::-1]
    N = b.shape[0] if tb else b.shape[1]
    assert (b.shape[1] if tb else b.shape[0]) == K, (a.shape, b.shape, ta, tb)
    tm, tn, tk = _fit(M, tm), _fit(N, tn), _fit(K, tk)
    nk = K // tk
    dn = (((0 if ta else 1,), (1 if tb else 0,)), ((), ()))

    def body(*refs):
        if c is None:
            a_ref, b_ref, o_ref, acc = refs
            c_ref = None
        else:
            a_ref, b_ref, c_ref, o_ref, acc = refs
        k = pl.program_id(2)
        p = lax.dot_general(_mx(a_ref[...]), _mx(b_ref[...]), dn, preferred_element_type=F32)

        @pl.when(k == 0)
        def _():
            acc[...] = p

        @pl.when(k > 0)
        def _():
            acc[...] += p

        @pl.when(k == nk - 1)
        def _():
            r = acc[...]
            if c_ref is not None:
                r = r + c_ref[...].astype(F32)
            o_ref[...] = r.astype(o_ref.dtype)

    a_spec = pl.BlockSpec((tk, tm), lambda i, j, k: (k, i)) if ta else pl.BlockSpec((tm, tk), lambda i, j, k: (i, k))
    b_spec = pl.BlockSpec((tn, tk), lambda i, j, k: (j, k)) if tb else pl.BlockSpec((tk, tn), lambda i, j, k: (k, j))
    o_spec = pl.BlockSpec((tm, tn), lambda i, j, k: (i, j))
    ins, in_specs = [a, b], [a_spec, b_spec]
    if c is not None:
        ins.append(c)
        in_specs.append(o_spec)
    return pl.pallas_call(
        body, name=name, grid=(M // tm, N // tn, nk), in_specs=in_specs, out_specs=o_spec,
        out_shape=jax.ShapeDtypeStruct((M, N), out_dtype), scratch_shapes=[pltpu.VMEM((tm, tn), F32)],
        compiler_params=_params(("parallel", "parallel", "arbitrary"), VMEM_LIMIT),
    )(*ins)


def _row_arg(arg):
    if isinstance(arg, tuple):
        return arg
    return arg, arg.shape[1], 0


def _row_specs(rows, n):
    arrs, specs, avals = [], [], []
    for arg in rows:
        arr, width, cb = _row_arg(arg)
        rb = arr.shape[0] // n
        arrs.append(arr)
        specs.append(pl.BlockSpec((rb, width), lambda i, cb=cb: (i, cb)))
        avals.append(jax.ShapeDtypeStruct((rb, width), F32))
    return arrs, specs, avals


def _par_specs(pars):
    specs = [pl.BlockSpec(p.shape, lambda i, nd=p.ndim: (0,) * nd) for p in pars]
    avals = [jax.ShapeDtypeStruct(p.shape, F32) for p in pars]
    return specs, avals


def rowwise(f, rows, pars, out_dtypes, *, tm, name):
    n = _row_arg(rows[0])[0].shape[0] // tm
    arrs, rspecs, ravals = _row_specs(rows, n)
    pspecs, pavals = _par_specs(pars)
    oavals = jax.eval_shape(f, *ravals, *pavals)
    nin = len(arrs) + len(pars)

    def body(*refs):
        outs = f(*[r[...].astype(F32) for r in refs[:nin]])
        for o_ref, o in zip(refs[nin:], outs):
            o_ref[...] = o.astype(o_ref.dtype)

    return pl.pallas_call(
        body, name=name, grid=(n,), in_specs=rspecs + pspecs,
        out_specs=[pl.BlockSpec(o.shape, lambda i: (i, 0)) for o in oavals],
        out_shape=[jax.ShapeDtypeStruct((n * o.shape[0], o.shape[1]), dt) for o, dt in zip(oavals, out_dtypes)],
        compiler_params=_params(("parallel",), VMEM_LIMIT),
    )(*arrs, *pars)


def rowwise_vjp(f, rows, pars, cots, row_grad, par_grad, d_dtypes, adds=None, *, tm, name):
    n = _row_arg(rows[0])[0].shape[0] // tm
    arrs, rspecs, ravals = _row_specs(rows, n)
    pspecs, pavals = _par_specs(pars)
    carrs, cspecs, _ = _row_specs(cots, n)
    adds = adds or [None] * len(row_grad)
    add_arrs = [a for a in adds if a is not None]
    _, aspecs, _ = _row_specs(add_arrs, n)
    nr, npar, nc, na = len(arrs), len(pars), len(carrs), len(add_arrs)
    diff = list(row_grad) + [nr + j for j in par_grad]
    ngr = len(row_grad)

    def body(*refs):
        ins = [r[...].astype(F32) for r in refs[:nr + npar]]
        cs = tuple(r[...].astype(F32) for r in refs[nr + npar:nr + npar + nc])
        add_refs = list(refs[nr + npar + nc:nr + npar + nc + na])
        outs = refs[nr + npar + nc + na:]

        def g(*d):
            full = list(ins)
            for idx, val in zip(diff, d):
                full[idx] = val
            return f(*full)

        _, pullback = jax.vjp(g, *[ins[idx] for idx in diff])
        ds = pullback(cs)
        for k in range(ngr):
            d = ds[k]
            if adds[k] is not None:
                d = d + add_refs.pop(0)[...].astype(F32)
            outs[k][...] = d.astype(outs[k].dtype)
        i = pl.program_id(0)
        for k in range(ngr, len(diff)):
            @pl.when(i == 0)
            def _(k=k):
                outs[k][...] = ds[k]

            @pl.when(i > 0)
            def _(k=k):
                outs[k][...] += ds[k]

    out_specs = [pl.BlockSpec(ravals[i].shape, lambda i: (i, 0)) for i in row_grad] + [pspecs[j] for j in par_grad]
    out_shape = [jax.ShapeDtypeStruct((arrs[i].shape[0], ravals[i].shape[1]), dt) for i, dt in zip(row_grad, d_dtypes)]
    out_shape += [jax.ShapeDtypeStruct(pars[j].shape, F32) for j in par_grad]
    return pl.pallas_call(
        body, name=name, grid=(n,), in_specs=rspecs + pspecs + cspecs + aspecs, out_specs=out_specs, out_shape=out_shape,
        compiler_params=_params(("arbitrary",), VMEM_LIMIT),
    )(*arrs, *pars, *carrs, *add_arrs)


def _logsig(x):
    return jnp.minimum(x, 0.0) - jnp.log(1.0 + jnp.exp(-jnp.abs(x)))


def _sigmoid(x):
    return 1.0 / (1.0 + jnp.exp(-x))


def _silu(x):
    return x * _sigmoid(x)


def _rms(x, g):
    return x * lax.rsqrt(jnp.mean(x * x, axis=-1, keepdims=True) + NORM_EPS) * g


def f_rms(x, g):
    return (_rms(x, g),)


def f_swiglu(a, b):
    return (_silu(a) * b,)


def f_half_post(x, f, g):
    return (x + 0.5 * _rms(f, g),)


def f_post(x, m, g):
    return (x + _rms(m, g),)


def f_glu(a, g):
    return (a * _sigmoid(g),)


def f_conv_ln(y, lg, lb):
    mu = jnp.mean(y, axis=-1, keepdims=True)
    var = jnp.mean(jnp.square(y - mu), axis=-1, keepdims=True)
    return (_silu((y - mu) * lax.rsqrt(var + NORM_EPS) * lg + lb),)


def f_merge(g0, g1, g2, b0, b1, b2):
    return (_sigmoid(g0) * b0 + _sigmoid(g1) * b1 + _sigmoid(g2) * b2,)


def f_gla_post(o, r, g):
    w = o.shape[1]
    hv = w // 4
    i = lax.broadcasted_iota(jnp.int32, (w, w), 0) // hv
    j = lax.broadcasted_iota(jnp.int32, (w, w), 1) // hv
    avg = jnp.where(i == j, 1.0 / hv, 0.0).astype(F32)
    ms = jnp.dot(o * o, avg, precision=HIGHEST, preferred_element_type=F32)
    return (o * lax.rsqrt(ms + NORM_EPS) * g * _silu(r),)


def f_gla_pre(q, k, lr, wa, ba):
    tm = q.shape[0]
    pre = jnp.dot(_mx_round(lr), _mx_round(wa), precision=HIGHEST, preferred_element_type=F32) + ba
    la = _logsig(pre) / GLA_TAU
    i = lax.broadcasted_iota(jnp.int32, (tm, tm), 0)
    j = lax.broadcasted_iota(jnp.int32, (tm, tm), 1)
    later = jnp.where((i // CHUNK == j // CHUNK) & (j > i), 1.0, 0.0).astype(F32)
    ci = lax.broadcasted_iota(jnp.int32, (tm // CHUNK, tm), 0)
    cj = lax.broadcasted_iota(jnp.int32, (tm // CHUNK, tm), 1) // CHUNK
    chunk_sum = jnp.where(ci == cj, 1.0, 0.0).astype(F32)
    to_end = jnp.dot(later, la, precision=HIGHEST, preferred_element_type=F32)
    lam = jnp.exp(jnp.dot(chunk_sum, la, precision=HIGHEST, preferred_element_type=F32))
    return q * GLA_SCALE, k * jnp.exp(to_end), lam


def _split_dot(x, u):
    hi = _mx(x)
    lo = _mx(x - hi.astype(F32))
    return jnp.dot(hi, u, preferred_element_type=F32) + jnp.dot(lo, u, preferred_element_type=F32)


def _tri(tq, tk):
    row = lax.broadcasted_iota(jnp.int32, (tq, tk), 0)
    col = lax.broadcasted_iota(jnp.int32, (tq, tk), 1)
    return row, col


def _sb_tile(qh, kh, valid, suf, run):
    z = lax.dot_general(qh, kh, NT, preferred_element_type=F32) * SB_SCALE
    lp = jnp.minimum(z, 0.0) - jnp.log(1.0 + jnp.exp(-jnp.abs(z)))
    lk = jnp.where(valid, lp - z, 0.0)
    inc = _split_dot(lk, suf)
    a = jnp.where(valid, jnp.exp(lp + (inc - lk + run)), 0.0)
    return lp, a, run + inc[:, 0:1]


def _sticks_left(run0, run1):
    return (jnp.maximum(jnp.max(run0), jnp.max(run1)) > -SB_CUTOFF).astype(jnp.int32)


def sb_attn_fwd(qkv, *, tq, name):
    S = qkv.shape[0]
    tk = tq
    pairs = BRANCH // 128

    def body(q_ref, k_ref, v_ref, o_ref):
        i = pl.program_id(1)
        row, col = _tri(tq, tk)
        suf = _mx(row >= col)
        q = q_ref[...]

        def step(state):
            j, _, carry = state
            ks = pl.multiple_of((i - j) * tk, tk)
            kb = k_ref[pl.ds(ks, tk), :]
            vb = v_ref[pl.ds(ks, tk), :]
            valid = (col - row) < j * tq
            new = []
            for h in range(2):
                acc, run = carry[h]
                sl = slice(64 * h, 64 * h + 64)
                _, a, run = _sb_tile(q[:, sl], kb[:, sl], valid, suf, run)
                acc = acc + jnp.dot(_mx(a), vb[:, sl], preferred_element_type=F32)
                new.append((acc, run))
            return j + 1, _sticks_left(new[0][1], new[1][1]), tuple(new)

        zero = (jnp.zeros((tq, 64), F32), jnp.zeros((tq, 1), F32))
        _, _, res = lax.while_loop(lambda s: (s[0] <= i) & (s[1] > 0), step, (jnp.int32(0), jnp.int32(1), (zero, zero)))
        o_ref[...] = jnp.concatenate([res[0][0], res[1][0]], axis=1).astype(o_ref.dtype)

    return pl.pallas_call(
        body, name=name, grid=(pairs, S // tq),
        in_specs=[pl.BlockSpec((tq, 128), lambda p, i: (i, p)),
                  pl.BlockSpec((S, 128), lambda p, i: (0, pairs + p)),
                  pl.BlockSpec((S, 128), lambda p, i: (0, 2 * pairs + p))],
        out_specs=pl.BlockSpec((tq, 128), lambda p, i: (i, p)),
        out_shape=jax.ShapeDtypeStruct((S, BRANCH), MXU_DTYPE),
        compiler_params=_params(("parallel", "parallel"), VMEM_LIMIT),
    )(qkv, qkv, qkv)


def sb_attn_bwd(qkv, do, *, tq, name):
    S = qkv.shape[0]
    tk = tq
    nq = S // tq
    pairs = BRANCH // 128

    def body(q_ref, k_ref, v_ref, do_ref, dq_ref, dk_ref, dv_ref, g_sc, b_sc):
        i = pl.program_id(1)

        @pl.when(i == 0)
        def _():
            dk_ref[...] = jnp.zeros_like(dk_ref)
            dv_ref[...] = jnp.zeros_like(dv_ref)

        row, col = _tri(tq, tk)
        suf = _mx(row >= col)
        pre = _mx(row <= col)
        q = q_ref[...]
        dout = do_ref[...]

        def sweep1(state):
            j, _, carry = state
            kblk = i - j
            ks = pl.multiple_of(kblk * tk, tk)
            kb = k_ref[pl.ds(ks, tk), :]
            vb = v_ref[pl.ds(ks, tk), :]
            valid = (col - row) < j * tq
            runs, dvs = [], []
            for h in range(2):
                sl = slice(64 * h, 64 * h + 64)
                lp, a, run = _sb_tile(q[:, sl], kb[:, sl], valid, suf, carry[h])
                da = lax.dot_general(dout[:, sl], vb[:, sl], NT, preferred_element_type=F32)
                g_sc[h, kblk] = (a * da).astype(g_sc.dtype)
                b_sc[h, kblk] = jnp.where(valid, jnp.exp(lp), 0.0).astype(b_sc.dtype)
                dvs.append(lax.dot_general(_mx(a), dout[:, sl], TN, preferred_element_type=F32))
                runs.append(run)
            dv_ref[pl.ds(ks, tk), :] += jnp.concatenate(dvs, axis=1)
            return j + 1, _sticks_left(runs[0], runs[1]), tuple(runs)

        start = (jnp.int32(0), jnp.int32(1), (jnp.zeros((tq, 1), F32), jnp.zeros((tq, 1), F32)))
        tiles, _, _ = lax.while_loop(lambda s: (s[0] <= i) & (s[1] > 0), sweep1, start)

        def sweep2(kblk, carry):
            ks = pl.multiple_of(kblk * tk, tk)
            kb = k_ref[pl.ds(ks, tk), :]
            new, dks = [], []
            for h in range(2):
                dq, run = carry[h]
                sl = slice(64 * h, 64 * h + 64)
                g = g_sc[h, kblk]
                beta = b_sc[h, kblk].astype(F32)
                inc = jnp.dot(g, pre, preferred_element_type=F32)
                g = g.astype(F32)
                dz = _mx((g - beta * (inc + run)) * SB_SCALE)
                dq = dq + jnp.dot(dz, kb[:, sl], preferred_element_type=F32)
                dks.append(lax.dot_general(dz, q[:, sl], TN, preferred_element_type=F32))
                new.append((dq, run + inc[:, tk - 1:tk]))
            dk_ref[pl.ds(ks, tk), :] += jnp.concatenate(dks, axis=1)
            return tuple(new)

        zero = (jnp.zeros((tq, 64), F32), jnp.zeros((tq, 1), F32))
        res = lax.fori_loop(i + 1 - tiles, i + 1, sweep2, (zero, zero))
        dq_ref[...] = jnp.concatenate([res[0][0], res[1][0]], axis=1).astype(dq_ref.dtype)

    return pl.pallas_call(
        body, name=name, grid=(pairs, nq),
        in_specs=[pl.BlockSpec((tq, 128), lambda p, i: (i, p)),
                  pl.BlockSpec((S, 128), lambda p, i: (0, pairs + p)),
                  pl.BlockSpec((S, 128), lambda p, i: (0, 2 * pairs + p)),
                  pl.BlockSpec((tq, 128), lambda p, i: (i, p))],
        out_specs=[pl.BlockSpec((tq, 128), lambda p, i: (i, p)),
                   pl.BlockSpec((S, 128), lambda p, i: (0, p)),
                   pl.BlockSpec((S, 128), lambda p, i: (0, p))],
        out_shape=[jax.ShapeDtypeStruct((S, BRANCH), MXU_DTYPE), jax.ShapeDtypeStruct((S, BRANCH), F32),
                   jax.ShapeDtypeStruct((S, BRANCH), F32)],
        scratch_shapes=[pltpu.VMEM((2, nq, tq, tk), MXU_DTYPE), pltpu.VMEM((2, nq, tq, tk), MXU_DTYPE)],
        compiler_params=_params(("parallel", "arbitrary"), VMEM_LIMIT),
    )(qkv, qkv, qkv, do)


def conv_fwd(u, w, b, *, tm, name):
    S, C = u.shape
    hb = tm // CONV_PAD

    def body(u_ref, halo_ref, w_ref, b_ref, y_ref, buf):
        i = pl.program_id(0)
        buf[pl.ds(CONV_PAD, tm), :] = u_ref[...]
        buf[pl.ds(0, CONV_PAD), :] = jnp.where(i > 0, halo_ref[...], 0.0)
        acc = jnp.broadcast_to(b_ref[...], (tm, 128))
        for j in range(CONV_WIDTH):
            acc = acc + buf[pl.ds(CONV_PAD - (CONV_WIDTH - 1) + j, tm), :] * w_ref[pl.ds(j, 1), :]
        y_ref[...] = acc

    return pl.pallas_call(
        body, name=name, grid=(S // tm, C // 128),
        in_specs=[pl.BlockSpec((tm, 128), lambda i, c: (i, c)),
                  pl.BlockSpec((CONV_PAD, 128), lambda i, c: (jnp.maximum(i * hb - 1, 0), c)),
                  pl.BlockSpec((CONV_PAD, 128), lambda i, c: (0, c)),
                  pl.BlockSpec((1, 128), lambda i, c: (0, c))],
        out_specs=pl.BlockSpec((tm, 128), lambda i, c: (i, c)),
        out_shape=jax.ShapeDtypeStruct((S, C), F32),
        scratch_shapes=[pltpu.VMEM((tm + CONV_PAD, 128), F32)],
        compiler_params=_params(("parallel", "parallel")),
    )(u, u, w, b)


def conv_bwd(dy, u, w, *, tm, name):
    S, C = u.shape
    hb = tm // CONV_PAD
    n = S // tm

    def body(dy_ref, dyn_ref, u_ref, up_ref, w_ref, du_ref, dw_ref, bufy, bufu):
        i = pl.program_id(1)
        dyv = dy_ref[...]
        bufy[pl.ds(0, tm), :] = dyv
        bufy[pl.ds(tm, CONV_PAD), :] = jnp.where(i < n - 1, dyn_ref[...], 0.0)
        bufu[pl.ds(CONV_PAD, tm), :] = u_ref[...]
        bufu[pl.ds(0, CONV_PAD), :] = jnp.where(i > 0, up_ref[...], 0.0)

        @pl.when(i == 0)
        def _():
            dw_ref[...] = jnp.zeros_like(dw_ref)

        acc = jnp.zeros((tm, 128), F32)
        for j in range(CONV_WIDTH):
            acc = acc + bufy[pl.ds(CONV_WIDTH - 1 - j, tm), :] * w_ref[pl.ds(j, 1), :]
            shifted = bufu[pl.ds(CONV_PAD - (CONV_WIDTH - 1) + j, tm), :]
            dw_ref[pl.ds(j, 1), :] += jnp.sum(dyv * shifted, axis=0, keepdims=True)
        du_ref[...] = acc
        dw_ref[pl.ds(CONV_WIDTH, 1), :] += jnp.sum(dyv, axis=0, keepdims=True)

    return pl.pallas_call(
        body, name=name, grid=(C // 128, n),
        in_specs=[pl.BlockSpec((tm, 128), lambda c, i: (i, c)),
                  pl.BlockSpec((CONV_PAD, 128), lambda c, i: (jnp.minimum((i + 1) * hb, n * hb - 1), c)),
                  pl.BlockSpec((tm, 128), lambda c, i: (i, c)),
                  pl.BlockSpec((CONV_PAD, 128), lambda c, i: (jnp.maximum(i * hb - 1, 0), c)),
                  pl.BlockSpec((CONV_PAD, 128), lambda c, i: (0, c))],
        out_specs=[pl.BlockSpec((tm, 128), lambda c, i: (i, c)), pl.BlockSpec((CONV_PAD, 128), lambda c, i: (0, c))],
        out_shape=[jax.ShapeDtypeStruct((S, C), F32), jax.ShapeDtypeStruct((CONV_PAD, C), F32)],
        scratch_shapes=[pltpu.VMEM((tm + CONV_PAD, 128), F32), pltpu.VMEM((tm + CONV_PAD, 128), F32)],
        compiler_params=_params(("parallel", "arbitrary")),
    )(dy, dy, u, u, w)


GLA_HEADS, GLA_DK, GLA_DV = 4, 64, 128


def gla_scan_fwd(qs, kd, gl, lam, *, tm, name):
    S = qs.shape[0]
    nc = tm // CHUNK

    def body(qs_ref, kd_ref, v_ref, lam_ref, o_ref, st_ref, state):
        @pl.when(pl.program_id(0) == 0)
        def _():
            state[...] = jnp.zeros_like(state)

        for c in range(nc):
            rows = pl.ds(c * CHUNK, CHUNK)
            q, k, v = _mx(qs_ref[rows, :]), _mx(kd_ref[rows, :]), _mx(v_ref[rows, :])
            upd = [lax.dot_general(v[:, h * GLA_DV:(h + 1) * GLA_DV], k[:, h * GLA_DK:(h + 1) * GLA_DK], TN,
                                   preferred_element_type=F32) for h in range(GLA_HEADS)]
            st = state[...] * lam_ref[pl.ds(c, 1), :] + jnp.concatenate(upd, axis=1)
            state[...] = st
            st_ref[c] = st
            stm = _mx(st)
            o = [lax.dot_general(q[:, h * GLA_DK:(h + 1) * GLA_DK], stm[:, h * GLA_DK:(h + 1) * GLA_DK], NT,
                                 preferred_element_type=F32) for h in range(GLA_HEADS)]
            o_ref[rows, :] = jnp.concatenate(o, axis=1)

    dk_all = GLA_HEADS * GLA_DK
    dv_all = GLA_HEADS * GLA_DV
    return pl.pallas_call(
        body, name=name, grid=(S // tm,),
        in_specs=[pl.BlockSpec((tm, dk_all), lambda i: (i, 0)), pl.BlockSpec((tm, dk_all), lambda i: (i, 0)),
                  pl.BlockSpec((tm, dv_all), lambda i: (i, 1)), pl.BlockSpec((nc, dk_all), lambda i: (i, 0))],
        out_specs=[pl.BlockSpec((tm, dv_all), lambda i: (i, 0)), pl.BlockSpec((nc, GLA_DV, dk_all), lambda i: (i, 0, 0))],
        out_shape=[jax.ShapeDtypeStruct((S, dv_all), F32), jax.ShapeDtypeStruct((S // CHUNK, GLA_DV, dk_all), F32)],
        scratch_shapes=[pltpu.VMEM((GLA_DV, dk_all), F32)],
        compiler_params=_params(("arbitrary",)),
    )(qs, kd, gl, lam)


def gla_scan_bwd(do, qs, kd, gl, lam, states, *, tm, name):
    S = qs.shape[0]
    nc = tm // CHUNK
    n = S // tm
    dk_all = GLA_HEADS * GLA_DK
    dv_all = GLA_HEADS * GLA_DV

    def body(do_ref, qs_ref, kd_ref, v_ref, lam_ref, st_ref, prev_ref, dqs_ref, dkd_ref, dv_ref, dlam_ref, carry):
        i = pl.program_id(0)

        @pl.when(i == 0)
        def _():
            carry[...] = jnp.zeros_like(carry)

        for c in reversed(range(nc)):
            rows = pl.ds(c * CHUNK, CHUNK)
            q, k, v, d = _mx(qs_ref[rows, :]), _mx(kd_ref[rows, :]), _mx(v_ref[rows, :]), _mx(do_ref[rows, :])
            st = _mx(st_ref[c])
            before = st_ref[c - 1] if c > 0 else jnp.where(i < n - 1, prev_ref[0], 0.0)
            outer = [lax.dot_general(d[:, h * GLA_DV:(h + 1) * GLA_DV], q[:, h * GLA_DK:(h + 1) * GLA_DK], TN,
                                     preferred_element_type=F32) for h in range(GLA_HEADS)]
            dst = carry[...] + jnp.concatenate(outer, axis=1)
            dstm = _mx(dst)
            dq, dkk, dvv = [], [], []
            for h in range(GLA_HEADS):
                ksl = slice(h * GLA_DK, (h + 1) * GLA_DK)
                vsl = slice(h * GLA_DV, (h + 1) * GLA_DV)
                dq.append(jnp.dot(d[:, vsl], st[:, ksl], preferred_element_type=F32))
                dkk.append(jnp.dot(v[:, vsl], dstm[:, ksl], preferred_element_type=F32))
                dvv.append(lax.dot_general(k[:, ksl], dstm[:, ksl], NT, preferred_element_type=F32))
            dqs_ref[rows, :] = jnp.concatenate(dq, axis=1)
            dkd_ref[rows, :] = jnp.concatenate(dkk, axis=1)
            dv_ref[rows, :] = jnp.concatenate(dvv, axis=1)
            dlam_ref[pl.ds(c, 1), :] = jnp.sum(dst * before, axis=0, keepdims=True)
            carry[...] = dst * lam_ref[pl.ds(c, 1), :]

    rev = lambda i: n - 1 - i
    return pl.pallas_call(
        body, name=name, grid=(n,),
        in_specs=[pl.BlockSpec((tm, dv_all), lambda i: (rev(i), 0)), pl.BlockSpec((tm, dk_all), lambda i: (rev(i), 0)),
                  pl.BlockSpec((tm, dk_all), lambda i: (rev(i), 0)), pl.BlockSpec((tm, dv_all), lambda i: (rev(i), 1)),
                  pl.BlockSpec((nc, dk_all), lambda i: (rev(i), 0)),
                  pl.BlockSpec((nc, GLA_DV, dk_all), lambda i: (rev(i), 0, 0)),
                  pl.BlockSpec((1, GLA_DV, dk_all), lambda i: (jnp.maximum(rev(i) * nc - 1, 0), 0, 0))],
        out_specs=[pl.BlockSpec((tm, dk_all), lambda i: (rev(i), 0)), pl.BlockSpec((tm, dk_all), lambda i: (rev(i), 0)),
                   pl.BlockSpec((tm, dv_all), lambda i: (rev(i), 0)), pl.BlockSpec((nc, dk_all), lambda i: (rev(i), 0))],
        out_shape=[jax.ShapeDtypeStruct((S, dk_all), F32), jax.ShapeDtypeStruct((S, dk_all), F32),
                   jax.ShapeDtypeStruct((S, dv_all), F32), jax.ShapeDtypeStruct((S // CHUNK, dk_all), F32)],
        scratch_shapes=[pltpu.VMEM((GLA_DV, dk_all), F32)],
        compiler_params=_params(("arbitrary",)),
    )(do, qs, kd, gl, lam, states, states)


def loss_head(y, target, *, tm, name):
    S, D = y.shape

    def body(y_ref, t_ref, dy_ref, sq_ref):
        err = y_ref[...] - t_ref[...]
        dy_ref[...] = err * (1.0 / D)
        part = jnp.sum(err * err, axis=0, keepdims=True)

        @pl.when(pl.program_id(0) == 0)
        def _():
            sq_ref[...] = part

        @pl.when(pl.program_id(0) > 0)
        def _():
            sq_ref[...] += part

    return pl.pallas_call(
        body, name=name, grid=(S // tm,),
        in_specs=[pl.BlockSpec((tm, D), lambda i: (i, 0)), pl.BlockSpec((tm, D), lambda i: (i, 0))],
        out_specs=[pl.BlockSpec((tm, D), lambda i: (i, 0)), pl.BlockSpec((1, D), lambda i: (0, 0))],
        out_shape=[jax.ShapeDtypeStruct((S, D), F32), jax.ShapeDtypeStruct((1, D), F32)],
        compiler_params=_params(("arbitrary",)),
    )(y, target)


def f_adamw(w, g, m, v):
    m = ADAM_B1 * m + (1.0 - ADAM_B1) * g
    v = ADAM_B2 * v + (1.0 - ADAM_B2) * jnp.square(g)
    m_hat = m / (1.0 - ADAM_B1 ** ADAM_STEP)
    v_hat = v / (1.0 - ADAM_B2 ** ADAM_STEP)
    return -ADAM_LR * (m_hat / (jnp.sqrt(v_hat) + ADAM_EPS) + ADAM_WD * w), m, v


def adamw(w, g, m, v, *, name):
    shape = w.shape
    cols = shape[-1]
    rows = w.size // cols
    tm = rows
    while tm % 16 == 0 and tm * cols * 4 > (1 << 20):
        tm //= 2
    flat = [t.reshape(rows, cols) for t in (w, g, m, v)]
    outs = rowwise(f_adamw, flat, [], [F32, F32, F32], tm=tm, name=name)
    return [o.reshape(shape) for o in outs]


def _place():
    x, y, c = lax.axis_index("x"), lax.axis_index("y"), lax.axis_index("c")
    return x, y, c, [(1 - x, y), (x, 1 - y), (1 - x, 1 - y)]


def _my_chip():
    return 2 * lax.axis_index("x") + lax.axis_index("y")


def _any():
    return pl.BlockSpec(memory_space=pl.ANY)


DMA_CHUNKS = 8


def _chunks(rows):
    n = DMA_CHUNKS
    while n > 1 and rows % (n * 16):
        n //= 2
    return [(k * (rows // n), rows // n) for k in range(n)]


def _start_chunked(make, rows):
    for off, size in _chunks(rows):
        make(off, size).start()


def all_gather_chips(v, *, name):
    R, C = v.shape
    H = R // 2

    def body(v_ref, o_ref, send, recv, fsend, frecv):
        x, y, c, chips = _place()
        me = 2 * x + y
        ids = [2 * px + py for px, py in chips]
        mine = pl.ds(pl.multiple_of(c * H, 8), H)
        other = pl.ds(pl.multiple_of((1 - c) * H, 8), H)

        def cross(j, src_chip, rows):
            return pltpu.make_async_remote_copy(
                src_ref=v_ref.at[rows], dst_ref=o_ref.at[src_chip, rows], send_sem=send.at[j], recv_sem=recv.at[j],
                device_id=(*chips[j], c), device_id_type=MESH)

        def handed(j, rows):
            return pltpu.make_async_remote_copy(
                src_ref=o_ref.at[ids[j], rows], dst_ref=o_ref.at[ids[j], rows], send_sem=fsend.at[j],
                recv_sem=frecv.at[j], device_id=(x, y, 1 - c), device_id_type=MESH)

        def my_rows(off, size):
            return pl.ds(pl.multiple_of(c * H + off, 8), size)

        for j in range(3):
            _start_chunked(lambda off, size, j=j: cross(j, me, my_rows(off, size)), H)
        for j in range(3):
            cross(j, ids[j], mine).wait_recv()
            _start_chunked(lambda off, size, j=j: handed(j, my_rows(off, size)), H)
        for j in range(3):
            handed(j, other).wait_recv()
        for j in range(3):
            cross(j, me, mine).wait_send()
            handed(j, mine).wait_send()

    got = pl.pallas_call(
        body, name=name, in_specs=[_any()], out_specs=_any(), out_shape=jax.ShapeDtypeStruct((4, R, C), v.dtype),
        scratch_shapes=[pltpu.SemaphoreType.DMA((3,)), pltpu.SemaphoreType.DMA((3,)), pltpu.SemaphoreType.DMA((3,)),
                        pltpu.SemaphoreType.DMA((3,))],
    )(v)
    return lax.dynamic_update_slice(got, v[None], (_my_chip(), 0, 0))


def swap_halves(g, *, name):
    n, R, C = g.shape
    H = R // 2

    def body(g_ref, theirs_ref, send, recv):
        x, y, c, _ = _place()

        def give(q, off, size):
            return pltpu.make_async_remote_copy(
                src_ref=g_ref.at[q, pl.ds(pl.multiple_of((1 - c) * H + off, 8), size)],
                dst_ref=theirs_ref.at[q, pl.ds(off, size)], send_sem=send.at[q], recv_sem=recv.at[q],
                device_id=(x, y, 1 - c), device_id_type=MESH)

        for q in range(n):
            _start_chunked(functools.partial(give, q), H)
        for q in range(n):
            give(q, 0, H).wait()

    return pl.pallas_call(
        body, name=name, in_specs=[_any()], out_specs=_any(), out_shape=jax.ShapeDtypeStruct((n, H, C), g.dtype),
        scratch_shapes=[pltpu.SemaphoreType.DMA((n,)), pltpu.SemaphoreType.DMA((n,))],
    )(g)


def exchange_pieces(p, *, name):
    H = p.shape[1]

    def body(p_ref, b_ref, send, recv):
        x, y, c, chips = _place()
        me = 2 * x + y
        ids = [2 * px + py for px, py in chips]

        def cross(j, piece, slot, off, size):
            return pltpu.make_async_remote_copy(
                src_ref=p_ref.at[piece, pl.ds(off, size)], dst_ref=b_ref.at[slot, pl.ds(off, size)], send_sem=send.at[j],
                recv_sem=recv.at[j], device_id=(*chips[j], c), device_id_type=MESH)

        for j in range(3):
            _start_chunked(functools.partial(cross, j, ids[j], me), H)
        for j in range(3):
            cross(j, me, ids[j], 0, H).wait_recv()
        for j in range(3):
            cross(j, ids[j], me, 0, H).wait_send()

    got = pl.pallas_call(
        body, name=name, in_specs=[_any()], out_specs=_any(), out_shape=jax.ShapeDtypeStruct(p.shape, p.dtype),
        scratch_shapes=[pltpu.SemaphoreType.DMA((3,)), pltpu.SemaphoreType.DMA((3,))],
    )(p)
    me = _my_chip()
    return lax.dynamic_update_slice(got, lax.dynamic_slice_in_dim(p, me, 1, axis=0), (me, 0, 0))


def join_halves(f, *, name):
    H, C = f.shape

    def body(f_ref, o_ref, send, recv):
        x, y, c, _ = _place()

        def give(off, size):
            return pltpu.make_async_remote_copy(
                src_ref=f_ref.at[pl.ds(off, size)], dst_ref=o_ref.at[pl.ds(off, size)], send_sem=send, recv_sem=recv,
                device_id=(x, y, 1 - c), device_id_type=MESH)

        _start_chunked(give, H)
        give(0, H).wait()

    theirs = pl.pallas_call(
        body, name=name, in_specs=[_any()], out_specs=_any(), out_shape=jax.ShapeDtypeStruct((H, C), f.dtype),
        scratch_shapes=[pltpu.SemaphoreType.DMA, pltpu.SemaphoreType.DMA],
    )(f)
    south = lax.axis_index("c") == 0
    return jnp.concatenate([jnp.where(south, f, theirs), jnp.where(south, theirs, f)], axis=0)


def _row_tile(rows, want):
    tm = want
    while rows % tm:
        tm //= 2
    return tm


def _add_pair(g, theirs, out_dtype, *, name):
    n, R, C = g.shape
    H = R // 2
    tm = _row_tile(H, 512)
    nb = H // tm

    def body(c_ref, g_ref, t_ref, o_ref):
        o_ref[...] = (g_ref[...] + t_ref[...]).astype(o_ref.dtype)

    grid_spec = pltpu.PrefetchScalarGridSpec(
        num_scalar_prefetch=1, grid=(n, nb),
        in_specs=[pl.BlockSpec((None, tm, C), lambda q, i, c: (q, c[0] * nb + i, 0)),
                  pl.BlockSpec((None, tm, C), lambda q, i, c: (q, i, 0))],
        out_specs=pl.BlockSpec((None, tm, C), lambda q, i, c: (q, i, 0)))
    return pl.pallas_call(
        body, name=name, grid_spec=grid_spec, out_shape=jax.ShapeDtypeStruct((n, H, C), out_dtype),
        compiler_params=_params(("parallel", "parallel")),
    )(lax.axis_index("c").astype(jnp.int32).reshape(1), g, theirs)


def _add_stack(b, *, name):
    n, H, C = b.shape
    tm = _row_tile(H, 256)

    def body(b_ref, o_ref):
        s = b_ref[0].astype(F32)
        for k in range(1, n):
            s = s + b_ref[k].astype(F32)
        o_ref[...] = s

    return pl.pallas_call(
        body, name=name, grid=(H // tm,), in_specs=[pl.BlockSpec((n, tm, C), lambda i: (0, i, 0))],
        out_specs=pl.BlockSpec((tm, C), lambda i: (i, 0)), out_shape=jax.ShapeDtypeStruct((H, C), F32),
        compiler_params=_params(("parallel",)),
    )(b)


def reduce_scatter(g):
    pair = _add_pair(g, swap_halves(g, name="rs_swap_halves"), GRAD_WIRE_DTYPE, name="rs_add_pair")
    total = _add_stack(exchange_pieces(pair, name="rs_exchange"), name="rs_add_chips")
    return join_halves(total, name="rs_join")


BIG = ("ffn1_w_gate", "ffn1_w_up", "ffn1_w_down", "ffn2_w_gate", "ffn2_w_up", "ffn2_w_down", "w_in", "w_branch", "w_out")
SMALL = ("norm_pre", "norm_post", "conv_w", "gla_w_alpha")
REPLICATED = ("conv_b", "conv_ln_g", "conv_ln_b", "gla_b_alpha", "gla_norm_g")
WEIGHTS = ("norm_pre", "norm_post", "ffn1_w_gate", "ffn1_w_up", "ffn1_w_down", "ffn2_w_gate", "ffn2_w_up", "ffn2_w_down",
           "w_in", "conv_w", "conv_b", "conv_ln_g", "conv_ln_b", "gla_w_alpha", "gla_b_alpha", "gla_norm_g", "w_branch",
           "w_out")
SHARD_AXIS = {"ffn1_w_gate": 2, "ffn1_w_up": 2, "ffn1_w_down": 1, "ffn2_w_gate": 2, "ffn2_w_up": 2, "ffn2_w_down": 1,
              "w_in": 2, "w_branch": 3, "w_out": 1, "norm_pre": 2, "norm_post": 2, "conv_w": 2, "gla_w_alpha": 2}


def _size(shape):
    n = 1
    for d in shape:
        n *= d
    return n


def _pack_rows(shape):
    return -(-_size(shape) // (16 * PACK_COLS)) * 16


def _pack(arrs, dtype, row_mult):
    parts, rows = [], 0
    for a in arrs:
        r = _pack_rows(a.shape)
        flat = a.astype(dtype).reshape(-1)
        if r * PACK_COLS != flat.shape[0]:
            flat = jnp.pad(flat, (0, r * PACK_COLS - flat.shape[0]))
        parts.append(flat.reshape(r, PACK_COLS))
        rows += r
    if rows % row_mult:
        parts.append(jnp.zeros((row_mult - rows % row_mult, PACK_COLS), dtype))
    return jnp.concatenate(parts, axis=0)


def _unpack(buf, shapes):
    lead = buf.shape[:-2]
    out, off = [], 0
    for s in shapes:
        r = _pack_rows(s)
        part = lax.slice_in_dim(buf, off, off + r, axis=buf.ndim - 2).reshape(lead + (r * PACK_COLS,))
        if r * PACK_COLS != _size(s):
            part = lax.slice_in_dim(part, 0, _size(s), axis=part.ndim - 1)
        out.append(part.reshape(lead + tuple(s)))
        off += r
    return out


def gather_weights(shards, names, dtype, row_mult, name):
    packed = _pack([shards[k] for k in names], dtype, row_mult)
    parts = _unpack(all_gather_chips(packed, name=name), [shards[k].shape for k in names])
    return {k: jnp.concatenate([p[q] for q in range(4)], axis=SHARD_AXIS[k]) for k, p in zip(names, parts)}


def scatter_grads(grads, shards):
    sharded = BIG + SMALL
    pieces = []
    for q in range(4):
        part = []
        for k in sharded:
            w = shards[k].shape[SHARD_AXIS[k]]
            part.append(lax.slice_in_dim(grads[k], q * w, (q + 1) * w, axis=SHARD_AXIS[k]))
        part += [grads[k] for k in REPLICATED]
        pieces.append(_pack(part, F32, 512))
    total = reduce_scatter(jnp.stack(pieces))
    names = sharded + REPLICATED
    return dict(zip(names, _unpack(total, [shards[k].shape for k in names])))


TM = 256
TM_GLA = 512
TQ = 128

IN_SB, IN_CONV, IN_GLA, IN_LR, IN_GATE = 0, 1536, 2560, 4096, 4112
IN_END = 7184


def layer_weights(full, l):
    w = {}
    for f in ("ffn1", "ffn2"):
        w[f + "_gu"] = jnp.concatenate([full[f + "_w_gate"][l], full[f + "_w_up"][l]], axis=1)
        w[f + "_d"] = full[f + "_w_down"][l]
    win = full["w_in"][l]
    w["in_sb"] = win[:, IN_SB:IN_CONV]
    w["in_conv"] = win[:, IN_CONV:IN_GLA]
    w["in_gla"] = win[:, IN_GLA:IN_LR]
    w["in_lr"] = jnp.pad(win[:, IN_LR:IN_GATE], ((0, 0), (0, GLA_RANK_PAD - GLA_RANK)))
    w["in_gate"] = win[:, IN_GATE:IN_END]
    w["branch"] = [full["w_branch"][l, g] for g in range(3)]
    w["out"] = full["w_out"][l]
    return w


def ffn_fwd(x, gpre, gpost, wgu, wd, tag):
    h = rowwise(f_rms, [x], [gpre], [MXU_DTYPE], tm=TM, name=tag + "_pre")[0]
    ab = mm(h, wgu, out_dtype=MXU_DTYPE, tn=512, name=tag + "_gu")
    z = rowwise(f_swiglu, [(ab, D_FF, 0), (ab, D_FF, 1)], [], [MXU_DTYPE], tm=TM, name=tag + "_act")[0]
    f = mm(z, wd, tk=1408, name=tag + "_down")
    x2 = rowwise(f_half_post, [x, f], [gpost], [F32], tm=TM, name=tag + "_post")[0]
    return x2, (x, h, ab, z, f)


def ffn_bwd(dx2, res, gpre, gpost, wgu, wd, tag):
    x, h, ab, z, f = res
    df, dgpost = rowwise_vjp(f_half_post, [x, f], [gpost], [dx2], [1], [0], [MXU_DTYPE], tm=TM, name=tag + "_post_b")
    dz = mm(df, wd, tb=True, out_dtype=MXU_DTYPE, tn=1408, name=tag + "_down_dx")
    dwd = mm(z, df, ta=True, tm=1408, name=tag + "_down_dw")
    da, db = rowwise_vjp(f_swiglu, [(ab, D_FF, 0), (ab, D_FF, 1)], [], [dz], [0, 1], [], [MXU_DTYPE, MXU_DTYPE], tm=TM,
                         name=tag + "_act_b")
    dab = jnp.concatenate([da, db], axis=1)
    dh = mm(dab, wgu, tb=True, tk=1408, name=tag + "_gu_dx")
    dwgu = mm(h, dab, ta=True, tn=1408, name=tag + "_gu_dw")
    dx, dgpre = rowwise_vjp(f_rms, [x], [gpre], [dh], [0], [0], [F32], [dx2], tm=TM, name=tag + "_pre_b")
    return dx, dgpre, dgpost, dwgu[:, :D_FF], dwgu[:, D_FF:], dwd


def mixer_fwd(x, p, w, tag):
    h = rowwise(f_rms, [x], [p["gpre"]], [MXU_DTYPE], tm=TM, name=tag + "_pre")[0]
    qkv = mm(h, w["in_sb"], out_dtype=MXU_DTYPE, tn=512, name=tag + "_in_sb")
    cv = mm(h, w["in_conv"], name=tag + "_in_conv")
    gl = mm(h, w["in_gla"], out_dtype=MXU_DTYPE, tn=512, name=tag + "_in_gla")
    lr = mm(h, w["in_lr"], out_dtype=MXU_DTYPE, name=tag + "_in_lr")
    gt = mm(h, w["in_gate"], out_dtype=MXU_DTYPE, name=tag + "_in_gate")
    sb = sb_attn_fwd(qkv, tq=TQ, name=tag + "_sb")
    u = rowwise(f_glu, [(cv, BRANCH, 0), (cv, BRANCH, 1)], [], [F32], tm=TM, name=tag + "_glu")[0]
    y = conv_fwd(u, p["conv_w"], p["conv_b"], tm=TM, name=tag + "_conv")
    cb = rowwise(f_conv_ln, [y], [p["ln_g"], p["ln_b"]], [MXU_DTYPE], tm=TM, name=tag + "_ln")[0]
    qs, kd, lam = rowwise(f_gla_pre, [(gl, 256, 0), (gl, 256, 1), lr], [p["wa"], p["ba"]], [MXU_DTYPE, MXU_DTYPE, F32],
                          tm=TM_GLA, name=tag + "_gla_pre")
    o, states = gla_scan_fwd(qs, kd, gl, lam, tm=TM_GLA, name=tag + "_gla_scan")
    gb = rowwise(f_gla_post, [o, (gl, BRANCH, 2)], [p["gn"]], [MXU_DTYPE], tm=TM, name=tag + "_gla_post")[0]
    branches = [sb, cb, gb]
    bd = [mm(branches[g], w["branch"][g], out_dtype=MXU_DTYPE, name=tag + f"_branch{g}") for g in range(3)]
    merged = rowwise(f_merge, [(gt, D_MODEL, 0), (gt, D_MODEL, 1), (gt, D_MODEL, 2)] + bd, [], [MXU_DTYPE], tm=TM,
                     name=tag + "_merge")[0]
    m = mm(merged, w["out"], name=tag + "_out")
    x2 = rowwise(f_post, [x, m], [p["gpost"]], [F32], tm=TM, name=tag + "_post")[0]
    return x2, (x, h, qkv, cv, gl, lr, gt, u, y, qs, kd, lam, o, states, branches, bd, merged, m)


def mixer_bwd(dx2, res, p, w, tag):
    x, h, qkv, cv, gl, lr, gt, u, y, qs, kd, lam, o, states, branches, bd, merged, m = res
    g = {}
    dm, g["gpost"] = rowwise_vjp(f_post, [x, m], [p["gpost"]], [dx2], [1], [0], [MXU_DTYPE], tm=TM, name=tag + "_post_b")
    dmerged = mm(dm, w["out"], tb=True, out_dtype=MXU_DTYPE, name=tag + "_out_dx")
    g["out"] = mm(merged, dm, ta=True, name=tag + "_out_dw")
    gts = [(gt, D_MODEL, 0), (gt, D_MODEL, 1), (gt, D_MODEL, 2)]
    dgate = rowwise_vjp(f_merge, gts + bd, [], [dmerged], [0, 1, 2, 3, 4, 5], [], [MXU_DTYPE] * 6, tm=TM,
                        name=tag + "_merge_b")
    dgt = jnp.concatenate(dgate[:3], axis=1)
    dbd = dgate[3:]
    g["branch"] = [mm(branches[k], dbd[k], ta=True, name=tag + f"_branch{k}_dw") for k in range(3)]
    dsb = mm(dbd[0], w["branch"][0], tb=True, out_dtype=MXU_DTYPE, name=tag + "_branch0_dx")
    dq, dk, dv = sb_attn_bwd(qkv, dsb, tq=TQ, name=tag + "_sb_b")
    dqkv = jnp.concatenate([dq, _mx(dk), _mx(dv)], axis=1)
    dcb = mm(dbd[1], w["branch"][1], tb=True, name=tag + "_branch1_dx")
    dy, g["ln_g"], g["ln_b"] = rowwise_vjp(f_conv_ln, [y], [p["ln_g"], p["ln_b"]], [dcb], [0], [0, 1], [F32], tm=TM,
                                           name=tag + "_ln_b")
    du, dwb = conv_bwd(dy, u, p["conv_w"], tm=TM, name=tag + "_conv_b")
    g["conv_w"], g["conv_b"] = dwb[:CONV_WIDTH], dwb[CONV_WIDTH:CONV_WIDTH + 1]
    dca, dcg = rowwise_vjp(f_glu, [(cv, BRANCH, 0), (cv, BRANCH, 1)], [], [du], [0, 1], [], [MXU_DTYPE, MXU_DTYPE], tm=TM,
                           name=tag + "_glu_b")
    dcv = jnp.concatenate([dca, dcg], axis=1)
    dgb = mm(dbd[2], w["branch"][2], tb=True, name=tag + "_branch2_dx")
    do, dr, g["gn"] = rowwise_vjp(f_gla_post, [o, (gl, BRANCH, 2)], [p["gn"]], [dgb], [0, 1], [0], [F32, MXU_DTYPE], tm=TM,
                                  name=tag + "_gla_post_b")
    dqs, dkd, dgv, dlam = gla_scan_bwd(do, qs, kd, gl, lam, states, tm=TM_GLA, name=tag + "_gla_scan_b")
    dgq, dgk, dlr, g["wa"], g["ba"] = rowwise_vjp(
        f_gla_pre, [(gl, 256, 0), (gl, 256, 1), lr], [p["wa"], p["ba"]], [dqs, dkd, dlam], [0, 1, 2], [0, 1],
        [MXU_DTYPE, MXU_DTYPE, MXU_DTYPE], tm=TM_GLA, name=tag + "_gla_pre_b")
    dgl = jnp.concatenate([dgq, dgk, _mx(dgv), dr], axis=1)
    secs = [("in_sb", dqkv), ("in_conv", dcv), ("in_gla", dgl), ("in_lr", dlr), ("in_gate", dgt)]
    dh = None
    for k, d in secs:
        dh = mm(d, w[k], dh, tb=True, name=tag + "_" + k + "_dx")
        g[k] = mm(h, d, ta=True, tn=1536, name=tag + "_" + k + "_dw")
    dx, g["gpre"] = rowwise_vjp(f_rms, [x], [p["gpre"]], [dh], [0], [0], [F32], [dx2], tm=TM, name=tag + "_pre_b")
    return dx, g


def kernel(x, norm_pre, norm_post, ffn1_w_gate, ffn1_w_up, ffn1_w_down, ffn2_w_gate, ffn2_w_up, ffn2_w_down, w_in, conv_w, conv_b, conv_ln_g, conv_ln_b, gla_w_alpha, gla_b_alpha, gla_norm_g, w_branch, w_out, loss_target, m_norm_pre, m_norm_post, m_ffn1_w_gate, m_ffn1_w_up, m_ffn1_w_down, m_ffn2_w_gate, m_ffn2_w_up, m_ffn2_w_down, m_w_in, m_conv_w, m_conv_b, m_conv_ln_g, m_conv_ln_b, m_gla_w_alpha, m_gla_b_alpha, m_gla_norm_g, m_w_branch, m_w_out, v_norm_pre, v_norm_post, v_ffn1_w_gate, v_ffn1_w_up, v_ffn1_w_down, v_ffn2_w_gate, v_ffn2_w_up, v_ffn2_w_down, v_w_in, v_conv_w, v_conv_b, v_conv_ln_g, v_conv_ln_b, v_gla_w_alpha, v_gla_b_alpha, v_gla_norm_g, v_w_branch, v_w_out):
    shards = dict(norm_pre=norm_pre, norm_post=norm_post, ffn1_w_gate=ffn1_w_gate, ffn1_w_up=ffn1_w_up,
                  ffn1_w_down=ffn1_w_down, ffn2_w_gate=ffn2_w_gate, ffn2_w_up=ffn2_w_up, ffn2_w_down=ffn2_w_down, w_in=w_in,
                  conv_w=conv_w, conv_b=conv_b, conv_ln_g=conv_ln_g, conv_ln_b=conv_ln_b, gla_w_alpha=gla_w_alpha,
                  gla_b_alpha=gla_b_alpha, gla_norm_g=gla_norm_g, w_branch=w_branch, w_out=w_out)
    mom_m = dict(zip(WEIGHTS, (m_norm_pre, m_norm_post, m_ffn1_w_gate, m_ffn1_w_up, m_ffn1_w_down, m_ffn2_w_gate,
                               m_ffn2_w_up, m_ffn2_w_down, m_w_in, m_conv_w, m_conv_b, m_conv_ln_g, m_conv_ln_b,
                               m_gla_w_alpha, m_gla_b_alpha, m_gla_norm_g, m_w_branch, m_w_out)))
    mom_v = dict(zip(WEIGHTS, (v_norm_pre, v_norm_post, v_ffn1_w_gate, v_ffn1_w_up, v_ffn1_w_down, v_ffn2_w_gate,
                               v_ffn2_w_up, v_ffn2_w_down, v_w_in, v_conv_w, v_conv_b, v_conv_ln_g, v_conv_ln_b,
                               v_gla_w_alpha, v_gla_b_alpha, v_gla_norm_g, v_w_branch, v_w_out)))
    depth = norm_pre.shape[0]
    full = gather_weights(shards, BIG, MXU_DTYPE, 256, "gather_big")
    full.update(gather_weights(shards, SMALL, F32, 16, "gather_small"))

    def layer_params(l):
        ffn = [dict(gpre=full["norm_pre"][l, k:k + 1], gpost=full["norm_post"][l, k:k + 1]) for k in (0, 2)]
        mix = dict(gpre=full["norm_pre"][l, 1:2], gpost=full["norm_post"][l, 1:2],
                   conv_w=jnp.pad(full["conv_w"][l], ((0, CONV_PAD - CONV_WIDTH), (0, 0))), conv_b=conv_b[l:l + 1],
                   ln_g=conv_ln_g[l:l + 1], ln_b=conv_ln_b[l:l + 1],
                   wa=jnp.pad(full["gla_w_alpha"][l], ((0, GLA_RANK_PAD - GLA_RANK), (0, 0))), ba=gla_b_alpha[l:l + 1],
                   gn=gla_norm_g[l:l + 1])
        return ffn, mix

    xs = x[0]
    saved = []
    for l in range(depth):
        w = layer_weights(full, l)
        ffn, mix = layer_params(l)
        xs, r1 = ffn_fwd(xs, ffn[0]["gpre"], ffn[0]["gpost"], w["ffn1_gu"], w["ffn1_d"], f"l{l}_ffn1")
        xs, r2 = mixer_fwd(xs, mix, w, f"l{l}_mix")
        xs, r3 = ffn_fwd(xs, ffn[1]["gpre"], ffn[1]["gpost"], w["ffn2_gu"], w["ffn2_d"], f"l{l}_ffn2")
        saved.append((w, ffn, mix, r1, r2, r3))
    dx, sq = loss_head(xs, loss_target[0], tm=TM, name="loss_head")
    loss = lax.psum(0.5 * jnp.sum(sq) / D_MODEL, ("x", "y", "c"))

    per_layer = []
    for l in reversed(range(depth)):
        w, ffn, mix, r1, r2, r3 = saved[l]
        g = {}
        dx, gpre2, gpost2, g["ffn2_w_gate"], g["ffn2_w_up"], g["ffn2_w_down"] = ffn_bwd(
            dx, r3, ffn[1]["gpre"], ffn[1]["gpost"], w["ffn2_gu"], w["ffn2_d"], f"l{l}_ffn2")
        dx, gm = mixer_bwd(dx, r2, mix, w, f"l{l}_mix")
        dx, gpre0, gpost0, g["ffn1_w_gate"], g["ffn1_w_up"], g["ffn1_w_down"] = ffn_bwd(
            dx, r1, ffn[0]["gpre"], ffn[0]["gpost"], w["ffn1_gu"], w["ffn1_d"], f"l{l}_ffn1")
        g["norm_pre"] = jnp.concatenate([gpre0, gm["gpre"], gpre2], axis=0)
        g["norm_post"] = jnp.concatenate([gpost0, gm["gpost"], gpost2], axis=0)
        g["w_in"] = jnp.concatenate([gm["in_sb"], gm["in_conv"], gm["in_gla"], gm["in_lr"][:, :GLA_RANK], gm["in_gate"]],
                                    axis=1)
        g["conv_w"], g["conv_b"] = gm["conv_w"], gm["conv_b"][0]
        g["conv_ln_g"], g["conv_ln_b"] = gm["ln_g"][0], gm["ln_b"][0]
        g["gla_w_alpha"], g["gla_b_alpha"], g["gla_norm_g"] = gm["wa"][:GLA_RANK], gm["ba"][0], gm["gn"][0]
        g["w_branch"] = jnp.stack(gm["branch"])
        g["w_out"] = gm["out"]
        per_layer.append(g)
    per_layer.reverse()
    grads = {k: jnp.stack([g[k] for g in per_layer]) for k in WEIGHTS}

    grad_w = scatter_grads(grads, shards)
    delta, new_m, new_v = {}, {}, {}
    for k in WEIGHTS:
        delta[k], new_m[k], new_v[k] = adamw(shards[k], grad_w[k], mom_m[k], mom_v[k], name="adamw_" + k)
    return (loss, dx[None], *[grad_w[k] for k in WEIGHTS], *[delta[k] for k in WEIGHTS], *[new_m[k] for k in WEIGHTS],
            *[new_v[k] for k in WEIGHTS])
```

```python
import functools

import jax
import jax.numpy as jnp
from jax import lax
from jax.experimental import pallas as pl
from jax.experimental.pallas import tpu as pltpu

F32 = jnp.float32
MXU_DTYPE = jnp.bfloat16
GRAD_WIRE_DTYPE = jnp.bfloat16
HIGHEST = lax.Precision.HIGHEST
MESH = pl.DeviceIdType.MESH

NORM_EPS = 1e-6
D_MODEL = 1024
D_FF = 2816
BRANCH = 512
CHUNK = 64
CONV_WIDTH = 31
CONV_PAD = 32
GLA_RANK = 16
GLA_RANK_PAD = 128
GLA_TAU = 16.0
SB_SCALE = 0.125
SB_CUTOFF = 60.0
GLA_SCALE = 0.125
PACK_COLS = 1024
VMEM_LIMIT = 56 * 1024 * 1024

ADAM_LR, ADAM_B1, ADAM_B2, ADAM_EPS, ADAM_WD, ADAM_STEP = 0.001, 0.9, 0.999, 1e-08, 0.01, 10

NT = (((1,), (1,)), ((), ()))
TN = (((0,), (0,)), ((), ()))
NN = (((1,), (0,)), ((), ()))


def _params(sem=None, vmem=None):
    return pltpu.CompilerParams(dimension_semantics=sem, vmem_limit_bytes=vmem)


def _mx(v):
    return v.astype(MXU_DTYPE)


def _mx_round(v):
    return v.astype(MXU_DTYPE).astype(F32)


def _fit(dim, want):
    if dim <= want:
        return dim
    for d in range(want - want % 128, 0, -128):
        if dim % d == 0:
            return d
    raise ValueError((dim, want))


def mm(a, b, c=None, *, ta=False, tb=False, out_dtype=F32, tm=1024, tn=1024, tk=1024, name):
    K, M = a.shape if ta else a.shape[::-1]
    N = b.shape[0] if tb else b.shape[1]
    assert (b.shape[1] if tb else b.shape[0]) == K, (a.shape, b.shape, ta, tb)
    tm, tn, tk = _fit(M, tm), _fit(N, tn), _fit(K, tk)
    nk = K // tk
    dn = (((0 if ta else 1,), (1 if tb else 0,)), ((), ()))

    def body(*refs):
        if c is None:
            a_ref, b_ref, o_ref, acc = refs
            c_ref = None
        else:
            a_ref, b_ref, c_ref, o_ref, acc = refs
        k = pl.program_id(2)
        p = lax.dot_general(_mx(a_ref[...]), _mx(b_ref[...]), dn, preferred_element_type=F32)

        @pl.when(k == 0)
        def _():
            acc[...] = p

        @pl.when(k > 0)
        def _():
            acc[...] += p

        @pl.when(k == nk - 1)
        def _():
            r = acc[...]
            if c_ref is not None:
                r = r + c_ref[...].astype(F32)
            o_ref[...] = r.astype(o_ref.dtype)

    a_spec = pl.BlockSpec((tk, tm), lambda i, j, k: (k, i)) if ta else pl.BlockSpec((tm, tk), lambda i, j, k: (i, k))
    b_spec = pl.BlockSpec((tn, tk), lambda i, j, k: (j, k)) if tb else pl.BlockSpec((tk, tn), lambda i, j, k: (k, j))
    o_spec = pl.BlockSpec((tm, tn), lambda i, j, k: (i, j))
    ins, in_specs = [a, b], [a_spec, b_spec]
    if c is not None:
        ins.append(c)
        in_specs.append(o_spec)
    return pl.pallas_call(
        body, name=name, grid=(M // tm, N // tn, nk), in_specs=in_specs, out_specs=o_spec,
        out_shape=jax.ShapeDtypeStruct((M, N), out_dtype), scratch_shapes=[pltpu.VMEM((tm, tn), F32)],
        compiler_params=_params(("parallel", "parallel", "arbitrary"), VMEM_LIMIT),
    )(*ins)


SLAB = D_FF // 4


def slab_nt(a, w6, t0, n, l, *, out_dtype, tm=1024, name):
    S, K = a.shape
    tm = _fit(S, tm)

    def body(a_ref, w_ref, o_ref):
        o_ref[...] = lax.dot_general(_mx(a_ref[...]), _mx(w_ref[...]), NT, preferred_element_type=F32).astype(o_ref.dtype)

    return pl.pallas_call(
        body, name=name, grid=(S // tm, n),
        in_specs=[pl.BlockSpec((tm, K), lambda i, s: (i, 0)),
                  pl.BlockSpec((None, None, None, SLAB, K), lambda i, s: (s % 4, t0 + s // 4, l, 0, 0))],
        out_specs=pl.BlockSpec((None, tm, SLAB), lambda i, s: (s, i, 0)),
        out_shape=jax.ShapeDtypeStruct((n, S, SLAB), out_dtype),
        compiler_params=_params(("parallel", "parallel"), VMEM_LIMIT),
    )(a, w6)


def slab_nn(a, w6, t0, l, *, tm=1024, name):
    n, S, _ = a.shape
    K = w6.shape[-1]
    tm = _fit(S, tm)

    def body(a_ref, w_ref, o_ref, acc):
        s = pl.program_id(1)
        p = jnp.dot(_mx(a_ref[...]), _mx(w_ref[...]), preferred_element_type=F32)

        @pl.when(s == 0)
        def _():
            acc[...] = p

        @pl.when(s > 0)
        def _():
            acc[...] += p

        @pl.when(s == n - 1)
        def _():
            o_ref[...] = acc[...]

    return pl.pallas_call(
        body, name=name, grid=(S // tm, n),
        in_specs=[pl.BlockSpec((None, tm, SLAB), lambda i, s: (s, i, 0)),
                  pl.BlockSpec((None, None, None, SLAB, K), lambda i, s: (s % 4, t0 + s // 4, l, 0, 0))],
        out_specs=pl.BlockSpec((tm, K), lambda i, s: (i, 0)),
        out_shape=jax.ShapeDtypeStruct((S, K), F32), scratch_shapes=[pltpu.VMEM((tm, K), F32)],
        compiler_params=_params(("parallel", "arbitrary"), VMEM_LIMIT),
    )(a, w6)


def slab_tn(a, b, g6, t0, l, *, tk=1024, name):
    n, S, _ = a.shape
    K = b.shape[1]
    tk = _fit(S, tk)
    nk = S // tk

    def body(a_ref, b_ref, g_ref, o_ref, acc):
        k = pl.program_id(1)
        p = lax.dot_general(_mx(a_ref[...]), _mx(b_ref[...]), TN, preferred_element_type=F32)

        @pl.when(k == 0)
        def _():
            acc[...] = p

        @pl.when(k > 0)
        def _():
            acc[...] += p

        @pl.when(k == nk - 1)
        def _():
            o_ref[...] = acc[...]

    return pl.pallas_call(
        body, name=name, grid=(n, nk),
        in_specs=[pl.BlockSpec((None, tk, SLAB), lambda s, k: (s, k, 0)), pl.BlockSpec((tk, K), lambda s, k: (k, 0)),
                  pl.BlockSpec(memory_space=pl.ANY)],
        out_specs=pl.BlockSpec((None, None, None, SLAB, K), lambda s, k: (s % 4, t0 + s // 4, l, 0, 0)),
        out_shape=jax.ShapeDtypeStruct(g6.shape, g6.dtype), scratch_shapes=[pltpu.VMEM((SLAB, K), F32)],
        input_output_aliases={2: 0},
        compiler_params=_params(("parallel", "arbitrary"), VMEM_LIMIT),
    )(a, b, g6)


def swiglu_fwd(ab, *, tm, name):
    _, R, C = ab.shape

    def body(ab_ref, o_ref):
        o_ref[...] = f_swiglu(ab_ref[0].astype(F32), ab_ref[1].astype(F32))[0].astype(o_ref.dtype)

    return pl.pallas_call(
        body, name=name, grid=(R // tm,), in_specs=[pl.BlockSpec((2, tm, C), lambda i: (0, i, 0))],
        out_specs=pl.BlockSpec((tm, C), lambda i: (i, 0)), out_shape=jax.ShapeDtypeStruct((R, C), MXU_DTYPE),
        compiler_params=_params(("parallel",), VMEM_LIMIT),
    )(ab)


def swiglu_bwd(ab, dz, *, tm, name):
    _, R, C = ab.shape

    def body(ab_ref, dz_ref, o_ref):
        _, pullback = jax.vjp(lambda a, b: f_swiglu(a, b)[0], ab_ref[0].astype(F32), ab_ref[1].astype(F32))
        da, db = pullback(dz_ref[...].astype(F32))
        o_ref[0] = da.astype(o_ref.dtype)
        o_ref[1] = db.astype(o_ref.dtype)

    return pl.pallas_call(
        body, name=name, grid=(R // tm,),
        in_specs=[pl.BlockSpec((2, tm, C), lambda i: (0, i, 0)), pl.BlockSpec((tm, C), lambda i: (i, 0))],
        out_specs=pl.BlockSpec((2, tm, C), lambda i: (0, i, 0)), out_shape=jax.ShapeDtypeStruct((2, R, C), MXU_DTYPE),
        compiler_params=_params(("parallel",), VMEM_LIMIT),
    )(ab, dz)


def _row_arg(arg):
    if isinstance(arg, tuple):
        return arg
    return arg, arg.shape[1], 0


def _row_specs(rows, n):
    arrs, specs, avals = [], [], []
    for arg in rows:
        arr, width, cb = _row_arg(arg)
        rb = arr.shape[0] // n
        arrs.append(arr)
        specs.append(pl.BlockSpec((rb, width), lambda i, cb=cb: (i, cb)))
        avals.append(jax.ShapeDtypeStruct((rb, width), F32))
    return arrs, specs, avals


def _par_specs(pars):
    specs = [pl.BlockSpec(p.shape, lambda i, nd=p.ndim: (0,) * nd) for p in pars]
    avals = [jax.ShapeDtypeStruct(p.shape, F32) for p in pars]
    return specs, avals


def rowwise(f, rows, pars, out_dtypes, *, tm, name):
    n = _row_arg(rows[0])[0].shape[0] // tm
    arrs, rspecs, ravals = _row_specs(rows, n)
    pspecs, pavals = _par_specs(pars)
    oavals = jax.eval_shape(f, *ravals, *pavals)
    nin = len(arrs) + len(pars)

    def body(*refs):
        outs = f(*[r[...].astype(F32) for r in refs[:nin]])
        for o_ref, o in zip(refs[nin:], outs):
            o_ref[...] = o.astype(o_ref.dtype)

    return pl.pallas_call(
        body, name=name, grid=(n,), in_specs=rspecs + pspecs,
        out_specs=[pl.BlockSpec(o.shape, lambda i: (i, 0)) for o in oavals],
        out_shape=[jax.ShapeDtypeStruct((n * o.shape[0], o.shape[1]), dt) for o, dt in zip(oavals, out_dtypes)],
        compiler_params=_params(("parallel",), VMEM_LIMIT),
    )(*arrs, *pars)


def rowwise_vjp(f, rows, pars, cots, row_grad, par_grad, d_dtypes, adds=None, *, tm, name):
    n = _row_arg(rows[0])[0].shape[0] // tm
    arrs, rspecs, ravals = _row_specs(rows, n)
    pspecs, pavals = _par_specs(pars)
    carrs, cspecs, _ = _row_specs(cots, n)
    adds = adds or [None] * len(row_grad)
    add_arrs = [a for a in adds if a is not None]
    _, aspecs, _ = _row_specs(add_arrs, n)
    nr, npar, nc, na = len(arrs), len(pars), len(carrs), len(add_arrs)
    diff = list(row_grad) + [nr + j for j in par_grad]
    ngr = len(row_grad)

    def body(*refs):
        ins = [r[...].astype(F32) for r in refs[:nr + npar]]
        cs = tuple(r[...].astype(F32) for r in refs[nr + npar:nr + npar + nc])
        add_refs = list(refs[nr + npar + nc:nr + npar + nc + na])
        outs = refs[nr + npar + nc + na:]

        def g(*d):
            full = list(ins)
            for idx, val in zip(diff, d):
                full[idx] = val
            return f(*full)

        _, pullback = jax.vjp(g, *[ins[idx] for idx in diff])
        ds = pullback(cs)
        for k in range(ngr):
            d = ds[k]
            if adds[k] is not None:
                d = d + add_refs.pop(0)[...].astype(F32)
            outs[k][...] = d.astype(outs[k].dtype)
        i = pl.program_id(0)
        for k in range(ngr, len(diff)):
            @pl.when(i == 0)
            def _(k=k):
                outs[k][...] = ds[k]

            @pl.when(i > 0)
            def _(k=k):
                outs[k][...] += ds[k]

    out_specs = [pl.BlockSpec(ravals[i].shape, lambda i: (i, 0)) for i in row_grad] + [pspecs[j] for j in par_grad]
    out_shape = [jax.ShapeDtypeStruct((arrs[i].shape[0], ravals[i].shape[1]), dt) for i, dt in zip(row_grad, d_dtypes)]
    out_shape += [jax.ShapeDtypeStruct(pars[j].shape, F32) for j in par_grad]
    return pl.pallas_call(
        body, name=name, grid=(n,), in_specs=rspecs + pspecs + cspecs + aspecs, out_specs=out_specs, out_shape=out_shape,
        compiler_params=_params(("arbitrary",), VMEM_LIMIT),
    )(*arrs, *pars, *carrs, *add_arrs)


def _logsig(x):
    return jnp.minimum(x, 0.0) - jnp.log(1.0 + jnp.exp(-jnp.abs(x)))


def _sigmoid(x):
    return 1.0 / (1.0 + jnp.exp(-x))


def _silu(x):
    return x * _sigmoid(x)


def _rms(x, g):
    return x * lax.rsqrt(jnp.mean(x * x, axis=-1, keepdims=True) + NORM_EPS) * g


def f_rms(x, g):
    return (_rms(x, g),)


def f_swiglu(a, b):
    return (_silu(a) * b,)


def f_half_post(x, f, g):
    return (x + 0.5 * _rms(f, g),)


def f_post(x, m, g):
    return (x + _rms(m, g),)


def f_glu(a, g):
    return (a * _sigmoid(g),)


def f_conv_ln(y, lg, lb):
    mu = jnp.mean(y, axis=-1, keepdims=True)
    var = jnp.mean(jnp.square(y - mu), axis=-1, keepdims=True)
    return (_silu((y - mu) * lax.rsqrt(var + NORM_EPS) * lg + lb),)


def f_merge(g0, g1, g2, b0, b1, b2):
    return (_sigmoid(g0) * b0 + _sigmoid(g1) * b1 + _sigmoid(g2) * b2,)


def f_gla_post(o, r, g):
    w = o.shape[1]
    hv = w // 4
    i = lax.broadcasted_iota(jnp.int32, (w, w), 0) // hv
    j = lax.broadcasted_iota(jnp.int32, (w, w), 1) // hv
    avg = jnp.where(i == j, 1.0 / hv, 0.0).astype(F32)
    ms = jnp.dot(o * o, avg, precision=HIGHEST, preferred_element_type=F32)
    return (o * lax.rsqrt(ms + NORM_EPS) * g * _silu(r),)


def f_gla_pre(q, k, lr, wa, ba):
    tm = q.shape[0]
    pre = jnp.dot(_mx_round(lr), _mx_round(wa), precision=HIGHEST, preferred_element_type=F32) + ba
    la = _logsig(pre) / GLA_TAU
    i = lax.broadcasted_iota(jnp.int32, (tm, tm), 0)
    j = lax.broadcasted_iota(jnp.int32, (tm, tm), 1)
    later = jnp.where((i // CHUNK == j // CHUNK) & (j > i), 1.0, 0.0).astype(F32)
    ci = lax.broadcasted_iota(jnp.int32, (tm // CHUNK, tm), 0)
    cj = lax.broadcasted_iota(jnp.int32, (tm // CHUNK, tm), 1) // CHUNK
    chunk_sum = jnp.where(ci == cj, 1.0, 0.0).astype(F32)
    to_end = jnp.dot(later, la, precision=HIGHEST, preferred_element_type=F32)
    lam = jnp.exp(jnp.dot(chunk_sum, la, precision=HIGHEST, preferred_element_type=F32))
    return q * GLA_SCALE, k * jnp.exp(to_end), lam


def _split_dot(x, u):
    hi = _mx(x)
    lo = _mx(x - hi.astype(F32))
    return jnp.dot(hi, u, preferred_element_type=F32) + jnp.dot(lo, u, preferred_element_type=F32)


def _tri(tq, tk):
    row = lax.broadcasted_iota(jnp.int32, (tq, tk), 0)
    col = lax.broadcasted_iota(jnp.int32, (tq, tk), 1)
    return row, col


def _sb_tile(qh, kh, valid, suf, run):
    z = lax.dot_general(qh, kh, NT, preferred_element_type=F32) * SB_SCALE
    lp = jnp.minimum(z, 0.0) - jnp.log(1.0 + jnp.exp(-jnp.abs(z)))
    lk = jnp.where(valid, lp - z, 0.0)
    inc = _split_dot(lk, suf)
    a = jnp.where(valid, jnp.exp(lp + (inc - lk + run)), 0.0)
    return lp, a, run + inc[:, 0:1]


def _sticks_left(run0, run1):
    return (jnp.maximum(jnp.max(run0), jnp.max(run1)) > -SB_CUTOFF).astype(jnp.int32)


def sb_attn_fwd(qkv, *, tq, name):
    S = qkv.shape[0]
    tk = tq
    pairs = BRANCH // 128

    def body(q_ref, k_ref, v_ref, o_ref):
        i = pl.program_id(1)
        row, col = _tri(tq, tk)
        suf = _mx(row >= col)
        q = q_ref[...]

        def step(state):
            j, _, carry = state
            ks = pl.multiple_of((i - j) * tk, tk)
            kb = k_ref[pl.ds(ks, tk), :]
            vb = v_ref[pl.ds(ks, tk), :]
            valid = (col - row) < j * tq
            new = []
            for h in range(2):
                acc, run = carry[h]
                sl = slice(64 * h, 64 * h + 64)
                _, a, run = _sb_tile(q[:, sl], kb[:, sl], valid, suf, run)
                acc = acc + jnp.dot(_mx(a), vb[:, sl], preferred_element_type=F32)
                new.append((acc, run))
            return j + 1, _sticks_left(new[0][1], new[1][1]), tuple(new)

        zero = (jnp.zeros((tq, 64), F32), jnp.zeros((tq, 1), F32))
        _, _, res = lax.while_loop(lambda s: (s[0] <= i) & (s[1] > 0), step, (jnp.int32(0), jnp.int32(1), (zero, zero)))
        o_ref[...] = jnp.concatenate([res[0][0], res[1][0]], axis=1).astype(o_ref.dtype)

    return pl.pallas_call(
        body, name=name, grid=(pairs, S // tq),
        in_specs=[pl.BlockSpec((tq, 128), lambda p, i: (i, p)),
                  pl.BlockSpec((S, 128), lambda p, i: (0, pairs + p)),
                  pl.BlockSpec((S, 128), lambda p, i: (0, 2 * pairs + p))],
        out_specs=pl.BlockSpec((tq, 128), lambda p, i: (i, p)),
        out_shape=jax.ShapeDtypeStruct((S, BRANCH), MXU_DTYPE),
        compiler_params=_params(("parallel", "parallel"), VMEM_LIMIT),
    )(qkv, qkv, qkv)


def sb_attn_bwd(qkv, do, *, tq, name):
    S = qkv.shape[0]
    tk = tq
    nq = S // tq
    pairs = BRANCH // 128

    def body(q_ref, k_ref, v_ref, do_ref, dq_ref, dk_ref, dv_ref, g_sc, b_sc):
        i = pl.program_id(1)

        @pl.when(i == 0)
        def _():
            dk_ref[...] = jnp.zeros_like(dk_ref)
            dv_ref[...] = jnp.zeros_like(dv_ref)

        row, col = _tri(tq, tk)
        suf = _mx(row >= col)
        pre = _mx(row <= col)
        q = q_ref[...]
        dout = do_ref[...]

        def sweep1(state):
            j, _, carry = state
            kblk = i - j
            ks = pl.multiple_of(kblk * tk, tk)
            kb = k_ref[pl.ds(ks, tk), :]
            vb = v_ref[pl.ds(ks, tk), :]
            valid = (col - row) < j * tq
            runs, dvs = [], []
            for h in range(2):
                sl = slice(64 * h, 64 * h + 64)
                lp, a, run = _sb_tile(q[:, sl], kb[:, sl], valid, suf, carry[h])
                da = lax.dot_general(dout[:, sl], vb[:, sl], NT, preferred_element_type=F32)
                g_sc[h, kblk] = (a * da).astype(g_sc.dtype)
                b_sc[h, kblk] = jnp.where(valid, jnp.exp(lp), 0.0).astype(b_sc.dtype)
                dvs.append(lax.dot_general(_mx(a), dout[:, sl], TN, preferred_element_type=F32))
                runs.append(run)
            dv_ref[pl.ds(ks, tk), :] += jnp.concatenate(dvs, axis=1)
            return j + 1, _sticks_left(runs[0], runs[1]), tuple(runs)

        start = (jnp.int32(0), jnp.int32(1), (jnp.zeros((tq, 1), F32), jnp.zeros((tq, 1), F32)))
        tiles, _, _ = lax.while_loop(lambda s: (s[0] <= i) & (s[1] > 0), sweep1, start)

        def sweep2(kblk, carry):
            ks = pl.multiple_of(kblk * tk, tk)
            kb = k_ref[pl.ds(ks, tk), :]
            new, dks = [], []
            for h in range(2):
                dq, run = carry[h]
                sl = slice(64 * h, 64 * h + 64)
                g = g_sc[h, kblk]
                beta = b_sc[h, kblk].astype(F32)
                inc = jnp.dot(g, pre, preferred_element_type=F32)
                g = g.astype(F32)
                dz = _mx((g - beta * (inc + run)) * SB_SCALE)
                dq = dq + jnp.dot(dz, kb[:, sl], preferred_element_type=F32)
                dks.append(lax.dot_general(dz, q[:, sl], TN, preferred_element_type=F32))
                new.append((dq, run + inc[:, tk - 1:tk]))
            dk_ref[pl.ds(ks, tk), :] += jnp.concatenate(dks, axis=1)
            return tuple(new)

        zero = (jnp.zeros((tq, 64), F32), jnp.zeros((tq, 1), F32))
        res = lax.fori_loop(i + 1 - tiles, i + 1, sweep2, (zero, zero))
        dq_ref[...] = jnp.concatenate([res[0][0], res[1][0]], axis=1).astype(dq_ref.dtype)

    return pl.pallas_call(
        body, name=name, grid=(pairs, nq),
        in_specs=[pl.BlockSpec((tq, 128), lambda p, i: (i, p)),
                  pl.BlockSpec((S, 128), lambda p, i: (0, pairs + p)),
                  pl.BlockSpec((S, 128), lambda p, i: (0, 2 * pairs + p)),
                  pl.BlockSpec((tq, 128), lambda p, i: (i, p))],
        out_specs=[pl.BlockSpec((tq, 128), lambda p, i: (i, p)),
                   pl.BlockSpec((S, 128), lambda p, i: (0, p)),
                   pl.BlockSpec((S, 128), lambda p, i: (0, p))],
        out_shape=[jax.ShapeDtypeStruct((S, BRANCH), MXU_DTYPE), jax.ShapeDtypeStruct((S, BRANCH), F32),
                   jax.ShapeDtypeStruct((S, BRANCH), F32)],
        scratch_shapes=[pltpu.VMEM((2, nq, tq, tk), MXU_DTYPE), pltpu.VMEM((2, nq, tq, tk), MXU_DTYPE)],
        compiler_params=_params(("parallel", "arbitrary"), VMEM_LIMIT),
    )(qkv, qkv, qkv, do)


def conv_fwd(u, w, b, *, tm, name):
    S, C = u.shape
    hb = tm // CONV_PAD

    def body(u_ref, halo_ref, w_ref, b_ref, y_ref, buf):
        i = pl.program_id(0)
        buf[pl.ds(CONV_PAD, tm), :] = u_ref[...]
        buf[pl.ds(0, CONV_PAD), :] = jnp.where(i > 0, halo_ref[...], 0.0)
        acc = jnp.broadcast_to(b_ref[...], (tm, 128))
        for j in range(CONV_WIDTH):
            acc = acc + buf[pl.ds(CONV_PAD - (CONV_WIDTH - 1) + j, tm), :] * w_ref[pl.ds(j, 1), :]
        y_ref[...] = acc

    return pl.pallas_call(
        body, name=name, grid=(S // tm, C // 128),
        in_specs=[pl.BlockSpec((tm, 128), lambda i, c: (i, c)),
                  pl.BlockSpec((CONV_PAD, 128), lambda i, c: (jnp.maximum(i * hb - 1, 0), c)),
                  pl.BlockSpec((CONV_PAD, 128), lambda i, c: (0, c)),
                  pl.BlockSpec((1, 128), lambda i, c: (0, c))],
        out_specs=pl.BlockSpec((tm, 128), lambda i, c: (i, c)),
        out_shape=jax.ShapeDtypeStruct((S, C), F32),
        scratch_shapes=[pltpu.VMEM((tm + CONV_PAD, 128), F32)],
        compiler_params=_params(("parallel", "parallel")),
    )(u, u, w, b)


def conv_bwd(dy, u, w, *, tm, name):
    S, C = u.shape
    hb = tm // CONV_PAD
    n = S // tm

    def body(dy_ref, dyn_ref, u_ref, up_ref, w_ref, du_ref, dw_ref, bufy, bufu):
        i = pl.program_id(1)
        dyv = dy_ref[...]
        bufy[pl.ds(0, tm), :] = dyv
        bufy[pl.ds(tm, CONV_PAD), :] = jnp.where(i < n - 1, dyn_ref[...], 0.0)
        bufu[pl.ds(CONV_PAD, tm), :] = u_ref[...]
        bufu[pl.ds(0, CONV_PAD), :] = jnp.where(i > 0, up_ref[...], 0.0)

        @pl.when(i == 0)
        def _():
            dw_ref[...] = jnp.zeros_like(dw_ref)

        acc = jnp.zeros((tm, 128), F32)
        for j in range(CONV_WIDTH):
            acc = acc + bufy[pl.ds(CONV_WIDTH - 1 - j, tm), :] * w_ref[pl.ds(j, 1), :]
            shifted = bufu[pl.ds(CONV_PAD - (CONV_WIDTH - 1) + j, tm), :]
            dw_ref[pl.ds(j, 1), :] += jnp.sum(dyv * shifted, axis=0, keepdims=True)
        du_ref[...] = acc
        dw_ref[pl.ds(CONV_WIDTH, 1), :] += jnp.sum(dyv, axis=0, keepdims=True)

    return pl.pallas_call(
        body, name=name, grid=(C // 128, n),
        in_specs=[pl.BlockSpec((tm, 128), lambda c, i: (i, c)),
                  pl.BlockSpec((CONV_PAD, 128), lambda c, i: (jnp.minimum((i + 1) * hb, n * hb - 1), c)),
                  pl.BlockSpec((tm, 128), lambda c, i: (i, c)),
                  pl.BlockSpec((CONV_PAD, 128), lambda c, i: (jnp.maximum(i * hb - 1, 0), c)),
                  pl.BlockSpec((CONV_PAD, 128), lambda c, i: (0, c))],
        out_specs=[pl.BlockSpec((tm, 128), lambda c, i: (i, c)), pl.BlockSpec((CONV_PAD, 128), lambda c, i: (0, c))],
        out_shape=[jax.ShapeDtypeStruct((S, C), F32), jax.ShapeDtypeStruct((CONV_PAD, C), F32)],
        scratch_shapes=[pltpu.VMEM((tm + CONV_PAD, 128), F32), pltpu.VMEM((tm + CONV_PAD, 128), F32)],
        compiler_params=_params(("parallel", "arbitrary")),
    )(dy, dy, u, u, w)


GLA_HEADS, GLA_DK, GLA_DV = 4, 64, 128


def gla_scan_fwd(qs, kd, gl, lam, *, tm, name):
    S = qs.shape[0]
    nc = tm // CHUNK

    def body(qs_ref, kd_ref, v_ref, lam_ref, o_ref, st_ref, state):
        @pl.when(pl.program_id(0) == 0)
        def _():
            state[...] = jnp.zeros_like(state)

        for c in range(nc):
            rows = pl.ds(c * CHUNK, CHUNK)
            q, k, v = _mx(qs_ref[rows, :]), _mx(kd_ref[rows, :]), _mx(v_ref[rows, :])
            upd = [lax.dot_general(v[:, h * GLA_DV:(h + 1) * GLA_DV], k[:, h * GLA_DK:(h + 1) * GLA_DK], TN,
                                   preferred_element_type=F32) for h in range(GLA_HEADS)]
            st = state[...] * lam_ref[pl.ds(c, 1), :] + jnp.concatenate(upd, axis=1)
            state[...] = st
            st_ref[c] = st
            stm = _mx(st)
            o = [lax.dot_general(q[:, h * GLA_DK:(h + 1) * GLA_DK], stm[:, h * GLA_DK:(h + 1) * GLA_DK], NT,
                                 preferred_element_type=F32) for h in range(GLA_HEADS)]
            o_ref[rows, :] = jnp.concatenate(o, axis=1)

    dk_all = GLA_HEADS * GLA_DK
    dv_all = GLA_HEADS * GLA_DV
    return pl.pallas_call(
        body, name=name, grid=(S // tm,),
        in_specs=[pl.BlockSpec((tm, dk_all), lambda i: (i, 0)), pl.BlockSpec((tm, dk_all), lambda i: (i, 0)),
                  pl.BlockSpec((tm, dv_all), lambda i: (i, 1)), pl.BlockSpec((nc, dk_all), lambda i: (i, 0))],
        out_specs=[pl.BlockSpec((tm, dv_all), lambda i: (i, 0)), pl.BlockSpec((nc, GLA_DV, dk_all), lambda i: (i, 0, 0))],
        out_shape=[jax.ShapeDtypeStruct((S, dv_all), F32), jax.ShapeDtypeStruct((S // CHUNK, GLA_DV, dk_all), F32)],
        scratch_shapes=[pltpu.VMEM((GLA_DV, dk_all), F32)],
        compiler_params=_params(("arbitrary",)),
    )(qs, kd, gl, lam)


def gla_scan_bwd(do, qs, kd, gl, lam, states, *, tm, name):
    S = qs.shape[0]
    nc = tm // CHUNK
    n = S // tm
    dk_all = GLA_HEADS * GLA_DK
    dv_all = GLA_HEADS * GLA_DV

    def body(do_ref, qs_ref, kd_ref, v_ref, lam_ref, st_ref, prev_ref, dqs_ref, dkd_ref, dv_ref, dlam_ref, carry):
        i = pl.program_id(0)

        @pl.when(i == 0)
        def _():
            carry[...] = jnp.zeros_like(carry)

        for c in reversed(range(nc)):
            rows = pl.ds(c * CHUNK, CHUNK)
            q, k, v, d = _mx(qs_ref[rows, :]), _mx(kd_ref[rows, :]), _mx(v_ref[rows, :]), _mx(do_ref[rows, :])
            st = _mx(st_ref[c])
            before = st_ref[c - 1] if c > 0 else jnp.where(i < n - 1, prev_ref[0], 0.0)
            outer = [lax.dot_general(d[:, h * GLA_DV:(h + 1) * GLA_DV], q[:, h * GLA_DK:(h + 1) * GLA_DK], TN,
                                     preferred_element_type=F32) for h in range(GLA_HEADS)]
            dst = carry[...] + jnp.concatenate(outer, axis=1)
            dstm = _mx(dst)
            dq, dkk, dvv = [], [], []
            for h in range(GLA_HEADS):
                ksl = slice(h * GLA_DK, (h + 1) * GLA_DK)
                vsl = slice(h * GLA_DV, (h + 1) * GLA_DV)
                dq.append(jnp.dot(d[:, vsl], st[:, ksl], preferred_element_type=F32))
                dkk.append(jnp.dot(v[:, vsl], dstm[:, ksl], preferred_element_type=F32))
                dvv.append(lax.dot_general(k[:, ksl], dstm[:, ksl], NT, preferred_element_type=F32))
            dqs_ref[rows, :] = jnp.concatenate(dq, axis=1)
            dkd_ref[rows, :] = jnp.concatenate(dkk, axis=1)
            dv_ref[rows, :] = jnp.concatenate(dvv, axis=1)
            dlam_ref[pl.ds(c, 1), :] = jnp.sum(dst * before, axis=0, keepdims=True)
            carry[...] = dst * lam_ref[pl.ds(c, 1), :]

    rev = lambda i: n - 1 - i
    return pl.pallas_call(
        body, name=name, grid=(n,),
        in_specs=[pl.BlockSpec((tm, dv_all), lambda i: (rev(i), 0)), pl.BlockSpec((tm, dk_all), lambda i: (rev(i), 0)),
                  pl.BlockSpec((tm, dk_all), lambda i: (rev(i), 0)), pl.BlockSpec((tm, dv_all), lambda i: (rev(i), 1)),
                  pl.BlockSpec((nc, dk_all), lambda i: (rev(i), 0)),
                  pl.BlockSpec((nc, GLA_DV, dk_all), lambda i: (rev(i), 0, 0)),
                  pl.BlockSpec((1, GLA_DV, dk_all), lambda i: (jnp.maximum(rev(i) * nc - 1, 0), 0, 0))],
        out_specs=[pl.BlockSpec((tm, dk_all), lambda i: (rev(i), 0)), pl.BlockSpec((tm, dk_all), lambda i: (rev(i), 0)),
                   pl.BlockSpec((tm, dv_all), lambda i: (rev(i), 0)), pl.BlockSpec((nc, dk_all), lambda i: (rev(i), 0))],
        out_shape=[jax.ShapeDtypeStruct((S, dk_all), F32), jax.ShapeDtypeStruct((S, dk_all), F32),
                   jax.ShapeDtypeStruct((S, dv_all), F32), jax.ShapeDtypeStruct((S // CHUNK, dk_all), F32)],
        scratch_shapes=[pltpu.VMEM((GLA_DV, dk_all), F32)],
        compiler_params=_params(("arbitrary",)),
    )(do, qs, kd, gl, lam, states, states)


def loss_head(y, target, *, tm, name):
    S, D = y.shape

    def body(y_ref, t_ref, dy_ref, sq_ref):
        err = y_ref[...] - t_ref[...]
        dy_ref[...] = err * (1.0 / D)
        part = jnp.sum(err * err, axis=0, keepdims=True)

        @pl.when(pl.program_id(0) == 0)
        def _():
            sq_ref[...] = part

        @pl.when(pl.program_id(0) > 0)
        def _():
            sq_ref[...] += part

    return pl.pallas_call(
        body, name=name, grid=(S // tm,),
        in_specs=[pl.BlockSpec((tm, D), lambda i: (i, 0)), pl.BlockSpec((tm, D), lambda i: (i, 0))],
        out_specs=[pl.BlockSpec((tm, D), lambda i: (i, 0)), pl.BlockSpec((1, D), lambda i: (0, 0))],
        out_shape=[jax.ShapeDtypeStruct((S, D), F32), jax.ShapeDtypeStruct((1, D), F32)],
        compiler_params=_params(("arbitrary",)),
    )(y, target)


def f_adamw(w, g, m, v):
    m = ADAM_B1 * m + (1.0 - ADAM_B1) * g
    v = ADAM_B2 * v + (1.0 - ADAM_B2) * jnp.square(g)
    m_hat = m / (1.0 - ADAM_B1 ** ADAM_STEP)
    v_hat = v / (1.0 - ADAM_B2 ** ADAM_STEP)
    return -ADAM_LR * (m_hat / (jnp.sqrt(v_hat) + ADAM_EPS) + ADAM_WD * w), m, v


def adamw(w, g, m, v, *, name):
    shape = w.shape
    cols = shape[-1]
    rows = w.size // cols
    tm = rows
    while tm % 16 == 0 and tm * cols * 4 > (1 << 20):
        tm //= 2
    flat = [t.reshape(rows, cols) for t in (w, g, m, v)]
    outs = rowwise(f_adamw, flat, [], [F32, F32, F32], tm=tm, name=name)
    return [o.reshape(shape) for o in outs]


def _place():
    x, y, c = lax.axis_index("x"), lax.axis_index("y"), lax.axis_index("c")
    return x, y, c, [(1 - x, y), (x, 1 - y), (1 - x, 1 - y)]


def _my_chip():
    return 2 * lax.axis_index("x") + lax.axis_index("y")


def _any():
    return pl.BlockSpec(memory_space=pl.ANY)


DMA_CHUNKS = 8


def _chunks(rows):
    n = DMA_CHUNKS
    while n > 1 and rows % (n * 16):
        n //= 2
    return [(k * (rows // n), rows // n) for k in range(n)]


def _start_chunked(make, rows):
    for off, size in _chunks(rows):
        make(off, size).start()


def all_gather_chips(v, *, name):
    R, C = v.shape
    H = R // 2

    def body(v_ref, o_ref, send, recv, fsend, frecv):
        x, y, c, chips = _place()
        me = 2 * x + y
        ids = [2 * px + py for px, py in chips]
        mine = pl.ds(pl.multiple_of(c * H, 8), H)
        other = pl.ds(pl.multiple_of((1 - c) * H, 8), H)

        def cross(j, src_chip, rows):
            return pltpu.make_async_remote_copy(
                src_ref=v_ref.at[rows], dst_ref=o_ref.at[src_chip, rows], send_sem=send.at[j], recv_sem=recv.at[j],
                device_id=(*chips[j], c), device_id_type=MESH)

        def handed(j, rows):
            return pltpu.make_async_remote_copy(
                src_ref=o_ref.at[ids[j], rows], dst_ref=o_ref.at[ids[j], rows], send_sem=fsend.at[j],
                recv_sem=frecv.at[j], device_id=(x, y, 1 - c), device_id_type=MESH)

        def my_rows(off, size):
            return pl.ds(pl.multiple_of(c * H + off, 8), size)

        for j in range(3):
            _start_chunked(lambda off, size, j=j: cross(j, me, my_rows(off, size)), H)
        for j in range(3):
            cross(j, ids[j], mine).wait_recv()
            _start_chunked(lambda off, size, j=j: handed(j, my_rows(off, size)), H)
        for j in range(3):
            handed(j, other).wait_recv()
        for j in range(3):
            cross(j, me, mine).wait_send()
            handed(j, mine).wait_send()

    got = pl.pallas_call(
        body, name=name, in_specs=[_any()], out_specs=_any(), out_shape=jax.ShapeDtypeStruct((4, R, C), v.dtype),
        scratch_shapes=[pltpu.SemaphoreType.DMA((3,)), pltpu.SemaphoreType.DMA((3,)), pltpu.SemaphoreType.DMA((3,)),
                        pltpu.SemaphoreType.DMA((3,))],
    )(v)
    return lax.dynamic_update_slice(got, v[None], (_my_chip(), 0, 0))


def swap_halves(g, *, name):
    n, R, C = g.shape
    H = R // 2

    def body(g_ref, theirs_ref, send, recv):
        x, y, c, _ = _place()

        def give(q, off, size):
            return pltpu.make_async_remote_copy(
                src_ref=g_ref.at[q, pl.ds(pl.multiple_of((1 - c) * H + off, 8), size)],
                dst_ref=theirs_ref.at[q, pl.ds(off, size)], send_sem=send.at[q], recv_sem=recv.at[q],
                device_id=(x, y, 1 - c), device_id_type=MESH)

        for q in range(n):
            _start_chunked(functools.partial(give, q), H)
        for q in range(n):
            give(q, 0, H).wait()

    return pl.pallas_call(
        body, name=name, in_specs=[_any()], out_specs=_any(), out_shape=jax.ShapeDtypeStruct((n, H, C), g.dtype),
        scratch_shapes=[pltpu.SemaphoreType.DMA((n,)), pltpu.SemaphoreType.DMA((n,))],
    )(g)


def exchange_pieces(p, *, name):
    H = p.shape[1]

    def body(p_ref, b_ref, send, recv):
        x, y, c, chips = _place()
        me = 2 * x + y
        ids = [2 * px + py for px, py in chips]

        def cross(j, piece, slot, off, size):
            return pltpu.make_async_remote_copy(
                src_ref=p_ref.at[piece, pl.ds(off, size)], dst_ref=b_ref.at[slot, pl.ds(off, size)], send_sem=send.at[j],
                recv_sem=recv.at[j], device_id=(*chips[j], c), device_id_type=MESH)

        for j in range(3):
            _start_chunked(functools.partial(cross, j, ids[j], me), H)
        for j in range(3):
            cross(j, me, ids[j], 0, H).wait_recv()
        for j in range(3):
            cross(j, ids[j], me, 0, H).wait_send()

    got = pl.pallas_call(
        body, name=name, in_specs=[_any()], out_specs=_any(), out_shape=jax.ShapeDtypeStruct(p.shape, p.dtype),
        scratch_shapes=[pltpu.SemaphoreType.DMA((3,)), pltpu.SemaphoreType.DMA((3,))],
    )(p)
    me = _my_chip()
    return lax.dynamic_update_slice(got, lax.dynamic_slice_in_dim(p, me, 1, axis=0), (me, 0, 0))


def join_halves(f, *, name):
    H, C = f.shape

    def body(f_ref, o_ref, send, recv):
        x, y, c, _ = _place()

        def give(off, size):
            return pltpu.make_async_remote_copy(
                src_ref=f_ref.at[pl.ds(off, size)], dst_ref=o_ref.at[pl.ds(off, size)], send_sem=send, recv_sem=recv,
                device_id=(x, y, 1 - c), device_id_type=MESH)

        _start_chunked(give, H)
        give(0, H).wait()

    theirs = pl.pallas_call(
        body, name=name, in_specs=[_any()], out_specs=_any(), out_shape=jax.ShapeDtypeStruct((H, C), f.dtype),
        scratch_shapes=[pltpu.SemaphoreType.DMA, pltpu.SemaphoreType.DMA],
    )(f)
    south = lax.axis_index("c") == 0
    return jnp.concatenate([jnp.where(south, f, theirs), jnp.where(south, theirs, f)], axis=0)


def _row_tile(rows, want):
    tm = want
    while rows % tm:
        tm //= 2
    return tm


def _add_pair(g, theirs, out_dtype, *, name):
    n, R, C = g.shape
    H = R // 2
    tm = _row_tile(H, 512)
    nb = H // tm

    def body(c_ref, g_ref, t_ref, o_ref):
        o_ref[...] = (g_ref[...] + t_ref[...]).astype(o_ref.dtype)

    grid_spec = pltpu.PrefetchScalarGridSpec(
        num_scalar_prefetch=1, grid=(n, nb),
        in_specs=[pl.BlockSpec((None, tm, C), lambda q, i, c: (q, c[0] * nb + i, 0)),
                  pl.BlockSpec((None, tm, C), lambda q, i, c: (q, i, 0))],
        out_specs=pl.BlockSpec((None, tm, C), lambda q, i, c: (q, i, 0)))
    return pl.pallas_call(
        body, name=name, grid_spec=grid_spec, out_shape=jax.ShapeDtypeStruct((n, H, C), out_dtype),
        compiler_params=_params(("parallel", "parallel")),
    )(lax.axis_index("c").astype(jnp.int32).reshape(1), g, theirs)


def _add_stack(b, *, name):
    n, H, C = b.shape
    tm = _row_tile(H, 256)

    def body(b_ref, o_ref):
        s = b_ref[0].astype(F32)
        for k in range(1, n):
            s = s + b_ref[k].astype(F32)
        o_ref[...] = s

    return pl.pallas_call(
        body, name=name, grid=(H // tm,), in_specs=[pl.BlockSpec((n, tm, C), lambda i: (0, i, 0))],
        out_specs=pl.BlockSpec((tm, C), lambda i: (i, 0)), out_shape=jax.ShapeDtypeStruct((H, C), F32),
        compiler_params=_params(("parallel",)),
    )(b)


def reduce_scatter(g, tag):
    pair = _add_pair(g, swap_halves(g, name=tag + "_swap_halves"), GRAD_WIRE_DTYPE, name=tag + "_add_pair")
    total = _add_stack(exchange_pieces(pair, name=tag + "_exchange"), name=tag + "_add_chips")
    return join_halves(total, name=tag + "_join")


FFN = ("ffn1_w_gate", "ffn1_w_up", "ffn2_w_gate", "ffn2_w_up", "ffn1_w_down", "ffn2_w_down")
BIG = ("w_in", "w_branch", "w_out")
SMALL = ("norm_pre", "norm_post", "conv_w", "gla_w_alpha")
REPLICATED = ("conv_b", "conv_ln_g", "conv_ln_b", "gla_b_alpha", "gla_norm_g")
WEIGHTS = ("norm_pre", "norm_post", "ffn1_w_gate", "ffn1_w_up", "ffn1_w_down", "ffn2_w_gate", "ffn2_w_up", "ffn2_w_down",
           "w_in", "conv_w", "conv_b", "conv_ln_g", "conv_ln_b", "gla_w_alpha", "gla_b_alpha", "gla_norm_g", "w_branch",
           "w_out")
SHARD_AXIS = {"ffn1_w_gate": 2, "ffn1_w_up": 2, "ffn1_w_down": 1, "ffn2_w_gate": 2, "ffn2_w_up": 2, "ffn2_w_down": 1,
              "w_in": 2, "w_branch": 3, "w_out": 1, "norm_pre": 2, "norm_post": 2, "conv_w": 2, "gla_w_alpha": 2}


def _size(shape):
    n = 1
    for d in shape:
        n *= d
    return n


def _pack_rows(shape):
    return -(-_size(shape) // (16 * PACK_COLS)) * 16


def _pack(arrs, dtype, row_mult):
    parts, rows = [], 0
    for a in arrs:
        r = _pack_rows(a.shape)
        flat = a.astype(dtype).reshape(-1)
        if r * PACK_COLS != flat.shape[0]:
            flat = jnp.pad(flat, (0, r * PACK_COLS - flat.shape[0]))
        parts.append(flat.reshape(r, PACK_COLS))
        rows += r
    if rows % row_mult:
        parts.append(jnp.zeros((row_mult - rows % row_mult, PACK_COLS), dtype))
    return jnp.concatenate(parts, axis=0)


def _unpack(buf, shapes):
    lead = buf.shape[:-2]
    out, off = [], 0
    for s in shapes:
        r = _pack_rows(s)
        part = lax.slice_in_dim(buf, off, off + r, axis=buf.ndim - 2).reshape(lead + (r * PACK_COLS,))
        if r * PACK_COLS != _size(s):
            part = lax.slice_in_dim(part, 0, _size(s), axis=part.ndim - 1)
        out.append(part.reshape(lead + tuple(s)))
        off += r
    return out


def gather_weights(shards, names, dtype, row_mult, name):
    packed = _pack([shards[k] for k in names], dtype, row_mult)
    parts = _unpack(all_gather_chips(packed, name=name), [shards[k].shape for k in names])
    return {k: jnp.concatenate([p[q] for q in range(4)], axis=SHARD_AXIS[k]) for k, p in zip(names, parts)}


def scatter_grads(grads, shards):
    sharded = BIG + SMALL
    pieces = []
    for q in range(4):
        part = []
        for k in sharded:
            w = shards[k].shape[SHARD_AXIS[k]]
            part.append(lax.slice_in_dim(grads[k], q * w, (q + 1) * w, axis=SHARD_AXIS[k]))
        part += [grads[k] for k in REPLICATED]
        pieces.append(_pack(part, F32, 512))
    total = reduce_scatter(jnp.stack(pieces), "rs_mix")
    names = sharded + REPLICATED
    return dict(zip(names, _unpack(total, [shards[k].shape for k in names])))


TM = 256
TM_GLA = 512
TQ = 256

IN_SB, IN_CONV, IN_GLA, IN_LR, IN_GATE = 0, 1536, 2560, 4096, 4112
IN_END = 7184


def layer_weights(full, l):
    w = {}
    win = full["w_in"][l]
    w["in_sb"] = win[:, IN_SB:IN_CONV]
    w["in_conv"] = win[:, IN_CONV:IN_GLA]
    w["in_gla"] = win[:, IN_GLA:IN_LR]
    w["in_lr"] = jnp.pad(win[:, IN_LR:IN_GATE], ((0, 0), (0, GLA_RANK_PAD - GLA_RANK)))
    w["in_gate"] = win[:, IN_GATE:IN_END]
    w["branch"] = [full["w_branch"][l, g] for g in range(3)]
    w["out"] = full["w_out"][l]
    return w


def ffn_fwd(x, gpre, gpost, w6, f, l, tag):
    S = x.shape[0]
    h = rowwise(f_rms, [x], [gpre], [MXU_DTYPE], tm=TM, name=tag + "_pre")[0]
    ab = slab_nt(h, w6, 2 * f, 8, l, out_dtype=MXU_DTYPE, name=tag + "_gu")
    z = swiglu_fwd(ab.reshape(2, 4 * S, SLAB), tm=2 * TM, name=tag + "_act").reshape(4, S, SLAB)
    out = slab_nn(z, w6, 4 + f, l, name=tag + "_down")
    x2 = rowwise(f_half_post, [x, out], [gpost], [F32], tm=TM, name=tag + "_post")[0]
    return x2, (x, h, ab, z, out)


def ffn_bwd(dx2, res, gpre, gpost, w6, g6, f, l, tag):
    x, h, ab, z, out = res
    S = x.shape[0]
    df, dgpost = rowwise_vjp(f_half_post, [x, out], [gpost], [dx2], [1], [0], [MXU_DTYPE], tm=TM, name=tag + "_post_b")
    dz = slab_nt(df, w6, 4 + f, 4, l, out_dtype=MXU_DTYPE, name=tag + "_down_dx")
    g6 = slab_tn(z, df, g6, 4 + f, l, name=tag + "_down_dw")
    dab = swiglu_bwd(ab.reshape(2, 4 * S, SLAB), dz.reshape(4 * S, SLAB), tm=2 * TM, name=tag + "_act_b")
    dab = dab.reshape(8, S, SLAB)
    dh = slab_nn(dab, w6, 2 * f, l, name=tag + "_gu_dx")
    g6 = slab_tn(dab, h, g6, 2 * f, l, name=tag + "_gu_dw")
    dx, dgpre = rowwise_vjp(f_rms, [x], [gpre], [dh], [0], [0], [F32], [dx2], tm=TM, name=tag + "_pre_b")
    return dx, dgpre, dgpost, g6


def mixer_fwd(x, p, w, tag):
    h = rowwise(f_rms, [x], [p["gpre"]], [MXU_DTYPE], tm=TM, name=tag + "_pre")[0]
    qkv = mm(h, w["in_sb"], out_dtype=MXU_DTYPE, tn=512, name=tag + "_in_sb")
    cv = mm(h, w["in_conv"], name=tag + "_in_conv")
    gl = mm(h, w["in_gla"], out_dtype=MXU_DTYPE, tn=512, name=tag + "_in_gla")
    lr = mm(h, w["in_lr"], out_dtype=MXU_DTYPE, name=tag + "_in_lr")
    gt = mm(h, w["in_gate"], out_dtype=MXU_DTYPE, name=tag + "_in_gate")
    sb = sb_attn_fwd(qkv, tq=TQ, name=tag + "_sb")
    u = rowwise(f_glu, [(cv, BRANCH, 0), (cv, BRANCH, 1)], [], [F32], tm=TM, name=tag + "_glu")[0]
    y = conv_fwd(u, p["conv_w"], p["conv_b"], tm=TM, name=tag + "_conv")
    cb = rowwise(f_conv_ln, [y], [p["ln_g"], p["ln_b"]], [MXU_DTYPE], tm=TM, name=tag + "_ln")[0]
    qs, kd, lam = rowwise(f_gla_pre, [(gl, 256, 0), (gl, 256, 1), lr], [p["wa"], p["ba"]], [MXU_DTYPE, MXU_DTYPE, F32],
                          tm=TM_GLA, name=tag + "_gla_pre")
    o, states = gla_scan_fwd(qs, kd, gl, lam, tm=TM_GLA, name=tag + "_gla_scan")
    gb = rowwise(f_gla_post, [o, (gl, BRANCH, 2)], [p["gn"]], [MXU_DTYPE], tm=TM, name=tag + "_gla_post")[0]
    branches = [sb, cb, gb]
    bd = [mm(branches[g], w["branch"][g], out_dtype=MXU_DTYPE, name=tag + f"_branch{g}") for g in range(3)]
    merged = rowwise(f_merge, [(gt, D_MODEL, 0), (gt, D_MODEL, 1), (gt, D_MODEL, 2)] + bd, [], [MXU_DTYPE], tm=TM,
                     name=tag + "_merge")[0]
    m = mm(merged, w["out"], name=tag + "_out")
    x2 = rowwise(f_post, [x, m], [p["gpost"]], [F32], tm=TM, name=tag + "_post")[0]
    return x2, (x, h, qkv, cv, gl, lr, gt, u, y, qs, kd, lam, o, states, branches, bd, merged, m)


def mixer_bwd(dx2, res, p, w, tag):
    x, h, qkv, cv, gl, lr, gt, u, y, qs, kd, lam, o, states, branches, bd, merged, m = res
    g = {}
    dm, g["gpost"] = rowwise_vjp(f_post, [x, m], [p["gpost"]], [dx2], [1], [0], [MXU_DTYPE], tm=TM, name=tag + "_post_b")
    dmerged = mm(dm, w["out"], tb=True, out_dtype=MXU_DTYPE, name=tag + "_out_dx")
    g["out"] = mm(merged, dm, ta=True, name=tag + "_out_dw")
    gts = [(gt, D_MODEL, 0), (gt, D_MODEL, 1), (gt, D_MODEL, 2)]
    dgate = rowwise_vjp(f_merge, gts + bd, [], [dmerged], [0, 1, 2, 3, 4, 5], [], [MXU_DTYPE] * 6, tm=TM,
                        name=tag + "_merge_b")
    dgt = jnp.concatenate(dgate[:3], axis=1)
    dbd = dgate[3:]
    g["branch"] = [mm(branches[k], dbd[k], ta=True, name=tag + f"_branch{k}_dw") for k in range(3)]
    dsb = mm(dbd[0], w["branch"][0], tb=True, out_dtype=MXU_DTYPE, name=tag + "_branch0_dx")
    dq, dk, dv = sb_attn_bwd(qkv, dsb, tq=TQ, name=tag + "_sb_b")
    dqkv = jnp.concatenate([dq, _mx(dk), _mx(dv)], axis=1)
    dcb = mm(dbd[1], w["branch"][1], tb=True, name=tag + "_branch1_dx")
    dy, g["ln_g"], g["ln_b"] = rowwise_vjp(f_conv_ln, [y], [p["ln_g"], p["ln_b"]], [dcb], [0], [0, 1], [F32], tm=TM,
                                           name=tag + "_ln_b")
    du, dwb = conv_bwd(dy, u, p["conv_w"], tm=TM, name=tag + "_conv_b")
    g["conv_w"], g["conv_b"] = dwb[:CONV_WIDTH], dwb[CONV_WIDTH:CONV_WIDTH + 1]
    dca, dcg = rowwise_vjp(f_glu, [(cv, BRANCH, 0), (cv, BRANCH, 1)], [], [du], [0, 1], [], [MXU_DTYPE, MXU_DTYPE], tm=TM,
                           name=tag + "_glu_b")
    dcv = jnp.concatenate([dca, dcg], axis=1)
    dgb = mm(dbd[2], w["branch"][2], tb=True, name=tag + "_branch2_dx")
    do, dr, g["gn"] = rowwise_vjp(f_gla_post, [o, (gl, BRANCH, 2)], [p["gn"]], [dgb], [0, 1], [0], [F32, MXU_DTYPE], tm=TM,
                                  name=tag + "_gla_post_b")
    dqs, dkd, dgv, dlam = gla_scan_bwd(do, qs, kd, gl, lam, states, tm=TM_GLA, name=tag + "_gla_scan_b")
    dgq, dgk, dlr, g["wa"], g["ba"] = rowwise_vjp(
        f_gla_pre, [(gl, 256, 0), (gl, 256, 1), lr], [p["wa"], p["ba"]], [dqs, dkd, dlam], [0, 1, 2], [0, 1],
        [MXU_DTYPE, MXU_DTYPE, MXU_DTYPE], tm=TM_GLA, name=tag + "_gla_pre_b")
    dgl = jnp.concatenate([dgq, dgk, _mx(dgv), dr], axis=1)
    secs = [("in_sb", dqkv), ("in_conv", dcv), ("in_gla", dgl), ("in_lr", dlr), ("in_gate", dgt)]
    dh = None
    for k, d in secs:
        dh = mm(d, w[k], dh, tb=True, name=tag + "_" + k + "_dx")
        g[k] = mm(h, d, ta=True, tn=1536, name=tag + "_" + k + "_dw")
    dx, g["gpre"] = rowwise_vjp(f_rms, [x], [p["gpre"]], [dh], [0], [0], [F32], [dx2], tm=TM, name=tag + "_pre_b")
    return dx, g


def kernel(x, norm_pre, norm_post, ffn1_w_gate, ffn1_w_up, ffn1_w_down, ffn2_w_gate, ffn2_w_up, ffn2_w_down, w_in, conv_w, conv_b, conv_ln_g, conv_ln_b, gla_w_alpha, gla_b_alpha, gla_norm_g, w_branch, w_out, loss_target, m_norm_pre, m_norm_post, m_ffn1_w_gate, m_ffn1_w_up, m_ffn1_w_down, m_ffn2_w_gate, m_ffn2_w_up, m_ffn2_w_down, m_w_in, m_conv_w, m_conv_b, m_conv_ln_g, m_conv_ln_b, m_gla_w_alpha, m_gla_b_alpha, m_gla_norm_g, m_w_branch, m_w_out, v_norm_pre, v_norm_post, v_ffn1_w_gate, v_ffn1_w_up, v_ffn1_w_down, v_ffn2_w_gate, v_ffn2_w_up, v_ffn2_w_down, v_w_in, v_conv_w, v_conv_b, v_conv_ln_g, v_conv_ln_b, v_gla_w_alpha, v_gla_b_alpha, v_gla_norm_g, v_w_branch, v_w_out):
    shards = dict(norm_pre=norm_pre, norm_post=norm_post, ffn1_w_gate=ffn1_w_gate, ffn1_w_up=ffn1_w_up,
                  ffn1_w_down=ffn1_w_down, ffn2_w_gate=ffn2_w_gate, ffn2_w_up=ffn2_w_up, ffn2_w_down=ffn2_w_down, w_in=w_in,
                  conv_w=conv_w, conv_b=conv_b, conv_ln_g=conv_ln_g, conv_ln_b=conv_ln_b, gla_w_alpha=gla_w_alpha,
                  gla_b_alpha=gla_b_alpha, gla_norm_g=gla_norm_g, w_branch=w_branch, w_out=w_out)
    mom_m = dict(zip(WEIGHTS, (m_norm_pre, m_norm_post, m_ffn1_w_gate, m_ffn1_w_up, m_ffn1_w_down, m_ffn2_w_gate,
                               m_ffn2_w_up, m_ffn2_w_down, m_w_in, m_conv_w, m_conv_b, m_conv_ln_g, m_conv_ln_b,
                               m_gla_w_alpha, m_gla_b_alpha, m_gla_norm_g, m_w_branch, m_w_out)))
    mom_v = dict(zip(WEIGHTS, (v_norm_pre, v_norm_post, v_ffn1_w_gate, v_ffn1_w_up, v_ffn1_w_down, v_ffn2_w_gate,
                               v_ffn2_w_up, v_ffn2_w_down, v_w_in, v_conv_w, v_conv_b, v_conv_ln_g, v_conv_ln_b,
                               v_gla_w_alpha, v_gla_b_alpha, v_gla_norm_g, v_w_branch, v_w_out)))
    depth = norm_pre.shape[0]
    members = [shards[k] if k.endswith("down") else jnp.swapaxes(shards[k], 1, 2) for k in FFN]
    ffn_rows = jnp.concatenate([_mx(t).reshape(depth * SLAB, D_MODEL) for t in members], axis=0)
    w6 = all_gather_chips(ffn_rows, name="gather_ffn").reshape(4, len(FFN), depth, SLAB, D_MODEL)
    full = gather_weights(shards, BIG, MXU_DTYPE, 256, "gather_big")
    full.update(gather_weights(shards, SMALL, F32, 16, "gather_small"))

    def layer_params(l):
        ffn = [dict(gpre=full["norm_pre"][l, k:k + 1], gpost=full["norm_post"][l, k:k + 1]) for k in (0, 2)]
        mix = dict(gpre=full["norm_pre"][l, 1:2], gpost=full["norm_post"][l, 1:2],
                   conv_w=jnp.pad(full["conv_w"][l], ((0, CONV_PAD - CONV_WIDTH), (0, 0))), conv_b=conv_b[l:l + 1],
                   ln_g=conv_ln_g[l:l + 1], ln_b=conv_ln_b[l:l + 1],
                   wa=jnp.pad(full["gla_w_alpha"][l], ((0, GLA_RANK_PAD - GLA_RANK), (0, 0))), ba=gla_b_alpha[l:l + 1],
                   gn=gla_norm_g[l:l + 1])
        return ffn, mix

    xs = x[0]
    saved = []
    for l in range(depth):
        w = layer_weights(full, l)
        ffn, mix = layer_params(l)
        xs, r1 = ffn_fwd(xs, ffn[0]["gpre"], ffn[0]["gpost"], w6, 0, l, f"l{l}_ffn1")
        xs, r2 = mixer_fwd(xs, mix, w, f"l{l}_mix")
        xs, r3 = ffn_fwd(xs, ffn[1]["gpre"], ffn[1]["gpost"], w6, 1, l, f"l{l}_ffn2")
        saved.append((w, ffn, mix, r1, r2, r3))
    dx, sq = loss_head(xs, loss_target[0], tm=TM, name="loss_head")
    loss = lax.psum(0.5 * jnp.sum(sq) / D_MODEL, ("x", "y", "c"))

    per_layer = []
    g6 = lax.empty(w6.shape, F32)
    for l in reversed(range(depth)):
        w, ffn, mix, r1, r2, r3 = saved[l]
        g = {}
        dx, gpre2, gpost2, g6 = ffn_bwd(dx, r3, ffn[1]["gpre"], ffn[1]["gpost"], w6, g6, 1, l, f"l{l}_ffn2")
        dx, gm = mixer_bwd(dx, r2, mix, w, f"l{l}_mix")
        dx, gpre0, gpost0, g6 = ffn_bwd(dx, r1, ffn[0]["gpre"], ffn[0]["gpost"], w6, g6, 0, l, f"l{l}_ffn1")
        g["norm_pre"] = jnp.concatenate([gpre0, gm["gpre"], gpre2], axis=0)
        g["norm_post"] = jnp.concatenate([gpost0, gm["gpost"], gpost2], axis=0)
        g["w_in"] = jnp.concatenate([gm["in_sb"], gm["in_conv"], gm["in_gla"], gm["in_lr"][:, :GLA_RANK], gm["in_gate"]],
                                    axis=1)
        g["conv_w"], g["conv_b"] = gm["conv_w"], gm["conv_b"][0]
        g["conv_ln_g"], g["conv_ln_b"] = gm["ln_g"][0], gm["ln_b"][0]
        g["gla_w_alpha"], g["gla_b_alpha"], g["gla_norm_g"] = gm["wa"][:GLA_RANK], gm["ba"][0], gm["gn"][0]
        g["w_branch"] = jnp.stack(gm["branch"])
        g["w_out"] = gm["out"]
        per_layer.append(g)
    per_layer.reverse()
    grads = {k: jnp.stack([g[k] for g in per_layer]) for k in BIG + SMALL + REPLICATED}

    grad_w = scatter_grads(grads, shards)
    ffn_sum = reduce_scatter(g6.reshape(4, len(FFN) * depth * SLAB, D_MODEL), "rs_ffn")
    ffn_sum = ffn_sum.reshape(len(FFN), depth, SLAB, D_MODEL)
    for t, k in enumerate(FFN):
        grad_w[k] = ffn_sum[t] if k.endswith("down") else jnp.swapaxes(ffn_sum[t], 1, 2)
    delta, new_m, new_v = {}, {}, {}
    for k in WEIGHTS:
        delta[k], new_m[k], new_v[k] = adamw(shards[k], grad_w[k], mom_m[k], mom_v[k], name="adamw_" + k)
    return (loss, dx[None], *[grad_w[k] for k in WEIGHTS], *[delta[k] for k in WEIGHTS], *[new_m[k] for k in WEIGHTS],
            *[new_v[k] for k in WEIGHTS])
```

```python
import functools

import jax
import jax.numpy as jnp
from jax import lax
from jax.experimental import pallas as pl
from jax.experimental.pallas import tpu as pltpu

F32 = jnp.float32
MXU_DTYPE = jnp.bfloat16
GRAD_WIRE_DTYPE = jnp.bfloat16
HIGHEST = lax.Precision.HIGHEST
MESH = pl.DeviceIdType.MESH

NORM_EPS = 1e-6
D_MODEL = 1024
D_FF = 2816
BRANCH = 512
CHUNK = 64
CONV_WIDTH = 31
CONV_PAD = 32
GLA_RANK = 16
GLA_RANK_PAD = 128
GLA_TAU = 16.0
SB_SCALE = 0.125
SB_CUTOFF = 60.0
GLA_SCALE = 0.125
PACK_COLS = 1024
VMEM_LIMIT = 56 * 1024 * 1024

ADAM_LR, ADAM_B1, ADAM_B2, ADAM_EPS, ADAM_WD, ADAM_STEP = 0.001, 0.9, 0.999, 1e-08, 0.01, 10

NT = (((1,), (1,)), ((), ()))
TN = (((0,), (0,)), ((), ()))
NN = (((1,), (0,)), ((), ()))


def _params(sem=None, vmem=None):
    return pltpu.CompilerParams(dimension_semantics=sem, vmem_limit_bytes=vmem)


def _mx(v):
    return v.astype(MXU_DTYPE)


def _mx_round(v):
    return v.astype(MXU_DTYPE).astype(F32)


def _fit(dim, want):
    if dim <= want:
        return dim
    for d in range(want - want % 128, 0, -128):
        if dim % d == 0:
            return d
    raise ValueError((dim, want))


def mm(a, b, c=None, *, ta=False, tb=False, out_dtype=F32, tm=1024, tn=1024, tk=1024, name):
    K, M = a.shape if ta else a.shape[::-1]
    N = b.shape[0] if tb else b.shape[1]
    assert (b.shape[1] if tb else b.shape[0]) == K, (a.shape, b.shape, ta, tb)
    tm, tn, tk = _fit(M, tm), _fit(N, tn), _fit(K, tk)
    nk = K // tk
    dn = (((0 if ta else 1,), (1 if tb else 0,)), ((), ()))

    def body(*refs):
        if c is None:
            a_ref, b_ref, o_ref, acc = refs
            c_ref = None
        else:
            a_ref, b_ref, c_ref, o_ref, acc = refs
        k = pl.program_id(2)
        p = lax.dot_general(_mx(a_ref[...]), _mx(b_ref[...]), dn, preferred_element_type=F32)

        @pl.when(k == 0)
        def _():
            acc[...] = p

        @pl.when(k > 0)
        def _():
            acc[...] += p

        @pl.when(k == nk - 1)
        def _():
            r = acc[...]
            if c_ref is not None:
                r = r + c_ref[...].astype(F32)
            o_ref[...] = r.astype(o_ref.dtype)

    a_spec = pl.BlockSpec((tk, tm), lambda i, j, k: (k, i)) if ta else pl.BlockSpec((tm, tk), lambda i, j, k: (i, k))
    b_spec = pl.BlockSpec((tn, tk), lambda i, j, k: (j, k)) if tb else pl.BlockSpec((tk, tn), lambda i, j, k: (k, j))
    o_spec = pl.BlockSpec((tm, tn), lambda i, j, k: (i, j))
    ins, in_specs = [a, b], [a_spec, b_spec]
    if c is not None:
        ins.append(c)
        in_specs.append(o_spec)
    return pl.pallas_call(
        body, name=name, grid=(M // tm, N // tn, nk), in_specs=in_specs, out_specs=o_spec,
        out_shape=jax.ShapeDtypeStruct((M, N), out_dtype), scratch_shapes=[pltpu.VMEM((tm, tn), F32)],
        compiler_params=_params(("parallel", "parallel", "arbitrary"), VMEM_LIMIT),
    )(*ins)


SLAB = D_FF // 4


def _swiglu_of(ab_ref):
    return _mx(f_swiglu(ab_ref[0].astype(F32), ab_ref[1].astype(F32))[0])


def _gated_spec(rows, index):
    return pl.BlockSpec((2, None, rows, SLAB), index)


def slab_nt(a, w6, t0, n, l, *, out_dtype, tm=1024, name):
    S, K = a.shape
    tm = _fit(S, tm)

    def body(a_ref, w_ref, o_ref):
        o_ref[...] = lax.dot_general(_mx(a_ref[...]), _mx(w_ref[...]), NT, preferred_element_type=F32).astype(o_ref.dtype)

    return pl.pallas_call(
        body, name=name, grid=(S // tm, n),
        in_specs=[pl.BlockSpec((tm, K), lambda i, s: (i, 0)),
                  pl.BlockSpec((None, None, None, SLAB, K), lambda i, s: (s % 4, t0 + s // 4, l, 0, 0))],
        out_specs=pl.BlockSpec((None, tm, SLAB), lambda i, s: (s, i, 0)),
        out_shape=jax.ShapeDtypeStruct((n, S, SLAB), out_dtype),
        compiler_params=_params(("parallel", "parallel"), VMEM_LIMIT),
    )(a, w6)


def slab_nt_swiglu_bwd(d, w6, t0, l, ab, *, tm=1024, name):
    S, K = d.shape
    tm = _fit(S, tm)

    def body(d_ref, w_ref, ab_ref, o_ref):
        dz = lax.dot_general(_mx(d_ref[...]), _mx(w_ref[...]), NT, preferred_element_type=F32)
        _, pullback = jax.vjp(lambda g, u: f_swiglu(g, u)[0], ab_ref[0].astype(F32), ab_ref[1].astype(F32))
        dg, du = pullback(_mx_round(dz))
        o_ref[0] = dg.astype(o_ref.dtype)
        o_ref[1] = du.astype(o_ref.dtype)

    return pl.pallas_call(
        body, name=name, grid=(S // tm, 4),
        in_specs=[pl.BlockSpec((tm, K), lambda i, s: (i, 0)),
                  pl.BlockSpec((None, None, None, SLAB, K), lambda i, s: (s, t0, l, 0, 0)),
                  _gated_spec(tm, lambda i, s: (0, s, i, 0))],
        out_specs=_gated_spec(tm, lambda i, s: (0, s, i, 0)),
        out_shape=jax.ShapeDtypeStruct(ab.shape, MXU_DTYPE),
        compiler_params=_params(("parallel", "parallel"), VMEM_LIMIT),
    )(d, w6, ab)


def slab_nn(a, w6, t0, l, *, gated=False, tail=None, tm=512, name):
    n, S, _ = a.shape[-3:]
    K = w6.shape[-1]
    tm = _fit(S, tm)
    kind = tail[0] if tail else None
    extra = list(tail[1:]) if tail else []

    def body(a_ref, w_ref, *rest):
        i, s = pl.program_id(0), pl.program_id(1)
        acc = rest[-1]
        lhs = _swiglu_of(a_ref) if gated else _mx(a_ref[...])
        p = jnp.dot(lhs, _mx(w_ref[...]), preferred_element_type=F32)

        @pl.when(s == 0)
        def _():
            acc[...] = p

        @pl.when(s > 0)
        def _():
            acc[...] += p

        @pl.when(s == n - 1)
        def _():
            r = acc[...]
            if kind is None:
                rest[0][...] = r
            elif kind == "half_post":
                x_ref, g_ref, o_ref, x2_ref = rest[:4]
                o_ref[...] = r
                x2_ref[...] = f_half_post(x_ref[...], r, g_ref[...])[0]
            else:
                x_ref, g_ref, d_ref, dx_ref, dg_ref = rest[:5]
                _, pullback = jax.vjp(lambda xv, gv: f_rms(xv, gv)[0], x_ref[...], g_ref[...])
                dxn, dg = pullback(r)
                dx_ref[...] = dxn + d_ref[...]

                @pl.when(i == 0)
                def _():
                    dg_ref[...] = dg

                @pl.when(i > 0)
                def _():
                    dg_ref[...] += dg

    rows = pl.BlockSpec((tm, K), lambda i, s: (i, 0))
    gain = pl.BlockSpec((1, K), lambda i, s: (0, 0))
    tail_in = {None: [], "half_post": [rows, gain], "rms_bwd": [rows, gain, rows]}[kind]
    full, vec = jax.ShapeDtypeStruct((S, K), F32), jax.ShapeDtypeStruct((1, K), F32)
    out_specs, out_shape = {None: (rows, full), "half_post": ([rows, rows], [full, full]),
                            "rms_bwd": ([rows, gain], [full, vec])}[kind]
    return pl.pallas_call(
        body, name=name, grid=(S // tm, n),
        in_specs=[_gated_spec(tm, lambda i, s: (0, s, i, 0)) if gated else
                  pl.BlockSpec((None, tm, SLAB), lambda i, s: (s, i, 0)),
                  pl.BlockSpec((None, None, None, SLAB, K), lambda i, s: (s % 4, t0 + s // 4, l, 0, 0))] + tail_in,
        out_specs=out_specs, out_shape=out_shape, scratch_shapes=[pltpu.VMEM((tm, K), F32)],
        compiler_params=_params(("arbitrary" if kind == "rms_bwd" else "parallel", "arbitrary"), VMEM_LIMIT),
    )(a, w6, *extra)


def slab_tn(a, b, g6, t0, l, *, gated=False, tk=1024, name):
    n, S, _ = a.shape[-3:]
    K = b.shape[1]
    tk = _fit(S, tk)
    nk = S // tk

    def body(a_ref, b_ref, g_ref, o_ref, acc):
        k = pl.program_id(1)
        lhs = _swiglu_of(a_ref) if gated else _mx(a_ref[...])
        p = lax.dot_general(lhs, _mx(b_ref[...]), TN, preferred_element_type=F32)

        @pl.when(k == 0)
        def _():
            acc[...] = p

        @pl.when(k > 0)
        def _():
            acc[...] += p

        @pl.when(k == nk - 1)
        def _():
            o_ref[...] = acc[...]

    return pl.pallas_call(
        body, name=name, grid=(n, nk),
        in_specs=[_gated_spec(tk, lambda s, k: (0, s, k, 0)) if gated else
                  pl.BlockSpec((None, tk, SLAB), lambda s, k: (s, k, 0)),
                  pl.BlockSpec((tk, K), lambda s, k: (k, 0)), pl.BlockSpec(memory_space=pl.ANY)],
        out_specs=pl.BlockSpec((None, None, None, SLAB, K), lambda s, k: (s % 4, t0 + s // 4, l, 0, 0)),
        out_shape=jax.ShapeDtypeStruct(g6.shape, g6.dtype), scratch_shapes=[pltpu.VMEM((SLAB, K), F32)],
        input_output_aliases={2: 0},
        compiler_params=_params(("parallel", "arbitrary"), VMEM_LIMIT),
    )(a, b, g6)


def _row_arg(arg):
    if isinstance(arg, tuple):
        return arg
    return arg, arg.shape[1], 0


def _row_specs(rows, n):
    arrs, specs, avals = [], [], []
    for arg in rows:
        arr, width, cb = _row_arg(arg)
        rb = arr.shape[0] // n
        arrs.append(arr)
        specs.append(pl.BlockSpec((rb, width), lambda i, cb=cb: (i, cb)))
        avals.append(jax.ShapeDtypeStruct((rb, width), F32))
    return arrs, specs, avals


def _par_specs(pars):
    specs = [pl.BlockSpec(p.shape, lambda i, nd=p.ndim: (0,) * nd) for p in pars]
    avals = [jax.ShapeDtypeStruct(p.shape, F32) for p in pars]
    return specs, avals


def rowwise(f, rows, pars, out_dtypes, *, tm, name):
    n = _row_arg(rows[0])[0].shape[0] // tm
    arrs, rspecs, ravals = _row_specs(rows, n)
    pspecs, pavals = _par_specs(pars)
    oavals = jax.eval_shape(f, *ravals, *pavals)
    nin = len(arrs) + len(pars)

    def body(*refs):
        outs = f(*[r[...].astype(F32) for r in refs[:nin]])
        for o_ref, o in zip(refs[nin:], outs):
            o_ref[...] = o.astype(o_ref.dtype)

    return pl.pallas_call(
        body, name=name, grid=(n,), in_specs=rspecs + pspecs,
        out_specs=[pl.BlockSpec(o.shape, lambda i: (i, 0)) for o in oavals],
        out_shape=[jax.ShapeDtypeStruct((n * o.shape[0], o.shape[1]), dt) for o, dt in zip(oavals, out_dtypes)],
        compiler_params=_params(("parallel",), VMEM_LIMIT),
    )(*arrs, *pars)


def rowwise_vjp(f, rows, pars, cots, row_grad, par_grad, d_dtypes, adds=None, *, tm, name):
    n = _row_arg(rows[0])[0].shape[0] // tm
    arrs, rspecs, ravals = _row_specs(rows, n)
    pspecs, pavals = _par_specs(pars)
    carrs, cspecs, _ = _row_specs(cots, n)
    adds = adds or [None] * len(row_grad)
    add_arrs = [a for a in adds if a is not None]
    _, aspecs, _ = _row_specs(add_arrs, n)
    nr, npar, nc, na = len(arrs), len(pars), len(carrs), len(add_arrs)
    diff = list(row_grad) + [nr + j for j in par_grad]
    ngr = len(row_grad)

    def body(*refs):
        ins = [r[...].astype(F32) for r in refs[:nr + npar]]
        cs = tuple(r[...].astype(F32) for r in refs[nr + npar:nr + npar + nc])
        add_refs = list(refs[nr + npar + nc:nr + npar + nc + na])
        outs = refs[nr + npar + nc + na:]

        def g(*d):
            full = list(ins)
            for idx, val in zip(diff, d):
                full[idx] = val
            return f(*full)

        _, pullback = jax.vjp(g, *[ins[idx] for idx in diff])
        ds = pullback(cs)
        for k in range(ngr):
            d = ds[k]
            if adds[k] is not None:
                d = d + add_refs.pop(0)[...].astype(F32)
            outs[k][...] = d.astype(outs[k].dtype)
        i = pl.program_id(0)
        for k in range(ngr, len(diff)):
            @pl.when(i == 0)
            def _(k=k):
                outs[k][...] = ds[k]

            @pl.when(i > 0)
            def _(k=k):
                outs[k][...] += ds[k]

    out_specs = [pl.BlockSpec(ravals[i].shape, lambda i: (i, 0)) for i in row_grad] + [pspecs[j] for j in par_grad]
    out_shape = [jax.ShapeDtypeStruct((arrs[i].shape[0], ravals[i].shape[1]), dt) for i, dt in zip(row_grad, d_dtypes)]
    out_shape += [jax.ShapeDtypeStruct(pars[j].shape, F32) for j in par_grad]
    return pl.pallas_call(
        body, name=name, grid=(n,), in_specs=rspecs + pspecs + cspecs + aspecs, out_specs=out_specs, out_shape=out_shape,
        compiler_params=_params(("arbitrary",), VMEM_LIMIT),
    )(*arrs, *pars, *carrs, *add_arrs)


def _logsig(x):
    return jnp.minimum(x, 0.0) - jnp.log(1.0 + jnp.exp(-jnp.abs(x)))


def _sigmoid(x):
    return 1.0 / (1.0 + jnp.exp(-x))


def _silu(x):
    return x * _sigmoid(x)


def _rms(x, g):
    return x * lax.rsqrt(jnp.mean(x * x, axis=-1, keepdims=True) + NORM_EPS) * g


def f_rms(x, g):
    return (_rms(x, g),)


def f_swiglu(a, b):
    return (_silu(a) * b,)


def f_half_post(x, f, g):
    return (x + 0.5 * _rms(f, g),)


def f_post(x, m, g):
    return (x + _rms(m, g),)


def f_glu(a, g):
    return (a * _sigmoid(g),)


def f_conv_ln(y, lg, lb):
    mu = jnp.mean(y, axis=-1, keepdims=True)
    var = jnp.mean(jnp.square(y - mu), axis=-1, keepdims=True)
    return (_silu((y - mu) * lax.rsqrt(var + NORM_EPS) * lg + lb),)


def f_merge(g0, g1, g2, b0, b1, b2):
    return (_sigmoid(g0) * b0 + _sigmoid(g1) * b1 + _sigmoid(g2) * b2,)


def f_gla_post(o, r, g):
    w = o.shape[1]
    hv = w // 4
    i = lax.broadcasted_iota(jnp.int32, (w, w), 0) // hv
    j = lax.broadcasted_iota(jnp.int32, (w, w), 1) // hv
    avg = jnp.where(i == j, 1.0 / hv, 0.0).astype(F32)
    ms = jnp.dot(o * o, avg, precision=HIGHEST, preferred_element_type=F32)
    return (o * lax.rsqrt(ms + NORM_EPS) * g * _silu(r),)


def f_gla_pre(q, k, lr, wa, ba):
    tm = q.shape[0]
    pre = jnp.dot(_mx_round(lr), _mx_round(wa), precision=HIGHEST, preferred_element_type=F32) + ba
    la = _logsig(pre) / GLA_TAU
    i = lax.broadcasted_iota(jnp.int32, (tm, tm), 0)
    j = lax.broadcasted_iota(jnp.int32, (tm, tm), 1)
    later = jnp.where((i // CHUNK == j // CHUNK) & (j > i), 1.0, 0.0).astype(F32)
    ci = lax.broadcasted_iota(jnp.int32, (tm // CHUNK, tm), 0)
    cj = lax.broadcasted_iota(jnp.int32, (tm // CHUNK, tm), 1) // CHUNK
    chunk_sum = jnp.where(ci == cj, 1.0, 0.0).astype(F32)
    to_end = jnp.dot(later, la, precision=HIGHEST, preferred_element_type=F32)
    lam = jnp.exp(jnp.dot(chunk_sum, la, precision=HIGHEST, preferred_element_type=F32))
    return q * GLA_SCALE, k * jnp.exp(to_end), lam


def _split_dot(x, u):
    hi = _mx(x)
    lo = _mx(x - hi.astype(F32))
    return jnp.dot(hi, u, preferred_element_type=F32) + jnp.dot(lo, u, preferred_element_type=F32)


def _tri(tq, tk):
    row = lax.broadcasted_iota(jnp.int32, (tq, tk), 0)
    col = lax.broadcasted_iota(jnp.int32, (tq, tk), 1)
    return row, col


def _sb_tile(qh, kh, valid, suf, run):
    z = lax.dot_general(qh, kh, NT, preferred_element_type=F32) * SB_SCALE
    lp = jnp.minimum(z, 0.0) - jnp.log(1.0 + jnp.exp(-jnp.abs(z)))
    lk = jnp.where(valid, lp - z, 0.0)
    inc = _split_dot(lk, suf)
    a = jnp.where(valid, jnp.exp(lp + (inc - lk + run)), 0.0)
    return lp, a, run + inc[:, 0:1]


def _sticks_left(run0, run1):
    return (jnp.maximum(jnp.max(run0), jnp.max(run1)) > -SB_CUTOFF).astype(jnp.int32)


def sb_attn_fwd(qkv, *, tq, name):
    S = qkv.shape[0]
    tk = tq
    pairs = BRANCH // 128

    def body(q_ref, k_ref, v_ref, o_ref):
        i = pl.program_id(1)
        row, col = _tri(tq, tk)
        suf = _mx(row >= col)
        q = q_ref[...]

        def step(state):
            j, _, carry = state
            ks = pl.multiple_of((i - j) * tk, tk)
            kb = k_ref[pl.ds(ks, tk), :]
            vb = v_ref[pl.ds(ks, tk), :]
            valid = (col - row) < j * tq
            new = []
            for h in range(2):
                acc, run = carry[h]
                sl = slice(64 * h, 64 * h + 64)
                _, a, run = _sb_tile(q[:, sl], kb[:, sl], valid, suf, run)
                acc = acc + jnp.dot(_mx(a), vb[:, sl], preferred_element_type=F32)
                new.append((acc, run))
            return j + 1, _sticks_left(new[0][1], new[1][1]), tuple(new)

        zero = (jnp.zeros((tq, 64), F32), jnp.zeros((tq, 1), F32))
        _, _, res = lax.while_loop(lambda s: (s[0] <= i) & (s[1] > 0), step, (jnp.int32(0), jnp.int32(1), (zero, zero)))
        o_ref[...] = jnp.concatenate([res[0][0], res[1][0]], axis=1).astype(o_ref.dtype)

    return pl.pallas_call(
        body, name=name, grid=(pairs, S // tq),
        in_specs=[pl.BlockSpec((tq, 128), lambda p, i: (i, p)),
                  pl.BlockSpec((S, 128), lambda p, i: (0, pairs + p)),
                  pl.BlockSpec((S, 128), lambda p, i: (0, 2 * pairs + p))],
        out_specs=pl.BlockSpec((tq, 128), lambda p, i: (i, p)),
        out_shape=jax.ShapeDtypeStruct((S, BRANCH), MXU_DTYPE),
        compiler_params=_params(("parallel", "parallel"), VMEM_LIMIT),
    )(qkv, qkv, qkv)


def sb_attn_bwd(qkv, do, *, tq, name):
    S = qkv.shape[0]
    tk = tq
    nq = S // tq
    pairs = BRANCH // 128

    def body(q_ref, k_ref, v_ref, do_ref, dq_ref, dk_ref, dv_ref, g_sc, b_sc):
        i = pl.program_id(1)

        @pl.when(i == 0)
        def _():
            dk_ref[...] = jnp.zeros_like(dk_ref)
            dv_ref[...] = jnp.zeros_like(dv_ref)

        row, col = _tri(tq, tk)
        suf = _mx(row >= col)
        pre = _mx(row <= col)
        q = q_ref[...]
        dout = do_ref[...]

        def sweep1(state):
            j, _, carry = state
            kblk = i - j
            ks = pl.multiple_of(kblk * tk, tk)
            kb = k_ref[pl.ds(ks, tk), :]
            vb = v_ref[pl.ds(ks, tk), :]
            valid = (col - row) < j * tq
            runs, dvs = [], []
            for h in range(2):
                sl = slice(64 * h, 64 * h + 64)
                lp, a, run = _sb_tile(q[:, sl], kb[:, sl], valid, suf, carry[h])
                da = lax.dot_general(dout[:, sl], vb[:, sl], NT, preferred_element_type=F32)
                g_sc[h, kblk] = (a * da).astype(g_sc.dtype)
                b_sc[h, kblk] = jnp.where(valid, jnp.exp(lp), 0.0).astype(b_sc.dtype)
                dvs.append(lax.dot_general(_mx(a), dout[:, sl], TN, preferred_element_type=F32))
                runs.append(run)
            dv_ref[pl.ds(ks, tk), :] += jnp.concatenate(dvs, axis=1)
            return j + 1, _sticks_left(runs[0], runs[1]), tuple(runs)

        start = (jnp.int32(0), jnp.int32(1), (jnp.zeros((tq, 1), F32), jnp.zeros((tq, 1), F32)))
        tiles, _, _ = lax.while_loop(lambda s: (s[0] <= i) & (s[1] > 0), sweep1, start)

        def sweep2(kblk, carry):
            ks = pl.multiple_of(kblk * tk, tk)
            kb = k_ref[pl.ds(ks, tk), :]
            new, dks = [], []
            for h in range(2):
                dq, run = carry[h]
                sl = slice(64 * h, 64 * h + 64)
                g = g_sc[h, kblk]
                beta = b_sc[h, kblk].astype(F32)
                inc = jnp.dot(g, pre, preferred_element_type=F32)
                g = g.astype(F32)
                dz = _mx((g - beta * (inc + run)) * SB_SCALE)
                dq = dq + jnp.dot(dz, kb[:, sl], preferred_element_type=F32)
                dks.append(lax.dot_general(dz, q[:, sl], TN, preferred_element_type=F32))
                new.append((dq, run + inc[:, tk - 1:tk]))
            dk_ref[pl.ds(ks, tk), :] += jnp.concatenate(dks, axis=1)
            return tuple(new)

        zero = (jnp.zeros((tq, 64), F32), jnp.zeros((tq, 1), F32))
        res = lax.fori_loop(i + 1 - tiles, i + 1, sweep2, (zero, zero))
        dq_ref[...] = jnp.concatenate([res[0][0], res[1][0]], axis=1).astype(dq_ref.dtype)

    return pl.pallas_call(
        body, name=name, grid=(pairs, nq),
        in_specs=[pl.BlockSpec((tq, 128), lambda p, i: (i, p)),
                  pl.BlockSpec((S, 128), lambda p, i: (0, pairs + p)),
                  pl.BlockSpec((S, 128), lambda p, i: (0, 2 * pairs + p)),
                  pl.BlockSpec((tq, 128), lambda p, i: (i, p))],
        out_specs=[pl.BlockSpec((tq, 128), lambda p, i: (i, p)),
                   pl.BlockSpec((S, 128), lambda p, i: (0, p)),
                   pl.BlockSpec((S, 128), lambda p, i: (0, p))],
        out_shape=[jax.ShapeDtypeStruct((S, BRANCH), MXU_DTYPE), jax.ShapeDtypeStruct((S, BRANCH), F32),
                   jax.ShapeDtypeStruct((S, BRANCH), F32)],
        scratch_shapes=[pltpu.VMEM((2, nq, tq, tk), MXU_DTYPE), pltpu.VMEM((2, nq, tq, tk), MXU_DTYPE)],
        compiler_params=_params(("parallel", "arbitrary"), VMEM_LIMIT),
    )(qkv, qkv, qkv, do)


def conv_fwd(u, w, b, *, tm, name):
    S, C = u.shape
    hb = tm // CONV_PAD

    def body(u_ref, halo_ref, w_ref, b_ref, y_ref, buf):
        i = pl.program_id(0)
        buf[pl.ds(CONV_PAD, tm), :] = u_ref[...]
        buf[pl.ds(0, CONV_PAD), :] = jnp.where(i > 0, halo_ref[...], 0.0)
        acc = jnp.broadcast_to(b_ref[...], (tm, 128))
        for j in range(CONV_WIDTH):
            acc = acc + buf[pl.ds(CONV_PAD - (CONV_WIDTH - 1) + j, tm), :] * w_ref[pl.ds(j, 1), :]
        y_ref[...] = acc

    return pl.pallas_call(
        body, name=name, grid=(S // tm, C // 128),
        in_specs=[pl.BlockSpec((tm, 128), lambda i, c: (i, c)),
                  pl.BlockSpec((CONV_PAD, 128), lambda i, c: (jnp.maximum(i * hb - 1, 0), c)),
                  pl.BlockSpec((CONV_PAD, 128), lambda i, c: (0, c)),
                  pl.BlockSpec((1, 128), lambda i, c: (0, c))],
        out_specs=pl.BlockSpec((tm, 128), lambda i, c: (i, c)),
        out_shape=jax.ShapeDtypeStruct((S, C), F32),
        scratch_shapes=[pltpu.VMEM((tm + CONV_PAD, 128), F32)],
        compiler_params=_params(("parallel", "parallel")),
    )(u, u, w, b)


def conv_bwd(dy, u, w, *, tm, name):
    S, C = u.shape
    hb = tm // CONV_PAD
    n = S // tm

    def body(dy_ref, dyn_ref, u_ref, up_ref, w_ref, du_ref, dw_ref, bufy, bufu):
        i = pl.program_id(1)
        dyv = dy_ref[...]
        bufy[pl.ds(0, tm), :] = dyv
        bufy[pl.ds(tm, CONV_PAD), :] = jnp.where(i < n - 1, dyn_ref[...], 0.0)
        bufu[pl.ds(CONV_PAD, tm), :] = u_ref[...]
        bufu[pl.ds(0, CONV_PAD), :] = jnp.where(i > 0, up_ref[...], 0.0)

        @pl.when(i == 0)
        def _():
            dw_ref[...] = jnp.zeros_like(dw_ref)

        acc = jnp.zeros((tm, 128), F32)
        for j in range(CONV_WIDTH):
            acc = acc + bufy[pl.ds(CONV_WIDTH - 1 - j, tm), :] * w_ref[pl.ds(j, 1), :]
            shifted = bufu[pl.ds(CONV_PAD - (CONV_WIDTH - 1) + j, tm), :]
            dw_ref[pl.ds(j, 1), :] += jnp.sum(dyv * shifted, axis=0, keepdims=True)
        du_ref[...] = acc
        dw_ref[pl.ds(CONV_WIDTH, 1), :] += jnp.sum(dyv, axis=0, keepdims=True)

    return pl.pallas_call(
        body, name=name, grid=(C // 128, n),
        in_specs=[pl.BlockSpec((tm, 128), lambda c, i: (i, c)),
                  pl.BlockSpec((CONV_PAD, 128), lambda c, i: (jnp.minimum((i + 1) * hb, n * hb - 1), c)),
                  pl.BlockSpec((tm, 128), lambda c, i: (i, c)),
                  pl.BlockSpec((CONV_PAD, 128), lambda c, i: (jnp.maximum(i * hb - 1, 0), c)),
                  pl.BlockSpec((CONV_PAD, 128), lambda c, i: (0, c))],
        out_specs=[pl.BlockSpec((tm, 128), lambda c, i: (i, c)), pl.BlockSpec((CONV_PAD, 128), lambda c, i: (0, c))],
        out_shape=[jax.ShapeDtypeStruct((S, C), F32), jax.ShapeDtypeStruct((CONV_PAD, C), F32)],
        scratch_shapes=[pltpu.VMEM((tm + CONV_PAD, 128), F32), pltpu.VMEM((tm + CONV_PAD, 128), F32)],
        compiler_params=_params(("parallel", "arbitrary")),
    )(dy, dy, u, u, w)


GLA_HEADS, GLA_DK, GLA_DV = 4, 64, 128


def gla_scan_fwd(qs, kd, gl, lam, *, tm, name):
    S = qs.shape[0]
    nc = tm // CHUNK

    def body(qs_ref, kd_ref, v_ref, lam_ref, o_ref, st_ref, state):
        @pl.when(pl.program_id(0) == 0)
        def _():
            state[...] = jnp.zeros_like(state)

        for c in range(nc):
            rows = pl.ds(c * CHUNK, CHUNK)
            q, k, v = _mx(qs_ref[rows, :]), _mx(kd_ref[rows, :]), _mx(v_ref[rows, :])
            upd = [lax.dot_general(v[:, h * GLA_DV:(h + 1) * GLA_DV], k[:, h * GLA_DK:(h + 1) * GLA_DK], TN,
                                   preferred_element_type=F32) for h in range(GLA_HEADS)]
            st = state[...] * lam_ref[pl.ds(c, 1), :] + jnp.concatenate(upd, axis=1)
            state[...] = st
            st_ref[c] = st
            stm = _mx(st)
            o = [lax.dot_general(q[:, h * GLA_DK:(h + 1) * GLA_DK], stm[:, h * GLA_DK:(h + 1) * GLA_DK], NT,
                                 preferred_element_type=F32) for h in range(GLA_HEADS)]
            o_ref[rows, :] = jnp.concatenate(o, axis=1)

    dk_all = GLA_HEADS * GLA_DK
    dv_all = GLA_HEADS * GLA_DV
    return pl.pallas_call(
        body, name=name, grid=(S // tm,),
        in_specs=[pl.BlockSpec((tm, dk_all), lambda i: (i, 0)), pl.BlockSpec((tm, dk_all), lambda i: (i, 0)),
                  pl.BlockSpec((tm, dv_all), lambda i: (i, 1)), pl.BlockSpec((nc, dk_all), lambda i: (i, 0))],
        out_specs=[pl.BlockSpec((tm, dv_all), lambda i: (i, 0)), pl.BlockSpec((nc, GLA_DV, dk_all), lambda i: (i, 0, 0))],
        out_shape=[jax.ShapeDtypeStruct((S, dv_all), F32), jax.ShapeDtypeStruct((S // CHUNK, GLA_DV, dk_all), F32)],
        scratch_shapes=[pltpu.VMEM((GLA_DV, dk_all), F32)],
        compiler_params=_params(("arbitrary",)),
    )(qs, kd, gl, lam)


def gla_scan_bwd(do, qs, kd, gl, lam, states, *, tm, name):
    S = qs.shape[0]
    nc = tm // CHUNK
    n = S // tm
    dk_all = GLA_HEADS * GLA_DK
    dv_all = GLA_HEADS * GLA_DV

    def body(do_ref, qs_ref, kd_ref, v_ref, lam_ref, st_ref, prev_ref, dqs_ref, dkd_ref, dv_ref, dlam_ref, carry):
        i = pl.program_id(0)

        @pl.when(i == 0)
        def _():
            carry[...] = jnp.zeros_like(carry)

        for c in reversed(range(nc)):
            rows = pl.ds(c * CHUNK, CHUNK)
            q, k, v, d = _mx(qs_ref[rows, :]), _mx(kd_ref[rows, :]), _mx(v_ref[rows, :]), _mx(do_ref[rows, :])
            st = _mx(st_ref[c])
            before = st_ref[c - 1] if c > 0 else jnp.where(i < n - 1, prev_ref[0], 0.0)
            outer = [lax.dot_general(d[:, h * GLA_DV:(h + 1) * GLA_DV], q[:, h * GLA_DK:(h + 1) * GLA_DK], TN,
                                     preferred_element_type=F32) for h in range(GLA_HEADS)]
            dst = carry[...] + jnp.concatenate(outer, axis=1)
            dstm = _mx(dst)
            dq, dkk, dvv = [], [], []
            for h in range(GLA_HEADS):
                ksl = slice(h * GLA_DK, (h + 1) * GLA_DK)
                vsl = slice(h * GLA_DV, (h + 1) * GLA_DV)
                dq.append(jnp.dot(d[:, vsl], st[:, ksl], preferred_element_type=F32))
                dkk.append(jnp.dot(v[:, vsl], dstm[:, ksl], preferred_element_type=F32))
                dvv.append(lax.dot_general(k[:, ksl], dstm[:, ksl], NT, preferred_element_type=F32))
            dqs_ref[rows, :] = jnp.concatenate(dq, axis=1)
            dkd_ref[rows, :] = jnp.concatenate(dkk, axis=1)
            dv_ref[rows, :] = jnp.concatenate(dvv, axis=1)
            dlam_ref[pl.ds(c, 1), :] = jnp.sum(dst * before, axis=0, keepdims=True)
            carry[...] = dst * lam_ref[pl.ds(c, 1), :]

    rev = lambda i: n - 1 - i
    return pl.pallas_call(
        body, name=name, grid=(n,),
        in_specs=[pl.BlockSpec((tm, dv_all), lambda i: (rev(i), 0)), pl.BlockSpec((tm, dk_all), lambda i: (rev(i), 0)),
                  pl.BlockSpec((tm, dk_all), lambda i: (rev(i), 0)), pl.BlockSpec((tm, dv_all), lambda i: (rev(i), 1)),
                  pl.BlockSpec((nc, dk_all), lambda i: (rev(i), 0)),
                  pl.BlockSpec((nc, GLA_DV, dk_all), lambda i: (rev(i), 0, 0)),
                  pl.BlockSpec((1, GLA_DV, dk_all), lambda i: (jnp.maximum(rev(i) * nc - 1, 0), 0, 0))],
        out_specs=[pl.BlockSpec((tm, dk_all), lambda i: (rev(i), 0)), pl.BlockSpec((tm, dk_all), lambda i: (rev(i), 0)),
                   pl.BlockSpec((tm, dv_all), lambda i: (rev(i), 0)), pl.BlockSpec((nc, dk_all), lambda i: (rev(i), 0))],
        out_shape=[jax.ShapeDtypeStruct((S, dk_all), F32), jax.ShapeDtypeStruct((S, dk_all), F32),
                   jax.ShapeDtypeStruct((S, dv_all), F32), jax.ShapeDtypeStruct((S // CHUNK, dk_all), F32)],
        scratch_shapes=[pltpu.VMEM((GLA_DV, dk_all), F32)],
        compiler_params=_params(("arbitrary",)),
    )(do, qs, kd, gl, lam, states, states)


def loss_head(y, target, *, tm, name):
    S, D = y.shape

    def body(y_ref, t_ref, dy_ref, sq_ref):
        err = y_ref[...] - t_ref[...]
        dy_ref[...] = err * (1.0 / D)
        part = jnp.sum(err * err, axis=0, keepdims=True)

        @pl.when(pl.program_id(0) == 0)
        def _():
            sq_ref[...] = part

        @pl.when(pl.program_id(0) > 0)
        def _():
            sq_ref[...] += part

    return pl.pallas_call(
        body, name=name, grid=(S // tm,),
        in_specs=[pl.BlockSpec((tm, D), lambda i: (i, 0)), pl.BlockSpec((tm, D), lambda i: (i, 0))],
        out_specs=[pl.BlockSpec((tm, D), lambda i: (i, 0)), pl.BlockSpec((1, D), lambda i: (0, 0))],
        out_shape=[jax.ShapeDtypeStruct((S, D), F32), jax.ShapeDtypeStruct((1, D), F32)],
        compiler_params=_params(("arbitrary",)),
    )(y, target)


def f_adamw(w, g, m, v):
    m = ADAM_B1 * m + (1.0 - ADAM_B1) * g
    v = ADAM_B2 * v + (1.0 - ADAM_B2) * jnp.square(g)
    m_hat = m / (1.0 - ADAM_B1 ** ADAM_STEP)
    v_hat = v / (1.0 - ADAM_B2 ** ADAM_STEP)
    return -ADAM_LR * (m_hat / (jnp.sqrt(v_hat) + ADAM_EPS) + ADAM_WD * w), m, v


def adamw(w, g, m, v, *, name):
    shape = w.shape
    cols = shape[-1]
    rows = w.size // cols
    tm = rows
    while tm % 16 == 0 and tm * cols * 4 > (1 << 20):
        tm //= 2
    flat = [t.reshape(rows, cols) for t in (w, g, m, v)]
    outs = rowwise(f_adamw, flat, [], [F32, F32, F32], tm=tm, name=name)
    return [o.reshape(shape) for o in outs]


def _place():
    x, y, c = lax.axis_index("x"), lax.axis_index("y"), lax.axis_index("c")
    return x, y, c, [(1 - x, y), (x, 1 - y), (1 - x, 1 - y)]


def _my_chip():
    return 2 * lax.axis_index("x") + lax.axis_index("y")


def _any():
    return pl.BlockSpec(memory_space=pl.ANY)


DMA_CHUNKS = 8


def _chunks(rows):
    n = DMA_CHUNKS
    while n > 1 and rows % (n * 16):
        n //= 2
    return [(k * (rows // n), rows // n) for k in range(n)]


def _start_chunked(make, rows):
    for off, size in _chunks(rows):
        make(off, size).start()


def all_gather_chips(v, *, name):
    R, C = v.shape
    H = R // 2

    def body(v_ref, o_ref, send, recv, fsend, frecv):
        x, y, c, chips = _place()
        me = 2 * x + y
        ids = [2 * px + py for px, py in chips]
        mine = pl.ds(pl.multiple_of(c * H, 8), H)
        other = pl.ds(pl.multiple_of((1 - c) * H, 8), H)

        def cross(j, src_chip, rows):
            return pltpu.make_async_remote_copy(
                src_ref=v_ref.at[rows], dst_ref=o_ref.at[src_chip, rows], send_sem=send.at[j], recv_sem=recv.at[j],
                device_id=(*chips[j], c), device_id_type=MESH)

        def handed(j, rows):
            return pltpu.make_async_remote_copy(
                src_ref=o_ref.at[ids[j], rows], dst_ref=o_ref.at[ids[j], rows], send_sem=fsend.at[j],
                recv_sem=frecv.at[j], device_id=(x, y, 1 - c), device_id_type=MESH)

        def my_rows(off, size):
            return pl.ds(pl.multiple_of(c * H + off, 8), size)

        for j in range(3):
            _start_chunked(lambda off, size, j=j: cross(j, me, my_rows(off, size)), H)
        for j in range(3):
            cross(j, ids[j], mine).wait_recv()
            _start_chunked(lambda off, size, j=j: handed(j, my_rows(off, size)), H)
        for j in range(3):
            handed(j, other).wait_recv()
        for j in range(3):
            cross(j, me, mine).wait_send()
            handed(j, mine).wait_send()

    got = pl.pallas_call(
        body, name=name, in_specs=[_any()], out_specs=_any(), out_shape=jax.ShapeDtypeStruct((4, R, C), v.dtype),
        scratch_shapes=[pltpu.SemaphoreType.DMA((3,)), pltpu.SemaphoreType.DMA((3,)), pltpu.SemaphoreType.DMA((3,)),
                        pltpu.SemaphoreType.DMA((3,))],
    )(v)
    return lax.dynamic_update_slice(got, v[None], (_my_chip(), 0, 0))


def swap_halves(g, *, name):
    n, R, C = g.shape
    H = R // 2

    def body(g_ref, theirs_ref, send, recv):
        x, y, c, _ = _place()

        def give(q, off, size):
            return pltpu.make_async_remote_copy(
                src_ref=g_ref.at[q, pl.ds(pl.multiple_of((1 - c) * H + off, 8), size)],
                dst_ref=theirs_ref.at[q, pl.ds(off, size)], send_sem=send.at[q], recv_sem=recv.at[q],
                device_id=(x, y, 1 - c), device_id_type=MESH)

        for q in range(n):
            _start_chunked(functools.partial(give, q), H)
        for q in range(n):
            give(q, 0, H).wait()

    return pl.pallas_call(
        body, name=name, in_specs=[_any()], out_specs=_any(), out_shape=jax.ShapeDtypeStruct((n, H, C), g.dtype),
        scratch_shapes=[pltpu.SemaphoreType.DMA((n,)), pltpu.SemaphoreType.DMA((n,))],
    )(g)


def exchange_pieces(p, *, name):
    H = p.shape[1]

    def body(p_ref, b_ref, send, recv):
        x, y, c, chips = _place()
        me = 2 * x + y
        ids = [2 * px + py for px, py in chips]

        def cross(j, piece, slot, off, size):
            return pltpu.make_async_remote_copy(
                src_ref=p_ref.at[piece, pl.ds(off, size)], dst_ref=b_ref.at[slot, pl.ds(off, size)], send_sem=send.at[j],
                recv_sem=recv.at[j], device_id=(*chips[j], c), device_id_type=MESH)

        for j in range(3):
            _start_chunked(functools.partial(cross, j, ids[j], me), H)
        for j in range(3):
            cross(j, me, ids[j], 0, H).wait_recv()
        for j in range(3):
            cross(j, ids[j], me, 0, H).wait_send()

    got = pl.pallas_call(
        body, name=name, in_specs=[_any()], out_specs=_any(), out_shape=jax.ShapeDtypeStruct(p.shape, p.dtype),
        scratch_shapes=[pltpu.SemaphoreType.DMA((3,)), pltpu.SemaphoreType.DMA((3,))],
    )(p)
    me = _my_chip()
    return lax.dynamic_update_slice(got, lax.dynamic_slice_in_dim(p, me, 1, axis=0), (me, 0, 0))


def join_halves(f, *, name):
    H, C = f.shape

    def body(f_ref, o_ref, send, recv):
        x, y, c, _ = _place()

        def give(off, size):
            return pltpu.make_async_remote_copy(
                src_ref=f_ref.at[pl.ds(off, size)], dst_ref=o_ref.at[pl.ds(off, size)], send_sem=send, recv_sem=recv,
                device_id=(x, y, 1 - c), device_id_type=MESH)

        _start_chunked(give, H)
        give(0, H).wait()

    theirs = pl.pallas_call(
        body, name=name, in_specs=[_any()], out_specs=_any(), out_shape=jax.ShapeDtypeStruct((H, C), f.dtype),
        scratch_shapes=[pltpu.SemaphoreType.DMA, pltpu.SemaphoreType.DMA],
    )(f)
    south = lax.axis_index("c") == 0
    return jnp.concatenate([jnp.where(south, f, theirs), jnp.where(south, theirs, f)], axis=0)


def _row_tile(rows, want):
    tm = want
    while rows % tm:
        tm //= 2
    return tm


def _add_pair(g, theirs, out_dtype, *, name):
    n, R, C = g.shape
    H = R // 2
    tm = _row_tile(H, 512)
    nb = H // tm

    def body(c_ref, g_ref, t_ref, o_ref):
        o_ref[...] = (g_ref[...] + t_ref[...]).astype(o_ref.dtype)

    grid_spec = pltpu.PrefetchScalarGridSpec(
        num_scalar_prefetch=1, grid=(n, nb),
        in_specs=[pl.BlockSpec((None, tm, C), lambda q, i, c: (q, c[0] * nb + i, 0)),
                  pl.BlockSpec((None, tm, C), lambda q, i, c: (q, i, 0))],
        out_specs=pl.BlockSpec((None, tm, C), lambda q, i, c: (q, i, 0)))
    return pl.pallas_call(
        body, name=name, grid_spec=grid_spec, out_shape=jax.ShapeDtypeStruct((n, H, C), out_dtype),
        compiler_params=_params(("parallel", "parallel")),
    )(lax.axis_index("c").astype(jnp.int32).reshape(1), g, theirs)


def _add_stack(b, *, name):
    n, H, C = b.shape
    tm = _row_tile(H, 256)

    def body(b_ref, o_ref):
        s = b_ref[0].astype(F32)
        for k in range(1, n):
            s = s + b_ref[k].astype(F32)
        o_ref[...] = s

    return pl.pallas_call(
        body, name=name, grid=(H // tm,), in_specs=[pl.BlockSpec((n, tm, C), lambda i: (0, i, 0))],
        out_specs=pl.BlockSpec((tm, C), lambda i: (i, 0)), out_shape=jax.ShapeDtypeStruct((H, C), F32),
        compiler_params=_params(("parallel",)),
    )(b)


def reduce_scatter(g, tag):
    pair = _add_pair(g, swap_halves(g, name=tag + "_swap_halves"), GRAD_WIRE_DTYPE, name=tag + "_add_pair")
    total = _add_stack(exchange_pieces(pair, name=tag + "_exchange"), name=tag + "_add_chips")
    return join_halves(total, name=tag + "_join")


FFN = ("ffn1_w_gate", "ffn1_w_up", "ffn2_w_gate", "ffn2_w_up", "ffn1_w_down", "ffn2_w_down")
BIG = ("w_in", "w_branch", "w_out")
SMALL = ("norm_pre", "norm_post", "conv_w", "gla_w_alpha")
REPLICATED = ("conv_b", "conv_ln_g", "conv_ln_b", "gla_b_alpha", "gla_norm_g")
WEIGHTS = ("norm_pre", "norm_post", "ffn1_w_gate", "ffn1_w_up", "ffn1_w_down", "ffn2_w_gate", "ffn2_w_up", "ffn2_w_down",
           "w_in", "conv_w", "conv_b", "conv_ln_g", "conv_ln_b", "gla_w_alpha", "gla_b_alpha", "gla_norm_g", "w_branch",
           "w_out")
SHARD_AXIS = {"ffn1_w_gate": 2, "ffn1_w_up": 2, "ffn1_w_down": 1, "ffn2_w_gate": 2, "ffn2_w_up": 2, "ffn2_w_down": 1,
              "w_in": 2, "w_branch": 3, "w_out": 1, "norm_pre": 2, "norm_post": 2, "conv_w": 2, "gla_w_alpha": 2}


def _size(shape):
    n = 1
    for d in shape:
        n *= d
    return n


def _pack_rows(shape):
    return -(-_size(shape) // (16 * PACK_COLS)) * 16


def _pack(arrs, dtype, row_mult):
    parts, rows = [], 0
    for a in arrs:
        r = _pack_rows(a.shape)
        flat = a.astype(dtype).reshape(-1)
        if r * PACK_COLS != flat.shape[0]:
            flat = jnp.pad(flat, (0, r * PACK_COLS - flat.shape[0]))
        parts.append(flat.reshape(r, PACK_COLS))
        rows += r
    if rows % row_mult:
        parts.append(jnp.zeros((row_mult - rows % row_mult, PACK_COLS), dtype))
    return jnp.concatenate(parts, axis=0)


def _unpack(buf, shapes):
    lead = buf.shape[:-2]
    out, off = [], 0
    for s in shapes:
        r = _pack_rows(s)
        part = lax.slice_in_dim(buf, off, off + r, axis=buf.ndim - 2).reshape(lead + (r * PACK_COLS,))
        if r * PACK_COLS != _size(s):
            part = lax.slice_in_dim(part, 0, _size(s), axis=part.ndim - 1)
        out.append(part.reshape(lead + tuple(s)))
        off += r
    return out


def gather_weights(shards, names, dtype, row_mult, name):
    packed = _pack([shards[k] for k in names], dtype, row_mult)
    parts = _unpack(all_gather_chips(packed, name=name), [shards[k].shape for k in names])
    return {k: jnp.concatenate([p[q] for q in range(4)], axis=SHARD_AXIS[k]) for k, p in zip(names, parts)}


def scatter_grads(grads, shards):
    sharded = BIG + SMALL
    pieces = []
    for q in range(4):
        part = []
        for k in sharded:
            w = shards[k].shape[SHARD_AXIS[k]]
            part.append(lax.slice_in_dim(grads[k], q * w, (q + 1) * w, axis=SHARD_AXIS[k]))
        part += [grads[k] for k in REPLICATED]
        pieces.append(_pack(part, F32, 512))
    total = reduce_scatter(jnp.stack(pieces), "rs_mix")
    names = sharded + REPLICATED
    return dict(zip(names, _unpack(total, [shards[k].shape for k in names])))


TM = 256
TM_GLA = 512
TQ = 256

IN_SB, IN_CONV, IN_GLA, IN_LR, IN_GATE = 0, 1536, 2560, 4096, 4112
IN_END = 7184


def layer_weights(full, l):
    w = {}
    win = full["w_in"][l]
    w["in_sb"] = win[:, IN_SB:IN_CONV]
    w["in_conv"] = win[:, IN_CONV:IN_GLA]
    w["in_gla"] = win[:, IN_GLA:IN_LR]
    w["in_lr"] = jnp.pad(win[:, IN_LR:IN_GATE], ((0, 0), (0, GLA_RANK_PAD - GLA_RANK)))
    w["in_gate"] = win[:, IN_GATE:IN_END]
    w["branch"] = [full["w_branch"][l, g] for g in range(3)]
    w["out"] = full["w_out"][l]
    return w


def ffn_fwd(x, gpre, gpost, w6, f, l, tag):
    S = x.shape[0]
    h = rowwise(f_rms, [x], [gpre], [MXU_DTYPE], tm=TM, name=tag + "_pre")[0]
    ab = slab_nt(h, w6, 2 * f, 8, l, out_dtype=MXU_DTYPE, name=tag + "_gu").reshape(2, 4, S, SLAB)
    out, x2 = slab_nn(ab, w6, 4 + f, l, gated=True, tail=("half_post", x, gpost), name=tag + "_down")
    return x2, (x, h, ab, out)


def ffn_bwd(dx2, res, gpre, gpost, w6, g6, f, l, tag):
    x, h, ab, out = res
    S = x.shape[0]
    df, dgpost = rowwise_vjp(f_half_post, [x, out], [gpost], [dx2], [1], [0], [MXU_DTYPE], tm=TM, name=tag + "_post_b")
    dab = slab_nt_swiglu_bwd(df, w6, 4 + f, l, ab, name=tag + "_down_dx").reshape(8, S, SLAB)
    g6 = slab_tn(ab, df, g6, 4 + f, l, gated=True, name=tag + "_down_dw")
    dx, dgpre = slab_nn(dab, w6, 2 * f, l, tail=("rms_bwd", x, gpre, dx2), name=tag + "_gu_dx")
    g6 = slab_tn(dab, h, g6, 2 * f, l, name=tag + "_gu_dw")
    return dx, dgpre, dgpost, g6


def mixer_fwd(x, p, w, tag):
    h = rowwise(f_rms, [x], [p["gpre"]], [MXU_DTYPE], tm=TM, name=tag + "_pre")[0]
    qkv = mm(h, w["in_sb"], out_dtype=MXU_DTYPE, tn=512, name=tag + "_in_sb")
    cv = mm(h, w["in_conv"], name=tag + "_in_conv")
    gl = mm(h, w["in_gla"], out_dtype=MXU_DTYPE, tn=512, name=tag + "_in_gla")
    lr = mm(h, w["in_lr"], out_dtype=MXU_DTYPE, name=tag + "_in_lr")
    gt = mm(h, w["in_gate"], out_dtype=MXU_DTYPE, name=tag + "_in_gate")
    sb = sb_attn_fwd(qkv, tq=TQ, name=tag + "_sb")
    u = rowwise(f_glu, [(cv, BRANCH, 0), (cv, BRANCH, 1)], [], [F32], tm=TM, name=tag + "_glu")[0]
    y = conv_fwd(u, p["conv_w"], p["conv_b"], tm=TM, name=tag + "_conv")
    cb = rowwise(f_conv_ln, [y], [p["ln_g"], p["ln_b"]], [MXU_DTYPE], tm=TM, name=tag + "_ln")[0]
    qs, kd, lam = rowwise(f_gla_pre, [(gl, 256, 0), (gl, 256, 1), lr], [p["wa"], p["ba"]], [MXU_DTYPE, MXU_DTYPE, F32],
                          tm=TM_GLA, name=tag + "_gla_pre")
    o, states = gla_scan_fwd(qs, kd, gl, lam, tm=TM_GLA, name=tag + "_gla_scan")
    gb = rowwise(f_gla_post, [o, (gl, BRANCH, 2)], [p["gn"]], [MXU_DTYPE], tm=TM, name=tag + "_gla_post")[0]
    branches = [sb, cb, gb]
    bd = [mm(branches[g], w["branch"][g], out_dtype=MXU_DTYPE, name=tag + f"_branch{g}") for g in range(3)]
    merged = rowwise(f_merge, [(gt, D_MODEL, 0), (gt, D_MODEL, 1), (gt, D_MODEL, 2)] + bd, [], [MXU_DTYPE], tm=TM,
                     name=tag + "_merge")[0]
    m = mm(merged, w["out"], name=tag + "_out")
    x2 = rowwise(f_post, [x, m], [p["gpost"]], [F32], tm=TM, name=tag + "_post")[0]
    return x2, (x, h, qkv, cv, gl, lr, gt, u, y, qs, kd, lam, o, states, branches, bd, merged, m)


def mixer_bwd(dx2, res, p, w, tag):
    x, h, qkv, cv, gl, lr, gt, u, y, qs, kd, lam, o, states, branches, bd, merged, m = res
    g = {}
    dm, g["gpost"] = rowwise_vjp(f_post, [x, m], [p["gpost"]], [dx2], [1], [0], [MXU_DTYPE], tm=TM, name=tag + "_post_b")
    dmerged = mm(dm, w["out"], tb=True, out_dtype=MXU_DTYPE, name=tag + "_out_dx")
    g["out"] = mm(merged, dm, ta=True, name=tag + "_out_dw")
    gts = [(gt, D_MODEL, 0), (gt, D_MODEL, 1), (gt, D_MODEL, 2)]
    dgate = rowwise_vjp(f_merge, gts + bd, [], [dmerged], [0, 1, 2, 3, 4, 5], [], [MXU_DTYPE] * 6, tm=TM,
                        name=tag + "_merge_b")
    dgt = jnp.concatenate(dgate[:3], axis=1)
    dbd = dgate[3:]
    g["branch"] = [mm(branches[k], dbd[k], ta=True, name=tag + f"_branch{k}_dw") for k in range(3)]
    dsb = mm(dbd[0], w["branch"][0], tb=True, out_dtype=MXU_DTYPE, name=tag + "_branch0_dx")
    dq, dk, dv = sb_attn_bwd(qkv, dsb, tq=TQ, name=tag + "_sb_b")
    dqkv = jnp.concatenate([dq, _mx(dk), _mx(dv)], axis=1)
    dcb = mm(dbd[1], w["branch"][1], tb=True, name=tag + "_branch1_dx")
    dy, g["ln_g"], g["ln_b"] = rowwise_vjp(f_conv_ln, [y], [p["ln_g"], p["ln_b"]], [dcb], [0], [0, 1], [F32], tm=TM,
                                           name=tag + "_ln_b")
    du, dwb = conv_bwd(dy, u, p["conv_w"], tm=TM, name=tag + "_conv_b")
    g["conv_w"], g["conv_b"] = dwb[:CONV_WIDTH], dwb[CONV_WIDTH:CONV_WIDTH + 1]
    dca, dcg = rowwise_vjp(f_glu, [(cv, BRANCH, 0), (cv, BRANCH, 1)], [], [du], [0, 1], [], [MXU_DTYPE, MXU_DTYPE], tm=TM,
                           name=tag + "_glu_b")
    dcv = jnp.concatenate([dca, dcg], axis=1)
    dgb = mm(dbd[2], w["branch"][2], tb=True, name=tag + "_branch2_dx")
    do, dr, g["gn"] = rowwise_vjp(f_gla_post, [o, (gl, BRANCH, 2)], [p["gn"]], [dgb], [0, 1], [0], [F32, MXU_DTYPE], tm=TM,
                                  name=tag + "_gla_post_b")
    dqs, dkd, dgv, dlam = gla_scan_bwd(do, qs, kd, gl, lam, states, tm=TM_GLA, name=tag + "_gla_scan_b")
    dgq, dgk, dlr, g["wa"], g["ba"] = rowwise_vjp(
        f_gla_pre, [(gl, 256, 0), (gl, 256, 1), lr], [p["wa"], p["ba"]], [dqs, dkd, dlam], [0, 1, 2], [0, 1],
        [MXU_DTYPE, MXU_DTYPE, MXU_DTYPE], tm=TM_GLA, name=tag + "_gla_pre_b")
    dgl = jnp.concatenate([dgq, dgk, _mx(dgv), dr], axis=1)
    secs = [("in_sb", dqkv), ("in_conv", dcv), ("in_gla", dgl), ("in_lr", dlr), ("in_gate", dgt)]
    dh = None
    for k, d in secs:
        dh = mm(d, w[k], dh, tb=True, name=tag + "_" + k + "_dx")
        g[k] = mm(h, d, ta=True, tn=1536, name=tag + "_" + k + "_dw")
    dx, g["gpre"] = rowwise_vjp(f_rms, [x], [p["gpre"]], [dh], [0], [0], [F32], [dx2], tm=TM, name=tag + "_pre_b")
    return dx, g


def kernel(x, norm_pre, norm_post, ffn1_w_gate, ffn1_w_up, ffn1_w_down, ffn2_w_gate, ffn2_w_up, ffn2_w_down, w_in, conv_w, conv_b, conv_ln_g, conv_ln_b, gla_w_alpha, gla_b_alpha, gla_norm_g, w_branch, w_out, loss_target, m_norm_pre, m_norm_post, m_ffn1_w_gate, m_ffn1_w_up, m_ffn1_w_down, m_ffn2_w_gate, m_ffn2_w_up, m_ffn2_w_down, m_w_in, m_conv_w, m_conv_b, m_conv_ln_g, m_conv_ln_b, m_gla_w_alpha, m_gla_b_alpha, m_gla_norm_g, m_w_branch, m_w_out, v_norm_pre, v_norm_post, v_ffn1_w_gate, v_ffn1_w_up, v_ffn1_w_down, v_ffn2_w_gate, v_ffn2_w_up, v_ffn2_w_down, v_w_in, v_conv_w, v_conv_b, v_conv_ln_g, v_conv_ln_b, v_gla_w_alpha, v_gla_b_alpha, v_gla_norm_g, v_w_branch, v_w_out):
    shards = dict(norm_pre=norm_pre, norm_post=norm_post, ffn1_w_gate=ffn1_w_gate, ffn1_w_up=ffn1_w_up,
                  ffn1_w_down=ffn1_w_down, ffn2_w_gate=ffn2_w_gate, ffn2_w_up=ffn2_w_up, ffn2_w_down=ffn2_w_down, w_in=w_in,
                  conv_w=conv_w, conv_b=conv_b, conv_ln_g=conv_ln_g, conv_ln_b=conv_ln_b, gla_w_alpha=gla_w_alpha,
                  gla_b_alpha=gla_b_alpha, gla_norm_g=gla_norm_g, w_branch=w_branch, w_out=w_out)
    mom_m = dict(zip(WEIGHTS, (m_norm_pre, m_norm_post, m_ffn1_w_gate, m_ffn1_w_up, m_ffn1_w_down, m_ffn2_w_gate,
                               m_ffn2_w_up, m_ffn2_w_down, m_w_in, m_conv_w, m_conv_b, m_conv_ln_g, m_conv_ln_b,
                               m_gla_w_alpha, m_gla_b_alpha, m_gla_norm_g, m_w_branch, m_w_out)))
    mom_v = dict(zip(WEIGHTS, (v_norm_pre, v_norm_post, v_ffn1_w_gate, v_ffn1_w_up, v_ffn1_w_down, v_ffn2_w_gate,
                               v_ffn2_w_up, v_ffn2_w_down, v_w_in, v_conv_w, v_conv_b, v_conv_ln_g, v_conv_ln_b,
                               v_gla_w_alpha, v_gla_b_alpha, v_gla_norm_g, v_w_branch, v_w_out)))
    depth = norm_pre.shape[0]
    members = [shards[k] if k.endswith("down") else jnp.swapaxes(shards[k], 1, 2) for k in FFN]
    ffn_rows = jnp.concatenate([_mx(t).reshape(depth * SLAB, D_MODEL) for t in members], axis=0)
    w6 = all_gather_chips(ffn_rows, name="gather_ffn").reshape(4, len(FFN), depth, SLAB, D_MODEL)
    full = gather_weights(shards, BIG, MXU_DTYPE, 256, "gather_big")
    full.update(gather_weights(shards, SMALL, F32, 16, "gather_small"))

    def layer_params(l):
        ffn = [dict(gpre=full["norm_pre"][l, k:k + 1], gpost=full["norm_post"][l, k:k + 1]) for k in (0, 2)]
        mix = dict(gpre=full["norm_pre"][l, 1:2], gpost=full["norm_post"][l, 1:2],
                   conv_w=jnp.pad(full["conv_w"][l], ((0, CONV_PAD - CONV_WIDTH), (0, 0))), conv_b=conv_b[l:l + 1],
                   ln_g=conv_ln_g[l:l + 1], ln_b=conv_ln_b[l:l + 1],
                   wa=jnp.pad(full["gla_w_alpha"][l], ((0, GLA_RANK_PAD - GLA_RANK), (0, 0))), ba=gla_b_alpha[l:l + 1],
                   gn=gla_norm_g[l:l + 1])
        return ffn, mix

    xs = x[0]
    saved = []
    for l in range(depth):
        w = layer_weights(full, l)
        ffn, mix = layer_params(l)
        xs, r1 = ffn_fwd(xs, ffn[0]["gpre"], ffn[0]["gpost"], w6, 0, l, f"l{l}_ffn1")
        xs, r2 = mixer_fwd(xs, mix, w, f"l{l}_mix")
        xs, r3 = ffn_fwd(xs, ffn[1]["gpre"], ffn[1]["gpost"], w6, 1, l, f"l{l}_ffn2")
        saved.append((w, ffn, mix, r1, r2, r3))
    dx, sq = loss_head(xs, loss_target[0], tm=TM, name="loss_head")
    loss = lax.psum(0.5 * jnp.sum(sq) / D_MODEL, ("x", "y", "c"))

    per_layer = []
    g6 = lax.empty(w6.shape, F32)
    for l in reversed(range(depth)):
        w, ffn, mix, r1, r2, r3 = saved[l]
        g = {}
        dx, gpre2, gpost2, g6 = ffn_bwd(dx, r3, ffn[1]["gpre"], ffn[1]["gpost"], w6, g6, 1, l, f"l{l}_ffn2")
        dx, gm = mixer_bwd(dx, r2, mix, w, f"l{l}_mix")
        dx, gpre0, gpost0, g6 = ffn_bwd(dx, r1, ffn[0]["gpre"], ffn[0]["gpost"], w6, g6, 0, l, f"l{l}_ffn1")
        g["norm_pre"] = jnp.concatenate([gpre0, gm["gpre"], gpre2], axis=0)
        g["norm_post"] = jnp.concatenate([gpost0, gm["gpost"], gpost2], axis=0)
        g["w_in"] = jnp.concatenate([gm["in_sb"], gm["in_conv"], gm["in_gla"], gm["in_lr"][:, :GLA_RANK], gm["in_gate"]],
                                    axis=1)
        g["conv_w"], g["conv_b"] = gm["conv_w"], gm["conv_b"][0]
        g["conv_ln_g"], g["conv_ln_b"] = gm["ln_g"][0], gm["ln_b"][0]
        g["gla_w_alpha"], g["gla_b_alpha"], g["gla_norm_g"] = gm["wa"][:GLA_RANK], gm["ba"][0], gm["gn"][0]
        g["w_branch"] = jnp.stack(gm["branch"])
        g["w_out"] = gm["out"]
        per_layer.append(g)
    per_layer.reverse()
    grads = {k: jnp.stack([g[k] for g in per_layer]) for k in BIG + SMALL + REPLICATED}

    grad_w = scatter_grads(grads, shards)
    ffn_sum = reduce_scatter(g6.reshape(4, len(FFN) * depth * SLAB, D_MODEL), "rs_ffn")
    ffn_sum = ffn_sum.reshape(len(FFN), depth, SLAB, D_MODEL)
    for t, k in enumerate(FFN):
        grad_w[k] = ffn_sum[t] if k.endswith("down") else jnp.swapaxes(ffn_sum[t], 1, 2)
    delta, new_m, new_v = {}, {}, {}
    for k in WEIGHTS:
        delta[k], new_m[k], new_v[k] = adamw(shards[k], grad_w[k], mom_m[k], mom_v[k], name="adamw_" + k)
    return (loss, dx[None], *[grad_w[k] for k in WEIGHTS], *[delta[k] for k in WEIGHTS], *[new_m[k] for k in WEIGHTS],
            *[new_v[k] for k in WEIGHTS])
```

```python
import functools

import jax
import jax.numpy as jnp
from jax import lax
from jax.experimental import pallas as pl
from jax.experimental.pallas import tpu as pltpu

F32 = jnp.float32
MXU_DTYPE = jnp.bfloat16
GRAD_WIRE_DTYPE = jnp.bfloat16
HIGHEST = lax.Precision.HIGHEST
MESH = pl.DeviceIdType.MESH

NORM_EPS = 1e-6
D_MODEL = 1024
D_FF = 2816
BRANCH = 512
CHUNK = 64
CONV_WIDTH = 31
CONV_PAD = 32
CONV_SUB = 256
GLA_RANK = 16
GLA_RANK_PAD = 128
GLA_TAU = 16.0
SB_SCALE = 0.125
SB_CUTOFF = 60.0
GLA_SCALE = 0.125
PACK_COLS = 1024
VMEM_LIMIT = 56 * 1024 * 1024

ADAM_LR, ADAM_B1, ADAM_B2, ADAM_EPS, ADAM_WD, ADAM_STEP = 0.001, 0.9, 0.999, 1e-08, 0.01, 10

NT = (((1,), (1,)), ((), ()))
TN = (((0,), (0,)), ((), ()))
NN = (((1,), (0,)), ((), ()))


def _params(sem=None, vmem=None):
    return pltpu.CompilerParams(dimension_semantics=sem, vmem_limit_bytes=vmem)


def _mx(v):
    return v.astype(MXU_DTYPE)


def _mx_round(v):
    return v.astype(MXU_DTYPE).astype(F32)


def _fit(dim, want):
    if dim <= want:
        return dim
    for d in range(want - want % 128, 0, -128):
        if dim % d == 0:
            return d
    raise ValueError((dim, want))


def mm(a, b, c=None, *, ta=False, tb=False, out_dtype=F32, tm=1024, tn=1024, tk=1024, name):
    K, M = a.shape if ta else a.shape[::-1]
    N = b.shape[0] if tb else b.shape[1]
    assert (b.shape[1] if tb else b.shape[0]) == K, (a.shape, b.shape, ta, tb)
    tm, tn, tk = _fit(M, tm), _fit(N, tn), _fit(K, tk)
    nk = K // tk
    dn = (((0 if ta else 1,), (1 if tb else 0,)), ((), ()))

    def body(*refs):
        if c is None:
            a_ref, b_ref, o_ref, acc = refs
            c_ref = None
        else:
            a_ref, b_ref, c_ref, o_ref, acc = refs
        k = pl.program_id(2)
        p = lax.dot_general(_mx(a_ref[...]), _mx(b_ref[...]), dn, preferred_element_type=F32)

        @pl.when(k == 0)
        def _():
            acc[...] = p

        @pl.when(k > 0)
        def _():
            acc[...] += p

        @pl.when(k == nk - 1)
        def _():
            r = acc[...]
            if c_ref is not None:
                r = r + c_ref[...].astype(F32)
            o_ref[...] = r.astype(o_ref.dtype)

    a_spec = pl.BlockSpec((tk, tm), lambda i, j, k: (k, i)) if ta else pl.BlockSpec((tm, tk), lambda i, j, k: (i, k))
    b_spec = pl.BlockSpec((tn, tk), lambda i, j, k: (j, k)) if tb else pl.BlockSpec((tk, tn), lambda i, j, k: (k, j))
    o_spec = pl.BlockSpec((tm, tn), lambda i, j, k: (i, j))
    ins, in_specs = [a, b], [a_spec, b_spec]
    if c is not None:
        ins.append(c)
        in_specs.append(o_spec)
    return pl.pallas_call(
        body, name=name, grid=(M // tm, N // tn, nk), in_specs=in_specs, out_specs=o_spec,
        out_shape=jax.ShapeDtypeStruct((M, N), out_dtype), scratch_shapes=[pltpu.VMEM((tm, tn), F32)],
        compiler_params=_params(("parallel", "parallel", "arbitrary"), VMEM_LIMIT),
    )(*ins)


SLAB = D_FF // 4


def _swiglu_of(ab_ref):
    return _mx(f_swiglu(ab_ref[0].astype(F32), ab_ref[1].astype(F32))[0])


def _gated_spec(rows, index):
    return pl.BlockSpec((2, None, rows, SLAB), index)


def slab_nt(a, w6, t0, n, l, *, out_dtype, tm=1024, name):
    S, K = a.shape
    tm = _fit(S, tm)

    def body(a_ref, w_ref, o_ref):
        o_ref[...] = lax.dot_general(_mx(a_ref[...]), _mx(w_ref[...]), NT, preferred_element_type=F32).astype(o_ref.dtype)

    return pl.pallas_call(
        body, name=name, grid=(S // tm, n),
        in_specs=[pl.BlockSpec((tm, K), lambda i, s: (i, 0)),
                  pl.BlockSpec((None, None, None, SLAB, K), lambda i, s: (s % 4, t0 + s // 4, l, 0, 0))],
        out_specs=pl.BlockSpec((None, tm, SLAB), lambda i, s: (s, i, 0)),
        out_shape=jax.ShapeDtypeStruct((n, S, SLAB), out_dtype),
        compiler_params=_params(("parallel", "parallel"), VMEM_LIMIT),
    )(a, w6)


def slab_nt_swiglu_bwd(d, w6, t0, l, ab, *, tm=1024, name):
    S, K = d.shape
    tm = _fit(S, tm)

    def body(d_ref, w_ref, ab_ref, o_ref):
        dz = lax.dot_general(_mx(d_ref[...]), _mx(w_ref[...]), NT, preferred_element_type=F32)
        _, pullback = jax.vjp(lambda g, u: f_swiglu(g, u)[0], ab_ref[0].astype(F32), ab_ref[1].astype(F32))
        dg, du = pullback(_mx_round(dz))
        o_ref[0] = dg.astype(o_ref.dtype)
        o_ref[1] = du.astype(o_ref.dtype)

    return pl.pallas_call(
        body, name=name, grid=(S // tm, 4),
        in_specs=[pl.BlockSpec((tm, K), lambda i, s: (i, 0)),
                  pl.BlockSpec((None, None, None, SLAB, K), lambda i, s: (s, t0, l, 0, 0)),
                  _gated_spec(tm, lambda i, s: (0, s, i, 0))],
        out_specs=_gated_spec(tm, lambda i, s: (0, s, i, 0)),
        out_shape=jax.ShapeDtypeStruct(ab.shape, MXU_DTYPE),
        compiler_params=_params(("parallel", "parallel"), VMEM_LIMIT),
    )(d, w6, ab)


def slab_nn(a, w6, t0, l, *, gated=False, tail=None, tm=1024, name):
    n, S, _ = a.shape[-3:]
    K = w6.shape[-1]
    tm = _fit(S, tm)
    kind = tail[0] if tail else None
    extra = list(tail[1:]) if tail else []

    def body(a_ref, w_ref, *rest):
        i, s = pl.program_id(0), pl.program_id(1)
        acc = rest[-1]
        lhs = _swiglu_of(a_ref) if gated else _mx(a_ref[...])
        p = jnp.dot(lhs, _mx(w_ref[...]), preferred_element_type=F32)

        @pl.when(s == 0)
        def _():
            acc[...] = p

        @pl.when(s > 0)
        def _():
            acc[...] += p

        @pl.when(s == n - 1)
        def _():
            r = acc[...]
            if kind is None:
                rest[0][...] = r
            elif kind == "half_post":
                x_ref, g_ref, o_ref, x2_ref = rest[:4]
                o_ref[...] = r
                x2_ref[...] = f_half_post(x_ref[...], r, g_ref[...])[0]
            else:
                x_ref, g_ref, d_ref, dx_ref, dg_ref = rest[:5]
                _, pullback = jax.vjp(lambda xv, gv: f_rms(xv, gv)[0], x_ref[...], g_ref[...])
                dxn, dg = pullback(r)
                dx_ref[...] = dxn + d_ref[...]

                @pl.when(i == 0)
                def _():
                    dg_ref[...] = dg

                @pl.when(i > 0)
                def _():
                    dg_ref[...] += dg

    rows = pl.BlockSpec((tm, K), lambda i, s: (i, 0))
    gain = pl.BlockSpec((1, K), lambda i, s: (0, 0))
    tail_in = {None: [], "half_post": [rows, gain], "rms_bwd": [rows, gain, rows]}[kind]
    full, vec = jax.ShapeDtypeStruct((S, K), F32), jax.ShapeDtypeStruct((1, K), F32)
    out_specs, out_shape = {None: (rows, full), "half_post": ([rows, rows], [full, full]),
                            "rms_bwd": ([rows, gain], [full, vec])}[kind]
    return pl.pallas_call(
        body, name=name, grid=(S // tm, n),
        in_specs=[_gated_spec(tm, lambda i, s: (0, s, i, 0)) if gated else
                  pl.BlockSpec((None, tm, SLAB), lambda i, s: (s, i, 0)),
                  pl.BlockSpec((None, None, None, SLAB, K), lambda i, s: (s % 4, t0 + s // 4, l, 0, 0))] + tail_in,
        out_specs=out_specs, out_shape=out_shape, scratch_shapes=[pltpu.VMEM((tm, K), F32)],
        compiler_params=_params(("arbitrary" if kind == "rms_bwd" else "parallel", "arbitrary"), VMEM_LIMIT),
    )(a, w6, *extra)


def slab_tn(a, b, g6, t0, l, *, gated=False, tk=1024, name):
    n, S, _ = a.shape[-3:]
    K = b.shape[1]
    tk = _fit(S, tk)
    nk = S // tk

    def body(a_ref, b_ref, g_ref, o_ref, acc):
        k = pl.program_id(1)
        lhs = _swiglu_of(a_ref) if gated else _mx(a_ref[...])
        p = lax.dot_general(lhs, _mx(b_ref[...]), TN, preferred_element_type=F32)

        @pl.when(k == 0)
        def _():
            acc[...] = p

        @pl.when(k > 0)
        def _():
            acc[...] += p

        @pl.when(k == nk - 1)
        def _():
            o_ref[...] = acc[...]

    return pl.pallas_call(
        body, name=name, grid=(n, nk),
        in_specs=[_gated_spec(tk, lambda s, k: (0, s, k, 0)) if gated else
                  pl.BlockSpec((None, tk, SLAB), lambda s, k: (s, k, 0)),
                  pl.BlockSpec((tk, K), lambda s, k: (k, 0)), pl.BlockSpec(memory_space=pl.ANY)],
        out_specs=pl.BlockSpec((None, None, None, SLAB, K), lambda s, k: (s % 4, t0 + s // 4, l, 0, 0)),
        out_shape=jax.ShapeDtypeStruct(g6.shape, g6.dtype), scratch_shapes=[pltpu.VMEM((SLAB, K), F32)],
        input_output_aliases={2: 0},
        compiler_params=_params(("parallel", "arbitrary"), VMEM_LIMIT),
    )(a, b, g6)


def _row_arg(arg):
    if isinstance(arg, tuple):
        return arg
    return arg, arg.shape[1], 0


def _row_specs(rows, n):
    arrs, specs, avals = [], [], []
    for arg in rows:
        arr, width, cb = _row_arg(arg)
        rb = arr.shape[0] // n
        arrs.append(arr)
        specs.append(pl.BlockSpec((rb, width), lambda i, cb=cb: (i, cb)))
        avals.append(jax.ShapeDtypeStruct((rb, width), F32))
    return arrs, specs, avals


def _par_specs(pars):
    specs = [pl.BlockSpec(p.shape, lambda i, nd=p.ndim: (0,) * nd) for p in pars]
    avals = [jax.ShapeDtypeStruct(p.shape, F32) for p in pars]
    return specs, avals


def rowwise(f, rows, pars, out_dtypes, *, tm, name):
    n = _row_arg(rows[0])[0].shape[0] // tm
    arrs, rspecs, ravals = _row_specs(rows, n)
    pspecs, pavals = _par_specs(pars)
    oavals = jax.eval_shape(f, *ravals, *pavals)
    nin = len(arrs) + len(pars)

    def body(*refs):
        outs = f(*[r[...].astype(F32) for r in refs[:nin]])
        for o_ref, o in zip(refs[nin:], outs):
            o_ref[...] = o.astype(o_ref.dtype)

    return pl.pallas_call(
        body, name=name, grid=(n,), in_specs=rspecs + pspecs,
        out_specs=[pl.BlockSpec(o.shape, lambda i: (i, 0)) for o in oavals],
        out_shape=[jax.ShapeDtypeStruct((n * o.shape[0], o.shape[1]), dt) for o, dt in zip(oavals, out_dtypes)],
        compiler_params=_params(("parallel",), VMEM_LIMIT),
    )(*arrs, *pars)


def rowwise_vjp(f, rows, pars, cots, row_grad, par_grad, d_dtypes, adds=None, *, tm, name):
    n = _row_arg(rows[0])[0].shape[0] // tm
    arrs, rspecs, ravals = _row_specs(rows, n)
    pspecs, pavals = _par_specs(pars)
    carrs, cspecs, _ = _row_specs(cots, n)
    adds = adds or [None] * len(row_grad)
    add_arrs = [a for a in adds if a is not None]
    _, aspecs, _ = _row_specs(add_arrs, n)
    nr, npar, nc, na = len(arrs), len(pars), len(carrs), len(add_arrs)
    diff = list(row_grad) + [nr + j for j in par_grad]
    ngr = len(row_grad)

    def body(*refs):
        ins = [r[...].astype(F32) for r in refs[:nr + npar]]
        cs = tuple(r[...].astype(F32) for r in refs[nr + npar:nr + npar + nc])
        add_refs = list(refs[nr + npar + nc:nr + npar + nc + na])
        outs = refs[nr + npar + nc + na:]

        def g(*d):
            full = list(ins)
            for idx, val in zip(diff, d):
                full[idx] = val
            return f(*full)

        _, pullback = jax.vjp(g, *[ins[idx] for idx in diff])
        ds = pullback(cs)
        for k in range(ngr):
            d = ds[k]
            if adds[k] is not None:
                d = d + add_refs.pop(0)[...].astype(F32)
            outs[k][...] = d.astype(outs[k].dtype)
        i = pl.program_id(0)
        for k in range(ngr, len(diff)):
            @pl.when(i == 0)
            def _(k=k):
                outs[k][...] = ds[k]

            @pl.when(i > 0)
            def _(k=k):
                outs[k][...] += ds[k]

    out_specs = [pl.BlockSpec(ravals[i].shape, lambda i: (i, 0)) for i in row_grad] + [pspecs[j] for j in par_grad]
    out_shape = [jax.ShapeDtypeStruct((arrs[i].shape[0], ravals[i].shape[1]), dt) for i, dt in zip(row_grad, d_dtypes)]
    out_shape += [jax.ShapeDtypeStruct(pars[j].shape, F32) for j in par_grad]
    return pl.pallas_call(
        body, name=name, grid=(n,), in_specs=rspecs + pspecs + cspecs + aspecs, out_specs=out_specs, out_shape=out_shape,
        compiler_params=_params(("arbitrary",), VMEM_LIMIT),
    )(*arrs, *pars, *carrs, *add_arrs)


def _logsig(x):
    return jnp.minimum(x, 0.0) - jnp.log(1.0 + jnp.exp(-jnp.abs(x)))


def _sigmoid(x):
    return 1.0 / (1.0 + jnp.exp(-x))


def _silu(x):
    return x * _sigmoid(x)


def _rms(x, g):
    return x * lax.rsqrt(jnp.mean(x * x, axis=-1, keepdims=True) + NORM_EPS) * g


def f_rms(x, g):
    return (_rms(x, g),)


def f_swiglu(a, b):
    return (_silu(a) * b,)


def f_half_post(x, f, g):
    return (x + 0.5 * _rms(f, g),)


def f_post(x, m, g):
    return (x + _rms(m, g),)


def f_glu(a, g):
    return (a * _sigmoid(g),)


def f_conv_ln(y, lg, lb):
    mu = jnp.mean(y, axis=-1, keepdims=True)
    var = jnp.mean(jnp.square(y - mu), axis=-1, keepdims=True)
    return (_silu((y - mu) * lax.rsqrt(var + NORM_EPS) * lg + lb),)


def f_merge(g0, g1, g2, b0, b1, b2):
    return (_sigmoid(g0) * b0 + _sigmoid(g1) * b1 + _sigmoid(g2) * b2,)


def f_gla_post(o, r, g):
    w = o.shape[1]
    hv = w // 4
    i = lax.broadcasted_iota(jnp.int32, (w, w), 0) // hv
    j = lax.broadcasted_iota(jnp.int32, (w, w), 1) // hv
    avg = jnp.where(i == j, 1.0 / hv, 0.0).astype(F32)
    ms = jnp.dot(o * o, avg, precision=HIGHEST, preferred_element_type=F32)
    return (o * lax.rsqrt(ms + NORM_EPS) * g * _silu(r),)


def f_gla_pre(q, k, lr, wa, ba):
    tm = q.shape[0]
    pre = jnp.dot(_mx_round(lr), _mx_round(wa), precision=HIGHEST, preferred_element_type=F32) + ba
    la = _logsig(pre) / GLA_TAU
    i = lax.broadcasted_iota(jnp.int32, (tm, tm), 0)
    j = lax.broadcasted_iota(jnp.int32, (tm, tm), 1)
    later = jnp.where((i // CHUNK == j // CHUNK) & (j > i), 1.0, 0.0).astype(F32)
    ci = lax.broadcasted_iota(jnp.int32, (tm // CHUNK, tm), 0)
    cj = lax.broadcasted_iota(jnp.int32, (tm // CHUNK, tm), 1) // CHUNK
    chunk_sum = jnp.where(ci == cj, 1.0, 0.0).astype(F32)
    to_end = jnp.dot(later, la, precision=HIGHEST, preferred_element_type=F32)
    lam = jnp.exp(jnp.dot(chunk_sum, la, precision=HIGHEST, preferred_element_type=F32))
    return q * GLA_SCALE, k * jnp.exp(to_end), lam


def _split_dot(x, u):
    hi = _mx(x)
    lo = _mx(x - hi.astype(F32))
    return jnp.dot(hi, u, preferred_element_type=F32) + jnp.dot(lo, u, preferred_element_type=F32)


def _tri(tq, tk):
    row = lax.broadcasted_iota(jnp.int32, (tq, tk), 0)
    col = lax.broadcasted_iota(jnp.int32, (tq, tk), 1)
    return row, col


def _sb_tile(qh, kh, valid, suf, run):
    z = lax.dot_general(qh, kh, NT, preferred_element_type=F32) * SB_SCALE
    lp = jnp.minimum(z, 0.0) - jnp.log(1.0 + jnp.exp(-jnp.abs(z)))
    lk = jnp.where(valid, lp - z, 0.0)
    inc = _split_dot(lk, suf)
    a = jnp.where(valid, jnp.exp(lp + (inc - lk + run)), 0.0)
    return lp, a, run + inc[:, 0:1]


def _sticks_left(run0, run1):
    return (jnp.maximum(jnp.max(run0), jnp.max(run1)) > -SB_CUTOFF).astype(jnp.int32)


def sb_attn_fwd(qkv, *, tq, name):
    S = qkv.shape[0]
    tk = tq
    pairs = BRANCH // 128

    def body(q_ref, k_ref, v_ref, o_ref):
        i = pl.program_id(1)
        row, col = _tri(tq, tk)
        suf = _mx(row >= col)
        q = q_ref[...]

        def step(state):
            j, _, carry = state
            ks = pl.multiple_of((i - j) * tk, tk)
            kb = k_ref[pl.ds(ks, tk), :]
            vb = v_ref[pl.ds(ks, tk), :]
            valid = (col - row) < j * tq
            new = []
            for h in range(2):
                acc, run = carry[h]
                sl = slice(64 * h, 64 * h + 64)
                _, a, run = _sb_tile(q[:, sl], kb[:, sl], valid, suf, run)
                acc = acc + jnp.dot(_mx(a), vb[:, sl], preferred_element_type=F32)
                new.append((acc, run))
            return j + 1, _sticks_left(new[0][1], new[1][1]), tuple(new)

        zero = (jnp.zeros((tq, 64), F32), jnp.zeros((tq, 1), F32))
        _, _, res = lax.while_loop(lambda s: (s[0] <= i) & (s[1] > 0), step, (jnp.int32(0), jnp.int32(1), (zero, zero)))
        o_ref[...] = jnp.concatenate([res[0][0], res[1][0]], axis=1).astype(o_ref.dtype)

    return pl.pallas_call(
        body, name=name, grid=(pairs, S // tq),
        in_specs=[pl.BlockSpec((tq, 128), lambda p, i: (i, p)),
                  pl.BlockSpec((S, 128), lambda p, i: (0, pairs + p)),
                  pl.BlockSpec((S, 128), lambda p, i: (0, 2 * pairs + p))],
        out_specs=pl.BlockSpec((tq, 128), lambda p, i: (i, p)),
        out_shape=jax.ShapeDtypeStruct((S, BRANCH), MXU_DTYPE),
        compiler_params=_params(("parallel", "parallel"), VMEM_LIMIT),
    )(qkv, qkv, qkv)


def sb_attn_bwd(qkv, do, *, tq, name):
    S = qkv.shape[0]
    tk = tq
    nq = S // tq
    pairs = BRANCH // 128

    def body(q_ref, k_ref, v_ref, do_ref, dq_ref, dk_ref, dv_ref, g_sc, b_sc):
        i = pl.program_id(1)

        @pl.when(i == 0)
        def _():
            dk_ref[...] = jnp.zeros_like(dk_ref)
            dv_ref[...] = jnp.zeros_like(dv_ref)

        row, col = _tri(tq, tk)
        suf = _mx(row >= col)
        pre = _mx(row <= col)
        q = q_ref[...]
        dout = do_ref[...]

        def sweep1(state):
            j, _, carry = state
            kblk = i - j
            ks = pl.multiple_of(kblk * tk, tk)
            kb = k_ref[pl.ds(ks, tk), :]
            vb = v_ref[pl.ds(ks, tk), :]
            valid = (col - row) < j * tq
            runs, dvs = [], []
            for h in range(2):
                sl = slice(64 * h, 64 * h + 64)
                lp, a, run = _sb_tile(q[:, sl], kb[:, sl], valid, suf, carry[h])
                da = lax.dot_general(dout[:, sl], vb[:, sl], NT, preferred_element_type=F32)
                g_sc[h, kblk] = (a * da).astype(g_sc.dtype)
                b_sc[h, kblk] = jnp.where(valid, jnp.exp(lp), 0.0).astype(b_sc.dtype)
                dvs.append(lax.dot_general(_mx(a), dout[:, sl], TN, preferred_element_type=F32))
                runs.append(run)
            dv_ref[pl.ds(ks, tk), :] += jnp.concatenate(dvs, axis=1)
            return j + 1, _sticks_left(runs[0], runs[1]), tuple(runs)

        start = (jnp.int32(0), jnp.int32(1), (jnp.zeros((tq, 1), F32), jnp.zeros((tq, 1), F32)))
        tiles, _, _ = lax.while_loop(lambda s: (s[0] <= i) & (s[1] > 0), sweep1, start)

        def sweep2(kblk, carry):
            ks = pl.multiple_of(kblk * tk, tk)
            kb = k_ref[pl.ds(ks, tk), :]
            new, dks = [], []
            for h in range(2):
                dq, run = carry[h]
                sl = slice(64 * h, 64 * h + 64)
                g = g_sc[h, kblk]
                beta = b_sc[h, kblk].astype(F32)
                inc = jnp.dot(g, pre, preferred_element_type=F32)
                g = g.astype(F32)
                dz = _mx((g - beta * (inc + run)) * SB_SCALE)
                dq = dq + jnp.dot(dz, kb[:, sl], preferred_element_type=F32)
                dks.append(lax.dot_general(dz, q[:, sl], TN, preferred_element_type=F32))
                new.append((dq, run + inc[:, tk - 1:tk]))
            dk_ref[pl.ds(ks, tk), :] += jnp.concatenate(dks, axis=1)
            return tuple(new)

        zero = (jnp.zeros((tq, 64), F32), jnp.zeros((tq, 1), F32))
        res = lax.fori_loop(i + 1 - tiles, i + 1, sweep2, (zero, zero))
        dq_ref[...] = jnp.concatenate([res[0][0], res[1][0]], axis=1).astype(dq_ref.dtype)

    return pl.pallas_call(
        body, name=name, grid=(pairs, nq),
        in_specs=[pl.BlockSpec((tq, 128), lambda p, i: (i, p)),
                  pl.BlockSpec((S, 128), lambda p, i: (0, pairs + p)),
                  pl.BlockSpec((S, 128), lambda p, i: (0, 2 * pairs + p)),
                  pl.BlockSpec((tq, 128), lambda p, i: (i, p))],
        out_specs=[pl.BlockSpec((tq, 128), lambda p, i: (i, p)),
                   pl.BlockSpec((S, 128), lambda p, i: (0, p)),
                   pl.BlockSpec((S, 128), lambda p, i: (0, p))],
        out_shape=[jax.ShapeDtypeStruct((S, BRANCH), MXU_DTYPE), jax.ShapeDtypeStruct((S, BRANCH), F32),
                   jax.ShapeDtypeStruct((S, BRANCH), F32)],
        scratch_shapes=[pltpu.VMEM((2, nq, tq, tk), MXU_DTYPE), pltpu.VMEM((2, nq, tq, tk), MXU_DTYPE)],
        compiler_params=_params(("parallel", "arbitrary"), VMEM_LIMIT),
    )(qkv, qkv, qkv, do)


def conv_fwd(u, w, b, *, tm, name):
    S, C = u.shape
    hb = tm // CONV_PAD

    def body(u_ref, halo_ref, w_ref, b_ref, y_ref, buf):
        i = pl.program_id(0)
        buf[pl.ds(CONV_PAD, tm), :] = u_ref[...]
        buf[pl.ds(0, CONV_PAD), :] = jnp.where(i > 0, halo_ref[...], 0.0)
        ts = min(tm, CONV_SUB)
        for r in range(0, tm, ts):
            acc = jnp.broadcast_to(b_ref[...], (ts, 128))
            for j in range(CONV_WIDTH):
                acc = acc + buf[pl.ds(r + CONV_PAD - (CONV_WIDTH - 1) + j, ts), :] * w_ref[pl.ds(j, 1), :]
            y_ref[pl.ds(r, ts), :] = acc

    return pl.pallas_call(
        body, name=name, grid=(S // tm, C // 128),
        in_specs=[pl.BlockSpec((tm, 128), lambda i, c: (i, c)),
                  pl.BlockSpec((CONV_PAD, 128), lambda i, c: (jnp.maximum(i * hb - 1, 0), c)),
                  pl.BlockSpec((CONV_PAD, 128), lambda i, c: (0, c)),
                  pl.BlockSpec((1, 128), lambda i, c: (0, c))],
        out_specs=pl.BlockSpec((tm, 128), lambda i, c: (i, c)),
        out_shape=jax.ShapeDtypeStruct((S, C), F32),
        scratch_shapes=[pltpu.VMEM((tm + CONV_PAD, 128), F32)],
        compiler_params=_params(("parallel", "parallel")),
    )(u, u, w, b)


def conv_bwd(dy, u, w, *, tm, name):
    S, C = u.shape
    hb = tm // CONV_PAD
    n = S // tm

    def body(dy_ref, dyn_ref, u_ref, up_ref, w_ref, du_ref, dw_ref, bufy, bufu):
        i = pl.program_id(1)
        dyv = dy_ref[...]
        bufy[pl.ds(0, tm), :] = dyv
        bufy[pl.ds(tm, CONV_PAD), :] = jnp.where(i < n - 1, dyn_ref[...], 0.0)
        bufu[pl.ds(CONV_PAD, tm), :] = u_ref[...]
        bufu[pl.ds(0, CONV_PAD), :] = jnp.where(i > 0, up_ref[...], 0.0)

        @pl.when(i == 0)
        def _():
            dw_ref[...] = jnp.zeros_like(dw_ref)

        ts = min(tm, CONV_SUB)
        for r in range(0, tm, ts):
            dys = dy_ref[pl.ds(r, ts), :]
            acc = jnp.zeros((ts, 128), F32)
            for j in range(CONV_WIDTH):
                acc = acc + bufy[pl.ds(r + CONV_WIDTH - 1 - j, ts), :] * w_ref[pl.ds(j, 1), :]
                shifted = bufu[pl.ds(r + CONV_PAD - (CONV_WIDTH - 1) + j, ts), :]
                dw_ref[pl.ds(j, 1), :] += jnp.sum(dys * shifted, axis=0, keepdims=True)
            du_ref[pl.ds(r, ts), :] = acc
            dw_ref[pl.ds(CONV_WIDTH, 1), :] += jnp.sum(dys, axis=0, keepdims=True)

    return pl.pallas_call(
        body, name=name, grid=(C // 128, n),
        in_specs=[pl.BlockSpec((tm, 128), lambda c, i: (i, c)),
                  pl.BlockSpec((CONV_PAD, 128), lambda c, i: (jnp.minimum((i + 1) * hb, n * hb - 1), c)),
                  pl.BlockSpec((tm, 128), lambda c, i: (i, c)),
                  pl.BlockSpec((CONV_PAD, 128), lambda c, i: (jnp.maximum(i * hb - 1, 0), c)),
                  pl.BlockSpec((CONV_PAD, 128), lambda c, i: (0, c))],
        out_specs=[pl.BlockSpec((tm, 128), lambda c, i: (i, c)), pl.BlockSpec((CONV_PAD, 128), lambda c, i: (0, c))],
        out_shape=[jax.ShapeDtypeStruct((S, C), F32), jax.ShapeDtypeStruct((CONV_PAD, C), F32)],
        scratch_shapes=[pltpu.VMEM((tm + CONV_PAD, 128), F32), pltpu.VMEM((tm + CONV_PAD, 128), F32)],
        compiler_params=_params(("parallel", "arbitrary")),
    )(dy, dy, u, u, w)


GLA_HEADS, GLA_DK, GLA_DV = 4, 64, 128


def gla_scan_fwd(qs, kd, gl, lam, *, tm, name):
    S = qs.shape[0]
    nc = tm // CHUNK

    def body(qs_ref, kd_ref, v_ref, lam_ref, o_ref, st_ref, state):
        @pl.when(pl.program_id(0) == 0)
        def _():
            state[...] = jnp.zeros_like(state)

        for c in range(nc):
            rows = pl.ds(c * CHUNK, CHUNK)
            q, k, v = _mx(qs_ref[rows, :]), _mx(kd_ref[rows, :]), _mx(v_ref[rows, :])
            upd = [lax.dot_general(v[:, h * GLA_DV:(h + 1) * GLA_DV], k[:, h * GLA_DK:(h + 1) * GLA_DK], TN,
                                   preferred_element_type=F32) for h in range(GLA_HEADS)]
            st = state[...] * lam_ref[pl.ds(c, 1), :] + jnp.concatenate(upd, axis=1)
            state[...] = st
            st_ref[c] = st
            stm = _mx(st)
            o = [lax.dot_general(q[:, h * GLA_DK:(h + 1) * GLA_DK], stm[:, h * GLA_DK:(h + 1) * GLA_DK], NT,
                                 preferred_element_type=F32) for h in range(GLA_HEADS)]
            o_ref[rows, :] = jnp.concatenate(o, axis=1)

    dk_all = GLA_HEADS * GLA_DK
    dv_all = GLA_HEADS * GLA_DV
    return pl.pallas_call(
        body, name=name, grid=(S // tm,),
        in_specs=[pl.BlockSpec((tm, dk_all), lambda i: (i, 0)), pl.BlockSpec((tm, dk_all), lambda i: (i, 0)),
                  pl.BlockSpec((tm, dv_all), lambda i: (i, 1)), pl.BlockSpec((nc, dk_all), lambda i: (i, 0))],
        out_specs=[pl.BlockSpec((tm, dv_all), lambda i: (i, 0)), pl.BlockSpec((nc, GLA_DV, dk_all), lambda i: (i, 0, 0))],
        out_shape=[jax.ShapeDtypeStruct((S, dv_all), F32), jax.ShapeDtypeStruct((S // CHUNK, GLA_DV, dk_all), F32)],
        scratch_shapes=[pltpu.VMEM((GLA_DV, dk_all), F32)],
        compiler_params=_params(("arbitrary",)),
    )(qs, kd, gl, lam)


def gla_scan_bwd(do, qs, kd, gl, lam, states, *, tm, name):
    S = qs.shape[0]
    nc = tm // CHUNK
    n = S // tm
    dk_all = GLA_HEADS * GLA_DK
    dv_all = GLA_HEADS * GLA_DV

    def body(do_ref, qs_ref, kd_ref, v_ref, lam_ref, st_ref, prev_ref, dqs_ref, dkd_ref, dv_ref, dlam_ref, carry):
        i = pl.program_id(0)

        @pl.when(i == 0)
        def _():
            carry[...] = jnp.zeros_like(carry)

        for c in reversed(range(nc)):
            rows = pl.ds(c * CHUNK, CHUNK)
            q, k, v, d = _mx(qs_ref[rows, :]), _mx(kd_ref[rows, :]), _mx(v_ref[rows, :]), _mx(do_ref[rows, :])
            st = _mx(st_ref[c])
            before = st_ref[c - 1] if c > 0 else jnp.where(i < n - 1, prev_ref[0], 0.0)
            outer = [lax.dot_general(d[:, h * GLA_DV:(h + 1) * GLA_DV], q[:, h * GLA_DK:(h + 1) * GLA_DK], TN,
                                     preferred_element_type=F32) for h in range(GLA_HEADS)]
            dst = carry[...] + jnp.concatenate(outer, axis=1)
            dstm = _mx(dst)
            dq, dkk, dvv = [], [], []
            for h in range(GLA_HEADS):
                ksl = slice(h * GLA_DK, (h + 1) * GLA_DK)
                vsl = slice(h * GLA_DV, (h + 1) * GLA_DV)
                dq.append(jnp.dot(d[:, vsl], st[:, ksl], preferred_element_type=F32))
                dkk.append(jnp.dot(v[:, vsl], dstm[:, ksl], preferred_element_type=F32))
                dvv.append(lax.dot_general(k[:, ksl], dstm[:, ksl], NT, preferred_element_type=F32))
            dqs_ref[rows, :] = jnp.concatenate(dq, axis=1)
            dkd_ref[rows, :] = jnp.concatenate(dkk, axis=1)
            dv_ref[rows, :] = jnp.concatenate(dvv, axis=1)
            dlam_ref[pl.ds(c, 1), :] = jnp.sum(dst * before, axis=0, keepdims=True)
            carry[...] = dst * lam_ref[pl.ds(c, 1), :]

    rev = lambda i: n - 1 - i
    return pl.pallas_call(
        body, name=name, grid=(n,),
        in_specs=[pl.BlockSpec((tm, dv_all), lambda i: (rev(i), 0)), pl.BlockSpec((tm, dk_all), lambda i: (rev(i), 0)),
                  pl.BlockSpec((tm, dk_all), lambda i: (rev(i), 0)), pl.BlockSpec((tm, dv_all), lambda i: (rev(i), 1)),
                  pl.BlockSpec((nc, dk_all), lambda i: (rev(i), 0)),
                  pl.BlockSpec((nc, GLA_DV, dk_all), lambda i: (rev(i), 0, 0)),
                  pl.BlockSpec((1, GLA_DV, dk_all), lambda i: (jnp.maximum(rev(i) * nc - 1, 0), 0, 0))],
        out_specs=[pl.BlockSpec((tm, dk_all), lambda i: (rev(i), 0)), pl.BlockSpec((tm, dk_all), lambda i: (rev(i), 0)),
                   pl.BlockSpec((tm, dv_all), lambda i: (rev(i), 0)), pl.BlockSpec((nc, dk_all), lambda i: (rev(i), 0))],
        out_shape=[jax.ShapeDtypeStruct((S, dk_all), F32), jax.ShapeDtypeStruct((S, dk_all), F32),
                   jax.ShapeDtypeStruct((S, dv_all), F32), jax.ShapeDtypeStruct((S // CHUNK, dk_all), F32)],
        scratch_shapes=[pltpu.VMEM((GLA_DV, dk_all), F32)],
        compiler_params=_params(("arbitrary",)),
    )(do, qs, kd, gl, lam, states, states)


def loss_head(y, target, *, tm, name):
    S, D = y.shape

    def body(y_ref, t_ref, dy_ref, sq_ref):
        err = y_ref[...] - t_ref[...]
        dy_ref[...] = err * (1.0 / D)
        part = jnp.sum(err * err, axis=0, keepdims=True)

        @pl.when(pl.program_id(0) == 0)
        def _():
            sq_ref[...] = part

        @pl.when(pl.program_id(0) > 0)
        def _():
            sq_ref[...] += part

    return pl.pallas_call(
        body, name=name, grid=(S // tm,),
        in_specs=[pl.BlockSpec((tm, D), lambda i: (i, 0)), pl.BlockSpec((tm, D), lambda i: (i, 0))],
        out_specs=[pl.BlockSpec((tm, D), lambda i: (i, 0)), pl.BlockSpec((1, D), lambda i: (0, 0))],
        out_shape=[jax.ShapeDtypeStruct((S, D), F32), jax.ShapeDtypeStruct((1, D), F32)],
        compiler_params=_params(("arbitrary",)),
    )(y, target)


def f_adamw(w, g, m, v):
    m = ADAM_B1 * m + (1.0 - ADAM_B1) * g
    v = ADAM_B2 * v + (1.0 - ADAM_B2) * jnp.square(g)
    m_hat = m / (1.0 - ADAM_B1 ** ADAM_STEP)
    v_hat = v / (1.0 - ADAM_B2 ** ADAM_STEP)
    return -ADAM_LR * (m_hat / (jnp.sqrt(v_hat) + ADAM_EPS) + ADAM_WD * w), m, v


def adamw(w, g, m, v, *, name):
    shape = w.shape
    cols = shape[-1]
    rows = w.size // cols
    tm = rows
    while tm % 16 == 0 and tm * cols * 4 > (1 << 20):
        tm //= 2
    flat = [t.reshape(rows, cols) for t in (w, g, m, v)]
    outs = rowwise(f_adamw, flat, [], [F32, F32, F32], tm=tm, name=name)
    return [o.reshape(shape) for o in outs]


def _place():
    x, y, c = lax.axis_index("x"), lax.axis_index("y"), lax.axis_index("c")
    return x, y, c, [(1 - x, y), (x, 1 - y), (1 - x, 1 - y)]


def _my_chip():
    return 2 * lax.axis_index("x") + lax.axis_index("y")


def _any():
    return pl.BlockSpec(memory_space=pl.ANY)


DMA_CHUNKS = 8


def _chunks(rows):
    n = DMA_CHUNKS
    while n > 1 and rows % (n * 16):
        n //= 2
    return [(k * (rows // n), rows // n) for k in range(n)]


def _start_chunked(make, rows):
    for off, size in _chunks(rows):
        make(off, size).start()


def all_gather_chips(v, *, name):
    R, C = v.shape
    H = R // 2

    def body(v_ref, o_ref, send, recv, fsend, frecv):
        x, y, c, chips = _place()
        me = 2 * x + y
        ids = [2 * px + py for px, py in chips]
        mine = pl.ds(pl.multiple_of(c * H, 8), H)
        other = pl.ds(pl.multiple_of((1 - c) * H, 8), H)

        def cross(j, src_chip, rows):
            return pltpu.make_async_remote_copy(
                src_ref=v_ref.at[rows], dst_ref=o_ref.at[src_chip, rows], send_sem=send.at[j], recv_sem=recv.at[j],
                device_id=(*chips[j], c), device_id_type=MESH)

        def handed(j, rows):
            return pltpu.make_async_remote_copy(
                src_ref=o_ref.at[ids[j], rows], dst_ref=o_ref.at[ids[j], rows], send_sem=fsend.at[j],
                recv_sem=frecv.at[j], device_id=(x, y, 1 - c), device_id_type=MESH)

        def my_rows(off, size):
            return pl.ds(pl.multiple_of(c * H + off, 8), size)

        for j in range(3):
            _start_chunked(lambda off, size, j=j: cross(j, me, my_rows(off, size)), H)
        for j in range(3):
            cross(j, ids[j], mine).wait_recv()
            _start_chunked(lambda off, size, j=j: handed(j, my_rows(off, size)), H)
        for j in range(3):
            handed(j, other).wait_recv()
        for j in range(3):
            cross(j, me, mine).wait_send()
            handed(j, mine).wait_send()

    got = pl.pallas_call(
        body, name=name, in_specs=[_any()], out_specs=_any(), out_shape=jax.ShapeDtypeStruct((4, R, C), v.dtype),
        scratch_shapes=[pltpu.SemaphoreType.DMA((3,)), pltpu.SemaphoreType.DMA((3,)), pltpu.SemaphoreType.DMA((3,)),
                        pltpu.SemaphoreType.DMA((3,))],
    )(v)
    return lax.dynamic_update_slice(got, v[None], (_my_chip(), 0, 0))


def swap_halves(g, *, name):
    n, R, C = g.shape
    H = R // 2

    def body(g_ref, theirs_ref, send, recv):
        x, y, c, _ = _place()

        def give(q, off, size):
            return pltpu.make_async_remote_copy(
                src_ref=g_ref.at[q, pl.ds(pl.multiple_of((1 - c) * H + off, 8), size)],
                dst_ref=theirs_ref.at[q, pl.ds(off, size)], send_sem=send.at[q], recv_sem=recv.at[q],
                device_id=(x, y, 1 - c), device_id_type=MESH)

        for q in range(n):
            _start_chunked(functools.partial(give, q), H)
        for q in range(n):
            give(q, 0, H).wait()

    return pl.pallas_call(
        body, name=name, in_specs=[_any()], out_specs=_any(), out_shape=jax.ShapeDtypeStruct((n, H, C), g.dtype),
        scratch_shapes=[pltpu.SemaphoreType.DMA((n,)), pltpu.SemaphoreType.DMA((n,))],
    )(g)


def exchange_pieces(p, *, name):
    H = p.shape[1]

    def body(p_ref, b_ref, send, recv):
        x, y, c, chips = _place()
        me = 2 * x + y
        ids = [2 * px + py for px, py in chips]

        def cross(j, piece, slot, off, size):
            return pltpu.make_async_remote_copy(
                src_ref=p_ref.at[piece, pl.ds(off, size)], dst_ref=b_ref.at[slot, pl.ds(off, size)], send_sem=send.at[j],
                recv_sem=recv.at[j], device_id=(*chips[j], c), device_id_type=MESH)

        for j in range(3):
            _start_chunked(functools.partial(cross, j, ids[j], me), H)
        for j in range(3):
            cross(j, me, ids[j], 0, H).wait_recv()
        for j in range(3):
            cross(j, ids[j], me, 0, H).wait_send()

    got = pl.pallas_call(
        body, name=name, in_specs=[_any()], out_specs=_any(), out_shape=jax.ShapeDtypeStruct(p.shape, p.dtype),
        scratch_shapes=[pltpu.SemaphoreType.DMA((3,)), pltpu.SemaphoreType.DMA((3,))],
    )(p)
    me = _my_chip()
    return lax.dynamic_update_slice(got, lax.dynamic_slice_in_dim(p, me, 1, axis=0), (me, 0, 0))


def join_halves(f, *, name):
    H, C = f.shape

    def body(f_ref, o_ref, send, recv):
        x, y, c, _ = _place()

        def give(off, size):
            return pltpu.make_async_remote_copy(
                src_ref=f_ref.at[pl.ds(off, size)], dst_ref=o_ref.at[pl.ds(off, size)], send_sem=send, recv_sem=recv,
                device_id=(x, y, 1 - c), device_id_type=MESH)

        _start_chunked(give, H)
        give(0, H).wait()

    theirs = pl.pallas_call(
        body, name=name, in_specs=[_any()], out_specs=_any(), out_shape=jax.ShapeDtypeStruct((H, C), f.dtype),
        scratch_shapes=[pltpu.SemaphoreType.DMA, pltpu.SemaphoreType.DMA],
    )(f)
    south = lax.axis_index("c") == 0
    return jnp.concatenate([jnp.where(south, f, theirs), jnp.where(south, theirs, f)], axis=0)


def _row_tile(rows, want):
    tm = want
    while rows % tm:
        tm //= 2
    return tm


def _add_pair(g, theirs, out_dtype, *, name):
    n, R, C = g.shape
    H = R // 2
    tm = _row_tile(H, 512)
    nb = H // tm

    def body(c_ref, g_ref, t_ref, o_ref):
        o_ref[...] = (g_ref[...] + t_ref[...]).astype(o_ref.dtype)

    grid_spec = pltpu.PrefetchScalarGridSpec(
        num_scalar_prefetch=1, grid=(n, nb),
        in_specs=[pl.BlockSpec((None, tm, C), lambda q, i, c: (q, c[0] * nb + i, 0)),
                  pl.BlockSpec((None, tm, C), lambda q, i, c: (q, i, 0))],
        out_specs=pl.BlockSpec((None, tm, C), lambda q, i, c: (q, i, 0)))
    return pl.pallas_call(
        body, name=name, grid_spec=grid_spec, out_shape=jax.ShapeDtypeStruct((n, H, C), out_dtype),
        compiler_params=_params(("parallel", "parallel")),
    )(lax.axis_index("c").astype(jnp.int32).reshape(1), g, theirs)


def _add_stack(b, *, name):
    n, H, C = b.shape
    tm = _row_tile(H, 256)

    def body(b_ref, o_ref):
        s = b_ref[0].astype(F32)
        for k in range(1, n):
            s = s + b_ref[k].astype(F32)
        o_ref[...] = s

    return pl.pallas_call(
        body, name=name, grid=(H // tm,), in_specs=[pl.BlockSpec((n, tm, C), lambda i: (0, i, 0))],
        out_specs=pl.BlockSpec((tm, C), lambda i: (i, 0)), out_shape=jax.ShapeDtypeStruct((H, C), F32),
        compiler_params=_params(("parallel",)),
    )(b)


def reduce_scatter(g, tag):
    pair = _add_pair(g, swap_halves(g, name=tag + "_swap_halves"), GRAD_WIRE_DTYPE, name=tag + "_add_pair")
    total = _add_stack(exchange_pieces(pair, name=tag + "_exchange"), name=tag + "_add_chips")
    return join_halves(total, name=tag + "_join")


FFN = ("ffn1_w_gate", "ffn1_w_up", "ffn2_w_gate", "ffn2_w_up", "ffn1_w_down", "ffn2_w_down")
BIG = ("w_in", "w_branch", "w_out")
SMALL = ("norm_pre", "norm_post", "conv_w", "gla_w_alpha")
REPLICATED = ("conv_b", "conv_ln_g", "conv_ln_b", "gla_b_alpha", "gla_norm_g")
WEIGHTS = ("norm_pre", "norm_post", "ffn1_w_gate", "ffn1_w_up", "ffn1_w_down", "ffn2_w_gate", "ffn2_w_up", "ffn2_w_down",
           "w_in", "conv_w", "conv_b", "conv_ln_g", "conv_ln_b", "gla_w_alpha", "gla_b_alpha", "gla_norm_g", "w_branch",
           "w_out")
SHARD_AXIS = {"ffn1_w_gate": 2, "ffn1_w_up": 2, "ffn1_w_down": 1, "ffn2_w_gate": 2, "ffn2_w_up": 2, "ffn2_w_down": 1,
              "w_in": 2, "w_branch": 3, "w_out": 1, "norm_pre": 2, "norm_post": 2, "conv_w": 2, "gla_w_alpha": 2}


def _size(shape):
    n = 1
    for d in shape:
        n *= d
    return n


def _pack_rows(shape):
    return -(-_size(shape) // (16 * PACK_COLS)) * 16


def _pack(arrs, dtype, row_mult):
    parts, rows = [], 0
    for a in arrs:
        r = _pack_rows(a.shape)
        flat = a.astype(dtype).reshape(-1)
        if r * PACK_COLS != flat.shape[0]:
            flat = jnp.pad(flat, (0, r * PACK_COLS - flat.shape[0]))
        parts.append(flat.reshape(r, PACK_COLS))
        rows += r
    if rows % row_mult:
        parts.append(jnp.zeros((row_mult - rows % row_mult, PACK_COLS), dtype))
    return jnp.concatenate(parts, axis=0)


def _unpack(buf, shapes):
    lead = buf.shape[:-2]
    out, off = [], 0
    for s in shapes:
        r = _pack_rows(s)
        part = lax.slice_in_dim(buf, off, off + r, axis=buf.ndim - 2).reshape(lead + (r * PACK_COLS,))
        if r * PACK_COLS != _size(s):
            part = lax.slice_in_dim(part, 0, _size(s), axis=part.ndim - 1)
        out.append(part.reshape(lead + tuple(s)))
        off += r
    return out


def gather_weights(shards, names, dtype, row_mult, name):
    packed = _pack([shards[k] for k in names], dtype, row_mult)
    parts = _unpack(all_gather_chips(packed, name=name), [shards[k].shape for k in names])
    return {k: jnp.concatenate([p[q] for q in range(4)], axis=SHARD_AXIS[k]) for k, p in zip(names, parts)}


def scatter_grads(grads, shards):
    sharded = BIG + SMALL
    pieces = []
    for q in range(4):
        part = []
        for k in sharded:
            w = shards[k].shape[SHARD_AXIS[k]]
            part.append(lax.slice_in_dim(grads[k], q * w, (q + 1) * w, axis=SHARD_AXIS[k]))
        part += [grads[k] for k in REPLICATED]
        pieces.append(_pack(part, F32, 512))
    total = reduce_scatter(jnp.stack(pieces), "rs_mix")
    names = sharded + REPLICATED
    return dict(zip(names, _unpack(total, [shards[k].shape for k in names])))


TM = 512
TM_MERGE = 256
TM_CONV = 1024
TM_GLA = 512
TQ = 256

IN_SB, IN_CONV, IN_GLA, IN_LR, IN_GATE = 0, 1536, 2560, 4096, 4112
IN_END = 7184


def layer_weights(full, l):
    w = {}
    win = full["w_in"][l]
    w["in_sb"] = win[:, IN_SB:IN_CONV]
    w["in_conv"] = win[:, IN_CONV:IN_GLA]
    w["in_gla"] = win[:, IN_GLA:IN_LR]
    w["in_lr"] = jnp.pad(win[:, IN_LR:IN_GATE], ((0, 0), (0, GLA_RANK_PAD - GLA_RANK)))
    w["in_gate"] = win[:, IN_GATE:IN_END]
    w["branch"] = [full["w_branch"][l, g] for g in range(3)]
    w["out"] = full["w_out"][l]
    return w


def ffn_fwd(x, gpre, gpost, w6, f, l, tag):
    S = x.shape[0]
    h = rowwise(f_rms, [x], [gpre], [MXU_DTYPE], tm=TM, name=tag + "_pre")[0]
    ab = slab_nt(h, w6, 2 * f, 8, l, out_dtype=MXU_DTYPE, name=tag + "_gu").reshape(2, 4, S, SLAB)
    out, x2 = slab_nn(ab, w6, 4 + f, l, gated=True, tail=("half_post", x, gpost), name=tag + "_down")
    return x2, (x, h, ab, out)


def ffn_bwd(dx2, res, gpre, gpost, w6, g6, f, l, tag):
    x, h, ab, out = res
    S = x.shape[0]
    df, dgpost = rowwise_vjp(f_half_post, [x, out], [gpost], [dx2], [1], [0], [MXU_DTYPE], tm=TM, name=tag + "_post_b")
    dab = slab_nt_swiglu_bwd(df, w6, 4 + f, l, ab, name=tag + "_down_dx").reshape(8, S, SLAB)
    g6 = slab_tn(ab, df, g6, 4 + f, l, gated=True, name=tag + "_down_dw")
    dx, dgpre = slab_nn(dab, w6, 2 * f, l, tail=("rms_bwd", x, gpre, dx2), name=tag + "_gu_dx")
    g6 = slab_tn(dab, h, g6, 2 * f, l, name=tag + "_gu_dw")
    return dx, dgpre, dgpost, g6


def mixer_fwd(x, p, w, tag):
    h = rowwise(f_rms, [x], [p["gpre"]], [MXU_DTYPE], tm=TM, name=tag + "_pre")[0]
    qkv = mm(h, w["in_sb"], out_dtype=MXU_DTYPE, tn=512, name=tag + "_in_sb")
    cv = mm(h, w["in_conv"], name=tag + "_in_conv")
    gl = mm(h, w["in_gla"], out_dtype=MXU_DTYPE, tn=512, name=tag + "_in_gla")
    lr = mm(h, w["in_lr"], out_dtype=MXU_DTYPE, name=tag + "_in_lr")
    gt = mm(h, w["in_gate"], out_dtype=MXU_DTYPE, name=tag + "_in_gate")
    sb = sb_attn_fwd(qkv, tq=TQ, name=tag + "_sb")
    u = rowwise(f_glu, [(cv, BRANCH, 0), (cv, BRANCH, 1)], [], [F32], tm=TM, name=tag + "_glu")[0]
    y = conv_fwd(u, p["conv_w"], p["conv_b"], tm=TM_CONV, name=tag + "_conv")
    cb = rowwise(f_conv_ln, [y], [p["ln_g"], p["ln_b"]], [MXU_DTYPE], tm=TM, name=tag + "_ln")[0]
    qs, kd, lam = rowwise(f_gla_pre, [(gl, 256, 0), (gl, 256, 1), lr], [p["wa"], p["ba"]], [MXU_DTYPE, MXU_DTYPE, F32],
                          tm=TM_GLA, name=tag + "_gla_pre")
    o, states = gla_scan_fwd(qs, kd, gl, lam, tm=TM_GLA, name=tag + "_gla_scan")
    gb = rowwise(f_gla_post, [o, (gl, BRANCH, 2)], [p["gn"]], [MXU_DTYPE], tm=TM, name=tag + "_gla_post")[0]
    branches = [sb, cb, gb]
    bd = [mm(branches[g], w["branch"][g], out_dtype=MXU_DTYPE, name=tag + f"_branch{g}") for g in range(3)]
    merged = rowwise(f_merge, [(gt, D_MODEL, 0), (gt, D_MODEL, 1), (gt, D_MODEL, 2)] + bd, [], [MXU_DTYPE], tm=TM_MERGE,
                     name=tag + "_merge")[0]
    m = mm(merged, w["out"], name=tag + "_out")
    x2 = rowwise(f_post, [x, m], [p["gpost"]], [F32], tm=TM, name=tag + "_post")[0]
    return x2, (x, h, qkv, cv, gl, lr, gt, u, y, qs, kd, lam, o, states, branches, bd, merged, m)


def mixer_bwd(dx2, res, p, w, tag):
    x, h, qkv, cv, gl, lr, gt, u, y, qs, kd, lam, o, states, branches, bd, merged, m = res
    g = {}
    dm, g["gpost"] = rowwise_vjp(f_post, [x, m], [p["gpost"]], [dx2], [1], [0], [MXU_DTYPE], tm=TM, name=tag + "_post_b")
    dmerged = mm(dm, w["out"], tb=True, out_dtype=MXU_DTYPE, name=tag + "_out_dx")
    g["out"] = mm(merged, dm, ta=True, name=tag + "_out_dw")
    gts = [(gt, D_MODEL, 0), (gt, D_MODEL, 1), (gt, D_MODEL, 2)]
    dgate = rowwise_vjp(f_merge, gts + bd, [], [dmerged], [0, 1, 2, 3, 4, 5], [], [MXU_DTYPE] * 6, tm=TM_MERGE,
                        name=tag + "_merge_b")
    dgt = jnp.concatenate(dgate[:3], axis=1)
    dbd = dgate[3:]
    g["branch"] = [mm(branches[k], dbd[k], ta=True, name=tag + f"_branch{k}_dw") for k in range(3)]
    dsb = mm(dbd[0], w["branch"][0], tb=True, out_dtype=MXU_DTYPE, name=tag + "_branch0_dx")
    dq, dk, dv = sb_attn_bwd(qkv, dsb, tq=TQ, name=tag + "_sb_b")
    dqkv = jnp.concatenate([dq, _mx(dk), _mx(dv)], axis=1)
    dcb = mm(dbd[1], w["branch"][1], tb=True, name=tag + "_branch1_dx")
    dy, g["ln_g"], g["ln_b"] = rowwise_vjp(f_conv_ln, [y], [p["ln_g"], p["ln_b"]], [dcb], [0], [0, 1], [F32], tm=TM,
                                           name=tag + "_ln_b")
    du, dwb = conv_bwd(dy, u, p["conv_w"], tm=TM_CONV, name=tag + "_conv_b")
    g["conv_w"], g["conv_b"] = dwb[:CONV_WIDTH], dwb[CONV_WIDTH:CONV_WIDTH + 1]
    dca, dcg = rowwise_vjp(f_glu, [(cv, BRANCH, 0), (cv, BRANCH, 1)], [], [du], [0, 1], [], [MXU_DTYPE, MXU_DTYPE], tm=TM,
                           name=tag + "_glu_b")
    dcv = jnp.concatenate([dca, dcg], axis=1)
    dgb = mm(dbd[2], w["branch"][2], tb=True, name=tag + "_branch2_dx")
    do, dr, g["gn"] = rowwise_vjp(f_gla_post, [o, (gl, BRANCH, 2)], [p["gn"]], [dgb], [0, 1], [0], [F32, MXU_DTYPE], tm=TM,
                                  name=tag + "_gla_post_b")
    dqs, dkd, dgv, dlam = gla_scan_bwd(do, qs, kd, gl, lam, states, tm=TM_GLA, name=tag + "_gla_scan_b")
    dgq, dgk, dlr, g["wa"], g["ba"] = rowwise_vjp(
        f_gla_pre, [(gl, 256, 0), (gl, 256, 1), lr], [p["wa"], p["ba"]], [dqs, dkd, dlam], [0, 1, 2], [0, 1],
        [MXU_DTYPE, MXU_DTYPE, MXU_DTYPE], tm=TM_GLA, name=tag + "_gla_pre_b")
    dgl = jnp.concatenate([dgq, dgk, _mx(dgv), dr], axis=1)
    secs = [("in_sb", dqkv), ("in_conv", dcv), ("in_gla", dgl), ("in_lr", dlr), ("in_gate", dgt)]
    dh = None
    for k, d in secs:
        dh = mm(d, w[k], dh, tb=True, name=tag + "_" + k + "_dx")
        g[k] = mm(h, d, ta=True, tn=1536, name=tag + "_" + k + "_dw")
    dx, g["gpre"] = rowwise_vjp(f_rms, [x], [p["gpre"]], [dh], [0], [0], [F32], [dx2], tm=TM, name=tag + "_pre_b")
    return dx, g


def kernel(x, norm_pre, norm_post, ffn1_w_gate, ffn1_w_up, ffn1_w_down, ffn2_w_gate, ffn2_w_up, ffn2_w_down, w_in, conv_w, conv_b, conv_ln_g, conv_ln_b, gla_w_alpha, gla_b_alpha, gla_norm_g, w_branch, w_out, loss_target, m_norm_pre, m_norm_post, m_ffn1_w_gate, m_ffn1_w_up, m_ffn1_w_down, m_ffn2_w_gate, m_ffn2_w_up, m_ffn2_w_down, m_w_in, m_conv_w, m_conv_b, m_conv_ln_g, m_conv_ln_b, m_gla_w_alpha, m_gla_b_alpha, m_gla_norm_g, m_w_branch, m_w_out, v_norm_pre, v_norm_post, v_ffn1_w_gate, v_ffn1_w_up, v_ffn1_w_down, v_ffn2_w_gate, v_ffn2_w_up, v_ffn2_w_down, v_w_in, v_conv_w, v_conv_b, v_conv_ln_g, v_conv_ln_b, v_gla_w_alpha, v_gla_b_alpha, v_gla_norm_g, v_w_branch, v_w_out):
    shards = dict(norm_pre=norm_pre, norm_post=norm_post, ffn1_w_gate=ffn1_w_gate, ffn1_w_up=ffn1_w_up,
                  ffn1_w_down=ffn1_w_down, ffn2_w_gate=ffn2_w_gate, ffn2_w_up=ffn2_w_up, ffn2_w_down=ffn2_w_down, w_in=w_in,
                  conv_w=conv_w, conv_b=conv_b, conv_ln_g=conv_ln_g, conv_ln_b=conv_ln_b, gla_w_alpha=gla_w_alpha,
                  gla_b_alpha=gla_b_alpha, gla_norm_g=gla_norm_g, w_branch=w_branch, w_out=w_out)
    mom_m = dict(zip(WEIGHTS, (m_norm_pre, m_norm_post, m_ffn1_w_gate, m_ffn1_w_up, m_ffn1_w_down, m_ffn2_w_gate,
                               m_ffn2_w_up, m_ffn2_w_down, m_w_in, m_conv_w, m_conv_b, m_conv_ln_g, m_conv_ln_b,
                               m_gla_w_alpha, m_gla_b_alpha, m_gla_norm_g, m_w_branch, m_w_out)))
    mom_v = dict(zip(WEIGHTS, (v_norm_pre, v_norm_post, v_ffn1_w_gate, v_ffn1_w_up, v_ffn1_w_down, v_ffn2_w_gate,
                               v_ffn2_w_up, v_ffn2_w_down, v_w_in, v_conv_w, v_conv_b, v_conv_ln_g, v_conv_ln_b,
                               v_gla_w_alpha, v_gla_b_alpha, v_gla_norm_g, v_w_branch, v_w_out)))
    depth = norm_pre.shape[0]
    members = [shards[k] if k.endswith("down") else jnp.swapaxes(shards[k], 1, 2) for k in FFN]
    ffn_rows = jnp.concatenate([_mx(t).reshape(depth * SLAB, D_MODEL) for t in members], axis=0)
    w6 = all_gather_chips(ffn_rows, name="gather_ffn").reshape(4, len(FFN), depth, SLAB, D_MODEL)
    full = gather_weights(shards, BIG, MXU_DTYPE, 256, "gather_big")
    full.update(gather_weights(shards, SMALL, F32, 16, "gather_small"))

    def layer_params(l):
        ffn = [dict(gpre=full["norm_pre"][l, k:k + 1], gpost=full["norm_post"][l, k:k + 1]) for k in (0, 2)]
        mix = dict(gpre=full["norm_pre"][l, 1:2], gpost=full["norm_post"][l, 1:2],
                   conv_w=jnp.pad(full["conv_w"][l], ((0, CONV_PAD - CONV_WIDTH), (0, 0))), conv_b=conv_b[l:l + 1],
                   ln_g=conv_ln_g[l:l + 1], ln_b=conv_ln_b[l:l + 1],
                   wa=jnp.pad(full["gla_w_alpha"][l], ((0, GLA_RANK_PAD - GLA_RANK), (0, 0))), ba=gla_b_alpha[l:l + 1],
                   gn=gla_norm_g[l:l + 1])
        return ffn, mix

    xs = x[0]
    saved = []
    for l in range(depth):
        w = layer_weights(full, l)
        ffn, mix = layer_params(l)
        xs, r1 = ffn_fwd(xs, ffn[0]["gpre"], ffn[0]["gpost"], w6, 0, l, f"l{l}_ffn1")
        xs, r2 = mixer_fwd(xs, mix, w, f"l{l}_mix")
        xs, r3 = ffn_fwd(xs, ffn[1]["gpre"], ffn[1]["gpost"], w6, 1, l, f"l{l}_ffn2")
        saved.append((w, ffn, mix, r1, r2, r3))
    dx, sq = loss_head(xs, loss_target[0], tm=TM, name="loss_head")
    loss = lax.psum(0.5 * jnp.sum(sq) / D_MODEL, ("x", "y", "c"))

    per_layer = []
    g6 = lax.empty(w6.shape, F32)
    for l in reversed(range(depth)):
        w, ffn, mix, r1, r2, r3 = saved[l]
        g = {}
        dx, gpre2, gpost2, g6 = ffn_bwd(dx, r3, ffn[1]["gpre"], ffn[1]["gpost"], w6, g6, 1, l, f"l{l}_ffn2")
        dx, gm = mixer_bwd(dx, r2, mix, w, f"l{l}_mix")
        dx, gpre0, gpost0, g6 = ffn_bwd(dx, r1, ffn[0]["gpre"], ffn[0]["gpost"], w6, g6, 0, l, f"l{l}_ffn1")
        g["norm_pre"] = jnp.concatenate([gpre0, gm["gpre"], gpre2], axis=0)
        g["norm_post"] = jnp.concatenate([gpost0, gm["gpost"], gpost2], axis=0)
        g["w_in"] = jnp.concatenate([gm["in_sb"], gm["in_conv"], gm["in_gla"], gm["in_lr"][:, :GLA_RANK], gm["in_gate"]],
                                    axis=1)
        g["conv_w"], g["conv_b"] = gm["conv_w"], gm["conv_b"][0]
        g["conv_ln_g"], g["conv_ln_b"] = gm["ln_g"][0], gm["ln_b"][0]
        g["gla_w_alpha"], g["gla_b_alpha"], g["gla_norm_g"] = gm["wa"][:GLA_RANK], gm["ba"][0], gm["gn"][0]
        g["w_branch"] = jnp.stack(gm["branch"])
        g["w_out"] = gm["out"]
        per_layer.append(g)
    per_layer.reverse()
    grads = {k: jnp.stack([g[k] for g in per_layer]) for k in BIG + SMALL + REPLICATED}

    grad_w = scatter_grads(grads, shards)
    ffn_sum = reduce_scatter(g6.reshape(4, len(FFN) * depth * SLAB, D_MODEL), "rs_ffn")
    ffn_sum = ffn_sum.reshape(len(FFN), depth, SLAB, D_MODEL)
    for t, k in enumerate(FFN):
        grad_w[k] = ffn_sum[t] if k.endswith("down") else jnp.swapaxes(ffn_sum[t], 1, 2)
    delta, new_m, new_v = {}, {}, {}
    for k in WEIGHTS:
        delta[k], new_m[k], new_v[k] = adamw(shards[k], grad_w[k], mom_m[k], mom_v[k], name="adamw_" + k)
    return (loss, dx[None], *[grad_w[k] for k in WEIGHTS], *[delta[k] for k in WEIGHTS], *[new_m[k] for k in WEIGHTS],
            *[new_v[k] for k in WEIGHTS])
```

```python
import functools

import jax
import jax.numpy as jnp
from jax import lax
from jax.experimental import pallas as pl
from jax.experimental.pallas import tpu as pltpu

F32 = jnp.float32
MXU_DTYPE = jnp.bfloat16
GRAD_WIRE_DTYPE = jnp.bfloat16
HIGHEST = lax.Precision.HIGHEST
MESH = pl.DeviceIdType.MESH

NORM_EPS = 1e-6
D_MODEL = 1024
D_FF = 2816
BRANCH = 512
CHUNK = 64
CONV_WIDTH = 31
CONV_PAD = 32
CONV_SUB = 256
GLA_RANK = 16
GLA_RANK_PAD = 128
GLA_TAU = 16.0
SB_SCALE = 0.125
SB_CUTOFF = 60.0
GLA_SCALE = 0.125
PACK_COLS = 1024
VMEM_LIMIT = 56 * 1024 * 1024

ADAM_LR, ADAM_B1, ADAM_B2, ADAM_EPS, ADAM_WD, ADAM_STEP = 0.001, 0.9, 0.999, 1e-08, 0.01, 10

NT = (((1,), (1,)), ((), ()))
TN = (((0,), (0,)), ((), ()))
NN = (((1,), (0,)), ((), ()))


def _params(sem=None, vmem=None):
    return pltpu.CompilerParams(dimension_semantics=sem, vmem_limit_bytes=vmem)


def _mx(v):
    return v.astype(MXU_DTYPE)


def _mx_round(v):
    return v.astype(MXU_DTYPE).astype(F32)


def _fit(dim, want):
    if dim <= want:
        return dim
    for d in range(want - want % 128, 0, -128):
        if dim % d == 0:
            return d
    raise ValueError((dim, want))


def mm(a, b, c=None, *, ta=False, tb=False, out_dtype=F32, tm=1024, tn=1024, tk=1024, name):
    K, M = a.shape if ta else a.shape[::-1]
    N = b.shape[0] if tb else b.shape[1]
    assert (b.shape[1] if tb else b.shape[0]) == K, (a.shape, b.shape, ta, tb)
    tm, tn, tk = _fit(M, tm), _fit(N, tn), _fit(K, tk)
    nk = K // tk
    dn = (((0 if ta else 1,), (1 if tb else 0,)), ((), ()))

    def body(*refs):
        if c is None:
            a_ref, b_ref, o_ref, acc = refs
            c_ref = None
        else:
            a_ref, b_ref, c_ref, o_ref, acc = refs
        k = pl.program_id(2)
        p = lax.dot_general(_mx(a_ref[...]), _mx(b_ref[...]), dn, preferred_element_type=F32)

        @pl.when(k == 0)
        def _():
            acc[...] = p

        @pl.when(k > 0)
        def _():
            acc[...] += p

        @pl.when(k == nk - 1)
        def _():
            r = acc[...]
            if c_ref is not None:
                r = r + c_ref[...].astype(F32)
            o_ref[...] = r.astype(o_ref.dtype)

    a_spec = pl.BlockSpec((tk, tm), lambda i, j, k: (k, i)) if ta else pl.BlockSpec((tm, tk), lambda i, j, k: (i, k))
    b_spec = pl.BlockSpec((tn, tk), lambda i, j, k: (j, k)) if tb else pl.BlockSpec((tk, tn), lambda i, j, k: (k, j))
    o_spec = pl.BlockSpec((tm, tn), lambda i, j, k: (i, j))
    ins, in_specs = [a, b], [a_spec, b_spec]
    if c is not None:
        ins.append(c)
        in_specs.append(o_spec)
    return pl.pallas_call(
        body, name=name, grid=(M // tm, N // tn, nk), in_specs=in_specs, out_specs=o_spec,
        out_shape=jax.ShapeDtypeStruct((M, N), out_dtype), scratch_shapes=[pltpu.VMEM((tm, tn), F32)],
        compiler_params=_params(("parallel", "parallel", "arbitrary"), VMEM_LIMIT),
    )(*ins)


def mm_sum_nt(pairs, *, tm=1024, tk=512, name):
    M, N = pairs[0][0].shape[0], pairs[0][1].shape[0]
    tm = _fit(M, tm)
    tks = [_fit(a.shape[1], tk) for a, _ in pairs]
    counts = [a.shape[1] // t for (a, _), t in zip(pairs, tks)]
    starts = [sum(counts[:i]) for i in range(len(pairs))]
    total = sum(counts)

    def body(*refs):
        o_ref, acc = refs[-2:]
        k = pl.program_id(1)

        @pl.when(k == 0)
        def _():
            acc[...] = jnp.zeros_like(acc)

        for i in range(len(pairs)):
            @pl.when((k >= starts[i]) & (k < starts[i] + counts[i]))
            def _(i=i):
                acc[...] += lax.dot_general(_mx(refs[2 * i][...]), _mx(refs[2 * i + 1][...]), NT,
                                            preferred_element_type=F32)

        @pl.when(k == total - 1)
        def _():
            o_ref[...] = acc[...]

    def turn(i):
        return lambda k: jnp.clip(k - starts[i], 0, counts[i] - 1)

    in_specs, ins = [], []
    for i, (a, b) in enumerate(pairs):
        in_specs.append(pl.BlockSpec((tm, tks[i]), lambda m, k, f=turn(i): (m, f(k))))
        in_specs.append(pl.BlockSpec((N, tks[i]), lambda m, k, f=turn(i): (0, f(k))))
        ins += [a, b]
    return pl.pallas_call(
        body, name=name, grid=(M // tm, total), in_specs=in_specs, out_specs=pl.BlockSpec((tm, N), lambda m, k: (m, 0)),
        out_shape=jax.ShapeDtypeStruct((M, N), F32), scratch_shapes=[pltpu.VMEM((tm, N), F32)],
        compiler_params=_params(("parallel", "arbitrary"), VMEM_LIMIT),
    )(*ins)


SLAB = D_FF // 4


def _swiglu_of(ab_ref):
    return _mx(f_swiglu(ab_ref[0].astype(F32), ab_ref[1].astype(F32))[0])


def _gated_spec(rows, index):
    return pl.BlockSpec((2, None, rows, SLAB), index)


def slab_nt(a, w6, t0, n, l, *, out_dtype, tm=1024, name):
    S, K = a.shape
    tm = _fit(S, tm)

    def body(a_ref, w_ref, o_ref):
        o_ref[...] = lax.dot_general(_mx(a_ref[...]), _mx(w_ref[...]), NT, preferred_element_type=F32).astype(o_ref.dtype)

    return pl.pallas_call(
        body, name=name, grid=(S // tm, n),
        in_specs=[pl.BlockSpec((tm, K), lambda i, s: (i, 0)),
                  pl.BlockSpec((None, None, None, SLAB, K), lambda i, s: (s % 4, t0 + s // 4, l, 0, 0))],
        out_specs=pl.BlockSpec((None, tm, SLAB), lambda i, s: (s, i, 0)),
        out_shape=jax.ShapeDtypeStruct((n, S, SLAB), out_dtype),
        compiler_params=_params(("parallel", "parallel"), VMEM_LIMIT),
    )(a, w6)


def slab_nt_swiglu_bwd(d, w6, t0, l, ab, *, tm=1024, name):
    S, K = d.shape
    tm = _fit(S, tm)

    def body(d_ref, w_ref, ab_ref, o_ref):
        dz = lax.dot_general(_mx(d_ref[...]), _mx(w_ref[...]), NT, preferred_element_type=F32)
        _, pullback = jax.vjp(lambda g, u: f_swiglu(g, u)[0], ab_ref[0].astype(F32), ab_ref[1].astype(F32))
        dg, du = pullback(_mx_round(dz))
        o_ref[0] = dg.astype(o_ref.dtype)
        o_ref[1] = du.astype(o_ref.dtype)

    return pl.pallas_call(
        body, name=name, grid=(S // tm, 4),
        in_specs=[pl.BlockSpec((tm, K), lambda i, s: (i, 0)),
                  pl.BlockSpec((None, None, None, SLAB, K), lambda i, s: (s, t0, l, 0, 0)),
                  _gated_spec(tm, lambda i, s: (0, s, i, 0))],
        out_specs=_gated_spec(tm, lambda i, s: (0, s, i, 0)),
        out_shape=jax.ShapeDtypeStruct(ab.shape, MXU_DTYPE),
        compiler_params=_params(("parallel", "parallel"), VMEM_LIMIT),
    )(d, w6, ab)


def slab_nn(a, w6, t0, l, *, gated=False, tail=None, tm=1024, name):
    n, S, _ = a.shape[-3:]
    K = w6.shape[-1]
    tm = _fit(S, tm)
    kind = tail[0] if tail else None
    extra = list(tail[1:]) if tail else []

    def body(a_ref, w_ref, *rest):
        i, s = pl.program_id(0), pl.program_id(1)
        acc = rest[-1]
        lhs = _swiglu_of(a_ref) if gated else _mx(a_ref[...])
        p = jnp.dot(lhs, _mx(w_ref[...]), preferred_element_type=F32)

        @pl.when(s == 0)
        def _():
            acc[...] = p

        @pl.when(s > 0)
        def _():
            acc[...] += p

        @pl.when(s == n - 1)
        def _():
            r = acc[...]
            if kind is None:
                rest[0][...] = r
            elif kind == "half_post":
                x_ref, g_ref, o_ref, x2_ref = rest[:4]
                o_ref[...] = r
                x2_ref[...] = f_half_post(x_ref[...], r, g_ref[...])[0]
            else:
                x_ref, g_ref, d_ref, dx_ref, dg_ref = rest[:5]
                _, pullback = jax.vjp(lambda xv, gv: f_rms(xv, gv)[0], x_ref[...], g_ref[...])
                dxn, dg = pullback(r)
                dx_ref[...] = dxn + d_ref[...]

                @pl.when(i == 0)
                def _():
                    dg_ref[...] = dg

                @pl.when(i > 0)
                def _():
                    dg_ref[...] += dg

    rows = pl.BlockSpec((tm, K), lambda i, s: (i, 0))
    gain = pl.BlockSpec((1, K), lambda i, s: (0, 0))
    tail_in = {None: [], "half_post": [rows, gain], "rms_bwd": [rows, gain, rows]}[kind]
    full, vec = jax.ShapeDtypeStruct((S, K), F32), jax.ShapeDtypeStruct((1, K), F32)
    out_specs, out_shape = {None: (rows, full), "half_post": ([rows, rows], [full, full]),
                            "rms_bwd": ([rows, gain], [full, vec])}[kind]
    return pl.pallas_call(
        body, name=name, grid=(S // tm, n),
        in_specs=[_gated_spec(tm, lambda i, s: (0, s, i, 0)) if gated else
                  pl.BlockSpec((None, tm, SLAB), lambda i, s: (s, i, 0)),
                  pl.BlockSpec((None, None, None, SLAB, K), lambda i, s: (s % 4, t0 + s // 4, l, 0, 0))] + tail_in,
        out_specs=out_specs, out_shape=out_shape, scratch_shapes=[pltpu.VMEM((tm, K), F32)],
        compiler_params=_params(("arbitrary" if kind == "rms_bwd" else "parallel", "arbitrary"), VMEM_LIMIT),
    )(a, w6, *extra)


def slab_tn(a, b, g6, t0, l, *, gated=False, tk=1024, name):
    n, S, _ = a.shape[-3:]
    K = b.shape[1]
    tk = _fit(S, tk)
    nk = S // tk

    def body(a_ref, b_ref, g_ref, o_ref, acc):
        k = pl.program_id(1)
        lhs = _swiglu_of(a_ref) if gated else _mx(a_ref[...])
        p = lax.dot_general(lhs, _mx(b_ref[...]), TN, preferred_element_type=F32)

        @pl.when(k == 0)
        def _():
            acc[...] = p

        @pl.when(k > 0)
        def _():
            acc[...] += p

        @pl.when(k == nk - 1)
        def _():
            o_ref[...] = acc[...]

    return pl.pallas_call(
        body, name=name, grid=(n, nk),
        in_specs=[_gated_spec(tk, lambda s, k: (0, s, k, 0)) if gated else
                  pl.BlockSpec((None, tk, SLAB), lambda s, k: (s, k, 0)),
                  pl.BlockSpec((tk, K), lambda s, k: (k, 0)), pl.BlockSpec(memory_space=pl.ANY)],
        out_specs=pl.BlockSpec((None, None, None, SLAB, K), lambda s, k: (s % 4, t0 + s // 4, l, 0, 0)),
        out_shape=jax.ShapeDtypeStruct(g6.shape, g6.dtype), scratch_shapes=[pltpu.VMEM((SLAB, K), F32)],
        input_output_aliases={2: 0},
        compiler_params=_params(("parallel", "arbitrary"), VMEM_LIMIT),
    )(a, b, g6)


def _row_arg(arg):
    if isinstance(arg, tuple):
        return arg
    return arg, arg.shape[1], 0


def _row_specs(rows, n):
    arrs, specs, avals = [], [], []
    for arg in rows:
        arr, width, cb = _row_arg(arg)
        rb = arr.shape[0] // n
        arrs.append(arr)
        specs.append(pl.BlockSpec((rb, width), lambda i, cb=cb: (i, cb)))
        avals.append(jax.ShapeDtypeStruct((rb, width), F32))
    return arrs, specs, avals


def _par_specs(pars):
    specs = [pl.BlockSpec(p.shape, lambda i, nd=p.ndim: (0,) * nd) for p in pars]
    avals = [jax.ShapeDtypeStruct(p.shape, F32) for p in pars]
    return specs, avals


def rowwise(f, rows, pars, out_dtypes, *, tm, name):
    n = _row_arg(rows[0])[0].shape[0] // tm
    arrs, rspecs, ravals = _row_specs(rows, n)
    pspecs, pavals = _par_specs(pars)
    oavals = jax.eval_shape(f, *ravals, *pavals)
    nin = len(arrs) + len(pars)

    def body(*refs):
        outs = f(*[r[...].astype(F32) for r in refs[:nin]])
        for o_ref, o in zip(refs[nin:], outs):
            o_ref[...] = o.astype(o_ref.dtype)

    return pl.pallas_call(
        body, name=name, grid=(n,), in_specs=rspecs + pspecs,
        out_specs=[pl.BlockSpec(o.shape, lambda i: (i, 0)) for o in oavals],
        out_shape=[jax.ShapeDtypeStruct((n * o.shape[0], o.shape[1]), dt) for o, dt in zip(oavals, out_dtypes)],
        compiler_params=_params(("parallel",), VMEM_LIMIT),
    )(*arrs, *pars)


def rowwise_vjp(f, rows, pars, cots, row_grad, par_grad, d_dtypes, adds=None, *, tm, name):
    n = _row_arg(rows[0])[0].shape[0] // tm
    arrs, rspecs, ravals = _row_specs(rows, n)
    pspecs, pavals = _par_specs(pars)
    carrs, cspecs, _ = _row_specs(cots, n)
    adds = adds or [None] * len(row_grad)
    add_arrs = [a for a in adds if a is not None]
    _, aspecs, _ = _row_specs(add_arrs, n)
    nr, npar, nc, na = len(arrs), len(pars), len(carrs), len(add_arrs)
    diff = list(row_grad) + [nr + j for j in par_grad]
    ngr = len(row_grad)

    def body(*refs):
        ins = [r[...].astype(F32) for r in refs[:nr + npar]]
        cs = tuple(r[...].astype(F32) for r in refs[nr + npar:nr + npar + nc])
        add_refs = list(refs[nr + npar + nc:nr + npar + nc + na])
        outs = refs[nr + npar + nc + na:]

        def g(*d):
            full = list(ins)
            for idx, val in zip(diff, d):
                full[idx] = val
            return f(*full)

        _, pullback = jax.vjp(g, *[ins[idx] for idx in diff])
        ds = pullback(cs)
        for k in range(ngr):
            d = ds[k]
            if adds[k] is not None:
                d = d + add_refs.pop(0)[...].astype(F32)
            outs[k][...] = d.astype(outs[k].dtype)
        i = pl.program_id(0)
        for k in range(ngr, len(diff)):
            @pl.when(i == 0)
            def _(k=k):
                outs[k][...] = ds[k]

            @pl.when(i > 0)
            def _(k=k):
                outs[k][...] += ds[k]

    out_specs = [pl.BlockSpec(ravals[i].shape, lambda i: (i, 0)) for i in row_grad] + [pspecs[j] for j in par_grad]
    out_shape = [jax.ShapeDtypeStruct((arrs[i].shape[0], ravals[i].shape[1]), dt) for i, dt in zip(row_grad, d_dtypes)]
    out_shape += [jax.ShapeDtypeStruct(pars[j].shape, F32) for j in par_grad]
    return pl.pallas_call(
        body, name=name, grid=(n,), in_specs=rspecs + pspecs + cspecs + aspecs, out_specs=out_specs, out_shape=out_shape,
        compiler_params=_params(("arbitrary",), VMEM_LIMIT),
    )(*arrs, *pars, *carrs, *add_arrs)


def _logsig(x):
    return jnp.minimum(x, 0.0) - jnp.log(1.0 + jnp.exp(-jnp.abs(x)))


def _sigmoid(x):
    return 1.0 / (1.0 + jnp.exp(-x))


def _silu(x):
    return x * _sigmoid(x)


def _rms(x, g):
    return x * lax.rsqrt(jnp.mean(x * x, axis=-1, keepdims=True) + NORM_EPS) * g


def f_rms(x, g):
    return (_rms(x, g),)


def f_swiglu(a, b):
    return (_silu(a) * b,)


def f_half_post(x, f, g):
    return (x + 0.5 * _rms(f, g),)


def f_post(x, m, g):
    return (x + _rms(m, g),)


def f_glu(a, g):
    return (a * _sigmoid(g),)


def f_conv_ln(y, lg, lb):
    mu = jnp.mean(y, axis=-1, keepdims=True)
    var = jnp.mean(jnp.square(y - mu), axis=-1, keepdims=True)
    return (_silu((y - mu) * lax.rsqrt(var + NORM_EPS) * lg + lb),)


def f_merge(g0, g1, g2, b0, b1, b2):
    return (_sigmoid(g0) * b0 + _sigmoid(g1) * b1 + _sigmoid(g2) * b2,)


def f_gla_post(o, r, g):
    w = o.shape[1]
    hv = w // 4
    i = lax.broadcasted_iota(jnp.int32, (w, w), 0) // hv
    j = lax.broadcasted_iota(jnp.int32, (w, w), 1) // hv
    avg = jnp.where(i == j, 1.0 / hv, 0.0).astype(F32)
    ms = jnp.dot(o * o, avg, precision=HIGHEST, preferred_element_type=F32)
    return (o * lax.rsqrt(ms + NORM_EPS) * g * _silu(r),)


def f_gla_pre(q, k, lr, wa, ba):
    tm = q.shape[0]
    pre = jnp.dot(_mx_round(lr), _mx_round(wa), precision=HIGHEST, preferred_element_type=F32) + ba
    la = _logsig(pre) / GLA_TAU
    i = lax.broadcasted_iota(jnp.int32, (tm, tm), 0)
    j = lax.broadcasted_iota(jnp.int32, (tm, tm), 1)
    later = jnp.where((i // CHUNK == j // CHUNK) & (j > i), 1.0, 0.0).astype(F32)
    ci = lax.broadcasted_iota(jnp.int32, (tm // CHUNK, tm), 0)
    cj = lax.broadcasted_iota(jnp.int32, (tm // CHUNK, tm), 1) // CHUNK
    chunk_sum = jnp.where(ci == cj, 1.0, 0.0).astype(F32)
    to_end = jnp.dot(later, la, precision=HIGHEST, preferred_element_type=F32)
    lam = jnp.exp(jnp.dot(chunk_sum, la, precision=HIGHEST, preferred_element_type=F32))
    return q * GLA_SCALE, k * jnp.exp(to_end), lam


def _split_dot(x, u):
    hi = _mx(x)
    lo = _mx(x - hi.astype(F32))
    return jnp.dot(hi, u, preferred_element_type=F32) + jnp.dot(lo, u, preferred_element_type=F32)


def _tri(tq, tk):
    row = lax.broadcasted_iota(jnp.int32, (tq, tk), 0)
    col = lax.broadcasted_iota(jnp.int32, (tq, tk), 1)
    return row, col


def _sb_tile(qh, kh, valid, suf, run):
    z = lax.dot_general(qh, kh, NT, preferred_element_type=F32) * SB_SCALE
    lp = jnp.minimum(z, 0.0) - jnp.log(1.0 + jnp.exp(-jnp.abs(z)))
    lk = jnp.where(valid, lp - z, 0.0)
    inc = _split_dot(lk, suf)
    a = jnp.where(valid, jnp.exp(lp + (inc - lk + run)), 0.0)
    return lp, a, run + inc[:, 0:1]


def _sticks_left(run0, run1):
    return (jnp.maximum(jnp.max(run0), jnp.max(run1)) > -SB_CUTOFF).astype(jnp.int32)


def sb_attn_fwd(qkv, *, tq, name):
    S = qkv.shape[0]
    tk = tq
    pairs = BRANCH // 128

    def body(q_ref, k_ref, v_ref, o_ref):
        i = pl.program_id(1)
        row, col = _tri(tq, tk)
        suf = _mx(row >= col)
        q = q_ref[...]

        def step(state):
            j, _, carry = state
            ks = pl.multiple_of((i - j) * tk, tk)
            kb = k_ref[pl.ds(ks, tk), :]
            vb = v_ref[pl.ds(ks, tk), :]
            valid = (col - row) < j * tq
            new = []
            for h in range(2):
                acc, run = carry[h]
                sl = slice(64 * h, 64 * h + 64)
                _, a, run = _sb_tile(q[:, sl], kb[:, sl], valid, suf, run)
                acc = acc + jnp.dot(_mx(a), vb[:, sl], preferred_element_type=F32)
                new.append((acc, run))
            return j + 1, _sticks_left(new[0][1], new[1][1]), tuple(new)

        zero = (jnp.zeros((tq, 64), F32), jnp.zeros((tq, 1), F32))
        _, _, res = lax.while_loop(lambda s: (s[0] <= i) & (s[1] > 0), step, (jnp.int32(0), jnp.int32(1), (zero, zero)))
        o_ref[...] = jnp.concatenate([res[0][0], res[1][0]], axis=1).astype(o_ref.dtype)

    return pl.pallas_call(
        body, name=name, grid=(pairs, S // tq),
        in_specs=[pl.BlockSpec((tq, 128), lambda p, i: (i, p)),
                  pl.BlockSpec((S, 128), lambda p, i: (0, pairs + p)),
                  pl.BlockSpec((S, 128), lambda p, i: (0, 2 * pairs + p))],
        out_specs=pl.BlockSpec((tq, 128), lambda p, i: (i, p)),
        out_shape=jax.ShapeDtypeStruct((S, BRANCH), MXU_DTYPE),
        compiler_params=_params(("parallel", "parallel"), VMEM_LIMIT),
    )(qkv, qkv, qkv)


def sb_attn_bwd(qkv, do, *, tq, name):
    S = qkv.shape[0]
    tk = tq
    nq = S // tq
    pairs = BRANCH // 128

    def body(q_ref, k_ref, v_ref, do_ref, dq_ref, dk_ref, dv_ref, g_sc, b_sc):
        i = pl.program_id(1)

        @pl.when(i == 0)
        def _():
            dk_ref[...] = jnp.zeros_like(dk_ref)
            dv_ref[...] = jnp.zeros_like(dv_ref)

        row, col = _tri(tq, tk)
        suf = _mx(row >= col)
        pre = _mx(row <= col)
        q = q_ref[...]
        dout = do_ref[...]

        def sweep1(state):
            j, _, carry = state
            kblk = i - j
            ks = pl.multiple_of(kblk * tk, tk)
            kb = k_ref[pl.ds(ks, tk), :]
            vb = v_ref[pl.ds(ks, tk), :]
            valid = (col - row) < j * tq
            runs, dvs = [], []
            for h in range(2):
                sl = slice(64 * h, 64 * h + 64)
                lp, a, run = _sb_tile(q[:, sl], kb[:, sl], valid, suf, carry[h])
                da = lax.dot_general(dout[:, sl], vb[:, sl], NT, preferred_element_type=F32)
                g_sc[h, kblk] = (a * da).astype(g_sc.dtype)
                b_sc[h, kblk] = jnp.where(valid, jnp.exp(lp), 0.0).astype(b_sc.dtype)
                dvs.append(lax.dot_general(_mx(a), dout[:, sl], TN, preferred_element_type=F32))
                runs.append(run)
            dv_ref[pl.ds(ks, tk), :] += jnp.concatenate(dvs, axis=1)
            return j + 1, _sticks_left(runs[0], runs[1]), tuple(runs)

        start = (jnp.int32(0), jnp.int32(1), (jnp.zeros((tq, 1), F32), jnp.zeros((tq, 1), F32)))
        tiles, _, _ = lax.while_loop(lambda s: (s[0] <= i) & (s[1] > 0), sweep1, start)

        def sweep2(kblk, carry):
            ks = pl.multiple_of(kblk * tk, tk)
            kb = k_ref[pl.ds(ks, tk), :]
            new, dks = [], []
            for h in range(2):
                dq, run = carry[h]
                sl = slice(64 * h, 64 * h + 64)
                g = g_sc[h, kblk]
                beta = b_sc[h, kblk].astype(F32)
                inc = jnp.dot(g, pre, preferred_element_type=F32)
                g = g.astype(F32)
                dz = _mx((g - beta * (inc + run)) * SB_SCALE)
                dq = dq + jnp.dot(dz, kb[:, sl], preferred_element_type=F32)
                dks.append(lax.dot_general(dz, q[:, sl], TN, preferred_element_type=F32))
                new.append((dq, run + inc[:, tk - 1:tk]))
            dk_ref[pl.ds(ks, tk), :] += jnp.concatenate(dks, axis=1)
            return tuple(new)

        zero = (jnp.zeros((tq, 64), F32), jnp.zeros((tq, 1), F32))
        res = lax.fori_loop(i + 1 - tiles, i + 1, sweep2, (zero, zero))
        dq_ref[...] = jnp.concatenate([res[0][0], res[1][0]], axis=1).astype(dq_ref.dtype)

    return pl.pallas_call(
        body, name=name, grid=(pairs, nq),
        in_specs=[pl.BlockSpec((tq, 128), lambda p, i: (i, p)),
                  pl.BlockSpec((S, 128), lambda p, i: (0, pairs + p)),
                  pl.BlockSpec((S, 128), lambda p, i: (0, 2 * pairs + p)),
                  pl.BlockSpec((tq, 128), lambda p, i: (i, p))],
        out_specs=[pl.BlockSpec((tq, 128), lambda p, i: (i, p)),
                   pl.BlockSpec((S, 128), lambda p, i: (0, p)),
                   pl.BlockSpec((S, 128), lambda p, i: (0, p))],
        out_shape=[jax.ShapeDtypeStruct((S, BRANCH), MXU_DTYPE), jax.ShapeDtypeStruct((S, BRANCH), F32),
                   jax.ShapeDtypeStruct((S, BRANCH), F32)],
        scratch_shapes=[pltpu.VMEM((2, nq, tq, tk), MXU_DTYPE), pltpu.VMEM((2, nq, tq, tk), MXU_DTYPE)],
        compiler_params=_params(("parallel", "arbitrary"), VMEM_LIMIT),
    )(qkv, qkv, qkv, do)


def conv_fwd(u, w, b, *, tm, name):
    S, C = u.shape
    hb = tm // CONV_PAD

    def body(u_ref, halo_ref, w_ref, b_ref, y_ref, buf):
        i = pl.program_id(0)
        buf[pl.ds(CONV_PAD, tm), :] = u_ref[...]
        buf[pl.ds(0, CONV_PAD), :] = jnp.where(i > 0, halo_ref[...], 0.0)
        ts = min(tm, CONV_SUB)
        for r in range(0, tm, ts):
            acc = jnp.broadcast_to(b_ref[...], (ts, 128))
            for j in range(CONV_WIDTH):
                acc = acc + buf[pl.ds(r + CONV_PAD - (CONV_WIDTH - 1) + j, ts), :] * w_ref[pl.ds(j, 1), :]
            y_ref[pl.ds(r, ts), :] = acc

    return pl.pallas_call(
        body, name=name, grid=(S // tm, C // 128),
        in_specs=[pl.BlockSpec((tm, 128), lambda i, c: (i, c)),
                  pl.BlockSpec((CONV_PAD, 128), lambda i, c: (jnp.maximum(i * hb - 1, 0), c)),
                  pl.BlockSpec((CONV_PAD, 128), lambda i, c: (0, c)),
                  pl.BlockSpec((1, 128), lambda i, c: (0, c))],
        out_specs=pl.BlockSpec((tm, 128), lambda i, c: (i, c)),
        out_shape=jax.ShapeDtypeStruct((S, C), F32),
        scratch_shapes=[pltpu.VMEM((tm + CONV_PAD, 128), F32)],
        compiler_params=_params(("parallel", "parallel")),
    )(u, u, w, b)


def conv_bwd(dy, u, w, *, tm, name):
    S, C = u.shape
    hb = tm // CONV_PAD
    n = S // tm

    def body(dy_ref, dyn_ref, u_ref, up_ref, w_ref, du_ref, dw_ref, bufy, bufu):
        i = pl.program_id(1)
        dyv = dy_ref[...]
        bufy[pl.ds(0, tm), :] = dyv
        bufy[pl.ds(tm, CONV_PAD), :] = jnp.where(i < n - 1, dyn_ref[...], 0.0)
        bufu[pl.ds(CONV_PAD, tm), :] = u_ref[...]
        bufu[pl.ds(0, CONV_PAD), :] = jnp.where(i > 0, up_ref[...], 0.0)

        @pl.when(i == 0)
        def _():
            dw_ref[...] = jnp.zeros_like(dw_ref)

        ts = min(tm, CONV_SUB)
        for r in range(0, tm, ts):
            dys = dy_ref[pl.ds(r, ts), :]
            acc = jnp.zeros((ts, 128), F32)
            for j in range(CONV_WIDTH):
                acc = acc + bufy[pl.ds(r + CONV_WIDTH - 1 - j, ts), :] * w_ref[pl.ds(j, 1), :]
                shifted = bufu[pl.ds(r + CONV_PAD - (CONV_WIDTH - 1) + j, ts), :]
                dw_ref[pl.ds(j, 1), :] += jnp.sum(dys * shifted, axis=0, keepdims=True)
            du_ref[pl.ds(r, ts), :] = acc
            dw_ref[pl.ds(CONV_WIDTH, 1), :] += jnp.sum(dys, axis=0, keepdims=True)

    return pl.pallas_call(
        body, name=name, grid=(C // 128, n),
        in_specs=[pl.BlockSpec((tm, 128), lambda c, i: (i, c)),
                  pl.BlockSpec((CONV_PAD, 128), lambda c, i: (jnp.minimum((i + 1) * hb, n * hb - 1), c)),
                  pl.BlockSpec((tm, 128), lambda c, i: (i, c)),
                  pl.BlockSpec((CONV_PAD, 128), lambda c, i: (jnp.maximum(i * hb - 1, 0), c)),
                  pl.BlockSpec((CONV_PAD, 128), lambda c, i: (0, c))],
        out_specs=[pl.BlockSpec((tm, 128), lambda c, i: (i, c)), pl.BlockSpec((CONV_PAD, 128), lambda c, i: (0, c))],
        out_shape=[jax.ShapeDtypeStruct((S, C), F32), jax.ShapeDtypeStruct((CONV_PAD, C), F32)],
        scratch_shapes=[pltpu.VMEM((tm + CONV_PAD, 128), F32), pltpu.VMEM((tm + CONV_PAD, 128), F32)],
        compiler_params=_params(("parallel", "arbitrary")),
    )(dy, dy, u, u, w)


GLA_HEADS, GLA_DK, GLA_DV = 4, 64, 128


def gla_scan_fwd(qs, kd, gl, lam, *, tm, name):
    S = qs.shape[0]
    nc = tm // CHUNK

    def body(qs_ref, kd_ref, v_ref, lam_ref, o_ref, st_ref, state):
        @pl.when(pl.program_id(0) == 0)
        def _():
            state[...] = jnp.zeros_like(state)

        for c in range(nc):
            rows = pl.ds(c * CHUNK, CHUNK)
            q, k, v = _mx(qs_ref[rows, :]), _mx(kd_ref[rows, :]), _mx(v_ref[rows, :])
            upd = [lax.dot_general(v[:, h * GLA_DV:(h + 1) * GLA_DV], k[:, h * GLA_DK:(h + 1) * GLA_DK], TN,
                                   preferred_element_type=F32) for h in range(GLA_HEADS)]
            st = state[...] * lam_ref[pl.ds(c, 1), :] + jnp.concatenate(upd, axis=1)
            state[...] = st
            st_ref[c] = st
            stm = _mx(st)
            o = [lax.dot_general(q[:, h * GLA_DK:(h + 1) * GLA_DK], stm[:, h * GLA_DK:(h + 1) * GLA_DK], NT,
                                 preferred_element_type=F32) for h in range(GLA_HEADS)]
            o_ref[rows, :] = jnp.concatenate(o, axis=1)

    dk_all = GLA_HEADS * GLA_DK
    dv_all = GLA_HEADS * GLA_DV
    return pl.pallas_call(
        body, name=name, grid=(S // tm,),
        in_specs=[pl.BlockSpec((tm, dk_all), lambda i: (i, 0)), pl.BlockSpec((tm, dk_all), lambda i: (i, 0)),
                  pl.BlockSpec((tm, dv_all), lambda i: (i, 1)), pl.BlockSpec((nc, dk_all), lambda i: (i, 0))],
        out_specs=[pl.BlockSpec((tm, dv_all), lambda i: (i, 0)), pl.BlockSpec((nc, GLA_DV, dk_all), lambda i: (i, 0, 0))],
        out_shape=[jax.ShapeDtypeStruct((S, dv_all), F32), jax.ShapeDtypeStruct((S // CHUNK, GLA_DV, dk_all), F32)],
        scratch_shapes=[pltpu.VMEM((GLA_DV, dk_all), F32)],
        compiler_params=_params(("arbitrary",)),
    )(qs, kd, gl, lam)


def gla_scan_bwd(do, qs, kd, gl, lam, states, *, tm, name):
    S = qs.shape[0]
    nc = tm // CHUNK
    n = S // tm
    dk_all = GLA_HEADS * GLA_DK
    dv_all = GLA_HEADS * GLA_DV

    def body(do_ref, qs_ref, kd_ref, v_ref, lam_ref, st_ref, prev_ref, dqs_ref, dkd_ref, dv_ref, dlam_ref, carry):
        i = pl.program_id(0)

        @pl.when(i == 0)
        def _():
            carry[...] = jnp.zeros_like(carry)

        for c in reversed(range(nc)):
            rows = pl.ds(c * CHUNK, CHUNK)
            q, k, v, d = _mx(qs_ref[rows, :]), _mx(kd_ref[rows, :]), _mx(v_ref[rows, :]), _mx(do_ref[rows, :])
            st = _mx(st_ref[c])
            before = st_ref[c - 1] if c > 0 else jnp.where(i < n - 1, prev_ref[0], 0.0)
            outer = [lax.dot_general(d[:, h * GLA_DV:(h + 1) * GLA_DV], q[:, h * GLA_DK:(h + 1) * GLA_DK], TN,
                                     preferred_element_type=F32) for h in range(GLA_HEADS)]
            dst = carry[...] + jnp.concatenate(outer, axis=1)
            dstm = _mx(dst)
            dq, dkk, dvv = [], [], []
            for h in range(GLA_HEADS):
                ksl = slice(h * GLA_DK, (h + 1) * GLA_DK)
                vsl = slice(h * GLA_DV, (h + 1) * GLA_DV)
                dq.append(jnp.dot(d[:, vsl], st[:, ksl], preferred_element_type=F32))
                dkk.append(jnp.dot(v[:, vsl], dstm[:, ksl], preferred_element_type=F32))
                dvv.append(lax.dot_general(k[:, ksl], dstm[:, ksl], NT, preferred_element_type=F32))
            dqs_ref[rows, :] = jnp.concatenate(dq, axis=1)
            dkd_ref[rows, :] = jnp.concatenate(dkk, axis=1)
            dv_ref[rows, :] = jnp.concatenate(dvv, axis=1)
            dlam_ref[pl.ds(c, 1), :] = jnp.sum(dst * before, axis=0, keepdims=True)
            carry[...] = dst * lam_ref[pl.ds(c, 1), :]

    rev = lambda i: n - 1 - i
    return pl.pallas_call(
        body, name=name, grid=(n,),
        in_specs=[pl.BlockSpec((tm, dv_all), lambda i: (rev(i), 0)), pl.BlockSpec((tm, dk_all), lambda i: (rev(i), 0)),
                  pl.BlockSpec((tm, dk_all), lambda i: (rev(i), 0)), pl.BlockSpec((tm, dv_all), lambda i: (rev(i), 1)),
                  pl.BlockSpec((nc, dk_all), lambda i: (rev(i), 0)),
                  pl.BlockSpec((nc, GLA_DV, dk_all), lambda i: (rev(i), 0, 0)),
                  pl.BlockSpec((1, GLA_DV, dk_all), lambda i: (jnp.maximum(rev(i) * nc - 1, 0), 0, 0))],
        out_specs=[pl.BlockSpec((tm, dk_all), lambda i: (rev(i), 0)), pl.BlockSpec((tm, dk_all), lambda i: (rev(i), 0)),
                   pl.BlockSpec((tm, dv_all), lambda i: (rev(i), 0)), pl.BlockSpec((nc, dk_all), lambda i: (rev(i), 0))],
        out_shape=[jax.ShapeDtypeStruct((S, dk_all), F32), jax.ShapeDtypeStruct((S, dk_all), F32),
                   jax.ShapeDtypeStruct((S, dv_all), F32), jax.ShapeDtypeStruct((S // CHUNK, dk_all), F32)],
        scratch_shapes=[pltpu.VMEM((GLA_DV, dk_all), F32)],
        compiler_params=_params(("arbitrary",)),
    )(do, qs, kd, gl, lam, states, states)


def loss_head(y, target, *, tm, name):
    S, D = y.shape

    def body(y_ref, t_ref, dy_ref, sq_ref):
        err = y_ref[...] - t_ref[...]
        dy_ref[...] = err * (1.0 / D)
        part = jnp.sum(err * err, axis=0, keepdims=True)

        @pl.when(pl.program_id(0) == 0)
        def _():
            sq_ref[...] = part

        @pl.when(pl.program_id(0) > 0)
        def _():
            sq_ref[...] += part

    return pl.pallas_call(
        body, name=name, grid=(S // tm,),
        in_specs=[pl.BlockSpec((tm, D), lambda i: (i, 0)), pl.BlockSpec((tm, D), lambda i: (i, 0))],
        out_specs=[pl.BlockSpec((tm, D), lambda i: (i, 0)), pl.BlockSpec((1, D), lambda i: (0, 0))],
        out_shape=[jax.ShapeDtypeStruct((S, D), F32), jax.ShapeDtypeStruct((1, D), F32)],
        compiler_params=_params(("arbitrary",)),
    )(y, target)


def f_adamw(w, g, m, v):
    m = ADAM_B1 * m + (1.0 - ADAM_B1) * g
    v = ADAM_B2 * v + (1.0 - ADAM_B2) * jnp.square(g)
    m_hat = m / (1.0 - ADAM_B1 ** ADAM_STEP)
    v_hat = v / (1.0 - ADAM_B2 ** ADAM_STEP)
    return -ADAM_LR * (m_hat / (jnp.sqrt(v_hat) + ADAM_EPS) + ADAM_WD * w), m, v


def adamw(w, g, m, v, *, name):
    shape = w.shape
    cols = shape[-1]
    rows = w.size // cols
    tm = rows
    while tm % 16 == 0 and tm * cols * 4 > (1 << 20):
        tm //= 2
    flat = [t.reshape(rows, cols) for t in (w, g, m, v)]
    outs = rowwise(f_adamw, flat, [], [F32, F32, F32], tm=tm, name=name)
    return [o.reshape(shape) for o in outs]


def _place():
    x, y, c = lax.axis_index("x"), lax.axis_index("y"), lax.axis_index("c")
    return x, y, c, [(1 - x, y), (x, 1 - y), (1 - x, 1 - y)]


def _my_chip():
    return 2 * lax.axis_index("x") + lax.axis_index("y")


def _any():
    return pl.BlockSpec(memory_space=pl.ANY)


DMA_CHUNKS = 8


def _chunks(rows):
    n = DMA_CHUNKS
    while n > 1 and rows % (n * 16):
        n //= 2
    return [(k * (rows // n), rows // n) for k in range(n)]


def _start_chunked(make, rows):
    for off, size in _chunks(rows):
        make(off, size).start()


def all_gather_chips(v, *, name):
    R, C = v.shape
    H = R // 2

    def body(v_ref, o_ref, send, recv, fsend, frecv):
        x, y, c, chips = _place()
        me = 2 * x + y
        ids = [2 * px + py for px, py in chips]
        mine = pl.ds(pl.multiple_of(c * H, 8), H)
        other = pl.ds(pl.multiple_of((1 - c) * H, 8), H)

        def cross(j, src_chip, rows):
            return pltpu.make_async_remote_copy(
                src_ref=v_ref.at[rows], dst_ref=o_ref.at[src_chip, rows], send_sem=send.at[j], recv_sem=recv.at[j],
                device_id=(*chips[j], c), device_id_type=MESH)

        def handed(j, rows):
            return pltpu.make_async_remote_copy(
                src_ref=o_ref.at[ids[j], rows], dst_ref=o_ref.at[ids[j], rows], send_sem=fsend.at[j],
                recv_sem=frecv.at[j], device_id=(x, y, 1 - c), device_id_type=MESH)

        def my_rows(off, size):
            return pl.ds(pl.multiple_of(c * H + off, 8), size)

        for j in range(3):
            _start_chunked(lambda off, size, j=j: cross(j, me, my_rows(off, size)), H)
        for j in range(3):
            cross(j, ids[j], mine).wait_recv()
            _start_chunked(lambda off, size, j=j: handed(j, my_rows(off, size)), H)
        for j in range(3):
            handed(j, other).wait_recv()
        for j in range(3):
            cross(j, me, mine).wait_send()
            handed(j, mine).wait_send()

    got = pl.pallas_call(
        body, name=name, in_specs=[_any()], out_specs=_any(), out_shape=jax.ShapeDtypeStruct((4, R, C), v.dtype),
        scratch_shapes=[pltpu.SemaphoreType.DMA((3,)), pltpu.SemaphoreType.DMA((3,)), pltpu.SemaphoreType.DMA((3,)),
                        pltpu.SemaphoreType.DMA((3,))],
    )(v)
    return lax.dynamic_update_slice(got, v[None], (_my_chip(), 0, 0))


def swap_halves(g, *, name):
    n, R, C = g.shape
    H = R // 2

    def body(g_ref, theirs_ref, send, recv):
        x, y, c, _ = _place()

        def give(q, off, size):
            return pltpu.make_async_remote_copy(
                src_ref=g_ref.at[q, pl.ds(pl.multiple_of((1 - c) * H + off, 8), size)],
                dst_ref=theirs_ref.at[q, pl.ds(off, size)], send_sem=send.at[q], recv_sem=recv.at[q],
                device_id=(x, y, 1 - c), device_id_type=MESH)

        for q in range(n):
            _start_chunked(functools.partial(give, q), H)
        for q in range(n):
            give(q, 0, H).wait()

    return pl.pallas_call(
        body, name=name, in_specs=[_any()], out_specs=_any(), out_shape=jax.ShapeDtypeStruct((n, H, C), g.dtype),
        scratch_shapes=[pltpu.SemaphoreType.DMA((n,)), pltpu.SemaphoreType.DMA((n,))],
    )(g)


def exchange_pieces(p, *, name):
    H = p.shape[1]

    def body(p_ref, b_ref, send, recv):
        x, y, c, chips = _place()
        me = 2 * x + y
        ids = [2 * px + py for px, py in chips]

        def cross(j, piece, slot, off, size):
            return pltpu.make_async_remote_copy(
                src_ref=p_ref.at[piece, pl.ds(off, size)], dst_ref=b_ref.at[slot, pl.ds(off, size)], send_sem=send.at[j],
                recv_sem=recv.at[j], device_id=(*chips[j], c), device_id_type=MESH)

        for j in range(3):
            _start_chunked(functools.partial(cross, j, ids[j], me), H)
        for j in range(3):
            cross(j, me, ids[j], 0, H).wait_recv()
        for j in range(3):
            cross(j, ids[j], me, 0, H).wait_send()

    got = pl.pallas_call(
        body, name=name, in_specs=[_any()], out_specs=_any(), out_shape=jax.ShapeDtypeStruct(p.shape, p.dtype),
        scratch_shapes=[pltpu.SemaphoreType.DMA((3,)), pltpu.SemaphoreType.DMA((3,))],
    )(p)
    me = _my_chip()
    return lax.dynamic_update_slice(got, lax.dynamic_slice_in_dim(p, me, 1, axis=0), (me, 0, 0))


def join_halves(f, *, name):
    H, C = f.shape

    def body(f_ref, o_ref, send, recv):
        x, y, c, _ = _place()

        def give(off, size):
            return pltpu.make_async_remote_copy(
                src_ref=f_ref.at[pl.ds(off, size)], dst_ref=o_ref.at[pl.ds(pl.multiple_of(c * H + off, 8), size)],
                send_sem=send, recv_sem=recv, device_id=(x, y, 1 - c), device_id_type=MESH)

        _start_chunked(give, H)
        give(0, H).wait()

    both = pl.pallas_call(
        body, name=name, in_specs=[_any()], out_specs=_any(), out_shape=jax.ShapeDtypeStruct((2 * H, C), f.dtype),
        scratch_shapes=[pltpu.SemaphoreType.DMA, pltpu.SemaphoreType.DMA],
    )(f)
    return lax.dynamic_update_slice(both, f, (lax.axis_index("c") * H, 0))


def _row_tile(rows, want):
    tm = want
    while rows % tm:
        tm //= 2
    return tm


def _add_pair(g, theirs, out_dtype, *, name):
    n, R, C = g.shape
    H = R // 2
    tm = _row_tile(H, 512)
    nb = H // tm

    def body(c_ref, g_ref, t_ref, o_ref):
        o_ref[...] = (g_ref[...] + t_ref[...]).astype(o_ref.dtype)

    grid_spec = pltpu.PrefetchScalarGridSpec(
        num_scalar_prefetch=1, grid=(n, nb),
        in_specs=[pl.BlockSpec((None, tm, C), lambda q, i, c: (q, c[0] * nb + i, 0)),
                  pl.BlockSpec((None, tm, C), lambda q, i, c: (q, i, 0))],
        out_specs=pl.BlockSpec((None, tm, C), lambda q, i, c: (q, i, 0)))
    return pl.pallas_call(
        body, name=name, grid_spec=grid_spec, out_shape=jax.ShapeDtypeStruct((n, H, C), out_dtype),
        compiler_params=_params(("parallel", "parallel")),
    )(lax.axis_index("c").astype(jnp.int32).reshape(1), g, theirs)


def _add_stack(b, *, name):
    n, H, C = b.shape
    tm = _row_tile(H, 256)

    def body(b_ref, o_ref):
        s = b_ref[0].astype(F32)
        for k in range(1, n):
            s = s + b_ref[k].astype(F32)
        o_ref[...] = s

    return pl.pallas_call(
        body, name=name, grid=(H // tm,), in_specs=[pl.BlockSpec((n, tm, C), lambda i: (0, i, 0))],
        out_specs=pl.BlockSpec((tm, C), lambda i: (i, 0)), out_shape=jax.ShapeDtypeStruct((H, C), F32),
        compiler_params=_params(("parallel",)),
    )(b)


def reduce_scatter(g, tag):
    pair = _add_pair(g, swap_halves(g, name=tag + "_swap_halves"), GRAD_WIRE_DTYPE, name=tag + "_add_pair")
    total = _add_stack(exchange_pieces(pair, name=tag + "_exchange"), name=tag + "_add_chips")
    return join_halves(total, name=tag + "_join")


FFN = ("ffn1_w_gate", "ffn1_w_up", "ffn2_w_gate", "ffn2_w_up", "ffn1_w_down", "ffn2_w_down")
BIG = ("w_in", "w_branch", "w_out")
SMALL = ("norm_pre", "norm_post", "conv_w", "gla_w_alpha")
REPLICATED = ("conv_b", "conv_ln_g", "conv_ln_b", "gla_b_alpha", "gla_norm_g")
WEIGHTS = ("norm_pre", "norm_post", "ffn1_w_gate", "ffn1_w_up", "ffn1_w_down", "ffn2_w_gate", "ffn2_w_up", "ffn2_w_down",
           "w_in", "conv_w", "conv_b", "conv_ln_g", "conv_ln_b", "gla_w_alpha", "gla_b_alpha", "gla_norm_g", "w_branch",
           "w_out")
SHARD_AXIS = {"ffn1_w_gate": 2, "ffn1_w_up": 2, "ffn1_w_down": 1, "ffn2_w_gate": 2, "ffn2_w_up": 2, "ffn2_w_down": 1,
              "w_in": 2, "w_branch": 3, "w_out": 1, "norm_pre": 2, "norm_post": 2, "conv_w": 2, "gla_w_alpha": 2}


def _size(shape):
    n = 1
    for d in shape:
        n *= d
    return n


def _pack_rows(shape):
    return -(-_size(shape) // (16 * PACK_COLS)) * 16


def _pack(arrs, dtype, row_mult):
    parts, rows = [], 0
    for a in arrs:
        r = _pack_rows(a.shape)
        flat = a.astype(dtype).reshape(-1)
        if r * PACK_COLS != flat.shape[0]:
            flat = jnp.pad(flat, (0, r * PACK_COLS - flat.shape[0]))
        parts.append(flat.reshape(r, PACK_COLS))
        rows += r
    if rows % row_mult:
        parts.append(jnp.zeros((row_mult - rows % row_mult, PACK_COLS), dtype))
    return jnp.concatenate(parts, axis=0)


def _unpack(buf, shapes):
    lead = buf.shape[:-2]
    out, off = [], 0
    for s in shapes:
        r = _pack_rows(s)
        part = lax.slice_in_dim(buf, off, off + r, axis=buf.ndim - 2).reshape(lead + (r * PACK_COLS,))
        if r * PACK_COLS != _size(s):
            part = lax.slice_in_dim(part, 0, _size(s), axis=part.ndim - 1)
        out.append(part.reshape(lead + tuple(s)))
        off += r
    return out


def gather_weights(shards, names, dtype, row_mult, name):
    packed = _pack([shards[k] for k in names], dtype, row_mult)
    parts = _unpack(all_gather_chips(packed, name=name), [shards[k].shape for k in names])
    return {k: jnp.concatenate([p[q] for q in range(4)], axis=SHARD_AXIS[k]) for k, p in zip(names, parts)}


def scatter_grads(grads, shards):
    sharded = BIG + SMALL
    pieces = []
    for q in range(4):
        part = []
        for k in sharded:
            w = shards[k].shape[SHARD_AXIS[k]]
            part.append(lax.slice_in_dim(grads[k], q * w, (q + 1) * w, axis=SHARD_AXIS[k]))
        part += [grads[k] for k in REPLICATED]
        pieces.append(_pack(part, F32, 512))
    total = reduce_scatter(jnp.stack(pieces), "rs_mix")
    names = sharded + REPLICATED
    return dict(zip(names, _unpack(total, [shards[k].shape for k in names])))


TM = 512
TM_MERGE = 256
TM_CONV = 1024
TM_GLA = 512
TQ = 256

IN_SB, IN_CONV, IN_GLA, IN_LR, IN_GATE = 0, 1536, 2560, 4096, 4112
IN_END = 7184


def layer_weights(full, l):
    w = {}
    win = full["w_in"][l]
    w["in_sb"] = win[:, IN_SB:IN_CONV]
    w["in_conv"] = win[:, IN_CONV:IN_GLA]
    w["in_gla"] = win[:, IN_GLA:IN_LR]
    w["in_lr"] = jnp.pad(win[:, IN_LR:IN_GATE], ((0, 0), (0, GLA_RANK_PAD - GLA_RANK)))
    w["in_gate"] = win[:, IN_GATE:IN_END]
    w["branch"] = [full["w_branch"][l, g] for g in range(3)]
    w["out"] = full["w_out"][l]
    return w


def ffn_fwd(x, gpre, gpost, w6, f, l, tag):
    S = x.shape[0]
    h = rowwise(f_rms, [x], [gpre], [MXU_DTYPE], tm=TM, name=tag + "_pre")[0]
    ab = slab_nt(h, w6, 2 * f, 8, l, out_dtype=MXU_DTYPE, name=tag + "_gu").reshape(2, 4, S, SLAB)
    out, x2 = slab_nn(ab, w6, 4 + f, l, gated=True, tail=("half_post", x, gpost), name=tag + "_down")
    return x2, (x, h, ab, out)


def ffn_bwd(dx2, res, gpre, gpost, w6, g6, f, l, tag):
    x, h, ab, out = res
    S = x.shape[0]
    df, dgpost = rowwise_vjp(f_half_post, [x, out], [gpost], [dx2], [1], [0], [MXU_DTYPE], tm=TM, name=tag + "_post_b")
    dab = slab_nt_swiglu_bwd(df, w6, 4 + f, l, ab, name=tag + "_down_dx").reshape(8, S, SLAB)
    g6 = slab_tn(ab, df, g6, 4 + f, l, gated=True, name=tag + "_down_dw")
    dx, dgpre = slab_nn(dab, w6, 2 * f, l, tail=("rms_bwd", x, gpre, dx2), name=tag + "_gu_dx")
    g6 = slab_tn(dab, h, g6, 2 * f, l, name=tag + "_gu_dw")
    return dx, dgpre, dgpost, g6


def mixer_fwd(x, p, w, tag):
    h = rowwise(f_rms, [x], [p["gpre"]], [MXU_DTYPE], tm=TM, name=tag + "_pre")[0]
    qkv = mm(h, w["in_sb"], out_dtype=MXU_DTYPE, tn=512, name=tag + "_in_sb")
    cv = mm(h, w["in_conv"], name=tag + "_in_conv")
    gl = mm(h, w["in_gla"], out_dtype=MXU_DTYPE, tn=512, name=tag + "_in_gla")
    lr = mm(h, w["in_lr"], out_dtype=MXU_DTYPE, name=tag + "_in_lr")
    gt = mm(h, w["in_gate"], out_dtype=MXU_DTYPE, name=tag + "_in_gate")
    sb = sb_attn_fwd(qkv, tq=TQ, name=tag + "_sb")
    u = rowwise(f_glu, [(cv, BRANCH, 0), (cv, BRANCH, 1)], [], [F32], tm=TM, name=tag + "_glu")[0]
    y = conv_fwd(u, p["conv_w"], p["conv_b"], tm=TM_CONV, name=tag + "_conv")
    cb = rowwise(f_conv_ln, [y], [p["ln_g"], p["ln_b"]], [MXU_DTYPE], tm=TM, name=tag + "_ln")[0]
    qs, kd, lam = rowwise(f_gla_pre, [(gl, 256, 0), (gl, 256, 1), lr], [p["wa"], p["ba"]], [MXU_DTYPE, MXU_DTYPE, F32],
                          tm=TM_GLA, name=tag + "_gla_pre")
    o, states = gla_scan_fwd(qs, kd, gl, lam, tm=TM_GLA, name=tag + "_gla_scan")
    gb = rowwise(f_gla_post, [o, (gl, BRANCH, 2)], [p["gn"]], [MXU_DTYPE], tm=TM, name=tag + "_gla_post")[0]
    branches = [sb, cb, gb]
    bd = [mm(branches[g], w["branch"][g], out_dtype=MXU_DTYPE, name=tag + f"_branch{g}") for g in range(3)]
    merged = rowwise(f_merge, [(gt, D_MODEL, 0), (gt, D_MODEL, 1), (gt, D_MODEL, 2)] + bd, [], [MXU_DTYPE], tm=TM_MERGE,
                     name=tag + "_merge")[0]
    m = mm(merged, w["out"], name=tag + "_out")
    x2 = rowwise(f_post, [x, m], [p["gpost"]], [F32], tm=TM, name=tag + "_post")[0]
    return x2, (x, h, qkv, cv, gl, lr, gt, u, y, qs, kd, lam, o, states, branches, bd, merged, m)


def mixer_bwd(dx2, res, p, w, tag):
    x, h, qkv, cv, gl, lr, gt, u, y, qs, kd, lam, o, states, branches, bd, merged, m = res
    g = {}
    dm, g["gpost"] = rowwise_vjp(f_post, [x, m], [p["gpost"]], [dx2], [1], [0], [MXU_DTYPE], tm=TM, name=tag + "_post_b")
    dmerged = mm(dm, w["out"], tb=True, out_dtype=MXU_DTYPE, name=tag + "_out_dx")
    g["out"] = mm(merged, dm, ta=True, name=tag + "_out_dw")
    gts = [(gt, D_MODEL, 0), (gt, D_MODEL, 1), (gt, D_MODEL, 2)]
    dgate = rowwise_vjp(f_merge, gts + bd, [], [dmerged], [0, 1, 2, 3, 4, 5], [], [MXU_DTYPE] * 6, tm=TM_MERGE,
                        name=tag + "_merge_b")
    dgt = jnp.concatenate(dgate[:3], axis=1)
    dbd = dgate[3:]
    g["branch"] = [mm(branches[k], dbd[k], ta=True, name=tag + f"_branch{k}_dw") for k in range(3)]
    dsb = mm(dbd[0], w["branch"][0], tb=True, out_dtype=MXU_DTYPE, name=tag + "_branch0_dx")
    dq, dk, dv = sb_attn_bwd(qkv, dsb, tq=TQ, name=tag + "_sb_b")
    dqkv = jnp.concatenate([dq, _mx(dk), _mx(dv)], axis=1)
    dcb = mm(dbd[1], w["branch"][1], tb=True, name=tag + "_branch1_dx")
    dy, g["ln_g"], g["ln_b"] = rowwise_vjp(f_conv_ln, [y], [p["ln_g"], p["ln_b"]], [dcb], [0], [0, 1], [F32], tm=TM,
                                           name=tag + "_ln_b")
    du, dwb = conv_bwd(dy, u, p["conv_w"], tm=TM_CONV, name=tag + "_conv_b")
    g["conv_w"], g["conv_b"] = dwb[:CONV_WIDTH], dwb[CONV_WIDTH:CONV_WIDTH + 1]
    dca, dcg = rowwise_vjp(f_glu, [(cv, BRANCH, 0), (cv, BRANCH, 1)], [], [du], [0, 1], [], [MXU_DTYPE, MXU_DTYPE], tm=TM,
                           name=tag + "_glu_b")
    dcv = jnp.concatenate([dca, dcg], axis=1)
    dgb = mm(dbd[2], w["branch"][2], tb=True, name=tag + "_branch2_dx")
    do, dr, g["gn"] = rowwise_vjp(f_gla_post, [o, (gl, BRANCH, 2)], [p["gn"]], [dgb], [0, 1], [0], [F32, MXU_DTYPE], tm=TM,
                                  name=tag + "_gla_post_b")
    dqs, dkd, dgv, dlam = gla_scan_bwd(do, qs, kd, gl, lam, states, tm=TM_GLA, name=tag + "_gla_scan_b")
    dgq, dgk, dlr, g["wa"], g["ba"] = rowwise_vjp(
        f_gla_pre, [(gl, 256, 0), (gl, 256, 1), lr], [p["wa"], p["ba"]], [dqs, dkd, dlam], [0, 1, 2], [0, 1],
        [MXU_DTYPE, MXU_DTYPE, MXU_DTYPE], tm=TM_GLA, name=tag + "_gla_pre_b")
    dgl = jnp.concatenate([dgq, dgk, _mx(dgv), dr], axis=1)
    secs = [("in_sb", dqkv), ("in_conv", dcv), ("in_gla", dgl), ("in_lr", dlr), ("in_gate", dgt)]
    dh = mm_sum_nt([(d, w[k]) for k, d in secs], name=tag + "_in_dx")
    for k, d in secs:
        g[k] = mm(h, d, ta=True, tn=1536, name=tag + "_" + k + "_dw")
    dx, g["gpre"] = rowwise_vjp(f_rms, [x], [p["gpre"]], [dh], [0], [0], [F32], [dx2], tm=TM, name=tag + "_pre_b")
    return dx, g


def kernel(x, norm_pre, norm_post, ffn1_w_gate, ffn1_w_up, ffn1_w_down, ffn2_w_gate, ffn2_w_up, ffn2_w_down, w_in, conv_w, conv_b, conv_ln_g, conv_ln_b, gla_w_alpha, gla_b_alpha, gla_norm_g, w_branch, w_out, loss_target, m_norm_pre, m_norm_post, m_ffn1_w_gate, m_ffn1_w_up, m_ffn1_w_down, m_ffn2_w_gate, m_ffn2_w_up, m_ffn2_w_down, m_w_in, m_conv_w, m_conv_b, m_conv_ln_g, m_conv_ln_b, m_gla_w_alpha, m_gla_b_alpha, m_gla_norm_g, m_w_branch, m_w_out, v_norm_pre, v_norm_post, v_ffn1_w_gate, v_ffn1_w_up, v_ffn1_w_down, v_ffn2_w_gate, v_ffn2_w_up, v_ffn2_w_down, v_w_in, v_conv_w, v_conv_b, v_conv_ln_g, v_conv_ln_b, v_gla_w_alpha, v_gla_b_alpha, v_gla_norm_g, v_w_branch, v_w_out):
    shards = dict(norm_pre=norm_pre, norm_post=norm_post, ffn1_w_gate=ffn1_w_gate, ffn1_w_up=ffn1_w_up,
                  ffn1_w_down=ffn1_w_down, ffn2_w_gate=ffn2_w_gate, ffn2_w_up=ffn2_w_up, ffn2_w_down=ffn2_w_down, w_in=w_in,
                  conv_w=conv_w, conv_b=conv_b, conv_ln_g=conv_ln_g, conv_ln_b=conv_ln_b, gla_w_alpha=gla_w_alpha,
                  gla_b_alpha=gla_b_alpha, gla_norm_g=gla_norm_g, w_branch=w_branch, w_out=w_out)
    mom_m = dict(zip(WEIGHTS, (m_norm_pre, m_norm_post, m_ffn1_w_gate, m_ffn1_w_up, m_ffn1_w_down, m_ffn2_w_gate,
                               m_ffn2_w_up, m_ffn2_w_down, m_w_in, m_conv_w, m_conv_b, m_conv_ln_g, m_conv_ln_b,
                               m_gla_w_alpha, m_gla_b_alpha, m_gla_norm_g, m_w_branch, m_w_out)))
    mom_v = dict(zip(WEIGHTS, (v_norm_pre, v_norm_post, v_ffn1_w_gate, v_ffn1_w_up, v_ffn1_w_down, v_ffn2_w_gate,
                               v_ffn2_w_up, v_ffn2_w_down, v_w_in, v_conv_w, v_conv_b, v_conv_ln_g, v_conv_ln_b,
                               v_gla_w_alpha, v_gla_b_alpha, v_gla_norm_g, v_w_branch, v_w_out)))
    depth = norm_pre.shape[0]
    members = [shards[k] if k.endswith("down") else jnp.swapaxes(shards[k], 1, 2) for k in FFN]
    ffn_rows = jnp.concatenate([_mx(t).reshape(depth * SLAB, D_MODEL) for t in members], axis=0)
    w6 = all_gather_chips(ffn_rows, name="gather_ffn").reshape(4, len(FFN), depth, SLAB, D_MODEL)
    full = gather_weights(shards, BIG, MXU_DTYPE, 256, "gather_big")
    full.update(gather_weights(shards, SMALL, F32, 16, "gather_small"))

    def layer_params(l):
        ffn = [dict(gpre=full["norm_pre"][l, k:k + 1], gpost=full["norm_post"][l, k:k + 1]) for k in (0, 2)]
        mix = dict(gpre=full["norm_pre"][l, 1:2], gpost=full["norm_post"][l, 1:2],
                   conv_w=jnp.pad(full["conv_w"][l], ((0, CONV_PAD - CONV_WIDTH), (0, 0))), conv_b=conv_b[l:l + 1],
                   ln_g=conv_ln_g[l:l + 1], ln_b=conv_ln_b[l:l + 1],
                   wa=jnp.pad(full["gla_w_alpha"][l], ((0, GLA_RANK_PAD - GLA_RANK), (0, 0))), ba=gla_b_alpha[l:l + 1],
                   gn=gla_norm_g[l:l + 1])
        return ffn, mix

    xs = x[0]
    saved = []
    for l in range(depth):
        w = layer_weights(full, l)
        ffn, mix = layer_params(l)
        xs, r1 = ffn_fwd(xs, ffn[0]["gpre"], ffn[0]["gpost"], w6, 0, l, f"l{l}_ffn1")
        xs, r2 = mixer_fwd(xs, mix, w, f"l{l}_mix")
        xs, r3 = ffn_fwd(xs, ffn[1]["gpre"], ffn[1]["gpost"], w6, 1, l, f"l{l}_ffn2")
        saved.append((w, ffn, mix, r1, r2, r3))
    dx, sq = loss_head(xs, loss_target[0], tm=TM, name="loss_head")
    loss = lax.psum(0.5 * jnp.sum(sq) / D_MODEL, ("x", "y", "c"))

    per_layer = []
    g6 = lax.empty(w6.shape, F32)
    for l in reversed(range(depth)):
        w, ffn, mix, r1, r2, r3 = saved[l]
        g = {}
        dx, gpre2, gpost2, g6 = ffn_bwd(dx, r3, ffn[1]["gpre"], ffn[1]["gpost"], w6, g6, 1, l, f"l{l}_ffn2")
        dx, gm = mixer_bwd(dx, r2, mix, w, f"l{l}_mix")
        dx, gpre0, gpost0, g6 = ffn_bwd(dx, r1, ffn[0]["gpre"], ffn[0]["gpost"], w6, g6, 0, l, f"l{l}_ffn1")
        g["norm_pre"] = jnp.concatenate([gpre0, gm["gpre"], gpre2], axis=0)
        g["norm_post"] = jnp.concatenate([gpost0, gm["gpost"], gpost2], axis=0)
        g["w_in"] = jnp.concatenate([gm["in_sb"], gm["in_conv"], gm["in_gla"], gm["in_lr"][:, :GLA_RANK], gm["in_gate"]],
                                    axis=1)
        g["conv_w"], g["conv_b"] = gm["conv_w"], gm["conv_b"][0]
        g["conv_ln_g"], g["conv_ln_b"] = gm["ln_g"][0], gm["ln_b"][0]
        g["gla_w_alpha"], g["gla_b_alpha"], g["gla_norm_g"] = gm["wa"][:GLA_RANK], gm["ba"][0], gm["gn"][0]
        g["w_branch"] = jnp.stack(gm["branch"])
        g["w_out"] = gm["out"]
        per_layer.append(g)
    per_layer.reverse()
    grads = {k: jnp.stack([g[k] for g in per_layer]) for k in BIG + SMALL + REPLICATED}

    grad_w = scatter_grads(grads, shards)
    ffn_sum = reduce_scatter(g6.reshape(4, len(FFN) * depth * SLAB, D_MODEL), "rs_ffn")
    ffn_sum = ffn_sum.reshape(len(FFN), depth, SLAB, D_MODEL)
    for t, k in enumerate(FFN):
        grad_w[k] = ffn_sum[t] if k.endswith("down") else jnp.swapaxes(ffn_sum[t], 1, 2)
    delta, new_m, new_v = {}, {}, {}
    for k in WEIGHTS:
        delta[k], new_m[k], new_v[k] = adamw(shards[k], grad_w[k], mom_m[k], mom_v[k], name="adamw_" + k)
    return (loss, dx[None], *[grad_w[k] for k in WEIGHTS], *[delta[k] for k in WEIGHTS], *[new_m[k] for k in WEIGHTS],
            *[new_v[k] for k in WEIGHTS])
```

```python
import functools

import jax
import jax.numpy as jnp
from jax import lax
from jax.experimental import pallas as pl
from jax.experimental.pallas import tpu as pltpu

F32 = jnp.float32
MXU_DTYPE = jnp.bfloat16
GRAD_WIRE_DTYPE = jnp.bfloat16
HIGHEST = lax.Precision.HIGHEST
MESH = pl.DeviceIdType.MESH

NORM_EPS = 1e-6
D_MODEL = 1024
D_FF = 2816
BRANCH = 512
CHUNK = 64
CONV_WIDTH = 31
CONV_PAD = 32
CONV_SUB = 256
GLA_RANK = 16
GLA_RANK_PAD = 128
GLA_TAU = 16.0
SB_SCALE = 0.125
SB_CUTOFF = 60.0
GLA_SCALE = 0.125
PACK_COLS = 1024
VMEM_LIMIT = 56 * 1024 * 1024

ADAM_LR, ADAM_B1, ADAM_B2, ADAM_EPS, ADAM_WD, ADAM_STEP = 0.001, 0.9, 0.999, 1e-08, 0.01, 10

NT = (((1,), (1,)), ((), ()))
TN = (((0,), (0,)), ((), ()))
NN = (((1,), (0,)), ((), ()))


def _params(sem=None, vmem=None):
    return pltpu.CompilerParams(dimension_semantics=sem, vmem_limit_bytes=vmem)


def _mx(v):
    return v.astype(MXU_DTYPE)


def _mx_round(v):
    return v.astype(MXU_DTYPE).astype(F32)


def _fit(dim, want):
    if dim <= want:
        return dim
    for d in range(want - want % 128, 0, -128):
        if dim % d == 0:
            return d
    raise ValueError((dim, want))


def mm(a, b, c=None, *, ta=False, tb=False, out_dtype=F32, tm=1024, tn=1024, tk=1024, name):
    K, M = a.shape if ta else a.shape[::-1]
    N = b.shape[0] if tb else b.shape[1]
    assert (b.shape[1] if tb else b.shape[0]) == K, (a.shape, b.shape, ta, tb)
    tm, tn, tk = _fit(M, tm), _fit(N, tn), _fit(K, tk)
    nk = K // tk
    dn = (((0 if ta else 1,), (1 if tb else 0,)), ((), ()))

    def body(*refs):
        if c is None:
            a_ref, b_ref, o_ref, acc = refs
            c_ref = None
        else:
            a_ref, b_ref, c_ref, o_ref, acc = refs
        k = pl.program_id(2)
        p = lax.dot_general(_mx(a_ref[...]), _mx(b_ref[...]), dn, preferred_element_type=F32)

        @pl.when(k == 0)
        def _():
            acc[...] = p

        @pl.when(k > 0)
        def _():
            acc[...] += p

        @pl.when(k == nk - 1)
        def _():
            r = acc[...]
            if c_ref is not None:
                r = r + c_ref[...].astype(F32)
            o_ref[...] = r.astype(o_ref.dtype)

    a_spec = pl.BlockSpec((tk, tm), lambda i, j, k: (k, i)) if ta else pl.BlockSpec((tm, tk), lambda i, j, k: (i, k))
    b_spec = pl.BlockSpec((tn, tk), lambda i, j, k: (j, k)) if tb else pl.BlockSpec((tk, tn), lambda i, j, k: (k, j))
    o_spec = pl.BlockSpec((tm, tn), lambda i, j, k: (i, j))
    ins, in_specs = [a, b], [a_spec, b_spec]
    if c is not None:
        ins.append(c)
        in_specs.append(o_spec)
    return pl.pallas_call(
        body, name=name, grid=(M // tm, N // tn, nk), in_specs=in_specs, out_specs=o_spec,
        out_shape=jax.ShapeDtypeStruct((M, N), out_dtype), scratch_shapes=[pltpu.VMEM((tm, tn), F32)],
        compiler_params=_params(("parallel", "parallel", "arbitrary"), VMEM_LIMIT),
    )(*ins)


def mm_sum_nt(pairs, *, tm=1024, tk=512, name):
    M, N = pairs[0][0].shape[0], pairs[0][1].shape[0]
    tm = _fit(M, tm)
    tks = [_fit(a.shape[1], tk) for a, _ in pairs]
    counts = [a.shape[1] // t for (a, _), t in zip(pairs, tks)]
    starts = [sum(counts[:i]) for i in range(len(pairs))]
    total = sum(counts)

    def body(*refs):
        o_ref, acc = refs[-2:]
        k = pl.program_id(1)

        @pl.when(k == 0)
        def _():
            acc[...] = jnp.zeros_like(acc)

        for i in range(len(pairs)):
            @pl.when((k >= starts[i]) & (k < starts[i] + counts[i]))
            def _(i=i):
                acc[...] += lax.dot_general(_mx(refs[2 * i][...]), _mx(refs[2 * i + 1][...]), NT,
                                            preferred_element_type=F32)

        @pl.when(k == total - 1)
        def _():
            o_ref[...] = acc[...]

    def turn(i):
        return lambda k: jnp.clip(k - starts[i], 0, counts[i] - 1)

    in_specs, ins = [], []
    for i, (a, b) in enumerate(pairs):
        in_specs.append(pl.BlockSpec((tm, tks[i]), lambda m, k, f=turn(i): (m, f(k))))
        in_specs.append(pl.BlockSpec((N, tks[i]), lambda m, k, f=turn(i): (0, f(k))))
        ins += [a, b]
    return pl.pallas_call(
        body, name=name, grid=(M // tm, total), in_specs=in_specs, out_specs=pl.BlockSpec((tm, N), lambda m, k: (m, 0)),
        out_shape=jax.ShapeDtypeStruct((M, N), F32), scratch_shapes=[pltpu.VMEM((tm, N), F32)],
        compiler_params=_params(("parallel", "arbitrary"), VMEM_LIMIT),
    )(*ins)


SLAB = D_FF // 4


def _swiglu_of(ab_ref):
    return _mx(f_swiglu(ab_ref[0].astype(F32), ab_ref[1].astype(F32))[0])


def _gated_spec(rows, index):
    return pl.BlockSpec((2, None, rows, SLAB), index)


def slab_nt(a, w6, t0, n, l, *, out_dtype, tm=1024, name):
    S, K = a.shape
    tm = _fit(S, tm)

    def body(a_ref, w_ref, o_ref):
        o_ref[...] = lax.dot_general(_mx(a_ref[...]), _mx(w_ref[...]), NT, preferred_element_type=F32).astype(o_ref.dtype)

    return pl.pallas_call(
        body, name=name, grid=(S // tm, n),
        in_specs=[pl.BlockSpec((tm, K), lambda i, s: (i, 0)),
                  pl.BlockSpec((None, None, None, SLAB, K), lambda i, s: (s % 4, t0 + s // 4, l, 0, 0))],
        out_specs=pl.BlockSpec((None, tm, SLAB), lambda i, s: (s, i, 0)),
        out_shape=jax.ShapeDtypeStruct((n, S, SLAB), out_dtype),
        compiler_params=_params(("parallel", "parallel"), VMEM_LIMIT),
    )(a, w6)


def slab_nt_swiglu_bwd(d, w6, t0, l, ab, *, tm=1024, name):
    S, K = d.shape
    tm = _fit(S, tm)

    def body(d_ref, w_ref, ab_ref, o_ref):
        dz = lax.dot_general(_mx(d_ref[...]), _mx(w_ref[...]), NT, preferred_element_type=F32)
        _, pullback = jax.vjp(lambda g, u: f_swiglu(g, u)[0], ab_ref[0].astype(F32), ab_ref[1].astype(F32))
        dg, du = pullback(_mx_round(dz))
        o_ref[0] = dg.astype(o_ref.dtype)
        o_ref[1] = du.astype(o_ref.dtype)

    return pl.pallas_call(
        body, name=name, grid=(S // tm, 4),
        in_specs=[pl.BlockSpec((tm, K), lambda i, s: (i, 0)),
                  pl.BlockSpec((None, None, None, SLAB, K), lambda i, s: (s, t0, l, 0, 0)),
                  _gated_spec(tm, lambda i, s: (0, s, i, 0))],
        out_specs=_gated_spec(tm, lambda i, s: (0, s, i, 0)),
        out_shape=jax.ShapeDtypeStruct(ab.shape, MXU_DTYPE),
        compiler_params=_params(("parallel", "parallel"), VMEM_LIMIT),
    )(d, w6, ab)


def slab_nn(a, w6, t0, l, *, gated=False, tail=None, tm=1024, name):
    n, S, _ = a.shape[-3:]
    K = w6.shape[-1]
    tm = _fit(S, tm)
    kind = tail[0] if tail else None
    extra = list(tail[1:]) if tail else []

    def body(a_ref, w_ref, *rest):
        i, s = pl.program_id(0), pl.program_id(1)
        acc = rest[-1]
        lhs = _swiglu_of(a_ref) if gated else _mx(a_ref[...])
        p = jnp.dot(lhs, _mx(w_ref[...]), preferred_element_type=F32)

        @pl.when(s == 0)
        def _():
            acc[...] = p

        @pl.when(s > 0)
        def _():
            acc[...] += p

        @pl.when(s == n - 1)
        def _():
            r = acc[...]
            if kind is None:
                rest[0][...] = r
            elif kind == "half_post":
                x_ref, g_ref, o_ref, x2_ref = rest[:4]
                o_ref[...] = r
                x2_ref[...] = f_half_post(x_ref[...], r, g_ref[...])[0]
            else:
                x_ref, g_ref, d_ref, dx_ref, dg_ref = rest[:5]
                _, pullback = jax.vjp(lambda xv, gv: f_rms(xv, gv)[0], x_ref[...], g_ref[...])
                dxn, dg = pullback(r)
                dx_ref[...] = dxn + d_ref[...]

                @pl.when(i == 0)
                def _():
                    dg_ref[...] = dg

                @pl.when(i > 0)
                def _():
                    dg_ref[...] += dg

    rows = pl.BlockSpec((tm, K), lambda i, s: (i, 0))
    gain = pl.BlockSpec((1, K), lambda i, s: (0, 0))
    tail_in = {None: [], "half_post": [rows, gain], "rms_bwd": [rows, gain, rows]}[kind]
    full, vec = jax.ShapeDtypeStruct((S, K), F32), jax.ShapeDtypeStruct((1, K), F32)
    out_specs, out_shape = {None: (rows, full), "half_post": ([rows, rows], [full, full]),
                            "rms_bwd": ([rows, gain], [full, vec])}[kind]
    return pl.pallas_call(
        body, name=name, grid=(S // tm, n),
        in_specs=[_gated_spec(tm, lambda i, s: (0, s, i, 0)) if gated else
                  pl.BlockSpec((None, tm, SLAB), lambda i, s: (s, i, 0)),
                  pl.BlockSpec((None, None, None, SLAB, K), lambda i, s: (s % 4, t0 + s // 4, l, 0, 0))] + tail_in,
        out_specs=out_specs, out_shape=out_shape, scratch_shapes=[pltpu.VMEM((tm, K), F32)],
        compiler_params=_params(("arbitrary" if kind == "rms_bwd" else "parallel", "arbitrary"), VMEM_LIMIT),
    )(a, w6, *extra)


def slab_tn(a, b, g6, t0, l, *, gated=False, tk=1024, name):
    n, S, _ = a.shape[-3:]
    K = b.shape[1]
    tk = _fit(S, tk)
    nk = S // tk

    def body(a_ref, b_ref, g_ref, o_ref, acc):
        k = pl.program_id(1)
        lhs = _swiglu_of(a_ref) if gated else _mx(a_ref[...])
        p = lax.dot_general(lhs, _mx(b_ref[...]), TN, preferred_element_type=F32)

        @pl.when(k == 0)
        def _():
            acc[...] = p

        @pl.when(k > 0)
        def _():
            acc[...] += p

        @pl.when(k == nk - 1)
        def _():
            o_ref[...] = acc[...]

    return pl.pallas_call(
        body, name=name, grid=(n, nk),
        in_specs=[_gated_spec(tk, lambda s, k: (0, s, k, 0)) if gated else
                  pl.BlockSpec((None, tk, SLAB), lambda s, k: (s, k, 0)),
                  pl.BlockSpec((tk, K), lambda s, k: (k, 0)), pl.BlockSpec(memory_space=pl.ANY)],
        out_specs=pl.BlockSpec((None, None, None, SLAB, K), lambda s, k: (s % 4, t0 + s // 4, l, 0, 0)),
        out_shape=jax.ShapeDtypeStruct(g6.shape, g6.dtype), scratch_shapes=[pltpu.VMEM((SLAB, K), F32)],
        input_output_aliases={2: 0},
        compiler_params=_params(("parallel", "arbitrary"), VMEM_LIMIT),
    )(a, b, g6)


def _row_arg(arg):
    if isinstance(arg, tuple):
        return arg
    return arg, arg.shape[1], 0


def _row_specs(rows, n):
    arrs, specs, avals = [], [], []
    for arg in rows:
        arr, width, cb = _row_arg(arg)
        rb = arr.shape[0] // n
        arrs.append(arr)
        specs.append(pl.BlockSpec((rb, width), lambda i, cb=cb: (i, cb)))
        avals.append(jax.ShapeDtypeStruct((rb, width), F32))
    return arrs, specs, avals


def _par_specs(pars):
    specs = [pl.BlockSpec(p.shape, lambda i, nd=p.ndim: (0,) * nd) for p in pars]
    avals = [jax.ShapeDtypeStruct(p.shape, F32) for p in pars]
    return specs, avals


def rowwise(f, rows, pars, out_dtypes, *, tm, name):
    n = _row_arg(rows[0])[0].shape[0] // tm
    arrs, rspecs, ravals = _row_specs(rows, n)
    pspecs, pavals = _par_specs(pars)
    oavals = jax.eval_shape(f, *ravals, *pavals)
    nin = len(arrs) + len(pars)

    def body(*refs):
        outs = f(*[r[...].astype(F32) for r in refs[:nin]])
        for o_ref, o in zip(refs[nin:], outs):
            o_ref[...] = o.astype(o_ref.dtype)

    return pl.pallas_call(
        body, name=name, grid=(n,), in_specs=rspecs + pspecs,
        out_specs=[pl.BlockSpec(o.shape, lambda i: (i, 0)) for o in oavals],
        out_shape=[jax.ShapeDtypeStruct((n * o.shape[0], o.shape[1]), dt) for o, dt in zip(oavals, out_dtypes)],
        compiler_params=_params(("parallel",), VMEM_LIMIT),
    )(*arrs, *pars)


def rowwise_vjp(f, rows, pars, cots, row_grad, par_grad, d_dtypes, adds=None, *, tm, name):
    n = _row_arg(rows[0])[0].shape[0] // tm
    arrs, rspecs, ravals = _row_specs(rows, n)
    pspecs, pavals = _par_specs(pars)
    carrs, cspecs, _ = _row_specs(cots, n)
    adds = adds or [None] * len(row_grad)
    add_arrs = [a for a in adds if a is not None]
    _, aspecs, _ = _row_specs(add_arrs, n)
    nr, npar, nc, na = len(arrs), len(pars), len(carrs), len(add_arrs)
    diff = list(row_grad) + [nr + j for j in par_grad]
    ngr = len(row_grad)

    def body(*refs):
        ins = [r[...].astype(F32) for r in refs[:nr + npar]]
        cs = tuple(r[...].astype(F32) for r in refs[nr + npar:nr + npar + nc])
        add_refs = list(refs[nr + npar + nc:nr + npar + nc + na])
        outs = refs[nr + npar + nc + na:]

        def g(*d):
            full = list(ins)
            for idx, val in zip(diff, d):
                full[idx] = val
            return f(*full)

        _, pullback = jax.vjp(g, *[ins[idx] for idx in diff])
        ds = pullback(cs)
        for k in range(ngr):
            d = ds[k]
            if adds[k] is not None:
                d = d + add_refs.pop(0)[...].astype(F32)
            outs[k][...] = d.astype(outs[k].dtype)
        i = pl.program_id(0)
        for k in range(ngr, len(diff)):
            @pl.when(i == 0)
            def _(k=k):
                outs[k][...] = ds[k]

            @pl.when(i > 0)
            def _(k=k):
                outs[k][...] += ds[k]

    out_specs = [pl.BlockSpec(ravals[i].shape, lambda i: (i, 0)) for i in row_grad] + [pspecs[j] for j in par_grad]
    out_shape = [jax.ShapeDtypeStruct((arrs[i].shape[0], ravals[i].shape[1]), dt) for i, dt in zip(row_grad, d_dtypes)]
    out_shape += [jax.ShapeDtypeStruct(pars[j].shape, F32) for j in par_grad]
    return pl.pallas_call(
        body, name=name, grid=(n,), in_specs=rspecs + pspecs + cspecs + aspecs, out_specs=out_specs, out_shape=out_shape,
        compiler_params=_params(("arbitrary",), VMEM_LIMIT),
    )(*arrs, *pars, *carrs, *add_arrs)


def _logsig(x):
    return jnp.minimum(x, 0.0) - jnp.log(1.0 + jnp.exp(-jnp.abs(x)))


def _sigmoid(x):
    return 1.0 / (1.0 + jnp.exp(-x))


def _silu(x):
    return x * _sigmoid(x)


def _rms(x, g):
    return x * lax.rsqrt(jnp.mean(x * x, axis=-1, keepdims=True) + NORM_EPS) * g


def f_rms(x, g):
    return (_rms(x, g),)


def f_swiglu(a, b):
    return (_silu(a) * b,)


def f_half_post(x, f, g):
    return (x + 0.5 * _rms(f, g),)


def f_post(x, m, g):
    return (x + _rms(m, g),)


def f_glu(a, g):
    return (a * _sigmoid(g),)


def f_conv_ln(y, lg, lb):
    mu = jnp.mean(y, axis=-1, keepdims=True)
    var = jnp.mean(jnp.square(y - mu), axis=-1, keepdims=True)
    return (_silu((y - mu) * lax.rsqrt(var + NORM_EPS) * lg + lb),)


def f_merge(g0, g1, g2, b0, b1, b2):
    return (_sigmoid(g0) * b0 + _sigmoid(g1) * b1 + _sigmoid(g2) * b2,)


def f_gla_post(o, r, g):
    w = o.shape[1]
    hv = w // 4
    i = lax.broadcasted_iota(jnp.int32, (w, w), 0) // hv
    j = lax.broadcasted_iota(jnp.int32, (w, w), 1) // hv
    avg = jnp.where(i == j, 1.0 / hv, 0.0).astype(F32)
    ms = jnp.dot(o * o, avg, precision=HIGHEST, preferred_element_type=F32)
    return (o * lax.rsqrt(ms + NORM_EPS) * g * _silu(r),)


def f_gla_pre(q, k, lr, wa, ba):
    tm = q.shape[0]
    pre = jnp.dot(_mx_round(lr), _mx_round(wa), precision=HIGHEST, preferred_element_type=F32) + ba
    la = _logsig(pre) / GLA_TAU
    i = lax.broadcasted_iota(jnp.int32, (tm, tm), 0)
    j = lax.broadcasted_iota(jnp.int32, (tm, tm), 1)
    later = jnp.where((i // CHUNK == j // CHUNK) & (j > i), 1.0, 0.0).astype(F32)
    ci = lax.broadcasted_iota(jnp.int32, (tm // CHUNK, tm), 0)
    cj = lax.broadcasted_iota(jnp.int32, (tm // CHUNK, tm), 1) // CHUNK
    chunk_sum = jnp.where(ci == cj, 1.0, 0.0).astype(F32)
    to_end = jnp.dot(later, la, precision=HIGHEST, preferred_element_type=F32)
    lam = jnp.exp(jnp.dot(chunk_sum, la, precision=HIGHEST, preferred_element_type=F32))
    return q * GLA_SCALE, k * jnp.exp(to_end), lam


def _split_dot(x, u):
    hi = _mx(x)
    lo = _mx(x - hi.astype(F32))
    return jnp.dot(hi, u, preferred_element_type=F32) + jnp.dot(lo, u, preferred_element_type=F32)


def _tri(tq, tk):
    row = lax.broadcasted_iota(jnp.int32, (tq, tk), 0)
    col = lax.broadcasted_iota(jnp.int32, (tq, tk), 1)
    return row, col


def _sb_tile(qh, kh, valid, suf, run):
    z = lax.dot_general(qh, kh, NT, preferred_element_type=F32)
    lp = jnp.minimum(z, 0.0) - jnp.log(1.0 + jnp.exp(-jnp.abs(z)))
    lk = lp - z if valid is None else jnp.where(valid, lp - z, 0.0)
    inc = _split_dot(lk, suf)
    a = jnp.exp(lp + (inc - lk + run))
    if valid is not None:
        a = jnp.where(valid, a, 0.0)
    return lp, a, run + inc[:, 0:1]


def _sticks_left(run0, run1):
    return (jnp.maximum(jnp.max(run0), jnp.max(run1)) > -SB_CUTOFF).astype(jnp.int32)


def sb_attn_fwd(qkv, *, tq, name):
    S = qkv.shape[0]
    tk = tq
    pairs = BRANCH // 128

    def body(q_ref, k_ref, v_ref, o_ref):
        i = pl.program_id(1)
        row, col = _tri(tq, tk)
        suf = _mx(row >= col)
        q = q_ref[...] * SB_SCALE

        def tile(j, carry, valid):
            ks = pl.multiple_of((i - j) * tk, tk)
            kb = k_ref[pl.ds(ks, tk), :]
            vb = v_ref[pl.ds(ks, tk), :]
            new = []
            for h in range(2):
                acc, run = carry[h]
                sl = slice(64 * h, 64 * h + 64)
                _, a, run = _sb_tile(q[:, sl], kb[:, sl], valid, suf, run)
                acc = acc + jnp.dot(_mx(a), vb[:, sl], preferred_element_type=F32)
                new.append((acc, run))
            return _sticks_left(new[0][1], new[1][1]), tuple(new)

        def step(state):
            j, _, carry = state
            return (j + 1, *tile(j, carry, None))

        zero = (jnp.zeros((tq, 64), F32), jnp.zeros((tq, 1), F32))
        first = tile(0, (zero, zero), col < row)
        _, _, res = lax.while_loop(lambda s: (s[0] <= i) & (s[1] > 0), step, (jnp.int32(1), *first))
        o_ref[...] = jnp.concatenate([res[0][0], res[1][0]], axis=1).astype(o_ref.dtype)

    return pl.pallas_call(
        body, name=name, grid=(pairs, S // tq),
        in_specs=[pl.BlockSpec((tq, 128), lambda p, i: (i, p)),
                  pl.BlockSpec((S, 128), lambda p, i: (0, pairs + p)),
                  pl.BlockSpec((S, 128), lambda p, i: (0, 2 * pairs + p))],
        out_specs=pl.BlockSpec((tq, 128), lambda p, i: (i, p)),
        out_shape=jax.ShapeDtypeStruct((S, BRANCH), MXU_DTYPE),
        compiler_params=_params(("parallel", "parallel"), VMEM_LIMIT),
    )(qkv, qkv, qkv)


def sb_attn_bwd(qkv, do, *, tq, name):
    S = qkv.shape[0]
    tk = tq
    nq = S // tq
    pairs = BRANCH // 128

    def body(q_ref, k_ref, v_ref, do_ref, dq_ref, dk_ref, dv_ref, g_sc, b_sc):
        i = pl.program_id(1)

        @pl.when(i == 0)
        def _():
            dk_ref[...] = jnp.zeros_like(dk_ref)
            dv_ref[...] = jnp.zeros_like(dv_ref)

        row, col = _tri(tq, tk)
        suf = _mx(row >= col)
        pre = _mx(row <= col)
        q = q_ref[...] * SB_SCALE
        dout = do_ref[...]

        def tile1(j, carry, valid):
            kblk = i - j
            ks = pl.multiple_of(kblk * tk, tk)
            kb = k_ref[pl.ds(ks, tk), :]
            vb = v_ref[pl.ds(ks, tk), :]
            runs, dvs = [], []
            for h in range(2):
                sl = slice(64 * h, 64 * h + 64)
                lp, a, run = _sb_tile(q[:, sl], kb[:, sl], valid, suf, carry[h])
                da = lax.dot_general(dout[:, sl], vb[:, sl], NT, preferred_element_type=F32)
                beta = jnp.exp(lp)
                g_sc[h, kblk] = (a * da).astype(g_sc.dtype)
                b_sc[h, kblk] = (beta if valid is None else jnp.where(valid, beta, 0.0)).astype(b_sc.dtype)
                dvs.append(lax.dot_general(_mx(a), dout[:, sl], TN, preferred_element_type=F32))
                runs.append(run)
            dv_ref[pl.ds(ks, tk), :] += jnp.concatenate(dvs, axis=1)
            return _sticks_left(runs[0], runs[1]), tuple(runs)

        def sweep1(state):
            j, _, carry = state
            return (j + 1, *tile1(j, carry, None))

        first = tile1(0, (jnp.zeros((tq, 1), F32), jnp.zeros((tq, 1), F32)), col < row)
        tiles, _, _ = lax.while_loop(lambda s: (s[0] <= i) & (s[1] > 0), sweep1, (jnp.int32(1), *first))

        def sweep2(kblk, carry):
            ks = pl.multiple_of(kblk * tk, tk)
            kb = k_ref[pl.ds(ks, tk), :]
            new, dks = [], []
            for h in range(2):
                dq, run = carry[h]
                sl = slice(64 * h, 64 * h + 64)
                g = g_sc[h, kblk]
                beta = b_sc[h, kblk].astype(F32)
                inc = jnp.dot(g, pre, preferred_element_type=F32)
                g = g.astype(F32)
                dz = _mx(g - beta * (inc + run))
                dq = dq + jnp.dot(dz, kb[:, sl], preferred_element_type=F32)
                dks.append(lax.dot_general(dz, q[:, sl], TN, preferred_element_type=F32))
                new.append((dq, run + inc[:, tk - 1:tk]))
            dk_ref[pl.ds(ks, tk), :] += jnp.concatenate(dks, axis=1)
            return tuple(new)

        zero = (jnp.zeros((tq, 64), F32), jnp.zeros((tq, 1), F32))
        res = lax.fori_loop(i + 1 - tiles, i + 1, sweep2, (zero, zero))
        dq_ref[...] = (jnp.concatenate([res[0][0], res[1][0]], axis=1) * SB_SCALE).astype(dq_ref.dtype)

    return pl.pallas_call(
        body, name=name, grid=(pairs, nq),
        in_specs=[pl.BlockSpec((tq, 128), lambda p, i: (i, p)),
                  pl.BlockSpec((S, 128), lambda p, i: (0, pairs + p)),
                  pl.BlockSpec((S, 128), lambda p, i: (0, 2 * pairs + p)),
                  pl.BlockSpec((tq, 128), lambda p, i: (i, p))],
        out_specs=[pl.BlockSpec((tq, 128), lambda p, i: (i, p)),
                   pl.BlockSpec((S, 128), lambda p, i: (0, p)),
                   pl.BlockSpec((S, 128), lambda p, i: (0, p))],
        out_shape=[jax.ShapeDtypeStruct((S, BRANCH), MXU_DTYPE), jax.ShapeDtypeStruct((S, BRANCH), F32),
                   jax.ShapeDtypeStruct((S, BRANCH), F32)],
        scratch_shapes=[pltpu.VMEM((2, nq, tq, tk), MXU_DTYPE), pltpu.VMEM((2, nq, tq, tk), MXU_DTYPE)],
        compiler_params=_params(("parallel", "arbitrary"), VMEM_LIMIT),
    )(qkv, qkv, qkv, do)


def conv_fwd(u, w, b, *, tm, name):
    S, C = u.shape
    hb = tm // CONV_PAD

    def body(u_ref, halo_ref, w_ref, b_ref, y_ref, buf):
        i = pl.program_id(0)
        buf[pl.ds(CONV_PAD, tm), :] = u_ref[...]
        buf[pl.ds(0, CONV_PAD), :] = jnp.where(i > 0, halo_ref[...], 0.0)
        ts = min(tm, CONV_SUB)
        for r in range(0, tm, ts):
            acc = jnp.broadcast_to(b_ref[...], (ts, 128))
            for j in range(CONV_WIDTH):
                acc = acc + buf[pl.ds(r + CONV_PAD - (CONV_WIDTH - 1) + j, ts), :] * w_ref[pl.ds(j, 1), :]
            y_ref[pl.ds(r, ts), :] = acc

    return pl.pallas_call(
        body, name=name, grid=(S // tm, C // 128),
        in_specs=[pl.BlockSpec((tm, 128), lambda i, c: (i, c)),
                  pl.BlockSpec((CONV_PAD, 128), lambda i, c: (jnp.maximum(i * hb - 1, 0), c)),
                  pl.BlockSpec((CONV_PAD, 128), lambda i, c: (0, c)),
                  pl.BlockSpec((1, 128), lambda i, c: (0, c))],
        out_specs=pl.BlockSpec((tm, 128), lambda i, c: (i, c)),
        out_shape=jax.ShapeDtypeStruct((S, C), F32),
        scratch_shapes=[pltpu.VMEM((tm + CONV_PAD, 128), F32)],
        compiler_params=_params(("parallel", "parallel")),
    )(u, u, w, b)


def conv_bwd(dy, u, w, *, tm, name):
    S, C = u.shape
    hb = tm // CONV_PAD
    n = S // tm

    def body(dy_ref, dyn_ref, u_ref, up_ref, w_ref, du_ref, dw_ref, bufy, bufu):
        i = pl.program_id(1)
        dyv = dy_ref[...]
        bufy[pl.ds(0, tm), :] = dyv
        bufy[pl.ds(tm, CONV_PAD), :] = jnp.where(i < n - 1, dyn_ref[...], 0.0)
        bufu[pl.ds(CONV_PAD, tm), :] = u_ref[...]
        bufu[pl.ds(0, CONV_PAD), :] = jnp.where(i > 0, up_ref[...], 0.0)

        @pl.when(i == 0)
        def _():
            dw_ref[...] = jnp.zeros_like(dw_ref)

        ts = min(tm, CONV_SUB)
        for r in range(0, tm, ts):
            dys = dy_ref[pl.ds(r, ts), :]
            acc = jnp.zeros((ts, 128), F32)
            for j in range(CONV_WIDTH):
                acc = acc + bufy[pl.ds(r + CONV_WIDTH - 1 - j, ts), :] * w_ref[pl.ds(j, 1), :]
                shifted = bufu[pl.ds(r + CONV_PAD - (CONV_WIDTH - 1) + j, ts), :]
                dw_ref[pl.ds(j, 1), :] += jnp.sum(dys * shifted, axis=0, keepdims=True)
            du_ref[pl.ds(r, ts), :] = acc
            dw_ref[pl.ds(CONV_WIDTH, 1), :] += jnp.sum(dys, axis=0, keepdims=True)

    return pl.pallas_call(
        body, name=name, grid=(C // 128, n),
        in_specs=[pl.BlockSpec((tm, 128), lambda c, i: (i, c)),
                  pl.BlockSpec((CONV_PAD, 128), lambda c, i: (jnp.minimum((i + 1) * hb, n * hb - 1), c)),
                  pl.BlockSpec((tm, 128), lambda c, i: (i, c)),
                  pl.BlockSpec((CONV_PAD, 128), lambda c, i: (jnp.maximum(i * hb - 1, 0), c)),
                  pl.BlockSpec((CONV_PAD, 128), lambda c, i: (0, c))],
        out_specs=[pl.BlockSpec((tm, 128), lambda c, i: (i, c)), pl.BlockSpec((CONV_PAD, 128), lambda c, i: (0, c))],
        out_shape=[jax.ShapeDtypeStruct((S, C), F32), jax.ShapeDtypeStruct((CONV_PAD, C), F32)],
        scratch_shapes=[pltpu.VMEM((tm + CONV_PAD, 128), F32), pltpu.VMEM((tm + CONV_PAD, 128), F32)],
        compiler_params=_params(("parallel", "arbitrary")),
    )(dy, dy, u, u, w)


GLA_HEADS, GLA_DK, GLA_DV = 4, 64, 128


def gla_scan_fwd(qs, kd, gl, lam, *, tm, name):
    S = qs.shape[0]
    nc = tm // CHUNK

    def body(qs_ref, kd_ref, v_ref, lam_ref, o_ref, st_ref, state):
        @pl.when(pl.program_id(0) == 0)
        def _():
            state[...] = jnp.zeros_like(state)

        for c in range(nc):
            rows = pl.ds(c * CHUNK, CHUNK)
            q, k, v = _mx(qs_ref[rows, :]), _mx(kd_ref[rows, :]), _mx(v_ref[rows, :])
            upd = [lax.dot_general(v[:, h * GLA_DV:(h + 1) * GLA_DV], k[:, h * GLA_DK:(h + 1) * GLA_DK], TN,
                                   preferred_element_type=F32) for h in range(GLA_HEADS)]
            st = state[...] * lam_ref[pl.ds(c, 1), :] + jnp.concatenate(upd, axis=1)
            state[...] = st
            st_ref[c] = st
            stm = _mx(st)
            o = [lax.dot_general(q[:, h * GLA_DK:(h + 1) * GLA_DK], stm[:, h * GLA_DK:(h + 1) * GLA_DK], NT,
                                 preferred_element_type=F32) for h in range(GLA_HEADS)]
            o_ref[rows, :] = jnp.concatenate(o, axis=1)

    dk_all = GLA_HEADS * GLA_DK
    dv_all = GLA_HEADS * GLA_DV
    return pl.pallas_call(
        body, name=name, grid=(S // tm,),
        in_specs=[pl.BlockSpec((tm, dk_all), lambda i: (i, 0)), pl.BlockSpec((tm, dk_all), lambda i: (i, 0)),
                  pl.BlockSpec((tm, dv_all), lambda i: (i, 1)), pl.BlockSpec((nc, dk_all), lambda i: (i, 0))],
        out_specs=[pl.BlockSpec((tm, dv_all), lambda i: (i, 0)), pl.BlockSpec((nc, GLA_DV, dk_all), lambda i: (i, 0, 0))],
        out_shape=[jax.ShapeDtypeStruct((S, dv_all), F32), jax.ShapeDtypeStruct((S // CHUNK, GLA_DV, dk_all), F32)],
        scratch_shapes=[pltpu.VMEM((GLA_DV, dk_all), F32)],
        compiler_params=_params(("arbitrary",)),
    )(qs, kd, gl, lam)


def gla_scan_bwd(do, qs, kd, gl, lam, states, *, tm, name):
    S = qs.shape[0]
    nc = tm // CHUNK
    n = S // tm
    dk_all = GLA_HEADS * GLA_DK
    dv_all = GLA_HEADS * GLA_DV

    def body(do_ref, qs_ref, kd_ref, v_ref, lam_ref, st_ref, prev_ref, dqs_ref, dkd_ref, dv_ref, dlam_ref, carry):
        i = pl.program_id(0)

        @pl.when(i == 0)
        def _():
            carry[...] = jnp.zeros_like(carry)

        for c in reversed(range(nc)):
            rows = pl.ds(c * CHUNK, CHUNK)
            q, k, v, d = _mx(qs_ref[rows, :]), _mx(kd_ref[rows, :]), _mx(v_ref[rows, :]), _mx(do_ref[rows, :])
            st = _mx(st_ref[c])
            before = st_ref[c - 1] if c > 0 else jnp.where(i < n - 1, prev_ref[0], 0.0)
            outer = [lax.dot_general(d[:, h * GLA_DV:(h + 1) * GLA_DV], q[:, h * GLA_DK:(h + 1) * GLA_DK], TN,
                                     preferred_element_type=F32) for h in range(GLA_HEADS)]
            dst = carry[...] + jnp.concatenate(outer, axis=1)
            dstm = _mx(dst)
            dq, dkk, dvv = [], [], []
            for h in range(GLA_HEADS):
                ksl = slice(h * GLA_DK, (h + 1) * GLA_DK)
                vsl = slice(h * GLA_DV, (h + 1) * GLA_DV)
                dq.append(jnp.dot(d[:, vsl], st[:, ksl], preferred_element_type=F32))
                dkk.append(jnp.dot(v[:, vsl], dstm[:, ksl], preferred_element_type=F32))
                dvv.append(lax.dot_general(k[:, ksl], dstm[:, ksl], NT, preferred_element_type=F32))
            dqs_ref[rows, :] = jnp.concatenate(dq, axis=1)
            dkd_ref[rows, :] = jnp.concatenate(dkk, axis=1)
            dv_ref[rows, :] = jnp.concatenate(dvv, axis=1)
            dlam_ref[pl.ds(c, 1), :] = jnp.sum(dst * before, axis=0, keepdims=True)
            carry[...] = dst * lam_ref[pl.ds(c, 1), :]

    rev = lambda i: n - 1 - i
    return pl.pallas_call(
        body, name=name, grid=(n,),
        in_specs=[pl.BlockSpec((tm, dv_all), lambda i: (rev(i), 0)), pl.BlockSpec((tm, dk_all), lambda i: (rev(i), 0)),
                  pl.BlockSpec((tm, dk_all), lambda i: (rev(i), 0)), pl.BlockSpec((tm, dv_all), lambda i: (rev(i), 1)),
                  pl.BlockSpec((nc, dk_all), lambda i: (rev(i), 0)),
                  pl.BlockSpec((nc, GLA_DV, dk_all), lambda i: (rev(i), 0, 0)),
                  pl.BlockSpec((1, GLA_DV, dk_all), lambda i: (jnp.maximum(rev(i) * nc - 1, 0), 0, 0))],
        out_specs=[pl.BlockSpec((tm, dk_all), lambda i: (rev(i), 0)), pl.BlockSpec((tm, dk_all), lambda i: (rev(i), 0)),
                   pl.BlockSpec((tm, dv_all), lambda i: (rev(i), 0)), pl.BlockSpec((nc, dk_all), lambda i: (rev(i), 0))],
        out_shape=[jax.ShapeDtypeStruct((S, dk_all), F32), jax.ShapeDtypeStruct((S, dk_all), F32),
                   jax.ShapeDtypeStruct((S, dv_all), F32), jax.ShapeDtypeStruct((S // CHUNK, dk_all), F32)],
        scratch_shapes=[pltpu.VMEM((GLA_DV, dk_all), F32)],
        compiler_params=_params(("arbitrary",)),
    )(do, qs, kd, gl, lam, states, states)


def loss_head(y, target, *, tm, name):
    S, D = y.shape

    def body(y_ref, t_ref, dy_ref, sq_ref):
        err = y_ref[...] - t_ref[...]
        dy_ref[...] = err * (1.0 / D)
        part = jnp.sum(err * err, axis=0, keepdims=True)

        @pl.when(pl.program_id(0) == 0)
        def _():
            sq_ref[...] = part

        @pl.when(pl.program_id(0) > 0)
        def _():
            sq_ref[...] += part

    return pl.pallas_call(
        body, name=name, grid=(S // tm,),
        in_specs=[pl.BlockSpec((tm, D), lambda i: (i, 0)), pl.BlockSpec((tm, D), lambda i: (i, 0))],
        out_specs=[pl.BlockSpec((tm, D), lambda i: (i, 0)), pl.BlockSpec((1, D), lambda i: (0, 0))],
        out_shape=[jax.ShapeDtypeStruct((S, D), F32), jax.ShapeDtypeStruct((1, D), F32)],
        compiler_params=_params(("arbitrary",)),
    )(y, target)


def f_adamw(w, g, m, v):
    m = ADAM_B1 * m + (1.0 - ADAM_B1) * g
    v = ADAM_B2 * v + (1.0 - ADAM_B2) * jnp.square(g)
    m_hat = m / (1.0 - ADAM_B1 ** ADAM_STEP)
    v_hat = v / (1.0 - ADAM_B2 ** ADAM_STEP)
    return -ADAM_LR * (m_hat / (jnp.sqrt(v_hat) + ADAM_EPS) + ADAM_WD * w), m, v


def adamw(w, g, m, v, *, name):
    shape = w.shape
    cols = shape[-1]
    rows = w.size // cols
    tm = rows
    while tm % 16 == 0 and tm * cols * 4 > (1 << 20):
        tm //= 2
    flat = [t.reshape(rows, cols) for t in (w, g, m, v)]
    outs = rowwise(f_adamw, flat, [], [F32, F32, F32], tm=tm, name=name)
    return [o.reshape(shape) for o in outs]


def _place():
    x, y, c = lax.axis_index("x"), lax.axis_index("y"), lax.axis_index("c")
    return x, y, c, [(1 - x, y), (x, 1 - y), (1 - x, 1 - y)]


def _my_chip():
    return 2 * lax.axis_index("x") + lax.axis_index("y")


def _any():
    return pl.BlockSpec(memory_space=pl.ANY)


DMA_CHUNKS = 8


def _chunks(rows):
    n = DMA_CHUNKS
    while n > 1 and rows % (n * 16):
        n //= 2
    return [(k * (rows // n), rows // n) for k in range(n)]


def _start_chunked(make, rows):
    for off, size in _chunks(rows):
        make(off, size).start()


def all_gather_chips(v, *, name):
    R, C = v.shape
    H = R // 2
    Q = H // 2
    assert R % 64 == 0, R

    def body(v_ref, o_ref, send, recv, psend, precv, fsend, frecv):
        x, y, c, chips = _place()
        me = 2 * x + y
        ids = [2 * px + py for px, py in chips]
        mine = pl.ds(pl.multiple_of(c * H, 8), H)
        other = pl.ds(pl.multiple_of((1 - c) * H, 8), H)

        def my_rows(off, size):
            return pl.ds(pl.multiple_of(c * H + off, 8), size)

        def cross(j, src_chip, rows):
            return pltpu.make_async_remote_copy(
                src_ref=v_ref.at[rows], dst_ref=o_ref.at[src_chip, rows], send_sem=send.at[j], recv_sem=recv.at[j],
                device_id=(*chips[j], c), device_id_type=MESH)

        def passed(j, off, size):
            rows = my_rows(j * Q + off, size)
            src_chip = ids[1 - j]
            return pltpu.make_async_remote_copy(
                src_ref=o_ref.at[src_chip, rows], dst_ref=o_ref.at[src_chip, rows], send_sem=psend.at[j],
                recv_sem=precv.at[j], device_id=(*chips[j], c), device_id_type=MESH)

        def landed(j):
            rows = my_rows(j * Q, Q)
            return pltpu.make_async_remote_copy(
                src_ref=o_ref.at[ids[2], rows], dst_ref=o_ref.at[ids[2], rows], send_sem=psend.at[j],
                recv_sem=precv.at[j], device_id=(*chips[j], c), device_id_type=MESH)

        def handed(j, rows):
            return pltpu.make_async_remote_copy(
                src_ref=o_ref.at[ids[j], rows], dst_ref=o_ref.at[ids[j], rows], send_sem=fsend.at[j],
                recv_sem=frecv.at[j], device_id=(x, y, 1 - c), device_id_type=MESH)

        for j in range(2):
            _start_chunked(lambda off, size, j=j: cross(j, me, my_rows(off, size)), H)
        for j in (1, 0):
            cross(j, ids[j], mine).wait_recv()
            _start_chunked(functools.partial(passed, 1 - j), Q)
            _start_chunked(lambda off, size, j=j: handed(j, my_rows(off, size)), H)
        for j in range(2):
            landed(j).wait_recv()
        _start_chunked(lambda off, size: handed(2, my_rows(off, size)), H)
        for j in range(3):
            handed(j, other).wait_recv()
        for j in range(2):
            cross(j, me, mine).wait_send()
            passed(j, 0, Q).wait_send()
        for j in range(3):
            handed(j, mine).wait_send()

    got = pl.pallas_call(
        body, name=name, in_specs=[_any()], out_specs=_any(), out_shape=jax.ShapeDtypeStruct((4, R, C), v.dtype),
        scratch_shapes=[pltpu.SemaphoreType.DMA((2,)), pltpu.SemaphoreType.DMA((2,)), pltpu.SemaphoreType.DMA((2,)),
                        pltpu.SemaphoreType.DMA((2,)), pltpu.SemaphoreType.DMA((3,)), pltpu.SemaphoreType.DMA((3,))],
    )(v)
    return lax.dynamic_update_slice(got, v[None], (_my_chip(), 0, 0))


def swap_halves(g, *, name):
    n, R, C = g.shape
    H = R // 2

    def body(g_ref, theirs_ref, send, recv):
        x, y, c, _ = _place()

        def give(q, off, size):
            return pltpu.make_async_remote_copy(
                src_ref=g_ref.at[q, pl.ds(pl.multiple_of((1 - c) * H + off, 8), size)],
                dst_ref=theirs_ref.at[q, pl.ds(off, size)], send_sem=send.at[q], recv_sem=recv.at[q],
                device_id=(x, y, 1 - c), device_id_type=MESH)

        for q in range(n):
            _start_chunked(functools.partial(give, q), H)
        for q in range(n):
            give(q, 0, H).wait()

    return pl.pallas_call(
        body, name=name, in_specs=[_any()], out_specs=_any(), out_shape=jax.ShapeDtypeStruct((n, H, C), g.dtype),
        scratch_shapes=[pltpu.SemaphoreType.DMA((n,)), pltpu.SemaphoreType.DMA((n,))],
    )(g)


def exchange_pieces(p, *, name):
    H = p.shape[1]

    def body(p_ref, b_ref, send, recv):
        x, y, c, chips = _place()
        me = 2 * x + y
        ids = [2 * px + py for px, py in chips]

        def cross(j, piece, slot, off, size):
            return pltpu.make_async_remote_copy(
                src_ref=p_ref.at[piece, pl.ds(off, size)], dst_ref=b_ref.at[slot, pl.ds(off, size)], send_sem=send.at[j],
                recv_sem=recv.at[j], device_id=(*chips[j], c), device_id_type=MESH)

        for j in range(3):
            _start_chunked(functools.partial(cross, j, ids[j], me), H)
        for j in range(3):
            cross(j, me, ids[j], 0, H).wait_recv()
        for j in range(3):
            cross(j, ids[j], me, 0, H).wait_send()

    got = pl.pallas_call(
        body, name=name, in_specs=[_any()], out_specs=_any(), out_shape=jax.ShapeDtypeStruct(p.shape, p.dtype),
        scratch_shapes=[pltpu.SemaphoreType.DMA((3,)), pltpu.SemaphoreType.DMA((3,))],
    )(p)
    me = _my_chip()
    return lax.dynamic_update_slice(got, lax.dynamic_slice_in_dim(p, me, 1, axis=0), (me, 0, 0))


def join_halves(f, *, name):
    H, C = f.shape

    def body(f_ref, o_ref, send, recv):
        x, y, c, _ = _place()

        def give(off, size):
            return pltpu.make_async_remote_copy(
                src_ref=f_ref.at[pl.ds(off, size)], dst_ref=o_ref.at[pl.ds(pl.multiple_of(c * H + off, 8), size)],
                send_sem=send, recv_sem=recv, device_id=(x, y, 1 - c), device_id_type=MESH)

        _start_chunked(give, H)
        give(0, H).wait()

    both = pl.pallas_call(
        body, name=name, in_specs=[_any()], out_specs=_any(), out_shape=jax.ShapeDtypeStruct((2 * H, C), f.dtype),
        scratch_shapes=[pltpu.SemaphoreType.DMA, pltpu.SemaphoreType.DMA],
    )(f)
    return lax.dynamic_update_slice(both, f, (lax.axis_index("c") * H, 0))


def _row_tile(rows, want):
    tm = want
    while rows % tm:
        tm //= 2
    return tm


def _add_pair(g, theirs, out_dtype, *, name):
    n, R, C = g.shape
    H = R // 2
    tm = _row_tile(H, 512)
    nb = H // tm

    def body(c_ref, g_ref, t_ref, o_ref):
        o_ref[...] = (g_ref[...] + t_ref[...]).astype(o_ref.dtype)

    grid_spec = pltpu.PrefetchScalarGridSpec(
        num_scalar_prefetch=1, grid=(n, nb),
        in_specs=[pl.BlockSpec((None, tm, C), lambda q, i, c: (q, c[0] * nb + i, 0)),
                  pl.BlockSpec((None, tm, C), lambda q, i, c: (q, i, 0))],
        out_specs=pl.BlockSpec((None, tm, C), lambda q, i, c: (q, i, 0)))
    return pl.pallas_call(
        body, name=name, grid_spec=grid_spec, out_shape=jax.ShapeDtypeStruct((n, H, C), out_dtype),
        compiler_params=_params(("parallel", "parallel")),
    )(lax.axis_index("c").astype(jnp.int32).reshape(1), g, theirs)


def _add_stack(b, *, name):
    n, H, C = b.shape
    tm = _row_tile(H, 256)

    def body(b_ref, o_ref):
        s = b_ref[0].astype(F32)
        for k in range(1, n):
            s = s + b_ref[k].astype(F32)
        o_ref[...] = s

    return pl.pallas_call(
        body, name=name, grid=(H // tm,), in_specs=[pl.BlockSpec((n, tm, C), lambda i: (0, i, 0))],
        out_specs=pl.BlockSpec((tm, C), lambda i: (i, 0)), out_shape=jax.ShapeDtypeStruct((H, C), F32),
        compiler_params=_params(("parallel",)),
    )(b)


def reduce_scatter(g, tag):
    pair = _add_pair(g, swap_halves(g, name=tag + "_swap_halves"), GRAD_WIRE_DTYPE, name=tag + "_add_pair")
    total = _add_stack(exchange_pieces(pair, name=tag + "_exchange"), name=tag + "_add_chips")
    return join_halves(total, name=tag + "_join")


FFN = ("ffn1_w_gate", "ffn1_w_up", "ffn2_w_gate", "ffn2_w_up", "ffn1_w_down", "ffn2_w_down")
BIG = ("w_in", "w_branch", "w_out")
SMALL = ("norm_pre", "norm_post", "conv_w", "gla_w_alpha")
REPLICATED = ("conv_b", "conv_ln_g", "conv_ln_b", "gla_b_alpha", "gla_norm_g")
WEIGHTS = ("norm_pre", "norm_post", "ffn1_w_gate", "ffn1_w_up", "ffn1_w_down", "ffn2_w_gate", "ffn2_w_up", "ffn2_w_down",
           "w_in", "conv_w", "conv_b", "conv_ln_g", "conv_ln_b", "gla_w_alpha", "gla_b_alpha", "gla_norm_g", "w_branch",
           "w_out")
SHARD_AXIS = {"ffn1_w_gate": 2, "ffn1_w_up": 2, "ffn1_w_down": 1, "ffn2_w_gate": 2, "ffn2_w_up": 2, "ffn2_w_down": 1,
              "w_in": 2, "w_branch": 3, "w_out": 1, "norm_pre": 2, "norm_post": 2, "conv_w": 2, "gla_w_alpha": 2}


def _size(shape):
    n = 1
    for d in shape:
        n *= d
    return n


def _pack_rows(shape):
    return -(-_size(shape) // (16 * PACK_COLS)) * 16


def _pack(arrs, dtype, row_mult):
    parts, rows = [], 0
    for a in arrs:
        r = _pack_rows(a.shape)
        flat = a.astype(dtype).reshape(-1)
        if r * PACK_COLS != flat.shape[0]:
            flat = jnp.pad(flat, (0, r * PACK_COLS - flat.shape[0]))
        parts.append(flat.reshape(r, PACK_COLS))
        rows += r
    if rows % row_mult:
        parts.append(jnp.zeros((row_mult - rows % row_mult, PACK_COLS), dtype))
    return jnp.concatenate(parts, axis=0)


def _unpack(buf, shapes):
    lead = buf.shape[:-2]
    out, off = [], 0
    for s in shapes:
        r = _pack_rows(s)
        part = lax.slice_in_dim(buf, off, off + r, axis=buf.ndim - 2).reshape(lead + (r * PACK_COLS,))
        if r * PACK_COLS != _size(s):
            part = lax.slice_in_dim(part, 0, _size(s), axis=part.ndim - 1)
        out.append(part.reshape(lead + tuple(s)))
        off += r
    return out


def gather_weights(shards, names, dtype, row_mult, name):
    packed = _pack([shards[k] for k in names], dtype, row_mult)
    parts = _unpack(all_gather_chips(packed, name=name), [shards[k].shape for k in names])
    return {k: jnp.concatenate([p[q] for q in range(4)], axis=SHARD_AXIS[k]) for k, p in zip(names, parts)}


def scatter_grads(grads, shards):
    sharded = BIG + SMALL
    pieces = []
    for q in range(4):
        part = []
        for k in sharded:
            w = shards[k].shape[SHARD_AXIS[k]]
            part.append(lax.slice_in_dim(grads[k], q * w, (q + 1) * w, axis=SHARD_AXIS[k]))
        part += [grads[k] for k in REPLICATED]
        pieces.append(_pack(part, F32, 512))
    total = reduce_scatter(jnp.stack(pieces), "rs_mix")
    names = sharded + REPLICATED
    return dict(zip(names, _unpack(total, [shards[k].shape for k in names])))


TM = 512
TM_MERGE = 256
TM_CONV = 1024
TM_GLA = 512
TQ = 256

IN_SB, IN_CONV, IN_GLA, IN_LR, IN_GATE = 0, 1536, 2560, 4096, 4112
IN_END = 7184


def layer_weights(full, l):
    w = {}
    win = full["w_in"][l]
    w["in_sb"] = win[:, IN_SB:IN_CONV]
    w["in_conv"] = win[:, IN_CONV:IN_GLA]
    w["in_gla"] = win[:, IN_GLA:IN_LR]
    w["in_lr"] = jnp.pad(win[:, IN_LR:IN_GATE], ((0, 0), (0, GLA_RANK_PAD - GLA_RANK)))
    w["in_gate"] = win[:, IN_GATE:IN_END]
    w["branch"] = [full["w_branch"][l, g] for g in range(3)]
    w["out"] = full["w_out"][l]
    return w


def ffn_fwd(x, gpre, gpost, w6, f, l, tag):
    S = x.shape[0]
    h = rowwise(f_rms, [x], [gpre], [MXU_DTYPE], tm=TM, name=tag + "_pre")[0]
    ab = slab_nt(h, w6, 2 * f, 8, l, out_dtype=MXU_DTYPE, name=tag + "_gu").reshape(2, 4, S, SLAB)
    out, x2 = slab_nn(ab, w6, 4 + f, l, gated=True, tail=("half_post", x, gpost), name=tag + "_down")
    return x2, (x, h, ab, out)


def ffn_bwd(dx2, res, gpre, gpost, w6, g6, f, l, tag):
    x, h, ab, out = res
    S = x.shape[0]
    df, dgpost = rowwise_vjp(f_half_post, [x, out], [gpost], [dx2], [1], [0], [MXU_DTYPE], tm=TM, name=tag + "_post_b")
    dab = slab_nt_swiglu_bwd(df, w6, 4 + f, l, ab, name=tag + "_down_dx").reshape(8, S, SLAB)
    g6 = slab_tn(ab, df, g6, 4 + f, l, gated=True, name=tag + "_down_dw")
    dx, dgpre = slab_nn(dab, w6, 2 * f, l, tail=("rms_bwd", x, gpre, dx2), name=tag + "_gu_dx")
    g6 = slab_tn(dab, h, g6, 2 * f, l, name=tag + "_gu_dw")
    return dx, dgpre, dgpost, g6


def mixer_fwd(x, p, w, tag):
    h = rowwise(f_rms, [x], [p["gpre"]], [MXU_DTYPE], tm=TM, name=tag + "_pre")[0]
    qkv = mm(h, w["in_sb"], out_dtype=MXU_DTYPE, tn=512, name=tag + "_in_sb")
    cv = mm(h, w["in_conv"], name=tag + "_in_conv")
    gl = mm(h, w["in_gla"], out_dtype=MXU_DTYPE, tn=512, name=tag + "_in_gla")
    lr = mm(h, w["in_lr"], out_dtype=MXU_DTYPE, name=tag + "_in_lr")
    gt = mm(h, w["in_gate"], out_dtype=MXU_DTYPE, name=tag + "_in_gate")
    sb = sb_attn_fwd(qkv, tq=TQ, name=tag + "_sb")
    u = rowwise(f_glu, [(cv, BRANCH, 0), (cv, BRANCH, 1)], [], [F32], tm=TM, name=tag + "_glu")[0]
    y = conv_fwd(u, p["conv_w"], p["conv_b"], tm=TM_CONV, name=tag + "_conv")
    cb = rowwise(f_conv_ln, [y], [p["ln_g"], p["ln_b"]], [MXU_DTYPE], tm=TM, name=tag + "_ln")[0]
    qs, kd, lam = rowwise(f_gla_pre, [(gl, 256, 0), (gl, 256, 1), lr], [p["wa"], p["ba"]], [MXU_DTYPE, MXU_DTYPE, F32],
                          tm=TM_GLA, name=tag + "_gla_pre")
    o, states = gla_scan_fwd(qs, kd, gl, lam, tm=TM_GLA, name=tag + "_gla_scan")
    gb = rowwise(f_gla_post, [o, (gl, BRANCH, 2)], [p["gn"]], [MXU_DTYPE], tm=TM, name=tag + "_gla_post")[0]
    branches = [sb, cb, gb]
    bd = [mm(branches[g], w["branch"][g], out_dtype=MXU_DTYPE, name=tag + f"_branch{g}") for g in range(3)]
    merged = rowwise(f_merge, [(gt, D_MODEL, 0), (gt, D_MODEL, 1), (gt, D_MODEL, 2)] + bd, [], [MXU_DTYPE], tm=TM_MERGE,
                     name=tag + "_merge")[0]
    m = mm(merged, w["out"], name=tag + "_out")
    x2 = rowwise(f_post, [x, m], [p["gpost"]], [F32], tm=TM, name=tag + "_post")[0]
    return x2, (x, h, qkv, cv, gl, lr, gt, u, y, qs, kd, lam, o, states, branches, bd, merged, m)


def mixer_bwd(dx2, res, p, w, tag):
    x, h, qkv, cv, gl, lr, gt, u, y, qs, kd, lam, o, states, branches, bd, merged, m = res
    g = {}
    dm, g["gpost"] = rowwise_vjp(f_post, [x, m], [p["gpost"]], [dx2], [1], [0], [MXU_DTYPE], tm=TM, name=tag + "_post_b")
    dmerged = mm(dm, w["out"], tb=True, out_dtype=MXU_DTYPE, name=tag + "_out_dx")
    g["out"] = mm(merged, dm, ta=True, name=tag + "_out_dw")
    gts = [(gt, D_MODEL, 0), (gt, D_MODEL, 1), (gt, D_MODEL, 2)]
    dgate = rowwise_vjp(f_merge, gts + bd, [], [dmerged], [0, 1, 2, 3, 4, 5], [], [MXU_DTYPE] * 6, tm=TM_MERGE,
                        name=tag + "_merge_b")
    dgt = jnp.concatenate(dgate[:3], axis=1)
    dbd = dgate[3:]
    g["branch"] = [mm(branches[k], dbd[k], ta=True, name=tag + f"_branch{k}_dw") for k in range(3)]
    dsb = mm(dbd[0], w["branch"][0], tb=True, out_dtype=MXU_DTYPE, name=tag + "_branch0_dx")
    dq, dk, dv = sb_attn_bwd(qkv, dsb, tq=TQ, name=tag + "_sb_b")
    dqkv = jnp.concatenate([dq, _mx(dk), _mx(dv)], axis=1)
    dcb = mm(dbd[1], w["branch"][1], tb=True, name=tag + "_branch1_dx")
    dy, g["ln_g"], g["ln_b"] = rowwise_vjp(f_conv_ln, [y], [p["ln_g"], p["ln_b"]], [dcb], [0], [0, 1], [F32], tm=TM,
                                           name=tag + "_ln_b")
    du, dwb = conv_bwd(dy, u, p["conv_w"], tm=TM_CONV, name=tag + "_conv_b")
    g["conv_w"], g["conv_b"] = dwb[:CONV_WIDTH], dwb[CONV_WIDTH:CONV_WIDTH + 1]
    dca, dcg = rowwise_vjp(f_glu, [(cv, BRANCH, 0), (cv, BRANCH, 1)], [], [du], [0, 1], [], [MXU_DTYPE, MXU_DTYPE], tm=TM,
                           name=tag + "_glu_b")
    dcv = jnp.concatenate([dca, dcg], axis=1)
    dgb = mm(dbd[2], w["branch"][2], tb=True, name=tag + "_branch2_dx")
    do, dr, g["gn"] = rowwise_vjp(f_gla_post, [o, (gl, BRANCH, 2)], [p["gn"]], [dgb], [0, 1], [0], [F32, MXU_DTYPE], tm=TM,
                                  name=tag + "_gla_post_b")
    dqs, dkd, dgv, dlam = gla_scan_bwd(do, qs, kd, gl, lam, states, tm=TM_GLA, name=tag + "_gla_scan_b")
    dgq, dgk, dlr, g["wa"], g["ba"] = rowwise_vjp(
        f_gla_pre, [(gl, 256, 0), (gl, 256, 1), lr], [p["wa"], p["ba"]], [dqs, dkd, dlam], [0, 1, 2], [0, 1],
        [MXU_DTYPE, MXU_DTYPE, MXU_DTYPE], tm=TM_GLA, name=tag + "_gla_pre_b")
    dgl = jnp.concatenate([dgq, dgk, _mx(dgv), dr], axis=1)
    secs = [("in_sb", dqkv), ("in_conv", dcv), ("in_gla", dgl), ("in_lr", dlr), ("in_gate", dgt)]
    dh = mm_sum_nt([(d, w[k]) for k, d in secs], name=tag + "_in_dx")
    for k, d in secs:
        g[k] = mm(h, d, ta=True, tn=1536, name=tag + "_" + k + "_dw")
    dx, g["gpre"] = rowwise_vjp(f_rms, [x], [p["gpre"]], [dh], [0], [0], [F32], [dx2], tm=TM, name=tag + "_pre_b")
    return dx, g


def kernel(x, norm_pre, norm_post, ffn1_w_gate, ffn1_w_up, ffn1_w_down, ffn2_w_gate, ffn2_w_up, ffn2_w_down, w_in, conv_w, conv_b, conv_ln_g, conv_ln_b, gla_w_alpha, gla_b_alpha, gla_norm_g, w_branch, w_out, loss_target, m_norm_pre, m_norm_post, m_ffn1_w_gate, m_ffn1_w_up, m_ffn1_w_down, m_ffn2_w_gate, m_ffn2_w_up, m_ffn2_w_down, m_w_in, m_conv_w, m_conv_b, m_conv_ln_g, m_conv_ln_b, m_gla_w_alpha, m_gla_b_alpha, m_gla_norm_g, m_w_branch, m_w_out, v_norm_pre, v_norm_post, v_ffn1_w_gate, v_ffn1_w_up, v_ffn1_w_down, v_ffn2_w_gate, v_ffn2_w_up, v_ffn2_w_down, v_w_in, v_conv_w, v_conv_b, v_conv_ln_g, v_conv_ln_b, v_gla_w_alpha, v_gla_b_alpha, v_gla_norm_g, v_w_branch, v_w_out):
    shards = dict(norm_pre=norm_pre, norm_post=norm_post, ffn1_w_gate=ffn1_w_gate, ffn1_w_up=ffn1_w_up,
                  ffn1_w_down=ffn1_w_down, ffn2_w_gate=ffn2_w_gate, ffn2_w_up=ffn2_w_up, ffn2_w_down=ffn2_w_down, w_in=w_in,
                  conv_w=conv_w, conv_b=conv_b, conv_ln_g=conv_ln_g, conv_ln_b=conv_ln_b, gla_w_alpha=gla_w_alpha,
                  gla_b_alpha=gla_b_alpha, gla_norm_g=gla_norm_g, w_branch=w_branch, w_out=w_out)
    mom_m = dict(zip(WEIGHTS, (m_norm_pre, m_norm_post, m_ffn1_w_gate, m_ffn1_w_up, m_ffn1_w_down, m_ffn2_w_gate,
                               m_ffn2_w_up, m_ffn2_w_down, m_w_in, m_conv_w, m_conv_b, m_conv_ln_g, m_conv_ln_b,
                               m_gla_w_alpha, m_gla_b_alpha, m_gla_norm_g, m_w_branch, m_w_out)))
    mom_v = dict(zip(WEIGHTS, (v_norm_pre, v_norm_post, v_ffn1_w_gate, v_ffn1_w_up, v_ffn1_w_down, v_ffn2_w_gate,
                               v_ffn2_w_up, v_ffn2_w_down, v_w_in, v_conv_w, v_conv_b, v_conv_ln_g, v_conv_ln_b,
                               v_gla_w_alpha, v_gla_b_alpha, v_gla_norm_g, v_w_branch, v_w_out)))
    depth = norm_pre.shape[0]
    members = [shards[k] if k.endswith("down") else jnp.swapaxes(shards[k], 1, 2) for k in FFN]
    ffn_rows = jnp.concatenate([_mx(t).reshape(depth * SLAB, D_MODEL) for t in members], axis=0)
    w6 = all_gather_chips(ffn_rows, name="gather_ffn").reshape(4, len(FFN), depth, SLAB, D_MODEL)
    full = gather_weights(shards, BIG, MXU_DTYPE, 256, "gather_big")
    full.update(gather_weights(shards, SMALL, F32, 16, "gather_small"))

    def layer_params(l):
        ffn = [dict(gpre=full["norm_pre"][l, k:k + 1], gpost=full["norm_post"][l, k:k + 1]) for k in (0, 2)]
        mix = dict(gpre=full["norm_pre"][l, 1:2], gpost=full["norm_post"][l, 1:2],
                   conv_w=jnp.pad(full["conv_w"][l], ((0, CONV_PAD - CONV_WIDTH), (0, 0))), conv_b=conv_b[l:l + 1],
                   ln_g=conv_ln_g[l:l + 1], ln_b=conv_ln_b[l:l + 1],
                   wa=jnp.pad(full["gla_w_alpha"][l], ((0, GLA_RANK_PAD - GLA_RANK), (0, 0))), ba=gla_b_alpha[l:l + 1],
                   gn=gla_norm_g[l:l + 1])
        return ffn, mix

    xs = x[0]
    saved = []
    for l in range(depth):
        w = layer_weights(full, l)
        ffn, mix = layer_params(l)
        xs, r1 = ffn_fwd(xs, ffn[0]["gpre"], ffn[0]["gpost"], w6, 0, l, f"l{l}_ffn1")
        xs, r2 = mixer_fwd(xs, mix, w, f"l{l}_mix")
        xs, r3 = ffn_fwd(xs, ffn[1]["gpre"], ffn[1]["gpost"], w6, 1, l, f"l{l}_ffn2")
        saved.append((w, ffn, mix, r1, r2, r3))
    dx, sq = loss_head(xs, loss_target[0], tm=TM, name="loss_head")
    loss = lax.psum(0.5 * jnp.sum(sq) / D_MODEL, ("x", "y", "c"))

    per_layer = []
    g6 = lax.empty(w6.shape, F32)
    for l in reversed(range(depth)):
        w, ffn, mix, r1, r2, r3 = saved[l]
        g = {}
        dx, gpre2, gpost2, g6 = ffn_bwd(dx, r3, ffn[1]["gpre"], ffn[1]["gpost"], w6, g6, 1, l, f"l{l}_ffn2")
        dx, gm = mixer_bwd(dx, r2, mix, w, f"l{l}_mix")
        dx, gpre0, gpost0, g6 = ffn_bwd(dx, r1, ffn[0]["gpre"], ffn[0]["gpost"], w6, g6, 0, l, f"l{l}_ffn1")
        g["norm_pre"] = jnp.concatenate([gpre0, gm["gpre"], gpre2], axis=0)
        g["norm_post"] = jnp.concatenate([gpost0, gm["gpost"], gpost2], axis=0)
        g["w_in"] = jnp.concatenate([gm["in_sb"], gm["in_conv"], gm["in_gla"], gm["in_lr"][:, :GLA_RANK], gm["in_gate"]],
                                    axis=1)
        g["conv_w"], g["conv_b"] = gm["conv_w"], gm["conv_b"][0]
        g["conv_ln_g"], g["conv_ln_b"] = gm["ln_g"][0], gm["ln_b"][0]
        g["gla_w_alpha"], g["gla_b_alpha"], g["gla_norm_g"] = gm["wa"][:GLA_RANK], gm["ba"][0], gm["gn"][0]
        g["w_branch"] = jnp.stack(gm["branch"])
        g["w_out"] = gm["out"]
        per_layer.append(g)
    per_layer.reverse()
    grads = {k: jnp.stack([g[k] for g in per_layer]) for k in BIG + SMALL + REPLICATED}

    grad_w = scatter_grads(grads, shards)
    ffn_sum = reduce_scatter(g6.reshape(4, len(FFN) * depth * SLAB, D_MODEL), "rs_ffn")
    ffn_sum = ffn_sum.reshape(len(FFN), depth, SLAB, D_MODEL)
    for t, k in enumerate(FFN):
        grad_w[k] = ffn_sum[t] if k.endswith("down") else jnp.swapaxes(ffn_sum[t], 1, 2)
    delta, new_m, new_v = {}, {}, {}
    for k in WEIGHTS:
        delta[k], new_m[k], new_v[k] = adamw(shards[k], grad_w[k], mom_m[k], mom_v[k], name="adamw_" + k)
    return (loss, dx[None], *[grad_w[k] for k in WEIGHTS], *[delta[k] for k in WEIGHTS], *[new_m[k] for k in WEIGHTS],
            *[new_v[k] for k in WEIGHTS])
```

```python
import functools

import jax
import jax.numpy as jnp
from jax import lax
from jax.experimental import pallas as pl
from jax.experimental.pallas import tpu as pltpu

F32 = jnp.float32
MXU_DTYPE = jnp.bfloat16
GRAD_WIRE_DTYPE = jnp.bfloat16
HIGHEST = lax.Precision.HIGHEST
MESH = pl.DeviceIdType.MESH

NORM_EPS = 1e-6
D_MODEL = 1024
D_FF = 2816
BRANCH = 512
CHUNK = 64
CONV_WIDTH = 31
CONV_PAD = 32
CONV_SUB = 256
GLA_RANK = 16
GLA_RANK_PAD = 128
GLA_TAU = 16.0
SB_SCALE = 0.125
SB_CUTOFF = 60.0
GLA_SCALE = 0.125
PACK_COLS = 1024
VMEM_LIMIT = 56 * 1024 * 1024

ADAM_LR, ADAM_B1, ADAM_B2, ADAM_EPS, ADAM_WD, ADAM_STEP = 0.001, 0.9, 0.999, 1e-08, 0.01, 10

NT = (((1,), (1,)), ((), ()))
TN = (((0,), (0,)), ((), ()))
NN = (((1,), (0,)), ((), ()))


def _params(sem=None, vmem=None):
    return pltpu.CompilerParams(dimension_semantics=sem, vmem_limit_bytes=vmem)


def _mx(v):
    return v.astype(MXU_DTYPE)


def _mx_round(v):
    return v.astype(MXU_DTYPE).astype(F32)


def _fit(dim, want):
    if dim <= want:
        return dim
    for d in range(want - want % 128, 0, -128):
        if dim % d == 0:
            return d
    raise ValueError((dim, want))


def mm(a, b, c=None, *, ta=False, tb=False, out_dtype=F32, tm=1024, tn=1024, tk=1024, name):
    K, M = a.shape if ta else a.shape[::-1]
    N = b.shape[0] if tb else b.shape[1]
    assert (b.shape[1] if tb else b.shape[0]) == K, (a.shape, b.shape, ta, tb)
    tm, tn, tk = _fit(M, tm), _fit(N, tn), _fit(K, tk)
    nk = K // tk
    dn = (((0 if ta else 1,), (1 if tb else 0,)), ((), ()))

    def body(*refs):
        if c is None:
            a_ref, b_ref, o_ref, acc = refs
            c_ref = None
        else:
            a_ref, b_ref, c_ref, o_ref, acc = refs
        k = pl.program_id(2)
        p = lax.dot_general(_mx(a_ref[...]), _mx(b_ref[...]), dn, preferred_element_type=F32)

        @pl.when(k == 0)
        def _():
            acc[...] = p

        @pl.when(k > 0)
        def _():
            acc[...] += p

        @pl.when(k == nk - 1)
        def _():
            r = acc[...]
            if c_ref is not None:
                r = r + c_ref[...].astype(F32)
            o_ref[...] = r.astype(o_ref.dtype)

    a_spec = pl.BlockSpec((tk, tm), lambda i, j, k: (k, i)) if ta else pl.BlockSpec((tm, tk), lambda i, j, k: (i, k))
    b_spec = pl.BlockSpec((tn, tk), lambda i, j, k: (j, k)) if tb else pl.BlockSpec((tk, tn), lambda i, j, k: (k, j))
    o_spec = pl.BlockSpec((tm, tn), lambda i, j, k: (i, j))
    ins, in_specs = [a, b], [a_spec, b_spec]
    if c is not None:
        ins.append(c)
        in_specs.append(o_spec)
    return pl.pallas_call(
        body, name=name, grid=(M // tm, N // tn, nk), in_specs=in_specs, out_specs=o_spec,
        out_shape=jax.ShapeDtypeStruct((M, N), out_dtype), scratch_shapes=[pltpu.VMEM((tm, tn), F32)],
        compiler_params=_params(("parallel", "parallel", "arbitrary"), VMEM_LIMIT),
    )(*ins)


def mm_sum_nt(pairs, *, tm=1024, tk=512, name):
    M, N = pairs[0][0].shape[0], pairs[0][1].shape[0]
    tm = _fit(M, tm)
    tks = [_fit(a.shape[1], tk) for a, _ in pairs]
    counts = [a.shape[1] // t for (a, _), t in zip(pairs, tks)]
    starts = [sum(counts[:i]) for i in range(len(pairs))]
    total = sum(counts)

    def body(*refs):
        o_ref, acc = refs[-2:]
        k = pl.program_id(1)

        @pl.when(k == 0)
        def _():
            acc[...] = jnp.zeros_like(acc)

        for i in range(len(pairs)):
            @pl.when((k >= starts[i]) & (k < starts[i] + counts[i]))
            def _(i=i):
                acc[...] += lax.dot_general(_mx(refs[2 * i][...]), _mx(refs[2 * i + 1][...]), NT,
                                            preferred_element_type=F32)

        @pl.when(k == total - 1)
        def _():
            o_ref[...] = acc[...]

    def turn(i):
        return lambda k: jnp.clip(k - starts[i], 0, counts[i] - 1)

    in_specs, ins = [], []
    for i, (a, b) in enumerate(pairs):
        in_specs.append(pl.BlockSpec((tm, tks[i]), lambda m, k, f=turn(i): (m, f(k))))
        in_specs.append(pl.BlockSpec((N, tks[i]), lambda m, k, f=turn(i): (0, f(k))))
        ins += [a, b]
    return pl.pallas_call(
        body, name=name, grid=(M // tm, total), in_specs=in_specs, out_specs=pl.BlockSpec((tm, N), lambda m, k: (m, 0)),
        out_shape=jax.ShapeDtypeStruct((M, N), F32), scratch_shapes=[pltpu.VMEM((tm, N), F32)],
        compiler_params=_params(("parallel", "arbitrary"), VMEM_LIMIT),
    )(*ins)


SLAB = D_FF // 4


def _swiglu_of(ab_ref):
    return _mx(f_swiglu(ab_ref[0].astype(F32), ab_ref[1].astype(F32))[0])


def _gated_spec(rows, index):
    return pl.BlockSpec((2, None, rows, SLAB), index)


def slab_nt(a, w6, t0, n, l, *, out_dtype, tm=1024, name):
    S, K = a.shape
    tm = _fit(S, tm)

    def body(a_ref, w_ref, o_ref):
        o_ref[...] = lax.dot_general(_mx(a_ref[...]), _mx(w_ref[...]), NT, preferred_element_type=F32).astype(o_ref.dtype)

    return pl.pallas_call(
        body, name=name, grid=(S // tm, n),
        in_specs=[pl.BlockSpec((tm, K), lambda i, s: (i, 0)),
                  pl.BlockSpec((None, None, None, SLAB, K), lambda i, s: (s % 4, t0 + s // 4, l, 0, 0))],
        out_specs=pl.BlockSpec((None, tm, SLAB), lambda i, s: (s, i, 0)),
        out_shape=jax.ShapeDtypeStruct((n, S, SLAB), out_dtype),
        compiler_params=_params(("parallel", "parallel"), VMEM_LIMIT),
    )(a, w6)


def slab_nt_swiglu_bwd(d, w6, t0, l, ab, *, tm=1024, name):
    S, K = d.shape
    tm = _fit(S, tm)

    def body(d_ref, w_ref, ab_ref, o_ref):
        dz = lax.dot_general(_mx(d_ref[...]), _mx(w_ref[...]), NT, preferred_element_type=F32)
        _, pullback = jax.vjp(lambda g, u: f_swiglu(g, u)[0], ab_ref[0].astype(F32), ab_ref[1].astype(F32))
        dg, du = pullback(_mx_round(dz))
        o_ref[0] = dg.astype(o_ref.dtype)
        o_ref[1] = du.astype(o_ref.dtype)

    return pl.pallas_call(
        body, name=name, grid=(S // tm, 4),
        in_specs=[pl.BlockSpec((tm, K), lambda i, s: (i, 0)),
                  pl.BlockSpec((None, None, None, SLAB, K), lambda i, s: (s, t0, l, 0, 0)),
                  _gated_spec(tm, lambda i, s: (0, s, i, 0))],
        out_specs=_gated_spec(tm, lambda i, s: (0, s, i, 0)),
        out_shape=jax.ShapeDtypeStruct(ab.shape, MXU_DTYPE),
        compiler_params=_params(("parallel", "parallel"), VMEM_LIMIT),
    )(d, w6, ab)


def slab_nn(a, w6, t0, l, *, gated=False, tail=None, tm=1024, name):
    n, S, _ = a.shape[-3:]
    K = w6.shape[-1]
    tm = _fit(S, tm)
    kind = tail[0] if tail else None
    extra = list(tail[1:]) if tail else []

    def body(a_ref, w_ref, *rest):
        i, s = pl.program_id(0), pl.program_id(1)
        acc = rest[-1]
        lhs = _swiglu_of(a_ref) if gated else _mx(a_ref[...])
        p = jnp.dot(lhs, _mx(w_ref[...]), preferred_element_type=F32)

        @pl.when(s == 0)
        def _():
            acc[...] = p

        @pl.when(s > 0)
        def _():
            acc[...] += p

        @pl.when(s == n - 1)
        def _():
            r = acc[...]
            if kind is None:
                rest[0][...] = r
            elif kind == "half_post":
                x_ref, g_ref, o_ref, x2_ref = rest[:4]
                o_ref[...] = r
                x2_ref[...] = f_half_post(x_ref[...], r, g_ref[...])[0]
            else:
                x_ref, g_ref, d_ref, dx_ref, dg_ref = rest[:5]
                _, pullback = jax.vjp(lambda xv, gv: f_rms(xv, gv)[0], x_ref[...], g_ref[...])
                dxn, dg = pullback(r)
                dx_ref[...] = dxn + d_ref[...]

                @pl.when(i == 0)
                def _():
                    dg_ref[...] = dg

                @pl.when(i > 0)
                def _():
                    dg_ref[...] += dg

    rows = pl.BlockSpec((tm, K), lambda i, s: (i, 0))
    gain = pl.BlockSpec((1, K), lambda i, s: (0, 0))
    tail_in = {None: [], "half_post": [rows, gain], "rms_bwd": [rows, gain, rows]}[kind]
    full, vec = jax.ShapeDtypeStruct((S, K), F32), jax.ShapeDtypeStruct((1, K), F32)
    out_specs, out_shape = {None: (rows, full), "half_post": ([rows, rows], [full, full]),
                            "rms_bwd": ([rows, gain], [full, vec])}[kind]
    return pl.pallas_call(
        body, name=name, grid=(S // tm, n),
        in_specs=[_gated_spec(tm, lambda i, s: (0, s, i, 0)) if gated else
                  pl.BlockSpec((None, tm, SLAB), lambda i, s: (s, i, 0)),
                  pl.BlockSpec((None, None, None, SLAB, K), lambda i, s: (s % 4, t0 + s // 4, l, 0, 0))] + tail_in,
        out_specs=out_specs, out_shape=out_shape, scratch_shapes=[pltpu.VMEM((tm, K), F32)],
        compiler_params=_params(("arbitrary" if kind == "rms_bwd" else "parallel", "arbitrary"), VMEM_LIMIT),
    )(a, w6, *extra)


def slab_tn(a, b, g6, t0, l, *, gated=False, tk=1024, name):
    n, S, _ = a.shape[-3:]
    K = b.shape[1]
    tk = _fit(S, tk)
    nk = S // tk

    def body(a_ref, b_ref, g_ref, o_ref, acc):
        k = pl.program_id(1)
        lhs = _swiglu_of(a_ref) if gated else _mx(a_ref[...])
        p = lax.dot_general(lhs, _mx(b_ref[...]), TN, preferred_element_type=F32)

        @pl.when(k == 0)
        def _():
            acc[...] = p

        @pl.when(k > 0)
        def _():
            acc[...] += p

        @pl.when(k == nk - 1)
        def _():
            o_ref[...] = acc[...]

    return pl.pallas_call(
        body, name=name, grid=(n, nk),
        in_specs=[_gated_spec(tk, lambda s, k: (0, s, k, 0)) if gated else
                  pl.BlockSpec((None, tk, SLAB), lambda s, k: (s, k, 0)),
                  pl.BlockSpec((tk, K), lambda s, k: (k, 0)), pl.BlockSpec(memory_space=pl.ANY)],
        out_specs=pl.BlockSpec((None, None, None, SLAB, K), lambda s, k: (s % 4, t0 + s // 4, l, 0, 0)),
        out_shape=jax.ShapeDtypeStruct(g6.shape, g6.dtype), scratch_shapes=[pltpu.VMEM((SLAB, K), F32)],
        input_output_aliases={2: 0},
        compiler_params=_params(("parallel", "arbitrary"), VMEM_LIMIT),
    )(a, b, g6)


def _row_arg(arg):
    if isinstance(arg, tuple):
        return arg
    return arg, arg.shape[1], 0


def _row_specs(rows, n):
    arrs, specs, avals = [], [], []
    for arg in rows:
        arr, width, cb = _row_arg(arg)
        rb = arr.shape[0] // n
        arrs.append(arr)
        specs.append(pl.BlockSpec((rb, width), lambda i, cb=cb: (i, cb)))
        avals.append(jax.ShapeDtypeStruct((rb, width), F32))
    return arrs, specs, avals


def _par_specs(pars):
    specs = [pl.BlockSpec(p.shape, lambda i, nd=p.ndim: (0,) * nd) for p in pars]
    avals = [jax.ShapeDtypeStruct(p.shape, F32) for p in pars]
    return specs, avals


def rowwise(f, rows, pars, out_dtypes, *, tm, name):
    n = _row_arg(rows[0])[0].shape[0] // tm
    arrs, rspecs, ravals = _row_specs(rows, n)
    pspecs, pavals = _par_specs(pars)
    oavals = jax.eval_shape(f, *ravals, *pavals)
    nin = len(arrs) + len(pars)

    def body(*refs):
        outs = f(*[r[...].astype(F32) for r in refs[:nin]])
        for o_ref, o in zip(refs[nin:], outs):
            o_ref[...] = o.astype(o_ref.dtype)

    return pl.pallas_call(
        body, name=name, grid=(n,), in_specs=rspecs + pspecs,
        out_specs=[pl.BlockSpec(o.shape, lambda i: (i, 0)) for o in oavals],
        out_shape=[jax.ShapeDtypeStruct((n * o.shape[0], o.shape[1]), dt) for o, dt in zip(oavals, out_dtypes)],
        compiler_params=_params(("parallel",), VMEM_LIMIT),
    )(*arrs, *pars)


def rowwise_vjp(f, rows, pars, cots, row_grad, par_grad, d_dtypes, adds=None, *, tm, name):
    n = _row_arg(rows[0])[0].shape[0] // tm
    arrs, rspecs, ravals = _row_specs(rows, n)
    pspecs, pavals = _par_specs(pars)
    carrs, cspecs, _ = _row_specs(cots, n)
    adds = adds or [None] * len(row_grad)
    add_arrs = [a for a in adds if a is not None]
    _, aspecs, _ = _row_specs(add_arrs, n)
    nr, npar, nc, na = len(arrs), len(pars), len(carrs), len(add_arrs)
    diff = list(row_grad) + [nr + j for j in par_grad]
    ngr = len(row_grad)

    def body(*refs):
        ins = [r[...].astype(F32) for r in refs[:nr + npar]]
        cs = tuple(r[...].astype(F32) for r in refs[nr + npar:nr + npar + nc])
        add_refs = list(refs[nr + npar + nc:nr + npar + nc + na])
        outs = refs[nr + npar + nc + na:]

        def g(*d):
            full = list(ins)
            for idx, val in zip(diff, d):
                full[idx] = val
            return f(*full)

        _, pullback = jax.vjp(g, *[ins[idx] for idx in diff])
        ds = pullback(cs)
        for k in range(ngr):
            d = ds[k]
            if adds[k] is not None:
                d = d + add_refs.pop(0)[...].astype(F32)
            outs[k][...] = d.astype(outs[k].dtype)
        i = pl.program_id(0)
        for k in range(ngr, len(diff)):
            @pl.when(i == 0)
            def _(k=k):
                outs[k][...] = ds[k]

            @pl.when(i > 0)
            def _(k=k):
                outs[k][...] += ds[k]

    out_specs = [pl.BlockSpec(ravals[i].shape, lambda i: (i, 0)) for i in row_grad] + [pspecs[j] for j in par_grad]
    out_shape = [jax.ShapeDtypeStruct((arrs[i].shape[0], ravals[i].shape[1]), dt) for i, dt in zip(row_grad, d_dtypes)]
    out_shape += [jax.ShapeDtypeStruct(pars[j].shape, F32) for j in par_grad]
    return pl.pallas_call(
        body, name=name, grid=(n,), in_specs=rspecs + pspecs + cspecs + aspecs, out_specs=out_specs, out_shape=out_shape,
        compiler_params=_params(("arbitrary",), VMEM_LIMIT),
    )(*arrs, *pars, *carrs, *add_arrs)


def _logsig(x):
    return jnp.minimum(x, 0.0) - jnp.log(1.0 + jnp.exp(-jnp.abs(x)))


def _sigmoid(x):
    return 1.0 / (1.0 + jnp.exp(-x))


def _silu(x):
    return x * _sigmoid(x)


def _rms(x, g):
    return x * lax.rsqrt(jnp.mean(x * x, axis=-1, keepdims=True) + NORM_EPS) * g


def f_rms(x, g):
    return (_rms(x, g),)


def f_swiglu(a, b):
    return (_silu(a) * b,)


def f_half_post(x, f, g):
    return (x + 0.5 * _rms(f, g),)


def f_post(x, m, g):
    return (x + _rms(m, g),)


def f_glu(a, g):
    return (a * _sigmoid(g),)


def f_conv_ln(y, lg, lb):
    mu = jnp.mean(y, axis=-1, keepdims=True)
    var = jnp.mean(jnp.square(y - mu), axis=-1, keepdims=True)
    return (_silu((y - mu) * lax.rsqrt(var + NORM_EPS) * lg + lb),)


def f_merge(g0, g1, g2, b0, b1, b2):
    return (_sigmoid(g0) * b0 + _sigmoid(g1) * b1 + _sigmoid(g2) * b2,)


def f_gla_post(o, r, g):
    w = o.shape[1]
    hv = w // 4
    i = lax.broadcasted_iota(jnp.int32, (w, w), 0) // hv
    j = lax.broadcasted_iota(jnp.int32, (w, w), 1) // hv
    avg = jnp.where(i == j, 1.0 / hv, 0.0).astype(F32)
    ms = jnp.dot(o * o, avg, precision=HIGHEST, preferred_element_type=F32)
    return (o * lax.rsqrt(ms + NORM_EPS) * g * _silu(r),)


def f_gla_pre(q, k, lr, wa, ba):
    tm, width = q.shape
    nc = tm // CHUNK
    pre = jnp.dot(_mx_round(lr), _mx_round(wa), precision=HIGHEST, preferred_element_type=F32) + ba
    la = (_logsig(pre) / GLA_TAU).reshape(nc, CHUNK, width)
    i = lax.broadcasted_iota(jnp.int32, (nc, CHUNK, CHUNK), 1)
    j = lax.broadcasted_iota(jnp.int32, (nc, CHUNK, CHUNK), 2)
    later = jnp.where(j > i, 1.0, 0.0).astype(F32)
    to_end = lax.dot_general(later, la, (((2,), (1,)), ((0,), (0,))), precision=HIGHEST, preferred_element_type=F32)
    lam = jnp.exp(jnp.sum(la, axis=1))
    return q * GLA_SCALE, k * jnp.exp(to_end.reshape(tm, width)), lam


def _split_dot(x, u):
    hi = _mx(x)
    lo = _mx(x - hi.astype(F32))
    return jnp.dot(hi, u, preferred_element_type=F32) + jnp.dot(lo, u, preferred_element_type=F32)


def _tri(tq, tk):
    row = lax.broadcasted_iota(jnp.int32, (tq, tk), 0)
    col = lax.broadcasted_iota(jnp.int32, (tq, tk), 1)
    return row, col


def _sb_tile(qh, kh, valid, suf, run):
    z = lax.dot_general(qh, kh, NT, preferred_element_type=F32)
    lp = jnp.minimum(z, 0.0) - jnp.log(1.0 + jnp.exp(-jnp.abs(z)))
    lk = lp - z if valid is None else jnp.where(valid, lp - z, 0.0)
    inc = _split_dot(lk, suf)
    a = jnp.exp(lp + (inc - lk + run))
    if valid is not None:
        a = jnp.where(valid, a, 0.0)
    return lp, a, run + inc[:, 0:1]


def _sticks_left(run0, run1):
    return (jnp.maximum(jnp.max(run0), jnp.max(run1)) > -SB_CUTOFF).astype(jnp.int32)


def sb_attn_fwd(qkv, *, tq, name):
    S = qkv.shape[0]
    tk = tq
    pairs = BRANCH // 128

    def body(q_ref, k_ref, v_ref, o_ref):
        i = pl.program_id(1)
        row, col = _tri(tq, tk)
        suf = _mx(row >= col)
        q = q_ref[...] * SB_SCALE

        def tile(j, carry, valid):
            ks = pl.multiple_of((i - j) * tk, tk)
            kb = k_ref[pl.ds(ks, tk), :]
            vb = v_ref[pl.ds(ks, tk), :]
            new = []
            for h in range(2):
                acc, run = carry[h]
                sl = slice(64 * h, 64 * h + 64)
                _, a, run = _sb_tile(q[:, sl], kb[:, sl], valid, suf, run)
                acc = acc + jnp.dot(_mx(a), vb[:, sl], preferred_element_type=F32)
                new.append((acc, run))
            return _sticks_left(new[0][1], new[1][1]), tuple(new)

        def step(state):
            j, _, carry = state
            return (j + 1, *tile(j, carry, None))

        zero = (jnp.zeros((tq, 64), F32), jnp.zeros((tq, 1), F32))
        first = tile(0, (zero, zero), col < row)
        _, _, res = lax.while_loop(lambda s: (s[0] <= i) & (s[1] > 0), step, (jnp.int32(1), *first))
        o_ref[...] = jnp.concatenate([res[0][0], res[1][0]], axis=1).astype(o_ref.dtype)

    return pl.pallas_call(
        body, name=name, grid=(pairs, S // tq),
        in_specs=[pl.BlockSpec((tq, 128), lambda p, i: (i, p)),
                  pl.BlockSpec((S, 128), lambda p, i: (0, pairs + p)),
                  pl.BlockSpec((S, 128), lambda p, i: (0, 2 * pairs + p))],
        out_specs=pl.BlockSpec((tq, 128), lambda p, i: (i, p)),
        out_shape=jax.ShapeDtypeStruct((S, BRANCH), MXU_DTYPE),
        compiler_params=_params(("parallel", "parallel"), VMEM_LIMIT),
    )(qkv, qkv, qkv)


def sb_attn_bwd(qkv, do, *, tq, name):
    S = qkv.shape[0]
    tk = tq
    nq = S // tq
    pairs = BRANCH // 128

    def body(q_ref, k_ref, v_ref, do_ref, dq_ref, dk_ref, dv_ref, g_sc, b_sc):
        i = pl.program_id(1)

        @pl.when(i == 0)
        def _():
            dk_ref[...] = jnp.zeros_like(dk_ref)
            dv_ref[...] = jnp.zeros_like(dv_ref)

        row, col = _tri(tq, tk)
        suf = _mx(row >= col)
        pre = _mx(row <= col)
        q = q_ref[...] * SB_SCALE
        dout = do_ref[...]

        def tile1(j, carry, valid):
            kblk = i - j
            ks = pl.multiple_of(kblk * tk, tk)
            kb = k_ref[pl.ds(ks, tk), :]
            vb = v_ref[pl.ds(ks, tk), :]
            runs, dvs = [], []
            for h in range(2):
                sl = slice(64 * h, 64 * h + 64)
                lp, a, run = _sb_tile(q[:, sl], kb[:, sl], valid, suf, carry[h])
                da = lax.dot_general(dout[:, sl], vb[:, sl], NT, preferred_element_type=F32)
                beta = jnp.exp(lp)
                g_sc[h, kblk] = (a * da).astype(g_sc.dtype)
                b_sc[h, kblk] = (beta if valid is None else jnp.where(valid, beta, 0.0)).astype(b_sc.dtype)
                dvs.append(lax.dot_general(_mx(a), dout[:, sl], TN, preferred_element_type=F32))
                runs.append(run)
            dv_ref[pl.ds(ks, tk), :] += jnp.concatenate(dvs, axis=1)
            return _sticks_left(runs[0], runs[1]), tuple(runs)

        def sweep1(state):
            j, _, carry = state
            return (j + 1, *tile1(j, carry, None))

        first = tile1(0, (jnp.zeros((tq, 1), F32), jnp.zeros((tq, 1), F32)), col < row)
        tiles, _, _ = lax.while_loop(lambda s: (s[0] <= i) & (s[1] > 0), sweep1, (jnp.int32(1), *first))

        def sweep2(kblk, carry):
            ks = pl.multiple_of(kblk * tk, tk)
            kb = k_ref[pl.ds(ks, tk), :]
            new, dks = [], []
            for h in range(2):
                dq, run = carry[h]
                sl = slice(64 * h, 64 * h + 64)
                g = g_sc[h, kblk]
                beta = b_sc[h, kblk].astype(F32)
                inc = jnp.dot(g, pre, preferred_element_type=F32)
                g = g.astype(F32)
                dz = _mx(g - beta * (inc + run))
                dq = dq + jnp.dot(dz, kb[:, sl], preferred_element_type=F32)
                dks.append(lax.dot_general(dz, q[:, sl], TN, preferred_element_type=F32))
                new.append((dq, run + inc[:, tk - 1:tk]))
            dk_ref[pl.ds(ks, tk), :] += jnp.concatenate(dks, axis=1)
            return tuple(new)

        zero = (jnp.zeros((tq, 64), F32), jnp.zeros((tq, 1), F32))
        res = lax.fori_loop(i + 1 - tiles, i + 1, sweep2, (zero, zero))
        dq_ref[...] = (jnp.concatenate([res[0][0], res[1][0]], axis=1) * SB_SCALE).astype(dq_ref.dtype)

    return pl.pallas_call(
        body, name=name, grid=(pairs, nq),
        in_specs=[pl.BlockSpec((tq, 128), lambda p, i: (i, p)),
                  pl.BlockSpec((S, 128), lambda p, i: (0, pairs + p)),
                  pl.BlockSpec((S, 128), lambda p, i: (0, 2 * pairs + p)),
                  pl.BlockSpec((tq, 128), lambda p, i: (i, p))],
        out_specs=[pl.BlockSpec((tq, 128), lambda p, i: (i, p)),
                   pl.BlockSpec((S, 128), lambda p, i: (0, p)),
                   pl.BlockSpec((S, 128), lambda p, i: (0, p))],
        out_shape=[jax.ShapeDtypeStruct((S, BRANCH), MXU_DTYPE), jax.ShapeDtypeStruct((S, BRANCH), F32),
                   jax.ShapeDtypeStruct((S, BRANCH), F32)],
        scratch_shapes=[pltpu.VMEM((2, nq, tq, tk), MXU_DTYPE), pltpu.VMEM((2, nq, tq, tk), MXU_DTYPE)],
        compiler_params=_params(("parallel", "arbitrary"), VMEM_LIMIT),
    )(qkv, qkv, qkv, do)


def conv_fwd(u, w, b, *, tm, name):
    S, C = u.shape
    hb = tm // CONV_PAD

    def body(u_ref, halo_ref, w_ref, b_ref, y_ref, buf):
        i = pl.program_id(0)
        buf[pl.ds(CONV_PAD, tm), :] = u_ref[...]
        buf[pl.ds(0, CONV_PAD), :] = jnp.where(i > 0, halo_ref[...], 0.0)
        ts = min(tm, CONV_SUB)
        for r in range(0, tm, ts):
            acc = jnp.broadcast_to(b_ref[...], (ts, 128))
            for j in range(CONV_WIDTH):
                acc = acc + buf[pl.ds(r + CONV_PAD - (CONV_WIDTH - 1) + j, ts), :] * w_ref[pl.ds(j, 1), :]
            y_ref[pl.ds(r, ts), :] = acc

    return pl.pallas_call(
        body, name=name, grid=(S // tm, C // 128),
        in_specs=[pl.BlockSpec((tm, 128), lambda i, c: (i, c)),
                  pl.BlockSpec((CONV_PAD, 128), lambda i, c: (jnp.maximum(i * hb - 1, 0), c)),
                  pl.BlockSpec((CONV_PAD, 128), lambda i, c: (0, c)),
                  pl.BlockSpec((1, 128), lambda i, c: (0, c))],
        out_specs=pl.BlockSpec((tm, 128), lambda i, c: (i, c)),
        out_shape=jax.ShapeDtypeStruct((S, C), F32),
        scratch_shapes=[pltpu.VMEM((tm + CONV_PAD, 128), F32)],
        compiler_params=_params(("parallel", "parallel")),
    )(u, u, w, b)


def conv_bwd(dy, u, w, *, tm, name):
    S, C = u.shape
    hb = tm // CONV_PAD
    n = S // tm

    def body(dy_ref, dyn_ref, u_ref, up_ref, w_ref, du_ref, dw_ref, bufy, bufu):
        i = pl.program_id(1)
        dyv = dy_ref[...]
        bufy[pl.ds(0, tm), :] = dyv
        bufy[pl.ds(tm, CONV_PAD), :] = jnp.where(i < n - 1, dyn_ref[...], 0.0)
        bufu[pl.ds(CONV_PAD, tm), :] = u_ref[...]
        bufu[pl.ds(0, CONV_PAD), :] = jnp.where(i > 0, up_ref[...], 0.0)

        @pl.when(i == 0)
        def _():
            dw_ref[...] = jnp.zeros_like(dw_ref)

        ts = min(tm, CONV_SUB)
        for r in range(0, tm, ts):
            dys = dy_ref[pl.ds(r, ts), :]
            acc = jnp.zeros((ts, 128), F32)
            for j in range(CONV_WIDTH):
                acc = acc + bufy[pl.ds(r + CONV_WIDTH - 1 - j, ts), :] * w_ref[pl.ds(j, 1), :]
                shifted = bufu[pl.ds(r + CONV_PAD - (CONV_WIDTH - 1) + j, ts), :]
                dw_ref[pl.ds(j, 1), :] += jnp.sum(dys * shifted, axis=0, keepdims=True)
            du_ref[pl.ds(r, ts), :] = acc
            dw_ref[pl.ds(CONV_WIDTH, 1), :] += jnp.sum(dys, axis=0, keepdims=True)

    return pl.pallas_call(
        body, name=name, grid=(C // 128, n),
        in_specs=[pl.BlockSpec((tm, 128), lambda c, i: (i, c)),
                  pl.BlockSpec((CONV_PAD, 128), lambda c, i: (jnp.minimum((i + 1) * hb, n * hb - 1), c)),
                  pl.BlockSpec((tm, 128), lambda c, i: (i, c)),
                  pl.BlockSpec((CONV_PAD, 128), lambda c, i: (jnp.maximum(i * hb - 1, 0), c)),
                  pl.BlockSpec((CONV_PAD, 128), lambda c, i: (0, c))],
        out_specs=[pl.BlockSpec((tm, 128), lambda c, i: (i, c)), pl.BlockSpec((CONV_PAD, 128), lambda c, i: (0, c))],
        out_shape=[jax.ShapeDtypeStruct((S, C), F32), jax.ShapeDtypeStruct((CONV_PAD, C), F32)],
        scratch_shapes=[pltpu.VMEM((tm + CONV_PAD, 128), F32), pltpu.VMEM((tm + CONV_PAD, 128), F32)],
        compiler_params=_params(("parallel", "arbitrary")),
    )(dy, dy, u, u, w)


GLA_HEADS, GLA_DK, GLA_DV = 4, 64, 128


def gla_scan_fwd(qs, kd, gl, lam, *, tm, name):
    S = qs.shape[0]
    nc = tm // CHUNK

    def body(qs_ref, kd_ref, v_ref, lam_ref, o_ref, st_ref, state):
        @pl.when(pl.program_id(0) == 0)
        def _():
            state[...] = jnp.zeros_like(state)

        for c in range(nc):
            rows = pl.ds(c * CHUNK, CHUNK)
            q, k, v = _mx(qs_ref[rows, :]), _mx(kd_ref[rows, :]), _mx(v_ref[rows, :])
            upd = [lax.dot_general(v[:, h * GLA_DV:(h + 1) * GLA_DV], k[:, h * GLA_DK:(h + 1) * GLA_DK], TN,
                                   preferred_element_type=F32) for h in range(GLA_HEADS)]
            st = state[...] * lam_ref[pl.ds(c, 1), :] + jnp.concatenate(upd, axis=1)
            state[...] = st
            st_ref[c] = st
            stm = _mx(st)
            o = [lax.dot_general(q[:, h * GLA_DK:(h + 1) * GLA_DK], stm[:, h * GLA_DK:(h + 1) * GLA_DK], NT,
                                 preferred_element_type=F32) for h in range(GLA_HEADS)]
            o_ref[rows, :] = jnp.concatenate(o, axis=1)

    dk_all = GLA_HEADS * GLA_DK
    dv_all = GLA_HEADS * GLA_DV
    return pl.pallas_call(
        body, name=name, grid=(S // tm,),
        in_specs=[pl.BlockSpec((tm, dk_all), lambda i: (i, 0)), pl.BlockSpec((tm, dk_all), lambda i: (i, 0)),
                  pl.BlockSpec((tm, dv_all), lambda i: (i, 1)), pl.BlockSpec((nc, dk_all), lambda i: (i, 0))],
        out_specs=[pl.BlockSpec((tm, dv_all), lambda i: (i, 0)), pl.BlockSpec((nc, GLA_DV, dk_all), lambda i: (i, 0, 0))],
        out_shape=[jax.ShapeDtypeStruct((S, dv_all), F32), jax.ShapeDtypeStruct((S // CHUNK, GLA_DV, dk_all), F32)],
        scratch_shapes=[pltpu.VMEM((GLA_DV, dk_all), F32)],
        compiler_params=_params(("arbitrary",)),
    )(qs, kd, gl, lam)


def gla_scan_bwd(do, qs, kd, gl, lam, states, *, tm, name):
    S = qs.shape[0]
    nc = tm // CHUNK
    n = S // tm
    dk_all = GLA_HEADS * GLA_DK
    dv_all = GLA_HEADS * GLA_DV

    def body(do_ref, qs_ref, kd_ref, v_ref, lam_ref, st_ref, prev_ref, dqs_ref, dkd_ref, dv_ref, dlam_ref, carry):
        i = pl.program_id(0)

        @pl.when(i == 0)
        def _():
            carry[...] = jnp.zeros_like(carry)

        for c in reversed(range(nc)):
            rows = pl.ds(c * CHUNK, CHUNK)
            q, k, v, d = _mx(qs_ref[rows, :]), _mx(kd_ref[rows, :]), _mx(v_ref[rows, :]), _mx(do_ref[rows, :])
            st = _mx(st_ref[c])
            before = st_ref[c - 1] if c > 0 else jnp.where(i < n - 1, prev_ref[0], 0.0)
            outer = [lax.dot_general(d[:, h * GLA_DV:(h + 1) * GLA_DV], q[:, h * GLA_DK:(h + 1) * GLA_DK], TN,
                                     preferred_element_type=F32) for h in range(GLA_HEADS)]
            dst = carry[...] + jnp.concatenate(outer, axis=1)
            dstm = _mx(dst)
            dq, dkk, dvv = [], [], []
            for h in range(GLA_HEADS):
                ksl = slice(h * GLA_DK, (h + 1) * GLA_DK)
                vsl = slice(h * GLA_DV, (h + 1) * GLA_DV)
                dq.append(jnp.dot(d[:, vsl], st[:, ksl], preferred_element_type=F32))
                dkk.append(jnp.dot(v[:, vsl], dstm[:, ksl], preferred_element_type=F32))
                dvv.append(lax.dot_general(k[:, ksl], dstm[:, ksl], NT, preferred_element_type=F32))
            dqs_ref[rows, :] = jnp.concatenate(dq, axis=1)
            dkd_ref[rows, :] = jnp.concatenate(dkk, axis=1)
            dv_ref[rows, :] = jnp.concatenate(dvv, axis=1)
            dlam_ref[pl.ds(c, 1), :] = jnp.sum(dst * before, axis=0, keepdims=True)
            carry[...] = dst * lam_ref[pl.ds(c, 1), :]

    rev = lambda i: n - 1 - i
    return pl.pallas_call(
        body, name=name, grid=(n,),
        in_specs=[pl.BlockSpec((tm, dv_all), lambda i: (rev(i), 0)), pl.BlockSpec((tm, dk_all), lambda i: (rev(i), 0)),
                  pl.BlockSpec((tm, dk_all), lambda i: (rev(i), 0)), pl.BlockSpec((tm, dv_all), lambda i: (rev(i), 1)),
                  pl.BlockSpec((nc, dk_all), lambda i: (rev(i), 0)),
                  pl.BlockSpec((nc, GLA_DV, dk_all), lambda i: (rev(i), 0, 0)),
                  pl.BlockSpec((1, GLA_DV, dk_all), lambda i: (jnp.maximum(rev(i) * nc - 1, 0), 0, 0))],
        out_specs=[pl.BlockSpec((tm, dk_all), lambda i: (rev(i), 0)), pl.BlockSpec((tm, dk_all), lambda i: (rev(i), 0)),
                   pl.BlockSpec((tm, dv_all), lambda i: (rev(i), 0)), pl.BlockSpec((nc, dk_all), lambda i: (rev(i), 0))],
        out_shape=[jax.ShapeDtypeStruct((S, dk_all), F32), jax.ShapeDtypeStruct((S, dk_all), F32),
                   jax.ShapeDtypeStruct((S, dv_all), F32), jax.ShapeDtypeStruct((S // CHUNK, dk_all), F32)],
        scratch_shapes=[pltpu.VMEM((GLA_DV, dk_all), F32)],
        compiler_params=_params(("arbitrary",)),
    )(do, qs, kd, gl, lam, states, states)


def loss_head(y, target, *, tm, name):
    S, D = y.shape

    def body(y_ref, t_ref, dy_ref, sq_ref):
        err = y_ref[...] - t_ref[...]
        dy_ref[...] = err * (1.0 / D)
        part = jnp.sum(err * err, axis=0, keepdims=True)

        @pl.when(pl.program_id(0) == 0)
        def _():
            sq_ref[...] = part

        @pl.when(pl.program_id(0) > 0)
        def _():
            sq_ref[...] += part

    return pl.pallas_call(
        body, name=name, grid=(S // tm,),
        in_specs=[pl.BlockSpec((tm, D), lambda i: (i, 0)), pl.BlockSpec((tm, D), lambda i: (i, 0))],
        out_specs=[pl.BlockSpec((tm, D), lambda i: (i, 0)), pl.BlockSpec((1, D), lambda i: (0, 0))],
        out_shape=[jax.ShapeDtypeStruct((S, D), F32), jax.ShapeDtypeStruct((1, D), F32)],
        compiler_params=_params(("arbitrary",)),
    )(y, target)


def f_adamw(w, g, m, v):
    m = ADAM_B1 * m + (1.0 - ADAM_B1) * g
    v = ADAM_B2 * v + (1.0 - ADAM_B2) * jnp.square(g)
    m_hat = m / (1.0 - ADAM_B1 ** ADAM_STEP)
    v_hat = v / (1.0 - ADAM_B2 ** ADAM_STEP)
    return -ADAM_LR * (m_hat / (jnp.sqrt(v_hat) + ADAM_EPS) + ADAM_WD * w), m, v


def adamw(w, g, m, v, *, name):
    shape = w.shape
    cols = shape[-1]
    rows = w.size // cols
    tm = rows
    while tm % 16 == 0 and tm * cols * 4 > (1 << 20):
        tm //= 2
    flat = [t.reshape(rows, cols) for t in (w, g, m, v)]
    outs = rowwise(f_adamw, flat, [], [F32, F32, F32], tm=tm, name=name)
    return [o.reshape(shape) for o in outs]


def _place():
    x, y, c = lax.axis_index("x"), lax.axis_index("y"), lax.axis_index("c")
    return x, y, c, [(1 - x, y), (x, 1 - y), (1 - x, 1 - y)]


def _my_chip():
    return 2 * lax.axis_index("x") + lax.axis_index("y")


def _any():
    return pl.BlockSpec(memory_space=pl.ANY)


DMA_CHUNKS = 8


def _chunks(rows):
    n = DMA_CHUNKS
    while n > 1 and rows % (n * 16):
        n //= 2
    return [(k * (rows // n), rows // n) for k in range(n)]


def _start_chunked(make, rows):
    for off, size in _chunks(rows):
        make(off, size).start()


def all_gather_chips(v, *, name):
    R, C = v.shape
    H = R // 2
    Q = H // 2
    assert R % 64 == 0, R

    def body(v_ref, o_ref, send, recv, psend, precv, fsend, frecv):
        x, y, c, chips = _place()
        me = 2 * x + y
        ids = [2 * px + py for px, py in chips]
        mine = pl.ds(pl.multiple_of(c * H, 8), H)
        other = pl.ds(pl.multiple_of((1 - c) * H, 8), H)

        def my_rows(off, size):
            return pl.ds(pl.multiple_of(c * H + off, 8), size)

        def cross(j, src_chip, rows):
            return pltpu.make_async_remote_copy(
                src_ref=v_ref.at[rows], dst_ref=o_ref.at[src_chip, rows], send_sem=send.at[j], recv_sem=recv.at[j],
                device_id=(*chips[j], c), device_id_type=MESH)

        def passed(j, off, size):
            rows = my_rows(j * Q + off, size)
            src_chip = ids[1 - j]
            return pltpu.make_async_remote_copy(
                src_ref=o_ref.at[src_chip, rows], dst_ref=o_ref.at[src_chip, rows], send_sem=psend.at[j],
                recv_sem=precv.at[j], device_id=(*chips[j], c), device_id_type=MESH)

        def landed(j):
            rows = my_rows(j * Q, Q)
            return pltpu.make_async_remote_copy(
                src_ref=o_ref.at[ids[2], rows], dst_ref=o_ref.at[ids[2], rows], send_sem=psend.at[j],
                recv_sem=precv.at[j], device_id=(*chips[j], c), device_id_type=MESH)

        def handed(j, rows):
            return pltpu.make_async_remote_copy(
                src_ref=o_ref.at[ids[j], rows], dst_ref=o_ref.at[ids[j], rows], send_sem=fsend.at[j],
                recv_sem=frecv.at[j], device_id=(x, y, 1 - c), device_id_type=MESH)

        for j in range(2):
            _start_chunked(lambda off, size, j=j: cross(j, me, my_rows(off, size)), H)
        for j in (1, 0):
            cross(j, ids[j], mine).wait_recv()
            _start_chunked(functools.partial(passed, 1 - j), Q)
            _start_chunked(lambda off, size, j=j: handed(j, my_rows(off, size)), H)
        for j in range(2):
            landed(j).wait_recv()
        _start_chunked(lambda off, size: handed(2, my_rows(off, size)), H)
        for j in range(3):
            handed(j, other).wait_recv()
        for j in range(2):
            cross(j, me, mine).wait_send()
            passed(j, 0, Q).wait_send()
        for j in range(3):
            handed(j, mine).wait_send()

    got = pl.pallas_call(
        body, name=name, in_specs=[_any()], out_specs=_any(), out_shape=jax.ShapeDtypeStruct((4, R, C), v.dtype),
        scratch_shapes=[pltpu.SemaphoreType.DMA((2,)), pltpu.SemaphoreType.DMA((2,)), pltpu.SemaphoreType.DMA((2,)),
                        pltpu.SemaphoreType.DMA((2,)), pltpu.SemaphoreType.DMA((3,)), pltpu.SemaphoreType.DMA((3,))],
    )(v)
    return lax.dynamic_update_slice(got, v[None], (_my_chip(), 0, 0))


def swap_halves(g, *, name):
    n, R, C = g.shape
    H = R // 2

    def body(g_ref, theirs_ref, send, recv):
        x, y, c, _ = _place()

        def give(q, off, size):
            return pltpu.make_async_remote_copy(
                src_ref=g_ref.at[q, pl.ds(pl.multiple_of((1 - c) * H + off, 8), size)],
                dst_ref=theirs_ref.at[q, pl.ds(off, size)], send_sem=send.at[q], recv_sem=recv.at[q],
                device_id=(x, y, 1 - c), device_id_type=MESH)

        for q in range(n):
            _start_chunked(functools.partial(give, q), H)
        for q in range(n):
            give(q, 0, H).wait()

    return pl.pallas_call(
        body, name=name, in_specs=[_any()], out_specs=_any(), out_shape=jax.ShapeDtypeStruct((n, H, C), g.dtype),
        scratch_shapes=[pltpu.SemaphoreType.DMA((n,)), pltpu.SemaphoreType.DMA((n,))],
    )(g)


def exchange_pieces(p, *, name):
    H = p.shape[1]

    def body(p_ref, b_ref, send, recv):
        x, y, c, chips = _place()
        me = 2 * x + y
        ids = [2 * px + py for px, py in chips]

        def cross(j, piece, slot, off, size):
            return pltpu.make_async_remote_copy(
                src_ref=p_ref.at[piece, pl.ds(off, size)], dst_ref=b_ref.at[slot, pl.ds(off, size)], send_sem=send.at[j],
                recv_sem=recv.at[j], device_id=(*chips[j], c), device_id_type=MESH)

        for j in range(3):
            _start_chunked(functools.partial(cross, j, ids[j], me), H)
        for j in range(3):
            cross(j, me, ids[j], 0, H).wait_recv()
        for j in range(3):
            cross(j, ids[j], me, 0, H).wait_send()

    got = pl.pallas_call(
        body, name=name, in_specs=[_any()], out_specs=_any(), out_shape=jax.ShapeDtypeStruct(p.shape, p.dtype),
        scratch_shapes=[pltpu.SemaphoreType.DMA((3,)), pltpu.SemaphoreType.DMA((3,))],
    )(p)
    me = _my_chip()
    return lax.dynamic_update_slice(got, lax.dynamic_slice_in_dim(p, me, 1, axis=0), (me, 0, 0))


def join_halves(f, *, name):
    H, C = f.shape

    def body(f_ref, o_ref, send, recv):
        x, y, c, _ = _place()

        def give(off, size):
            return pltpu.make_async_remote_copy(
                src_ref=f_ref.at[pl.ds(off, size)], dst_ref=o_ref.at[pl.ds(pl.multiple_of(c * H + off, 8), size)],
                send_sem=send, recv_sem=recv, device_id=(x, y, 1 - c), device_id_type=MESH)

        _start_chunked(give, H)
        give(0, H).wait()

    both = pl.pallas_call(
        body, name=name, in_specs=[_any()], out_specs=_any(), out_shape=jax.ShapeDtypeStruct((2 * H, C), f.dtype),
        scratch_shapes=[pltpu.SemaphoreType.DMA, pltpu.SemaphoreType.DMA],
    )(f)
    return lax.dynamic_update_slice(both, f, (lax.axis_index("c") * H, 0))


def _row_tile(rows, want):
    tm = want
    while rows % tm:
        tm //= 2
    return tm


def _add_pair(g, theirs, out_dtype, *, name):
    n, R, C = g.shape
    H = R // 2
    tm = _row_tile(H, 512)
    nb = H // tm

    def body(c_ref, g_ref, t_ref, o_ref):
        o_ref[...] = (g_ref[...] + t_ref[...]).astype(o_ref.dtype)

    grid_spec = pltpu.PrefetchScalarGridSpec(
        num_scalar_prefetch=1, grid=(n, nb),
        in_specs=[pl.BlockSpec((None, tm, C), lambda q, i, c: (q, c[0] * nb + i, 0)),
                  pl.BlockSpec((None, tm, C), lambda q, i, c: (q, i, 0))],
        out_specs=pl.BlockSpec((None, tm, C), lambda q, i, c: (q, i, 0)))
    return pl.pallas_call(
        body, name=name, grid_spec=grid_spec, out_shape=jax.ShapeDtypeStruct((n, H, C), out_dtype),
        compiler_params=_params(("parallel", "parallel")),
    )(lax.axis_index("c").astype(jnp.int32).reshape(1), g, theirs)


def _add_stack(b, *, name):
    n, H, C = b.shape
    tm = _row_tile(H, 256)

    def body(b_ref, o_ref):
        s = b_ref[0].astype(F32)
        for k in range(1, n):
            s = s + b_ref[k].astype(F32)
        o_ref[...] = s

    return pl.pallas_call(
        body, name=name, grid=(H // tm,), in_specs=[pl.BlockSpec((n, tm, C), lambda i: (0, i, 0))],
        out_specs=pl.BlockSpec((tm, C), lambda i: (i, 0)), out_shape=jax.ShapeDtypeStruct((H, C), F32),
        compiler_params=_params(("parallel",)),
    )(b)


def reduce_scatter(g, tag):
    pair = _add_pair(g, swap_halves(g, name=tag + "_swap_halves"), GRAD_WIRE_DTYPE, name=tag + "_add_pair")
    total = _add_stack(exchange_pieces(pair, name=tag + "_exchange"), name=tag + "_add_chips")
    return join_halves(total, name=tag + "_join")


FFN = ("ffn1_w_gate", "ffn1_w_up", "ffn2_w_gate", "ffn2_w_up", "ffn1_w_down", "ffn2_w_down")
BIG = ("w_in", "w_branch", "w_out")
SMALL = ("norm_pre", "norm_post", "conv_w", "gla_w_alpha")
REPLICATED = ("conv_b", "conv_ln_g", "conv_ln_b", "gla_b_alpha", "gla_norm_g")
WEIGHTS = ("norm_pre", "norm_post", "ffn1_w_gate", "ffn1_w_up", "ffn1_w_down", "ffn2_w_gate", "ffn2_w_up", "ffn2_w_down",
           "w_in", "conv_w", "conv_b", "conv_ln_g", "conv_ln_b", "gla_w_alpha", "gla_b_alpha", "gla_norm_g", "w_branch",
           "w_out")
SHARD_AXIS = {"ffn1_w_gate": 2, "ffn1_w_up": 2, "ffn1_w_down": 1, "ffn2_w_gate": 2, "ffn2_w_up": 2, "ffn2_w_down": 1,
              "w_in": 2, "w_branch": 3, "w_out": 1, "norm_pre": 2, "norm_post": 2, "conv_w": 2, "gla_w_alpha": 2}


def _size(shape):
    n = 1
    for d in shape:
        n *= d
    return n


def _pack_rows(shape):
    return -(-_size(shape) // (16 * PACK_COLS)) * 16


def _pack(arrs, dtype, row_mult):
    parts, rows = [], 0
    for a in arrs:
        r = _pack_rows(a.shape)
        flat = a.astype(dtype).reshape(-1)
        if r * PACK_COLS != flat.shape[0]:
            flat = jnp.pad(flat, (0, r * PACK_COLS - flat.shape[0]))
        parts.append(flat.reshape(r, PACK_COLS))
        rows += r
    if rows % row_mult:
        parts.append(jnp.zeros((row_mult - rows % row_mult, PACK_COLS), dtype))
    return jnp.concatenate(parts, axis=0)


def _unpack(buf, shapes):
    lead = buf.shape[:-2]
    out, off = [], 0
    for s in shapes:
        r = _pack_rows(s)
        part = lax.slice_in_dim(buf, off, off + r, axis=buf.ndim - 2).reshape(lead + (r * PACK_COLS,))
        if r * PACK_COLS != _size(s):
            part = lax.slice_in_dim(part, 0, _size(s), axis=part.ndim - 1)
        out.append(part.reshape(lead + tuple(s)))
        off += r
    return out


def gather_weights(shards, names, dtype, row_mult, name):
    packed = _pack([shards[k] for k in names], dtype, row_mult)
    parts = _unpack(all_gather_chips(packed, name=name), [shards[k].shape for k in names])
    return {k: jnp.concatenate([p[q] for q in range(4)], axis=SHARD_AXIS[k]) for k, p in zip(names, parts)}


def scatter_grads(grads, shards):
    sharded = BIG + SMALL
    pieces = []
    for q in range(4):
        part = []
        for k in sharded:
            w = shards[k].shape[SHARD_AXIS[k]]
            part.append(lax.slice_in_dim(grads[k], q * w, (q + 1) * w, axis=SHARD_AXIS[k]))
        part += [grads[k] for k in REPLICATED]
        pieces.append(_pack(part, F32, 512))
    total = reduce_scatter(jnp.stack(pieces), "rs_mix")
    names = sharded + REPLICATED
    return dict(zip(names, _unpack(total, [shards[k].shape for k in names])))


TM = 512
TM_MERGE = 256
TM_CONV = 1024
TM_GLA = 512
TQ = 256

IN_SB, IN_CONV, IN_GLA, IN_LR, IN_GATE = 0, 1536, 2560, 4096, 4112
IN_END = 7184


def layer_weights(full, l):
    w = {}
    win = full["w_in"][l]
    w["in_sb"] = win[:, IN_SB:IN_CONV]
    w["in_conv"] = win[:, IN_CONV:IN_GLA]
    w["in_gla"] = win[:, IN_GLA:IN_LR]
    w["in_lr"] = jnp.pad(win[:, IN_LR:IN_GATE], ((0, 0), (0, GLA_RANK_PAD - GLA_RANK)))
    w["in_gate"] = win[:, IN_GATE:IN_END]
    w["branch"] = [full["w_branch"][l, g] for g in range(3)]
    w["out"] = full["w_out"][l]
    return w


def ffn_fwd(x, gpre, gpost, w6, f, l, tag):
    S = x.shape[0]
    h = rowwise(f_rms, [x], [gpre], [MXU_DTYPE], tm=TM, name=tag + "_pre")[0]
    ab = slab_nt(h, w6, 2 * f, 8, l, out_dtype=MXU_DTYPE, name=tag + "_gu").reshape(2, 4, S, SLAB)
    out, x2 = slab_nn(ab, w6, 4 + f, l, gated=True, tail=("half_post", x, gpost), name=tag + "_down")
    return x2, (x, h, ab, out)


def ffn_bwd(dx2, res, gpre, gpost, w6, g6, f, l, tag):
    x, h, ab, out = res
    S = x.shape[0]
    df, dgpost = rowwise_vjp(f_half_post, [x, out], [gpost], [dx2], [1], [0], [MXU_DTYPE], tm=TM, name=tag + "_post_b")
    dab = slab_nt_swiglu_bwd(df, w6, 4 + f, l, ab, name=tag + "_down_dx").reshape(8, S, SLAB)
    g6 = slab_tn(ab, df, g6, 4 + f, l, gated=True, name=tag + "_down_dw")
    dx, dgpre = slab_nn(dab, w6, 2 * f, l, tail=("rms_bwd", x, gpre, dx2), name=tag + "_gu_dx")
    g6 = slab_tn(dab, h, g6, 2 * f, l, name=tag + "_gu_dw")
    return dx, dgpre, dgpost, g6


def mixer_fwd(x, p, w, tag):
    h = rowwise(f_rms, [x], [p["gpre"]], [MXU_DTYPE], tm=TM, name=tag + "_pre")[0]
    qkv = mm(h, w["in_sb"], out_dtype=MXU_DTYPE, tn=512, name=tag + "_in_sb")
    cv = mm(h, w["in_conv"], name=tag + "_in_conv")
    gl = mm(h, w["in_gla"], out_dtype=MXU_DTYPE, tn=512, name=tag + "_in_gla")
    lr = mm(h, w["in_lr"], out_dtype=MXU_DTYPE, name=tag + "_in_lr")
    gt = mm(h, w["in_gate"], out_dtype=MXU_DTYPE, name=tag + "_in_gate")
    sb = sb_attn_fwd(qkv, tq=TQ, name=tag + "_sb")
    u = rowwise(f_glu, [(cv, BRANCH, 0), (cv, BRANCH, 1)], [], [F32], tm=TM, name=tag + "_glu")[0]
    y = conv_fwd(u, p["conv_w"], p["conv_b"], tm=TM_CONV, name=tag + "_conv")
    cb = rowwise(f_conv_ln, [y], [p["ln_g"], p["ln_b"]], [MXU_DTYPE], tm=TM, name=tag + "_ln")[0]
    qs, kd, lam = rowwise(f_gla_pre, [(gl, 256, 0), (gl, 256, 1), lr], [p["wa"], p["ba"]], [MXU_DTYPE, MXU_DTYPE, F32],
                          tm=TM_GLA, name=tag + "_gla_pre")
    o, states = gla_scan_fwd(qs, kd, gl, lam, tm=TM_GLA, name=tag + "_gla_scan")
    gb = rowwise(f_gla_post, [o, (gl, BRANCH, 2)], [p["gn"]], [MXU_DTYPE], tm=TM, name=tag + "_gla_post")[0]
    branches = [sb, cb, gb]
    bd = [mm(branches[g], w["branch"][g], out_dtype=MXU_DTYPE, name=tag + f"_branch{g}") for g in range(3)]
    merged = rowwise(f_merge, [(gt, D_MODEL, 0), (gt, D_MODEL, 1), (gt, D_MODEL, 2)] + bd, [], [MXU_DTYPE], tm=TM_MERGE,
                     name=tag + "_merge")[0]
    m = mm(merged, w["out"], name=tag + "_out")
    x2 = rowwise(f_post, [x, m], [p["gpost"]], [F32], tm=TM, name=tag + "_post")[0]
    return x2, (x, h, qkv, cv, gl, lr, gt, u, y, qs, kd, lam, o, states, branches, bd, merged, m)


def mixer_bwd(dx2, res, p, w, tag):
    x, h, qkv, cv, gl, lr, gt, u, y, qs, kd, lam, o, states, branches, bd, merged, m = res
    g = {}
    dm, g["gpost"] = rowwise_vjp(f_post, [x, m], [p["gpost"]], [dx2], [1], [0], [MXU_DTYPE], tm=TM, name=tag + "_post_b")
    dmerged = mm(dm, w["out"], tb=True, out_dtype=MXU_DTYPE, name=tag + "_out_dx")
    g["out"] = mm(merged, dm, ta=True, name=tag + "_out_dw")
    gts = [(gt, D_MODEL, 0), (gt, D_MODEL, 1), (gt, D_MODEL, 2)]
    dgate = rowwise_vjp(f_merge, gts + bd, [], [dmerged], [0, 1, 2, 3, 4, 5], [], [MXU_DTYPE] * 6, tm=TM_MERGE,
                        name=tag + "_merge_b")
    dgt = jnp.concatenate(dgate[:3], axis=1)
    dbd = dgate[3:]
    g["branch"] = [mm(branches[k], dbd[k], ta=True, name=tag + f"_branch{k}_dw") for k in range(3)]
    dsb = mm(dbd[0], w["branch"][0], tb=True, out_dtype=MXU_DTYPE, name=tag + "_branch0_dx")
    dq, dk, dv = sb_attn_bwd(qkv, dsb, tq=TQ, name=tag + "_sb_b")
    dqkv = jnp.concatenate([dq, _mx(dk), _mx(dv)], axis=1)
    dcb = mm(dbd[1], w["branch"][1], tb=True, name=tag + "_branch1_dx")
    dy, g["ln_g"], g["ln_b"] = rowwise_vjp(f_conv_ln, [y], [p["ln_g"], p["ln_b"]], [dcb], [0], [0, 1], [F32], tm=TM,
                                           name=tag + "_ln_b")
    du, dwb = conv_bwd(dy, u, p["conv_w"], tm=TM_CONV, name=tag + "_conv_b")
    g["conv_w"], g["conv_b"] = dwb[:CONV_WIDTH], dwb[CONV_WIDTH:CONV_WIDTH + 1]
    dca, dcg = rowwise_vjp(f_glu, [(cv, BRANCH, 0), (cv, BRANCH, 1)], [], [du], [0, 1], [], [MXU_DTYPE, MXU_DTYPE], tm=TM,
                           name=tag + "_glu_b")
    dcv = jnp.concatenate([dca, dcg], axis=1)
    dgb = mm(dbd[2], w["branch"][2], tb=True, name=tag + "_branch2_dx")
    do, dr, g["gn"] = rowwise_vjp(f_gla_post, [o, (gl, BRANCH, 2)], [p["gn"]], [dgb], [0, 1], [0], [F32, MXU_DTYPE], tm=TM,
                                  name=tag + "_gla_post_b")
    dqs, dkd, dgv, dlam = gla_scan_bwd(do, qs, kd, gl, lam, states, tm=TM_GLA, name=tag + "_gla_scan_b")
    dgq, dgk, dlr, g["wa"], g["ba"] = rowwise_vjp(
        f_gla_pre, [(gl, 256, 0), (gl, 256, 1), lr], [p["wa"], p["ba"]], [dqs, dkd, dlam], [0, 1, 2], [0, 1],
        [MXU_DTYPE, MXU_DTYPE, MXU_DTYPE], tm=TM_GLA, name=tag + "_gla_pre_b")
    dgl = jnp.concatenate([dgq, dgk, _mx(dgv), dr], axis=1)
    secs = [("in_sb", dqkv), ("in_conv", dcv), ("in_gla", dgl), ("in_lr", dlr), ("in_gate", dgt)]
    dh = mm_sum_nt([(d, w[k]) for k, d in secs], name=tag + "_in_dx")
    for k, d in secs:
        g[k] = mm(h, d, ta=True, tn=1536, name=tag + "_" + k + "_dw")
    dx, g["gpre"] = rowwise_vjp(f_rms, [x], [p["gpre"]], [dh], [0], [0], [F32], [dx2], tm=TM, name=tag + "_pre_b")
    return dx, g


def kernel(x, norm_pre, norm_post, ffn1_w_gate, ffn1_w_up, ffn1_w_down, ffn2_w_gate, ffn2_w_up, ffn2_w_down, w_in, conv_w, conv_b, conv_ln_g, conv_ln_b, gla_w_alpha, gla_b_alpha, gla_norm_g, w_branch, w_out, loss_target, m_norm_pre, m_norm_post, m_ffn1_w_gate, m_ffn1_w_up, m_ffn1_w_down, m_ffn2_w_gate, m_ffn2_w_up, m_ffn2_w_down, m_w_in, m_conv_w, m_conv_b, m_conv_ln_g, m_conv_ln_b, m_gla_w_alpha, m_gla_b_alpha, m_gla_norm_g, m_w_branch, m_w_out, v_norm_pre, v_norm_post, v_ffn1_w_gate, v_ffn1_w_up, v_ffn1_w_down, v_ffn2_w_gate, v_ffn2_w_up, v_ffn2_w_down, v_w_in, v_conv_w, v_conv_b, v_conv_ln_g, v_conv_ln_b, v_gla_w_alpha, v_gla_b_alpha, v_gla_norm_g, v_w_branch, v_w_out):
    shards = dict(norm_pre=norm_pre, norm_post=norm_post, ffn1_w_gate=ffn1_w_gate, ffn1_w_up=ffn1_w_up,
                  ffn1_w_down=ffn1_w_down, ffn2_w_gate=ffn2_w_gate, ffn2_w_up=ffn2_w_up, ffn2_w_down=ffn2_w_down, w_in=w_in,
                  conv_w=conv_w, conv_b=conv_b, conv_ln_g=conv_ln_g, conv_ln_b=conv_ln_b, gla_w_alpha=gla_w_alpha,
                  gla_b_alpha=gla_b_alpha, gla_norm_g=gla_norm_g, w_branch=w_branch, w_out=w_out)
    mom_m = dict(zip(WEIGHTS, (m_norm_pre, m_norm_post, m_ffn1_w_gate, m_ffn1_w_up, m_ffn1_w_down, m_ffn2_w_gate,
                               m_ffn2_w_up, m_ffn2_w_down, m_w_in, m_conv_w, m_conv_b, m_conv_ln_g, m_conv_ln_b,
                               m_gla_w_alpha, m_gla_b_alpha, m_gla_norm_g, m_w_branch, m_w_out)))
    mom_v = dict(zip(WEIGHTS, (v_norm_pre, v_norm_post, v_ffn1_w_gate, v_ffn1_w_up, v_ffn1_w_down, v_ffn2_w_gate,
                               v_ffn2_w_up, v_ffn2_w_down, v_w_in, v_conv_w, v_conv_b, v_conv_ln_g, v_conv_ln_b,
                               v_gla_w_alpha, v_gla_b_alpha, v_gla_norm_g, v_w_branch, v_w_out)))
    depth = norm_pre.shape[0]
    members = [shards[k] if k.endswith("down") else jnp.swapaxes(shards[k], 1, 2) for k in FFN]
    ffn_rows = jnp.concatenate([_mx(t).reshape(depth * SLAB, D_MODEL) for t in members], axis=0)
    w6 = all_gather_chips(ffn_rows, name="gather_ffn").reshape(4, len(FFN), depth, SLAB, D_MODEL)
    full = gather_weights(shards, BIG, MXU_DTYPE, 256, "gather_big")
    full.update(gather_weights(shards, SMALL, F32, 16, "gather_small"))

    def layer_params(l):
        ffn = [dict(gpre=full["norm_pre"][l, k:k + 1], gpost=full["norm_post"][l, k:k + 1]) for k in (0, 2)]
        mix = dict(gpre=full["norm_pre"][l, 1:2], gpost=full["norm_post"][l, 1:2],
                   conv_w=jnp.pad(full["conv_w"][l], ((0, CONV_PAD - CONV_WIDTH), (0, 0))), conv_b=conv_b[l:l + 1],
                   ln_g=conv_ln_g[l:l + 1], ln_b=conv_ln_b[l:l + 1],
                   wa=jnp.pad(full["gla_w_alpha"][l], ((0, GLA_RANK_PAD - GLA_RANK), (0, 0))), ba=gla_b_alpha[l:l + 1],
                   gn=gla_norm_g[l:l + 1])
        return ffn, mix

    xs = x[0]
    saved = []
    for l in range(depth):
        w = layer_weights(full, l)
        ffn, mix = layer_params(l)
        xs, r1 = ffn_fwd(xs, ffn[0]["gpre"], ffn[0]["gpost"], w6, 0, l, f"l{l}_ffn1")
        xs, r2 = mixer_fwd(xs, mix, w, f"l{l}_mix")
        xs, r3 = ffn_fwd(xs, ffn[1]["gpre"], ffn[1]["gpost"], w6, 1, l, f"l{l}_ffn2")
        saved.append((w, ffn, mix, r1, r2, r3))
    dx, sq = loss_head(xs, loss_target[0], tm=TM, name="loss_head")
    loss = lax.psum(0.5 * jnp.sum(sq) / D_MODEL, ("x", "y", "c"))

    per_layer = []
    g6 = lax.empty(w6.shape, F32)
    for l in reversed(range(depth)):
        w, ffn, mix, r1, r2, r3 = saved[l]
        g = {}
        dx, gpre2, gpost2, g6 = ffn_bwd(dx, r3, ffn[1]["gpre"], ffn[1]["gpost"], w6, g6, 1, l, f"l{l}_ffn2")
        dx, gm = mixer_bwd(dx, r2, mix, w, f"l{l}_mix")
        dx, gpre0, gpost0, g6 = ffn_bwd(dx, r1, ffn[0]["gpre"], ffn[0]["gpost"], w6, g6, 0, l, f"l{l}_ffn1")
        g["norm_pre"] = jnp.concatenate([gpre0, gm["gpre"], gpre2], axis=0)
        g["norm_post"] = jnp.concatenate([gpost0, gm["gpost"], gpost2], axis=0)
        g["w_in"] = jnp.concatenate([gm["in_sb"], gm["in_conv"], gm["in_gla"], gm["in_lr"][:, :GLA_RANK], gm["in_gate"]],
                                    axis=1)
        g["conv_w"], g["conv_b"] = gm["conv_w"], gm["conv_b"][0]
        g["conv_ln_g"], g["conv_ln_b"] = gm["ln_g"][0], gm["ln_b"][0]
        g["gla_w_alpha"], g["gla_b_alpha"], g["gla_norm_g"] = gm["wa"][:GLA_RANK], gm["ba"][0], gm["gn"][0]
        g["w_branch"] = jnp.stack(gm["branch"])
        g["w_out"] = gm["out"]
        per_layer.append(g)
    per_layer.reverse()
    grads = {k: jnp.stack([g[k] for g in per_layer]) for k in BIG + SMALL + REPLICATED}

    grad_w = scatter_grads(grads, shards)
    ffn_sum = reduce_scatter(g6.reshape(4, len(FFN) * depth * SLAB, D_MODEL), "rs_ffn")
    ffn_sum = ffn_sum.reshape(len(FFN), depth, SLAB, D_MODEL)
    for t, k in enumerate(FFN):
        grad_w[k] = ffn_sum[t] if k.endswith("down") else jnp.swapaxes(ffn_sum[t], 1, 2)
    delta, new_m, new_v = {}, {}, {}
    for k in WEIGHTS:
        delta[k], new_m[k], new_v[k] = adamw(shards[k], grad_w[k], mom_m[k], mom_v[k], name="adamw_" + k)
    return (loss, dx[None], *[grad_w[k] for k in WEIGHTS], *[delta[k] for k in WEIGHTS], *[new_m[k] for k in WEIGHTS],
            *[new_v[k] for k in WEIGHTS])
```

```python
import functools

import jax
import jax.numpy as jnp
from jax import lax
from jax.experimental import pallas as pl
from jax.experimental.pallas import tpu as pltpu

F32 = jnp.float32
MXU_DTYPE = jnp.bfloat16
GRAD_WIRE_DTYPE = jnp.bfloat16
HIGHEST = lax.Precision.HIGHEST
MESH = pl.DeviceIdType.MESH

NORM_EPS = 1e-6
D_MODEL = 1024
D_FF = 2816
BRANCH = 512
CHUNK = 64
CONV_WIDTH = 31
CONV_PAD = 32
CONV_SUB = 256
GLA_RANK = 16
GLA_RANK_PAD = 128
GLA_TAU = 16.0
SB_SCALE = 0.125
SB_CUTOFF = 60.0
GLA_SCALE = 0.125
PACK_COLS = 1024
VMEM_LIMIT = 56 * 1024 * 1024

ADAM_LR, ADAM_B1, ADAM_B2, ADAM_EPS, ADAM_WD, ADAM_STEP = 0.001, 0.9, 0.999, 1e-08, 0.01, 10

NT = (((1,), (1,)), ((), ()))
TN = (((0,), (0,)), ((), ()))
NN = (((1,), (0,)), ((), ()))


def _params(sem=None, vmem=None):
    return pltpu.CompilerParams(dimension_semantics=sem, vmem_limit_bytes=vmem)


def _mx(v):
    return v.astype(MXU_DTYPE)


def _mx_round(v):
    return v.astype(MXU_DTYPE).astype(F32)


def _fit(dim, want):
    if dim <= want:
        return dim
    for d in range(want - want % 128, 0, -128):
        if dim % d == 0:
            return d
    raise ValueError((dim, want))


def mm(a, b, c=None, *, ta=False, tb=False, out_dtype=F32, tm=1024, tn=1024, tk=1024, name):
    K, M = a.shape if ta else a.shape[::-1]
    N = b.shape[0] if tb else b.shape[1]
    assert (b.shape[1] if tb else b.shape[0]) == K, (a.shape, b.shape, ta, tb)
    tm, tn, tk = _fit(M, tm), _fit(N, tn), _fit(K, tk)
    nk = K // tk
    dn = (((0 if ta else 1,), (1 if tb else 0,)), ((), ()))

    def body(*refs):
        if c is None:
            a_ref, b_ref, o_ref, acc = refs
            c_ref = None
        else:
            a_ref, b_ref, c_ref, o_ref, acc = refs
        k = pl.program_id(2)
        p = lax.dot_general(_mx(a_ref[...]), _mx(b_ref[...]), dn, preferred_element_type=F32)

        @pl.when(k == 0)
        def _():
            acc[...] = p

        @pl.when(k > 0)
        def _():
            acc[...] += p

        @pl.when(k == nk - 1)
        def _():
            r = acc[...]
            if c_ref is not None:
                r = r + c_ref[...].astype(F32)
            o_ref[...] = r.astype(o_ref.dtype)

    a_spec = pl.BlockSpec((tk, tm), lambda i, j, k: (k, i)) if ta else pl.BlockSpec((tm, tk), lambda i, j, k: (i, k))
    b_spec = pl.BlockSpec((tn, tk), lambda i, j, k: (j, k)) if tb else pl.BlockSpec((tk, tn), lambda i, j, k: (k, j))
    o_spec = pl.BlockSpec((tm, tn), lambda i, j, k: (i, j))
    ins, in_specs = [a, b], [a_spec, b_spec]
    if c is not None:
        ins.append(c)
        in_specs.append(o_spec)
    return pl.pallas_call(
        body, name=name, grid=(M // tm, N // tn, nk), in_specs=in_specs, out_specs=o_spec,
        out_shape=jax.ShapeDtypeStruct((M, N), out_dtype), scratch_shapes=[pltpu.VMEM((tm, tn), F32)],
        compiler_params=_params(("parallel", "parallel", "arbitrary"), VMEM_LIMIT),
    )(*ins)


def mm_sum_nt(pairs, *, tm=1024, tk=512, name):
    M, N = pairs[0][0].shape[0], pairs[0][1].shape[0]
    tm = _fit(M, tm)
    tks = [_fit(a.shape[1], tk) for a, _ in pairs]
    counts = [a.shape[1] // t for (a, _), t in zip(pairs, tks)]
    starts = [sum(counts[:i]) for i in range(len(pairs))]
    total = sum(counts)

    def body(*refs):
        o_ref, acc = refs[-2:]
        k = pl.program_id(1)

        @pl.when(k == 0)
        def _():
            acc[...] = jnp.zeros_like(acc)

        for i in range(len(pairs)):
            @pl.when((k >= starts[i]) & (k < starts[i] + counts[i]))
            def _(i=i):
                acc[...] += lax.dot_general(_mx(refs[2 * i][...]), _mx(refs[2 * i + 1][...]), NT,
                                            preferred_element_type=F32)

        @pl.when(k == total - 1)
        def _():
            o_ref[...] = acc[...]

    def turn(i):
        return lambda k: jnp.clip(k - starts[i], 0, counts[i] - 1)

    in_specs, ins = [], []
    for i, (a, b) in enumerate(pairs):
        in_specs.append(pl.BlockSpec((tm, tks[i]), lambda m, k, f=turn(i): (m, f(k))))
        in_specs.append(pl.BlockSpec((N, tks[i]), lambda m, k, f=turn(i): (0, f(k))))
        ins += [a, b]
    return pl.pallas_call(
        body, name=name, grid=(M // tm, total), in_specs=in_specs, out_specs=pl.BlockSpec((tm, N), lambda m, k: (m, 0)),
        out_shape=jax.ShapeDtypeStruct((M, N), F32), scratch_shapes=[pltpu.VMEM((tm, N), F32)],
        compiler_params=_params(("parallel", "arbitrary"), VMEM_LIMIT),
    )(*ins)


SLAB = D_FF // 4


def _swiglu_of(ab_ref):
    return _mx(f_swiglu(ab_ref[0].astype(F32), ab_ref[1].astype(F32))[0])


def _gated_spec(rows, index):
    return pl.BlockSpec((2, None, rows, SLAB), index)


def slab_nt(a, w6, t0, n, l, *, out_dtype, tm=1024, name):
    S, K = a.shape
    tm = _fit(S, tm)

    def body(a_ref, w_ref, o_ref):
        o_ref[...] = lax.dot_general(_mx(a_ref[...]), _mx(w_ref[...]), NT, preferred_element_type=F32).astype(o_ref.dtype)

    return pl.pallas_call(
        body, name=name, grid=(S // tm, n),
        in_specs=[pl.BlockSpec((tm, K), lambda i, s: (i, 0)),
                  pl.BlockSpec((None, None, None, SLAB, K), lambda i, s: (s % 4, t0 + s // 4, l, 0, 0))],
        out_specs=pl.BlockSpec((None, tm, SLAB), lambda i, s: (s, i, 0)),
        out_shape=jax.ShapeDtypeStruct((n, S, SLAB), out_dtype),
        compiler_params=_params(("parallel", "parallel"), VMEM_LIMIT),
    )(a, w6)


def slab_nt_swiglu_bwd(d, w6, t0, l, ab, *, tm=1024, name):
    S, K = d.shape
    tm = _fit(S, tm)

    def body(d_ref, w_ref, ab_ref, o_ref):
        dz = lax.dot_general(_mx(d_ref[...]), _mx(w_ref[...]), NT, preferred_element_type=F32)
        _, pullback = jax.vjp(lambda g, u: f_swiglu(g, u)[0], ab_ref[0].astype(F32), ab_ref[1].astype(F32))
        dg, du = pullback(_mx_round(dz))
        o_ref[0] = dg.astype(o_ref.dtype)
        o_ref[1] = du.astype(o_ref.dtype)

    return pl.pallas_call(
        body, name=name, grid=(S // tm, 4),
        in_specs=[pl.BlockSpec((tm, K), lambda i, s: (i, 0)),
                  pl.BlockSpec((None, None, None, SLAB, K), lambda i, s: (s, t0, l, 0, 0)),
                  _gated_spec(tm, lambda i, s: (0, s, i, 0))],
        out_specs=_gated_spec(tm, lambda i, s: (0, s, i, 0)),
        out_shape=jax.ShapeDtypeStruct(ab.shape, MXU_DTYPE),
        compiler_params=_params(("parallel", "parallel"), VMEM_LIMIT),
    )(d, w6, ab)


def slab_nn(a, w6, t0, l, *, gated=False, tail=None, tm=1024, name):
    n, S, _ = a.shape[-3:]
    K = w6.shape[-1]
    tm = _fit(S, tm)
    kind = tail[0] if tail else None
    extra = list(tail[1:]) if tail else []

    def body(a_ref, w_ref, *rest):
        i, s = pl.program_id(0), pl.program_id(1)
        acc = rest[-1]
        lhs = _swiglu_of(a_ref) if gated else _mx(a_ref[...])
        p = jnp.dot(lhs, _mx(w_ref[...]), preferred_element_type=F32)

        @pl.when(s == 0)
        def _():
            acc[...] = p

        @pl.when(s > 0)
        def _():
            acc[...] += p

        @pl.when(s == n - 1)
        def _():
            r = acc[...]
            if kind is None:
                rest[0][...] = r
            elif kind == "half_post":
                x_ref, g_ref, o_ref, x2_ref = rest[:4]
                o_ref[...] = r
                x2_ref[...] = f_half_post(x_ref[...], r, g_ref[...])[0]
            else:
                x_ref, g_ref, d_ref, dx_ref, dg_ref = rest[:5]
                _, pullback = jax.vjp(lambda xv, gv: f_rms(xv, gv)[0], x_ref[...], g_ref[...])
                dxn, dg = pullback(r)
                dx_ref[...] = dxn + d_ref[...]

                @pl.when(i == 0)
                def _():
                    dg_ref[...] = dg

                @pl.when(i > 0)
                def _():
                    dg_ref[...] += dg

    rows = pl.BlockSpec((tm, K), lambda i, s: (i, 0))
    gain = pl.BlockSpec((1, K), lambda i, s: (0, 0))
    tail_in = {None: [], "half_post": [rows, gain], "rms_bwd": [rows, gain, rows]}[kind]
    full, vec = jax.ShapeDtypeStruct((S, K), F32), jax.ShapeDtypeStruct((1, K), F32)
    out_specs, out_shape = {None: (rows, full), "half_post": ([rows, rows], [full, full]),
                            "rms_bwd": ([rows, gain], [full, vec])}[kind]
    return pl.pallas_call(
        body, name=name, grid=(S // tm, n),
        in_specs=[_gated_spec(tm, lambda i, s: (0, s, i, 0)) if gated else
                  pl.BlockSpec((None, tm, SLAB), lambda i, s: (s, i, 0)),
                  pl.BlockSpec((None, None, None, SLAB, K), lambda i, s: (s % 4, t0 + s // 4, l, 0, 0))] + tail_in,
        out_specs=out_specs, out_shape=out_shape, scratch_shapes=[pltpu.VMEM((tm, K), F32)],
        compiler_params=_params(("arbitrary" if kind == "rms_bwd" else "parallel", "arbitrary"), VMEM_LIMIT),
    )(a, w6, *extra)


def slab_tn(a, b, g6, t0, l, *, gated=False, tk=1024, name):
    n, S, _ = a.shape[-3:]
    K = b.shape[1]
    tk = _fit(S, tk)
    nk = S // tk

    def body(a_ref, b_ref, g_ref, o_ref, acc):
        k = pl.program_id(1)
        lhs = _swiglu_of(a_ref) if gated else _mx(a_ref[...])
        p = lax.dot_general(lhs, _mx(b_ref[...]), TN, preferred_element_type=F32)

        @pl.when(k == 0)
        def _():
            acc[...] = p

        @pl.when(k > 0)
        def _():
            acc[...] += p

        @pl.when(k == nk - 1)
        def _():
            o_ref[...] = acc[...]

    return pl.pallas_call(
        body, name=name, grid=(n, nk),
        in_specs=[_gated_spec(tk, lambda s, k: (0, s, k, 0)) if gated else
                  pl.BlockSpec((None, tk, SLAB), lambda s, k: (s, k, 0)),
                  pl.BlockSpec((tk, K), lambda s, k: (k, 0)), pl.BlockSpec(memory_space=pl.ANY)],
        out_specs=pl.BlockSpec((None, None, None, SLAB, K), lambda s, k: (s % 4, t0 + s // 4, l, 0, 0)),
        out_shape=jax.ShapeDtypeStruct(g6.shape, g6.dtype), scratch_shapes=[pltpu.VMEM((SLAB, K), F32)],
        input_output_aliases={2: 0},
        compiler_params=_params(("parallel", "arbitrary"), VMEM_LIMIT),
    )(a, b, g6)


def _row_arg(arg):
    if isinstance(arg, tuple):
        return arg
    return arg, arg.shape[1], 0


def _row_specs(rows, n):
    arrs, specs, avals = [], [], []
    for arg in rows:
        arr, width, cb = _row_arg(arg)
        rb = arr.shape[0] // n
        arrs.append(arr)
        specs.append(pl.BlockSpec((rb, width), lambda i, cb=cb: (i, cb)))
        avals.append(jax.ShapeDtypeStruct((rb, width), F32))
    return arrs, specs, avals


def _par_specs(pars):
    specs = [pl.BlockSpec(p.shape, lambda i, nd=p.ndim: (0,) * nd) for p in pars]
    avals = [jax.ShapeDtypeStruct(p.shape, F32) for p in pars]
    return specs, avals


def rowwise(f, rows, pars, out_dtypes, *, tm, name):
    n = _row_arg(rows[0])[0].shape[0] // tm
    arrs, rspecs, ravals = _row_specs(rows, n)
    pspecs, pavals = _par_specs(pars)
    oavals = jax.eval_shape(f, *ravals, *pavals)
    nin = len(arrs) + len(pars)

    def body(*refs):
        outs = f(*[r[...].astype(F32) for r in refs[:nin]])
        for o_ref, o in zip(refs[nin:], outs):
            o_ref[...] = o.astype(o_ref.dtype)

    return pl.pallas_call(
        body, name=name, grid=(n,), in_specs=rspecs + pspecs,
        out_specs=[pl.BlockSpec(o.shape, lambda i: (i, 0)) for o in oavals],
        out_shape=[jax.ShapeDtypeStruct((n * o.shape[0], o.shape[1]), dt) for o, dt in zip(oavals, out_dtypes)],
        compiler_params=_params(("parallel",), VMEM_LIMIT),
    )(*arrs, *pars)


def rowwise_vjp(f, rows, pars, cots, row_grad, par_grad, d_dtypes, adds=None, *, tm, name):
    n = _row_arg(rows[0])[0].shape[0] // tm
    arrs, rspecs, ravals = _row_specs(rows, n)
    pspecs, pavals = _par_specs(pars)
    carrs, cspecs, _ = _row_specs(cots, n)
    adds = adds or [None] * len(row_grad)
    add_arrs = [a for a in adds if a is not None]
    _, aspecs, _ = _row_specs(add_arrs, n)
    nr, npar, nc, na = len(arrs), len(pars), len(carrs), len(add_arrs)
    diff = list(row_grad) + [nr + j for j in par_grad]
    ngr = len(row_grad)

    def body(*refs):
        ins = [r[...].astype(F32) for r in refs[:nr + npar]]
        cs = tuple(r[...].astype(F32) for r in refs[nr + npar:nr + npar + nc])
        add_refs = list(refs[nr + npar + nc:nr + npar + nc + na])
        outs = refs[nr + npar + nc + na:]

        def g(*d):
            full = list(ins)
            for idx, val in zip(diff, d):
                full[idx] = val
            return f(*full)

        _, pullback = jax.vjp(g, *[ins[idx] for idx in diff])
        ds = pullback(cs)
        for k in range(ngr):
            d = ds[k]
            if adds[k] is not None:
                d = d + add_refs.pop(0)[...].astype(F32)
            outs[k][...] = d.astype(outs[k].dtype)
        i = pl.program_id(0)
        for k in range(ngr, len(diff)):
            @pl.when(i == 0)
            def _(k=k):
                outs[k][...] = ds[k]

            @pl.when(i > 0)
            def _(k=k):
                outs[k][...] += ds[k]

    out_specs = [pl.BlockSpec(ravals[i].shape, lambda i: (i, 0)) for i in row_grad] + [pspecs[j] for j in par_grad]
    out_shape = [jax.ShapeDtypeStruct((arrs[i].shape[0], ravals[i].shape[1]), dt) for i, dt in zip(row_grad, d_dtypes)]
    out_shape += [jax.ShapeDtypeStruct(pars[j].shape, F32) for j in par_grad]
    return pl.pallas_call(
        body, name=name, grid=(n,), in_specs=rspecs + pspecs + cspecs + aspecs, out_specs=out_specs, out_shape=out_shape,
        compiler_params=_params(("arbitrary",), VMEM_LIMIT),
    )(*arrs, *pars, *carrs, *add_arrs)


def _logsig(x):
    return jnp.minimum(x, 0.0) - jnp.log(1.0 + jnp.exp(-jnp.abs(x)))


def _sigmoid(x):
    return 1.0 / (1.0 + jnp.exp(-x))


def _silu(x):
    return x * _sigmoid(x)


def _rms(x, g):
    return x * lax.rsqrt(jnp.mean(x * x, axis=-1, keepdims=True) + NORM_EPS) * g


def f_rms(x, g):
    return (_rms(x, g),)


def f_swiglu(a, b):
    return (_silu(a) * b,)


def f_half_post(x, f, g):
    return (x + 0.5 * _rms(f, g),)


def f_post(x, m, g):
    return (x + _rms(m, g),)


def f_glu(a, g):
    return (a * _sigmoid(g),)


def f_conv_ln(y, lg, lb):
    mu = jnp.mean(y, axis=-1, keepdims=True)
    var = jnp.mean(jnp.square(y - mu), axis=-1, keepdims=True)
    return (_silu((y - mu) * lax.rsqrt(var + NORM_EPS) * lg + lb),)


def f_merge(g0, g1, g2, b0, b1, b2):
    return (_sigmoid(g0) * b0 + _sigmoid(g1) * b1 + _sigmoid(g2) * b2,)


def f_gla_post(o, r, g):
    w = o.shape[1]
    hv = w // 4
    i = lax.broadcasted_iota(jnp.int32, (w, w), 0) // hv
    j = lax.broadcasted_iota(jnp.int32, (w, w), 1) // hv
    avg = jnp.where(i == j, 1.0 / hv, 0.0).astype(F32)
    ms = jnp.dot(o * o, avg, precision=HIGHEST, preferred_element_type=F32)
    return (o * lax.rsqrt(ms + NORM_EPS) * g * _silu(r),)


def f_gla_pre(q, k, lr, wa, ba):
    tm, width = q.shape
    nc = tm // CHUNK
    pre = jnp.dot(_mx_round(lr), _mx_round(wa), precision=HIGHEST, preferred_element_type=F32) + ba
    la = (_logsig(pre) / GLA_TAU).reshape(nc, CHUNK, width)
    i = lax.broadcasted_iota(jnp.int32, (nc, CHUNK, CHUNK), 1)
    j = lax.broadcasted_iota(jnp.int32, (nc, CHUNK, CHUNK), 2)
    later = jnp.where(j > i, 1.0, 0.0).astype(F32)
    to_end = lax.dot_general(later, la, (((2,), (1,)), ((0,), (0,))), precision=HIGHEST, preferred_element_type=F32)
    lam = jnp.exp(jnp.sum(la, axis=1))
    return q * GLA_SCALE, k * jnp.exp(to_end.reshape(tm, width)), lam


def _split_dot(x, u):
    hi = _mx(x)
    lo = _mx(x - hi.astype(F32))
    return jnp.dot(hi, u, preferred_element_type=F32) + jnp.dot(lo, u, preferred_element_type=F32)


def _tri(tq, tk):
    row = lax.broadcasted_iota(jnp.int32, (tq, tk), 0)
    col = lax.broadcasted_iota(jnp.int32, (tq, tk), 1)
    return row, col


def _sb_tile(qh, kh, valid, suf, run):
    z = lax.dot_general(qh, kh, NT, preferred_element_type=F32)
    lp = jnp.minimum(z, 0.0) - jnp.log(1.0 + jnp.exp(-jnp.abs(z)))
    lk = lp - z if valid is None else jnp.where(valid, lp - z, 0.0)
    inc = _split_dot(lk, suf)
    a = jnp.exp(lp + (inc - lk + run))
    if valid is not None:
        a = jnp.where(valid, a, 0.0)
    return lp, a, run + inc[:, 0:1]


def _sticks_left(run0, run1):
    return (jnp.maximum(jnp.max(run0), jnp.max(run1)) > -SB_CUTOFF).astype(jnp.int32)


def sb_attn_fwd(qkv, *, tq, name):
    S = qkv.shape[0]
    tk = tq
    pairs = BRANCH // 128

    def body(q_ref, k_ref, v_ref, o_ref):
        i = pl.program_id(1)
        row, col = _tri(tq, tk)
        suf = _mx(row >= col)
        q = q_ref[...] * SB_SCALE

        def tile(j, carry, valid):
            ks = pl.multiple_of((i - j) * tk, tk)
            kb = k_ref[pl.ds(ks, tk), :]
            vb = v_ref[pl.ds(ks, tk), :]
            new = []
            for h in range(2):
                acc, run = carry[h]
                sl = slice(64 * h, 64 * h + 64)
                _, a, run = _sb_tile(q[:, sl], kb[:, sl], valid, suf, run)
                acc = acc + jnp.dot(_mx(a), vb[:, sl], preferred_element_type=F32)
                new.append((acc, run))
            return _sticks_left(new[0][1], new[1][1]), tuple(new)

        def step(state):
            j, _, carry = state
            return (j + 1, *tile(j, carry, None))

        zero = (jnp.zeros((tq, 64), F32), jnp.zeros((tq, 1), F32))
        first = tile(0, (zero, zero), col < row)
        _, _, res = lax.while_loop(lambda s: (s[0] <= i) & (s[1] > 0), step, (jnp.int32(1), *first))
        o_ref[...] = jnp.concatenate([res[0][0], res[1][0]], axis=1).astype(o_ref.dtype)

    return pl.pallas_call(
        body, name=name, grid=(pairs, S // tq),
        in_specs=[pl.BlockSpec((tq, 128), lambda p, i: (i, p)),
                  pl.BlockSpec((S, 128), lambda p, i: (0, pairs + p)),
                  pl.BlockSpec((S, 128), lambda p, i: (0, 2 * pairs + p))],
        out_specs=pl.BlockSpec((tq, 128), lambda p, i: (i, p)),
        out_shape=jax.ShapeDtypeStruct((S, BRANCH), MXU_DTYPE),
        compiler_params=_params(("parallel", "parallel"), VMEM_LIMIT),
    )(qkv, qkv, qkv)


def sb_attn_bwd(qkv, do, *, tq, name):
    S = qkv.shape[0]
    tk = tq
    nq = S // tq
    pairs = BRANCH // 128

    def body(q_ref, k_ref, v_ref, do_ref, dq_ref, dk_ref, dv_ref, g_sc, b_sc):
        i = pl.program_id(1)

        @pl.when(i == 0)
        def _():
            dk_ref[...] = jnp.zeros_like(dk_ref)
            dv_ref[...] = jnp.zeros_like(dv_ref)

        row, col = _tri(tq, tk)
        suf = _mx(row >= col)
        pre = _mx(row <= col)
        q = q_ref[...] * SB_SCALE
        dout = do_ref[...]

        def tile1(j, carry, valid):
            kblk = i - j
            ks = pl.multiple_of(kblk * tk, tk)
            kb = k_ref[pl.ds(ks, tk), :]
            vb = v_ref[pl.ds(ks, tk), :]
            runs, dvs = [], []
            for h in range(2):
                sl = slice(64 * h, 64 * h + 64)
                lp, a, run = _sb_tile(q[:, sl], kb[:, sl], valid, suf, carry[h])
                da = lax.dot_general(dout[:, sl], vb[:, sl], NT, preferred_element_type=F32)
                beta = jnp.exp(lp)
                g_sc[h, kblk] = (a * da).astype(g_sc.dtype)
                b_sc[h, kblk] = (beta if valid is None else jnp.where(valid, beta, 0.0)).astype(b_sc.dtype)
                dvs.append(lax.dot_general(_mx(a), dout[:, sl], TN, preferred_element_type=F32))
                runs.append(run)
            dv_ref[pl.ds(ks, tk), :] += jnp.concatenate(dvs, axis=1)
            return _sticks_left(runs[0], runs[1]), tuple(runs)

        def sweep1(state):
            j, _, carry = state
            return (j + 1, *tile1(j, carry, None))

        first = tile1(0, (jnp.zeros((tq, 1), F32), jnp.zeros((tq, 1), F32)), col < row)
        tiles, _, _ = lax.while_loop(lambda s: (s[0] <= i) & (s[1] > 0), sweep1, (jnp.int32(1), *first))

        def sweep2(kblk, carry):
            ks = pl.multiple_of(kblk * tk, tk)
            kb = k_ref[pl.ds(ks, tk), :]
            new, dks = [], []
            for h in range(2):
                dq, run = carry[h]
                sl = slice(64 * h, 64 * h + 64)
                g = g_sc[h, kblk]
                beta = b_sc[h, kblk].astype(F32)
                inc = jnp.dot(g, pre, preferred_element_type=F32)
                g = g.astype(F32)
                dz = _mx(g - beta * (inc + run))
                dq = dq + jnp.dot(dz, kb[:, sl], preferred_element_type=F32)
                dks.append(lax.dot_general(dz, q[:, sl], TN, preferred_element_type=F32))
                new.append((dq, run + inc[:, tk - 1:tk]))
            dk_ref[pl.ds(ks, tk), :] += jnp.concatenate(dks, axis=1)
            return tuple(new)

        zero = (jnp.zeros((tq, 64), F32), jnp.zeros((tq, 1), F32))
        res = lax.fori_loop(i + 1 - tiles, i + 1, sweep2, (zero, zero))
        dq_ref[...] = (jnp.concatenate([res[0][0], res[1][0]], axis=1) * SB_SCALE).astype(dq_ref.dtype)

    return pl.pallas_call(
        body, name=name, grid=(pairs, nq),
        in_specs=[pl.BlockSpec((tq, 128), lambda p, i: (i, p)),
                  pl.BlockSpec((S, 128), lambda p, i: (0, pairs + p)),
                  pl.BlockSpec((S, 128), lambda p, i: (0, 2 * pairs + p)),
                  pl.BlockSpec((tq, 128), lambda p, i: (i, p))],
        out_specs=[pl.BlockSpec((tq, 128), lambda p, i: (i, p)),
                   pl.BlockSpec((S, 128), lambda p, i: (0, p)),
                   pl.BlockSpec((S, 128), lambda p, i: (0, p))],
        out_shape=[jax.ShapeDtypeStruct((S, BRANCH), MXU_DTYPE), jax.ShapeDtypeStruct((S, BRANCH), F32),
                   jax.ShapeDtypeStruct((S, BRANCH), F32)],
        scratch_shapes=[pltpu.VMEM((2, nq, tq, tk), MXU_DTYPE), pltpu.VMEM((2, nq, tq, tk), MXU_DTYPE)],
        compiler_params=_params(("parallel", "arbitrary"), VMEM_LIMIT),
    )(qkv, qkv, qkv, do)


def conv_fwd(u, w, b, *, tm, name):
    S, C = u.shape
    hb = tm // CONV_PAD

    def body(u_ref, halo_ref, w_ref, b_ref, y_ref, buf):
        i = pl.program_id(0)
        buf[pl.ds(CONV_PAD, tm), :] = u_ref[...]
        buf[pl.ds(0, CONV_PAD), :] = jnp.where(i > 0, halo_ref[...], 0.0)
        ts = min(tm, CONV_SUB)
        for r in range(0, tm, ts):
            acc = jnp.broadcast_to(b_ref[...], (ts, 128))
            for j in range(CONV_WIDTH):
                acc = acc + buf[pl.ds(r + CONV_PAD - (CONV_WIDTH - 1) + j, ts), :] * w_ref[pl.ds(j, 1), :]
            y_ref[pl.ds(r, ts), :] = acc

    return pl.pallas_call(
        body, name=name, grid=(S // tm, C // 128),
        in_specs=[pl.BlockSpec((tm, 128), lambda i, c: (i, c)),
                  pl.BlockSpec((CONV_PAD, 128), lambda i, c: (jnp.maximum(i * hb - 1, 0), c)),
                  pl.BlockSpec((CONV_PAD, 128), lambda i, c: (0, c)),
                  pl.BlockSpec((1, 128), lambda i, c: (0, c))],
        out_specs=pl.BlockSpec((tm, 128), lambda i, c: (i, c)),
        out_shape=jax.ShapeDtypeStruct((S, C), F32),
        scratch_shapes=[pltpu.VMEM((tm + CONV_PAD, 128), F32)],
        compiler_params=_params(("parallel", "parallel")),
    )(u, u, w, b)


def conv_bwd(dy, u, w, *, tm, name):
    S, C = u.shape
    hb = tm // CONV_PAD
    n = S // tm

    def body(dy_ref, dyn_ref, u_ref, up_ref, w_ref, du_ref, dw_ref, bufy, bufu):
        i = pl.program_id(1)
        dyv = dy_ref[...]
        bufy[pl.ds(0, tm), :] = dyv
        bufy[pl.ds(tm, CONV_PAD), :] = jnp.where(i < n - 1, dyn_ref[...], 0.0)
        bufu[pl.ds(CONV_PAD, tm), :] = u_ref[...]
        bufu[pl.ds(0, CONV_PAD), :] = jnp.where(i > 0, up_ref[...], 0.0)

        @pl.when(i == 0)
        def _():
            dw_ref[...] = jnp.zeros_like(dw_ref)

        ts = min(tm, CONV_SUB)
        for r in range(0, tm, ts):
            dys = dy_ref[pl.ds(r, ts), :]
            acc = jnp.zeros((ts, 128), F32)
            for j in range(CONV_WIDTH):
                acc = acc + bufy[pl.ds(r + CONV_WIDTH - 1 - j, ts), :] * w_ref[pl.ds(j, 1), :]
                shifted = bufu[pl.ds(r + CONV_PAD - (CONV_WIDTH - 1) + j, ts), :]
                dw_ref[pl.ds(j, 1), :] += jnp.sum(dys * shifted, axis=0, keepdims=True)
            du_ref[pl.ds(r, ts), :] = acc
            dw_ref[pl.ds(CONV_WIDTH, 1), :] += jnp.sum(dys, axis=0, keepdims=True)

    return pl.pallas_call(
        body, name=name, grid=(C // 128, n),
        in_specs=[pl.BlockSpec((tm, 128), lambda c, i: (i, c)),
                  pl.BlockSpec((CONV_PAD, 128), lambda c, i: (jnp.minimum((i + 1) * hb, n * hb - 1), c)),
                  pl.BlockSpec((tm, 128), lambda c, i: (i, c)),
                  pl.BlockSpec((CONV_PAD, 128), lambda c, i: (jnp.maximum(i * hb - 1, 0), c)),
                  pl.BlockSpec((CONV_PAD, 128), lambda c, i: (0, c))],
        out_specs=[pl.BlockSpec((tm, 128), lambda c, i: (i, c)), pl.BlockSpec((CONV_PAD, 128), lambda c, i: (0, c))],
        out_shape=[jax.ShapeDtypeStruct((S, C), F32), jax.ShapeDtypeStruct((CONV_PAD, C), F32)],
        scratch_shapes=[pltpu.VMEM((tm + CONV_PAD, 128), F32), pltpu.VMEM((tm + CONV_PAD, 128), F32)],
        compiler_params=_params(("parallel", "arbitrary")),
    )(dy, dy, u, u, w)


GLA_HEADS, GLA_DK, GLA_DV = 4, 64, 128


def gla_scan_fwd(qs, kd, gl, lam, *, tm, name):
    S = qs.shape[0]
    nc = tm // CHUNK

    def body(qs_ref, kd_ref, v_ref, lam_ref, o_ref, st_ref, state):
        @pl.when(pl.program_id(0) == 0)
        def _():
            state[...] = jnp.zeros_like(state)

        for c in range(nc):
            rows = pl.ds(c * CHUNK, CHUNK)
            q, k, v = _mx(qs_ref[rows, :]), _mx(kd_ref[rows, :]), _mx(v_ref[rows, :])
            upd = [lax.dot_general(v[:, h * GLA_DV:(h + 1) * GLA_DV], k[:, h * GLA_DK:(h + 1) * GLA_DK], TN,
                                   preferred_element_type=F32) for h in range(GLA_HEADS)]
            st = state[...] * lam_ref[pl.ds(c, 1), :] + jnp.concatenate(upd, axis=1)
            state[...] = st
            st_ref[c] = st
            stm = _mx(st)
            o = [lax.dot_general(q[:, h * GLA_DK:(h + 1) * GLA_DK], stm[:, h * GLA_DK:(h + 1) * GLA_DK], NT,
                                 preferred_element_type=F32) for h in range(GLA_HEADS)]
            o_ref[rows, :] = jnp.concatenate(o, axis=1)

    dk_all = GLA_HEADS * GLA_DK
    dv_all = GLA_HEADS * GLA_DV
    return pl.pallas_call(
        body, name=name, grid=(S // tm,),
        in_specs=[pl.BlockSpec((tm, dk_all), lambda i: (i, 0)), pl.BlockSpec((tm, dk_all), lambda i: (i, 0)),
                  pl.BlockSpec((tm, dv_all), lambda i: (i, 1)), pl.BlockSpec((nc, dk_all), lambda i: (i, 0))],
        out_specs=[pl.BlockSpec((tm, dv_all), lambda i: (i, 0)), pl.BlockSpec((nc, GLA_DV, dk_all), lambda i: (i, 0, 0))],
        out_shape=[jax.ShapeDtypeStruct((S, dv_all), F32), jax.ShapeDtypeStruct((S // CHUNK, GLA_DV, dk_all), F32)],
        scratch_shapes=[pltpu.VMEM((GLA_DV, dk_all), F32)],
        compiler_params=_params(("arbitrary",)),
    )(qs, kd, gl, lam)


def gla_scan_bwd(do, qs, kd, gl, lam, states, *, tm, name):
    S = qs.shape[0]
    nc = tm // CHUNK
    n = S // tm
    dk_all = GLA_HEADS * GLA_DK
    dv_all = GLA_HEADS * GLA_DV

    def body(do_ref, qs_ref, kd_ref, v_ref, lam_ref, st_ref, prev_ref, dqs_ref, dkd_ref, dv_ref, dlam_ref, carry):
        i = pl.program_id(0)

        @pl.when(i == 0)
        def _():
            carry[...] = jnp.zeros_like(carry)

        for c in reversed(range(nc)):
            rows = pl.ds(c * CHUNK, CHUNK)
            q, k, v, d = _mx(qs_ref[rows, :]), _mx(kd_ref[rows, :]), _mx(v_ref[rows, :]), _mx(do_ref[rows, :])
            st = _mx(st_ref[c])
            before = st_ref[c - 1] if c > 0 else jnp.where(i < n - 1, prev_ref[0], 0.0)
            outer = [lax.dot_general(d[:, h * GLA_DV:(h + 1) * GLA_DV], q[:, h * GLA_DK:(h + 1) * GLA_DK], TN,
                                     preferred_element_type=F32) for h in range(GLA_HEADS)]
            dst = carry[...] + jnp.concatenate(outer, axis=1)
            dstm = _mx(dst)
            dq, dkk, dvv = [], [], []
            for h in range(GLA_HEADS):
                ksl = slice(h * GLA_DK, (h + 1) * GLA_DK)
                vsl = slice(h * GLA_DV, (h + 1) * GLA_DV)
                dq.append(jnp.dot(d[:, vsl], st[:, ksl], preferred_element_type=F32))
                dkk.append(jnp.dot(v[:, vsl], dstm[:, ksl], preferred_element_type=F32))
                dvv.append(lax.dot_general(k[:, ksl], dstm[:, ksl], NT, preferred_element_type=F32))
            dqs_ref[rows, :] = jnp.concatenate(dq, axis=1)
            dkd_ref[rows, :] = jnp.concatenate(dkk, axis=1)
            dv_ref[rows, :] = jnp.concatenate(dvv, axis=1)
            dlam_ref[pl.ds(c, 1), :] = jnp.sum(dst * before, axis=0, keepdims=True)
            carry[...] = dst * lam_ref[pl.ds(c, 1), :]

    rev = lambda i: n - 1 - i
    return pl.pallas_call(
        body, name=name, grid=(n,),
        in_specs=[pl.BlockSpec((tm, dv_all), lambda i: (rev(i), 0)), pl.BlockSpec((tm, dk_all), lambda i: (rev(i), 0)),
                  pl.BlockSpec((tm, dk_all), lambda i: (rev(i), 0)), pl.BlockSpec((tm, dv_all), lambda i: (rev(i), 1)),
                  pl.BlockSpec((nc, dk_all), lambda i: (rev(i), 0)),
                  pl.BlockSpec((nc, GLA_DV, dk_all), lambda i: (rev(i), 0, 0)),
                  pl.BlockSpec((1, GLA_DV, dk_all), lambda i: (jnp.maximum(rev(i) * nc - 1, 0), 0, 0))],
        out_specs=[pl.BlockSpec((tm, dk_all), lambda i: (rev(i), 0)), pl.BlockSpec((tm, dk_all), lambda i: (rev(i), 0)),
                   pl.BlockSpec((tm, dv_all), lambda i: (rev(i), 0)), pl.BlockSpec((nc, dk_all), lambda i: (rev(i), 0))],
        out_shape=[jax.ShapeDtypeStruct((S, dk_all), F32), jax.ShapeDtypeStruct((S, dk_all), F32),
                   jax.ShapeDtypeStruct((S, dv_all), F32), jax.ShapeDtypeStruct((S // CHUNK, dk_all), F32)],
        scratch_shapes=[pltpu.VMEM((GLA_DV, dk_all), F32)],
        compiler_params=_params(("arbitrary",)),
    )(do, qs, kd, gl, lam, states, states)


def loss_head(y, target, *, tm, name):
    S, D = y.shape

    def body(y_ref, t_ref, dy_ref, sq_ref):
        err = y_ref[...] - t_ref[...]
        dy_ref[...] = err * (1.0 / D)
        part = jnp.sum(err * err, axis=0, keepdims=True)

        @pl.when(pl.program_id(0) == 0)
        def _():
            sq_ref[...] = part

        @pl.when(pl.program_id(0) > 0)
        def _():
            sq_ref[...] += part

    return pl.pallas_call(
        body, name=name, grid=(S // tm,),
        in_specs=[pl.BlockSpec((tm, D), lambda i: (i, 0)), pl.BlockSpec((tm, D), lambda i: (i, 0))],
        out_specs=[pl.BlockSpec((tm, D), lambda i: (i, 0)), pl.BlockSpec((1, D), lambda i: (0, 0))],
        out_shape=[jax.ShapeDtypeStruct((S, D), F32), jax.ShapeDtypeStruct((1, D), F32)],
        compiler_params=_params(("arbitrary",)),
    )(y, target)


def f_adamw(w, g, m, v):
    m = ADAM_B1 * m + (1.0 - ADAM_B1) * g
    v = ADAM_B2 * v + (1.0 - ADAM_B2) * jnp.square(g)
    m_hat = m / (1.0 - ADAM_B1 ** ADAM_STEP)
    v_hat = v / (1.0 - ADAM_B2 ** ADAM_STEP)
    return -ADAM_LR * (m_hat / (jnp.sqrt(v_hat) + ADAM_EPS) + ADAM_WD * w), m, v


def adamw(w, g, m, v, *, name):
    shape = w.shape
    cols = shape[-1]
    rows = w.size // cols
    tm = rows
    while tm % 16 == 0 and tm * cols * 4 > (1 << 20):
        tm //= 2
    flat = [t.reshape(rows, cols) for t in (w, g, m, v)]
    outs = rowwise(f_adamw, flat, [], [F32, F32, F32], tm=tm, name=name)
    return [o.reshape(shape) for o in outs]


def _place():
    x, y, c = lax.axis_index("x"), lax.axis_index("y"), lax.axis_index("c")
    return x, y, c, [(1 - x, y), (x, 1 - y), (1 - x, 1 - y)]


def _my_chip():
    return 2 * lax.axis_index("x") + lax.axis_index("y")


def _any():
    return pl.BlockSpec(memory_space=pl.ANY)


DMA_CHUNKS = 8


def _chunks(rows):
    n = DMA_CHUNKS
    while n > 1 and rows % (n * 16):
        n //= 2
    return [(k * (rows // n), rows // n) for k in range(n)]


def _start_chunked(make, rows):
    for off, size in _chunks(rows):
        make(off, size).start()


def all_gather_chips(v, *, name):
    R, C = v.shape
    H = R // 2
    Q = H // 2
    assert R % 64 == 0, R

    def body(v_ref, o_ref, send, recv, psend, precv, fsend, frecv):
        x, y, c, chips = _place()
        me = 2 * x + y
        ids = [2 * px + py for px, py in chips]
        mine = pl.ds(pl.multiple_of(c * H, 8), H)
        other = pl.ds(pl.multiple_of((1 - c) * H, 8), H)

        def my_rows(off, size):
            return pl.ds(pl.multiple_of(c * H + off, 8), size)

        def cross(j, src_chip, rows):
            return pltpu.make_async_remote_copy(
                src_ref=v_ref.at[rows], dst_ref=o_ref.at[src_chip, rows], send_sem=send.at[j], recv_sem=recv.at[j],
                device_id=(*chips[j], c), device_id_type=MESH)

        def passed(j, off, size):
            rows = my_rows(j * Q + off, size)
            src_chip = ids[1 - j]
            return pltpu.make_async_remote_copy(
                src_ref=o_ref.at[src_chip, rows], dst_ref=o_ref.at[src_chip, rows], send_sem=psend.at[j],
                recv_sem=precv.at[j], device_id=(*chips[j], c), device_id_type=MESH)

        def landed(j):
            rows = my_rows(j * Q, Q)
            return pltpu.make_async_remote_copy(
                src_ref=o_ref.at[ids[2], rows], dst_ref=o_ref.at[ids[2], rows], send_sem=psend.at[j],
                recv_sem=precv.at[j], device_id=(*chips[j], c), device_id_type=MESH)

        def handed(j, rows):
            return pltpu.make_async_remote_copy(
                src_ref=o_ref.at[ids[j], rows], dst_ref=o_ref.at[ids[j], rows], send_sem=fsend.at[j],
                recv_sem=frecv.at[j], device_id=(x, y, 1 - c), device_id_type=MESH)

        for j in range(2):
            _start_chunked(lambda off, size, j=j: cross(j, me, my_rows(off, size)), H)
        for j in (1, 0):
            cross(j, ids[j], mine).wait_recv()
            _start_chunked(functools.partial(passed, 1 - j), Q)
            _start_chunked(lambda off, size, j=j: handed(j, my_rows(off, size)), H)
        for j in range(2):
            landed(j).wait_recv()
        _start_chunked(lambda off, size: handed(2, my_rows(off, size)), H)
        for j in range(3):
            handed(j, other).wait_recv()
        for j in range(2):
            cross(j, me, mine).wait_send()
            passed(j, 0, Q).wait_send()
        for j in range(3):
            handed(j, mine).wait_send()

    got = pl.pallas_call(
        body, name=name, in_specs=[_any()], out_specs=_any(), out_shape=jax.ShapeDtypeStruct((4, R, C), v.dtype),
        scratch_shapes=[pltpu.SemaphoreType.DMA((2,)), pltpu.SemaphoreType.DMA((2,)), pltpu.SemaphoreType.DMA((2,)),
                        pltpu.SemaphoreType.DMA((2,)), pltpu.SemaphoreType.DMA((3,)), pltpu.SemaphoreType.DMA((3,))],
    )(v)
    return lax.dynamic_update_slice(got, v[None], (_my_chip(), 0, 0))


def swap_halves(g, *, name):
    n, R, C = g.shape
    H = R // 2

    def body(g_ref, theirs_ref, send, recv):
        x, y, c, _ = _place()

        def give(q, off, size):
            return pltpu.make_async_remote_copy(
                src_ref=g_ref.at[q, pl.ds(pl.multiple_of((1 - c) * H + off, 8), size)],
                dst_ref=theirs_ref.at[q, pl.ds(off, size)], send_sem=send.at[q], recv_sem=recv.at[q],
                device_id=(x, y, 1 - c), device_id_type=MESH)

        for q in range(n):
            _start_chunked(functools.partial(give, q), H)
        for q in range(n):
            give(q, 0, H).wait()

    return pl.pallas_call(
        body, name=name, in_specs=[_any()], out_specs=_any(), out_shape=jax.ShapeDtypeStruct((n, H, C), g.dtype),
        scratch_shapes=[pltpu.SemaphoreType.DMA((n,)), pltpu.SemaphoreType.DMA((n,))],
    )(g)


def exchange_pieces(p, *, name):
    H = p.shape[1]

    def body(p_ref, b_ref, send, recv):
        x, y, c, chips = _place()
        me = 2 * x + y
        ids = [2 * px + py for px, py in chips]

        def cross(j, piece, slot, off, size):
            return pltpu.make_async_remote_copy(
                src_ref=p_ref.at[piece, pl.ds(off, size)], dst_ref=b_ref.at[slot, pl.ds(off, size)], send_sem=send.at[j],
                recv_sem=recv.at[j], device_id=(*chips[j], c), device_id_type=MESH)

        for j in range(3):
            _start_chunked(functools.partial(cross, j, ids[j], me), H)
        for j in range(3):
            cross(j, me, ids[j], 0, H).wait_recv()
        for j in range(3):
            cross(j, ids[j], me, 0, H).wait_send()

    got = pl.pallas_call(
        body, name=name, in_specs=[_any()], out_specs=_any(), out_shape=jax.ShapeDtypeStruct(p.shape, p.dtype),
        scratch_shapes=[pltpu.SemaphoreType.DMA((3,)), pltpu.SemaphoreType.DMA((3,))],
    )(p)
    me = _my_chip()
    return lax.dynamic_update_slice(got, lax.dynamic_slice_in_dim(p, me, 1, axis=0), (me, 0, 0))


def join_halves(f, *, name):
    H, C = f.shape

    def body(f_ref, o_ref, send, recv):
        x, y, c, _ = _place()

        def give(off, size):
            return pltpu.make_async_remote_copy(
                src_ref=f_ref.at[pl.ds(off, size)], dst_ref=o_ref.at[pl.ds(pl.multiple_of(c * H + off, 8), size)],
                send_sem=send, recv_sem=recv, device_id=(x, y, 1 - c), device_id_type=MESH)

        _start_chunked(give, H)
        give(0, H).wait()

    both = pl.pallas_call(
        body, name=name, in_specs=[_any()], out_specs=_any(), out_shape=jax.ShapeDtypeStruct((2 * H, C), f.dtype),
        scratch_shapes=[pltpu.SemaphoreType.DMA, pltpu.SemaphoreType.DMA],
    )(f)
    return lax.dynamic_update_slice(both, f, (lax.axis_index("c") * H, 0))


def _row_tile(rows, want):
    tm = want
    while rows % tm:
        tm //= 2
    return tm


def _add_pair(g, theirs, out_dtype, *, name):
    n, R, C = g.shape
    H = R // 2
    tm = _row_tile(H, 512)
    nb = H // tm

    def body(c_ref, g_ref, t_ref, o_ref):
        o_ref[...] = (g_ref[...] + t_ref[...]).astype(o_ref.dtype)

    grid_spec = pltpu.PrefetchScalarGridSpec(
        num_scalar_prefetch=1, grid=(n, nb),
        in_specs=[pl.BlockSpec((None, tm, C), lambda q, i, c: (q, c[0] * nb + i, 0)),
                  pl.BlockSpec((None, tm, C), lambda q, i, c: (q, i, 0))],
        out_specs=pl.BlockSpec((None, tm, C), lambda q, i, c: (q, i, 0)))
    return pl.pallas_call(
        body, name=name, grid_spec=grid_spec, out_shape=jax.ShapeDtypeStruct((n, H, C), out_dtype),
        compiler_params=_params(("parallel", "parallel")),
    )(lax.axis_index("c").astype(jnp.int32).reshape(1), g, theirs)


def _add_stack(b, *, name):
    n, H, C = b.shape
    tm = _row_tile(H, 256)

    def body(b_ref, o_ref):
        s = b_ref[0].astype(F32)
        for k in range(1, n):
            s = s + b_ref[k].astype(F32)
        o_ref[...] = s

    return pl.pallas_call(
        body, name=name, grid=(H // tm,), in_specs=[pl.BlockSpec((n, tm, C), lambda i: (0, i, 0))],
        out_specs=pl.BlockSpec((tm, C), lambda i: (i, 0)), out_shape=jax.ShapeDtypeStruct((H, C), F32),
        compiler_params=_params(("parallel",)),
    )(b)


def reduce_scatter(g, tag):
    pair = _add_pair(g, swap_halves(g, name=tag + "_swap_halves"), GRAD_WIRE_DTYPE, name=tag + "_add_pair")
    total = _add_stack(exchange_pieces(pair, name=tag + "_exchange"), name=tag + "_add_chips")
    return join_halves(total, name=tag + "_join")


FFN = ("ffn1_w_gate", "ffn1_w_up", "ffn2_w_gate", "ffn2_w_up", "ffn1_w_down", "ffn2_w_down")
BIG = ("w_in", "w_branch", "w_out")
SMALL = ("norm_pre", "norm_post", "conv_w", "gla_w_alpha")
REPLICATED = ("conv_b", "conv_ln_g", "conv_ln_b", "gla_b_alpha", "gla_norm_g")
WEIGHTS = ("norm_pre", "norm_post", "ffn1_w_gate", "ffn1_w_up", "ffn1_w_down", "ffn2_w_gate", "ffn2_w_up", "ffn2_w_down",
           "w_in", "conv_w", "conv_b", "conv_ln_g", "conv_ln_b", "gla_w_alpha", "gla_b_alpha", "gla_norm_g", "w_branch",
           "w_out")
SHARD_AXIS = {"ffn1_w_gate": 2, "ffn1_w_up": 2, "ffn1_w_down": 1, "ffn2_w_gate": 2, "ffn2_w_up": 2, "ffn2_w_down": 1,
              "w_in": 2, "w_branch": 3, "w_out": 1, "norm_pre": 2, "norm_post": 2, "conv_w": 2, "gla_w_alpha": 2}


def _size(shape):
    n = 1
    for d in shape:
        n *= d
    return n


def _pack_rows(shape):
    return -(-_size(shape) // (16 * PACK_COLS)) * 16


def _pack(arrs, dtype, row_mult):
    parts, rows = [], 0
    for a in arrs:
        r = _pack_rows(a.shape)
        flat = a.astype(dtype).reshape(-1)
        if r * PACK_COLS != flat.shape[0]:
            flat = jnp.pad(flat, (0, r * PACK_COLS - flat.shape[0]))
        parts.append(flat.reshape(r, PACK_COLS))
        rows += r
    if rows % row_mult:
        parts.append(jnp.zeros((row_mult - rows % row_mult, PACK_COLS), dtype))
    return jnp.concatenate(parts, axis=0)


def _unpack(buf, shapes):
    lead = buf.shape[:-2]
    out, off = [], 0
    for s in shapes:
        r = _pack_rows(s)
        part = lax.slice_in_dim(buf, off, off + r, axis=buf.ndim - 2).reshape(lead + (r * PACK_COLS,))
        if r * PACK_COLS != _size(s):
            part = lax.slice_in_dim(part, 0, _size(s), axis=part.ndim - 1)
        out.append(part.reshape(lead + tuple(s)))
        off += r
    return out


def gather_weights(shards, names, dtype, row_mult, name):
    packed = _pack([shards[k] for k in names], dtype, row_mult)
    parts = _unpack(all_gather_chips(packed, name=name), [shards[k].shape for k in names])
    return {k: jnp.concatenate([p[q] for q in range(4)], axis=SHARD_AXIS[k]) for k, p in zip(names, parts)}


def scatter_grads(grads, shards):
    sharded = BIG + SMALL
    pieces = []
    for q in range(4):
        part = []
        for k in sharded:
            w = shards[k].shape[SHARD_AXIS[k]]
            part.append(lax.slice_in_dim(grads[k], q * w, (q + 1) * w, axis=SHARD_AXIS[k]))
        part += [grads[k] for k in REPLICATED]
        pieces.append(_pack(part, F32, 512))
    total = reduce_scatter(jnp.stack(pieces), "rs_mix")
    names = sharded + REPLICATED
    return dict(zip(names, _unpack(total, [shards[k].shape for k in names])))


TM = 512
TM_MERGE = 256
TM_CONV = 1024
TM_GLA = 512
TQ = 256

IN_SB, IN_CONV, IN_GLA, IN_LR, IN_GATE = 0, 1536, 2560, 4096, 4112
IN_END = 7184


def layer_weights(full, l):
    w = {}
    win = full["w_in"][l]
    w["in_sb"] = win[:, IN_SB:IN_CONV]
    w["in_conv"] = win[:, IN_CONV:IN_GLA]
    w["in_gla"] = win[:, IN_GLA:IN_LR]
    w["in_lr"] = jnp.pad(win[:, IN_LR:IN_GATE], ((0, 0), (0, GLA_RANK_PAD - GLA_RANK)))
    w["in_gate"] = win[:, IN_GATE:IN_END]
    w["branch"] = [full["w_branch"][l, g] for g in range(3)]
    w["out"] = full["w_out"][l]
    return w


def ffn_fwd(x, gpre, gpost, w6, f, l, tag):
    S = x.shape[0]
    h = rowwise(f_rms, [x], [gpre], [MXU_DTYPE], tm=TM, name=tag + "_pre")[0]
    ab = slab_nt(h, w6, 2 * f, 8, l, out_dtype=MXU_DTYPE, tm=2048, name=tag + "_gu").reshape(2, 4, S, SLAB)
    out, x2 = slab_nn(ab, w6, 4 + f, l, gated=True, tail=("half_post", x, gpost), name=tag + "_down")
    return x2, (x, h, ab, out)


def ffn_bwd(dx2, res, gpre, gpost, w6, g6, f, l, tag):
    x, h, ab, out = res
    S = x.shape[0]
    df, dgpost = rowwise_vjp(f_half_post, [x, out], [gpost], [dx2], [1], [0], [MXU_DTYPE], tm=TM, name=tag + "_post_b")
    dab = slab_nt_swiglu_bwd(df, w6, 4 + f, l, ab, name=tag + "_down_dx").reshape(8, S, SLAB)
    g6 = slab_tn(ab, df, g6, 4 + f, l, gated=True, name=tag + "_down_dw")
    dx, dgpre = slab_nn(dab, w6, 2 * f, l, tail=("rms_bwd", x, gpre, dx2), name=tag + "_gu_dx")
    g6 = slab_tn(dab, h, g6, 2 * f, l, name=tag + "_gu_dw")
    return dx, dgpre, dgpost, g6


def mixer_fwd(x, p, w, tag):
    h = rowwise(f_rms, [x], [p["gpre"]], [MXU_DTYPE], tm=TM, name=tag + "_pre")[0]
    qkv = mm(h, w["in_sb"], out_dtype=MXU_DTYPE, tn=512, name=tag + "_in_sb")
    cv = mm(h, w["in_conv"], name=tag + "_in_conv")
    gl = mm(h, w["in_gla"], out_dtype=MXU_DTYPE, tn=512, name=tag + "_in_gla")
    lr = mm(h, w["in_lr"], out_dtype=MXU_DTYPE, name=tag + "_in_lr")
    gt = mm(h, w["in_gate"], out_dtype=MXU_DTYPE, name=tag + "_in_gate")
    sb = sb_attn_fwd(qkv, tq=TQ, name=tag + "_sb")
    u = rowwise(f_glu, [(cv, BRANCH, 0), (cv, BRANCH, 1)], [], [F32], tm=TM, name=tag + "_glu")[0]
    y = conv_fwd(u, p["conv_w"], p["conv_b"], tm=TM_CONV, name=tag + "_conv")
    cb = rowwise(f_conv_ln, [y], [p["ln_g"], p["ln_b"]], [MXU_DTYPE], tm=TM, name=tag + "_ln")[0]
    qs, kd, lam = rowwise(f_gla_pre, [(gl, 256, 0), (gl, 256, 1), lr], [p["wa"], p["ba"]], [MXU_DTYPE, MXU_DTYPE, F32],
                          tm=TM_GLA, name=tag + "_gla_pre")
    o, states = gla_scan_fwd(qs, kd, gl, lam, tm=TM_GLA, name=tag + "_gla_scan")
    gb = rowwise(f_gla_post, [o, (gl, BRANCH, 2)], [p["gn"]], [MXU_DTYPE], tm=TM, name=tag + "_gla_post")[0]
    branches = [sb, cb, gb]
    bd = [mm(branches[g], w["branch"][g], out_dtype=MXU_DTYPE, name=tag + f"_branch{g}") for g in range(3)]
    merged = rowwise(f_merge, [(gt, D_MODEL, 0), (gt, D_MODEL, 1), (gt, D_MODEL, 2)] + bd, [], [MXU_DTYPE], tm=TM_MERGE,
                     name=tag + "_merge")[0]
    m = mm(merged, w["out"], name=tag + "_out")
    x2 = rowwise(f_post, [x, m], [p["gpost"]], [F32], tm=TM, name=tag + "_post")[0]
    return x2, (x, h, qkv, cv, gl, lr, gt, u, y, qs, kd, lam, o, states, branches, bd, merged, m)


def mixer_bwd(dx2, res, p, w, tag):
    x, h, qkv, cv, gl, lr, gt, u, y, qs, kd, lam, o, states, branches, bd, merged, m = res
    g = {}
    dm, g["gpost"] = rowwise_vjp(f_post, [x, m], [p["gpost"]], [dx2], [1], [0], [MXU_DTYPE], tm=TM, name=tag + "_post_b")
    dmerged = mm(dm, w["out"], tb=True, out_dtype=MXU_DTYPE, name=tag + "_out_dx")
    g["out"] = mm(merged, dm, ta=True, name=tag + "_out_dw")
    gts = [(gt, D_MODEL, 0), (gt, D_MODEL, 1), (gt, D_MODEL, 2)]
    dgate = rowwise_vjp(f_merge, gts + bd, [], [dmerged], [0, 1, 2, 3, 4, 5], [], [MXU_DTYPE] * 6, tm=TM_MERGE,
                        name=tag + "_merge_b")
    dgt = jnp.concatenate(dgate[:3], axis=1)
    dbd = dgate[3:]
    g["branch"] = [mm(branches[k], dbd[k], ta=True, name=tag + f"_branch{k}_dw") for k in range(3)]
    dsb = mm(dbd[0], w["branch"][0], tb=True, out_dtype=MXU_DTYPE, name=tag + "_branch0_dx")
    dq, dk, dv = sb_attn_bwd(qkv, dsb, tq=TQ, name=tag + "_sb_b")
    dqkv = jnp.concatenate([dq, _mx(dk), _mx(dv)], axis=1)
    dcb = mm(dbd[1], w["branch"][1], tb=True, name=tag + "_branch1_dx")
    dy, g["ln_g"], g["ln_b"] = rowwise_vjp(f_conv_ln, [y], [p["ln_g"], p["ln_b"]], [dcb], [0], [0, 1], [F32], tm=TM,
                                           name=tag + "_ln_b")
    du, dwb = conv_bwd(dy, u, p["conv_w"], tm=TM_CONV, name=tag + "_conv_b")
    g["conv_w"], g["conv_b"] = dwb[:CONV_WIDTH], dwb[CONV_WIDTH:CONV_WIDTH + 1]
    dca, dcg = rowwise_vjp(f_glu, [(cv, BRANCH, 0), (cv, BRANCH, 1)], [], [du], [0, 1], [], [MXU_DTYPE, MXU_DTYPE], tm=TM,
                           name=tag + "_glu_b")
    dcv = jnp.concatenate([dca, dcg], axis=1)
    dgb = mm(dbd[2], w["branch"][2], tb=True, name=tag + "_branch2_dx")
    do, dr, g["gn"] = rowwise_vjp(f_gla_post, [o, (gl, BRANCH, 2)], [p["gn"]], [dgb], [0, 1], [0], [F32, MXU_DTYPE], tm=TM,
                                  name=tag + "_gla_post_b")
    dqs, dkd, dgv, dlam = gla_scan_bwd(do, qs, kd, gl, lam, states, tm=TM_GLA, name=tag + "_gla_scan_b")
    dgq, dgk, dlr, g["wa"], g["ba"] = rowwise_vjp(
        f_gla_pre, [(gl, 256, 0), (gl, 256, 1), lr], [p["wa"], p["ba"]], [dqs, dkd, dlam], [0, 1, 2], [0, 1],
        [MXU_DTYPE, MXU_DTYPE, MXU_DTYPE], tm=TM_GLA, name=tag + "_gla_pre_b")
    dgl = jnp.concatenate([dgq, dgk, _mx(dgv), dr], axis=1)
    secs = [("in_sb", dqkv), ("in_conv", dcv), ("in_gla", dgl), ("in_lr", dlr), ("in_gate", dgt)]
    dh = mm_sum_nt([(d, w[k]) for k, d in secs], name=tag + "_in_dx")
    for k, d in secs:
        g[k] = mm(h, d, ta=True, tn=1536, name=tag + "_" + k + "_dw")
    dx, g["gpre"] = rowwise_vjp(f_rms, [x], [p["gpre"]], [dh], [0], [0], [F32], [dx2], tm=TM, name=tag + "_pre_b")
    return dx, g


def kernel(x, norm_pre, norm_post, ffn1_w_gate, ffn1_w_up, ffn1_w_down, ffn2_w_gate, ffn2_w_up, ffn2_w_down, w_in, conv_w, conv_b, conv_ln_g, conv_ln_b, gla_w_alpha, gla_b_alpha, gla_norm_g, w_branch, w_out, loss_target, m_norm_pre, m_norm_post, m_ffn1_w_gate, m_ffn1_w_up, m_ffn1_w_down, m_ffn2_w_gate, m_ffn2_w_up, m_ffn2_w_down, m_w_in, m_conv_w, m_conv_b, m_conv_ln_g, m_conv_ln_b, m_gla_w_alpha, m_gla_b_alpha, m_gla_norm_g, m_w_branch, m_w_out, v_norm_pre, v_norm_post, v_ffn1_w_gate, v_ffn1_w_up, v_ffn1_w_down, v_ffn2_w_gate, v_ffn2_w_up, v_ffn2_w_down, v_w_in, v_conv_w, v_conv_b, v_conv_ln_g, v_conv_ln_b, v_gla_w_alpha, v_gla_b_alpha, v_gla_norm_g, v_w_branch, v_w_out):
    shards = dict(norm_pre=norm_pre, norm_post=norm_post, ffn1_w_gate=ffn1_w_gate, ffn1_w_up=ffn1_w_up,
                  ffn1_w_down=ffn1_w_down, ffn2_w_gate=ffn2_w_gate, ffn2_w_up=ffn2_w_up, ffn2_w_down=ffn2_w_down, w_in=w_in,
                  conv_w=conv_w, conv_b=conv_b, conv_ln_g=conv_ln_g, conv_ln_b=conv_ln_b, gla_w_alpha=gla_w_alpha,
                  gla_b_alpha=gla_b_alpha, gla_norm_g=gla_norm_g, w_branch=w_branch, w_out=w_out)
    mom_m = dict(zip(WEIGHTS, (m_norm_pre, m_norm_post, m_ffn1_w_gate, m_ffn1_w_up, m_ffn1_w_down, m_ffn2_w_gate,
                               m_ffn2_w_up, m_ffn2_w_down, m_w_in, m_conv_w, m_conv_b, m_conv_ln_g, m_conv_ln_b,
                               m_gla_w_alpha, m_gla_b_alpha, m_gla_norm_g, m_w_branch, m_w_out)))
    mom_v = dict(zip(WEIGHTS, (v_norm_pre, v_norm_post, v_ffn1_w_gate, v_ffn1_w_up, v_ffn1_w_down, v_ffn2_w_gate,
                               v_ffn2_w_up, v_ffn2_w_down, v_w_in, v_conv_w, v_conv_b, v_conv_ln_g, v_conv_ln_b,
                               v_gla_w_alpha, v_gla_b_alpha, v_gla_norm_g, v_w_branch, v_w_out)))
    depth = norm_pre.shape[0]
    members = [shards[k] if k.endswith("down") else jnp.swapaxes(shards[k], 1, 2) for k in FFN]
    ffn_rows = jnp.concatenate([_mx(t).reshape(depth * SLAB, D_MODEL) for t in members], axis=0)
    w6 = all_gather_chips(ffn_rows, name="gather_ffn").reshape(4, len(FFN), depth, SLAB, D_MODEL)
    full = gather_weights(shards, BIG, MXU_DTYPE, 256, "gather_big")
    full.update(gather_weights(shards, SMALL, F32, 16, "gather_small"))

    def layer_params(l):
        ffn = [dict(gpre=full["norm_pre"][l, k:k + 1], gpost=full["norm_post"][l, k:k + 1]) for k in (0, 2)]
        mix = dict(gpre=full["norm_pre"][l, 1:2], gpost=full["norm_post"][l, 1:2],
                   conv_w=jnp.pad(full["conv_w"][l], ((0, CONV_PAD - CONV_WIDTH), (0, 0))), conv_b=conv_b[l:l + 1],
                   ln_g=conv_ln_g[l:l + 1], ln_b=conv_ln_b[l:l + 1],
                   wa=jnp.pad(full["gla_w_alpha"][l], ((0, GLA_RANK_PAD - GLA_RANK), (0, 0))), ba=gla_b_alpha[l:l + 1],
                   gn=gla_norm_g[l:l + 1])
        return ffn, mix

    xs = x[0]
    saved = []
    for l in range(depth):
        w = layer_weights(full, l)
        ffn, mix = layer_params(l)
        xs, r1 = ffn_fwd(xs, ffn[0]["gpre"], ffn[0]["gpost"], w6, 0, l, f"l{l}_ffn1")
        xs, r2 = mixer_fwd(xs, mix, w, f"l{l}_mix")
        xs, r3 = ffn_fwd(xs, ffn[1]["gpre"], ffn[1]["gpost"], w6, 1, l, f"l{l}_ffn2")
        saved.append((w, ffn, mix, r1, r2, r3))
    dx, sq = loss_head(xs, loss_target[0], tm=TM, name="loss_head")
    loss = lax.psum(0.5 * jnp.sum(sq) / D_MODEL, ("x", "y", "c"))

    per_layer = []
    g6 = lax.empty(w6.shape, F32)
    for l in reversed(range(depth)):
        w, ffn, mix, r1, r2, r3 = saved[l]
        g = {}
        dx, gpre2, gpost2, g6 = ffn_bwd(dx, r3, ffn[1]["gpre"], ffn[1]["gpost"], w6, g6, 1, l, f"l{l}_ffn2")
        dx, gm = mixer_bwd(dx, r2, mix, w, f"l{l}_mix")
        dx, gpre0, gpost0, g6 = ffn_bwd(dx, r1, ffn[0]["gpre"], ffn[0]["gpost"], w6, g6, 0, l, f"l{l}_ffn1")
        g["norm_pre"] = jnp.concatenate([gpre0, gm["gpre"], gpre2], axis=0)
        g["norm_post"] = jnp.concatenate([gpost0, gm["gpost"], gpost2], axis=0)
        g["w_in"] = jnp.concatenate([gm["in_sb"], gm["in_conv"], gm["in_gla"], gm["in_lr"][:, :GLA_RANK], gm["in_gate"]],
                                    axis=1)
        g["conv_w"], g["conv_b"] = gm["conv_w"], gm["conv_b"][0]
        g["conv_ln_g"], g["conv_ln_b"] = gm["ln_g"][0], gm["ln_b"][0]
        g["gla_w_alpha"], g["gla_b_alpha"], g["gla_norm_g"] = gm["wa"][:GLA_RANK], gm["ba"][0], gm["gn"][0]
        g["w_branch"] = jnp.stack(gm["branch"])
        g["w_out"] = gm["out"]
        per_layer.append(g)
    per_layer.reverse()
    grads = {k: jnp.stack([g[k] for g in per_layer]) for k in BIG + SMALL + REPLICATED}

    grad_w = scatter_grads(grads, shards)
    ffn_sum = reduce_scatter(g6.reshape(4, len(FFN) * depth * SLAB, D_MODEL), "rs_ffn")
    ffn_sum = ffn_sum.reshape(len(FFN), depth, SLAB, D_MODEL)
    for t, k in enumerate(FFN):
        grad_w[k] = ffn_sum[t] if k.endswith("down") else jnp.swapaxes(ffn_sum[t], 1, 2)
    delta, new_m, new_v = {}, {}, {}
    for k in WEIGHTS:
        delta[k], new_m[k], new_v[k] = adamw(shards[k], grad_w[k], mom_m[k], mom_v[k], name="adamw_" + k)
    return (loss, dx[None], *[grad_w[k] for k in WEIGHTS], *[delta[k] for k in WEIGHTS], *[new_m[k] for k in WEIGHTS],
            *[new_v[k] for k in WEIGHTS])
```

```python
import functools

import jax
import jax.numpy as jnp
from jax import lax
from jax.experimental import pallas as pl
from jax.experimental.pallas import tpu as pltpu

F32 = jnp.float32
MXU_DTYPE = jnp.bfloat16
GRAD_WIRE_DTYPE = jnp.bfloat16
HIGHEST = lax.Precision.HIGHEST
MESH = pl.DeviceIdType.MESH

NORM_EPS = 1e-6
D_MODEL = 1024
D_FF = 2816
BRANCH = 512
CHUNK = 64
CONV_WIDTH = 31
CONV_PAD = 32
CONV_SUB = 256
GLA_RANK = 16
GLA_RANK_PAD = 128
GLA_TAU = 16.0
SB_SCALE = 0.125
SB_CUTOFF = 60.0
GLA_SCALE = 0.125
PACK_COLS = 1024
VMEM_LIMIT = 56 * 1024 * 1024

ADAM_LR, ADAM_B1, ADAM_B2, ADAM_EPS, ADAM_WD, ADAM_STEP = 0.001, 0.9, 0.999, 1e-08, 0.01, 10

NT = (((1,), (1,)), ((), ()))
TN = (((0,), (0,)), ((), ()))
NN = (((1,), (0,)), ((), ()))


def _params(sem=None, vmem=None):
    return pltpu.CompilerParams(dimension_semantics=sem, vmem_limit_bytes=vmem)


def _mx(v):
    return v.astype(MXU_DTYPE)


def _mx_round(v):
    return v.astype(MXU_DTYPE).astype(F32)


def _fit(dim, want):
    if dim <= want:
        return dim
    for d in range(want - want % 128, 0, -128):
        if dim % d == 0:
            return d
    raise ValueError((dim, want))


def mm(a, b, c=None, *, ta=False, tb=False, out_dtype=F32, tm=1024, tn=1024, tk=1024, name):
    K, M = a.shape if ta else a.shape[::-1]
    N = b.shape[0] if tb else b.shape[1]
    assert (b.shape[1] if tb else b.shape[0]) == K, (a.shape, b.shape, ta, tb)
    tm, tn, tk = _fit(M, tm), _fit(N, tn), _fit(K, tk)
    nk = K // tk
    dn = (((0 if ta else 1,), (1 if tb else 0,)), ((), ()))

    def body(*refs):
        if c is None:
            a_ref, b_ref, o_ref, acc = refs
            c_ref = None
        else:
            a_ref, b_ref, c_ref, o_ref, acc = refs
        k = pl.program_id(2)
        p = lax.dot_general(_mx(a_ref[...]), _mx(b_ref[...]), dn, preferred_element_type=F32)

        @pl.when(k == 0)
        def _():
            acc[...] = p

        @pl.when(k > 0)
        def _():
            acc[...] += p

        @pl.when(k == nk - 1)
        def _():
            r = acc[...]
            if c_ref is not None:
                r = r + c_ref[...].astype(F32)
            o_ref[...] = r.astype(o_ref.dtype)

    a_spec = pl.BlockSpec((tk, tm), lambda i, j, k: (k, i)) if ta else pl.BlockSpec((tm, tk), lambda i, j, k: (i, k))
    b_spec = pl.BlockSpec((tn, tk), lambda i, j, k: (j, k)) if tb else pl.BlockSpec((tk, tn), lambda i, j, k: (k, j))
    o_spec = pl.BlockSpec((tm, tn), lambda i, j, k: (i, j))
    ins, in_specs = [a, b], [a_spec, b_spec]
    if c is not None:
        ins.append(c)
        in_specs.append(o_spec)
    return pl.pallas_call(
        body, name=name, grid=(M // tm, N // tn, nk), in_specs=in_specs, out_specs=o_spec,
        out_shape=jax.ShapeDtypeStruct((M, N), out_dtype), scratch_shapes=[pltpu.VMEM((tm, tn), F32)],
        compiler_params=_params(("parallel", "parallel", "arbitrary"), VMEM_LIMIT),
    )(*ins)


def mm_sum_nt(pairs, *, tm=1024, tk=512, name):
    M, N = pairs[0][0].shape[0], pairs[0][1].shape[0]
    tm = _fit(M, tm)
    tks = [_fit(a.shape[1], tk) for a, _ in pairs]
    counts = [a.shape[1] // t for (a, _), t in zip(pairs, tks)]
    starts = [sum(counts[:i]) for i in range(len(pairs))]
    total = sum(counts)

    def body(*refs):
        o_ref, acc = refs[-2:]
        k = pl.program_id(1)

        @pl.when(k == 0)
        def _():
            acc[...] = jnp.zeros_like(acc)

        for i in range(len(pairs)):
            @pl.when((k >= starts[i]) & (k < starts[i] + counts[i]))
            def _(i=i):
                acc[...] += lax.dot_general(_mx(refs[2 * i][...]), _mx(refs[2 * i + 1][...]), NT,
                                            preferred_element_type=F32)

        @pl.when(k == total - 1)
        def _():
            o_ref[...] = acc[...]

    def turn(i):
        return lambda k: jnp.clip(k - starts[i], 0, counts[i] - 1)

    in_specs, ins = [], []
    for i, (a, b) in enumerate(pairs):
        in_specs.append(pl.BlockSpec((tm, tks[i]), lambda m, k, f=turn(i): (m, f(k))))
        in_specs.append(pl.BlockSpec((N, tks[i]), lambda m, k, f=turn(i): (0, f(k))))
        ins += [a, b]
    return pl.pallas_call(
        body, name=name, grid=(M // tm, total), in_specs=in_specs, out_specs=pl.BlockSpec((tm, N), lambda m, k: (m, 0)),
        out_shape=jax.ShapeDtypeStruct((M, N), F32), scratch_shapes=[pltpu.VMEM((tm, N), F32)],
        compiler_params=_params(("parallel", "arbitrary"), VMEM_LIMIT),
    )(*ins)


SLAB = D_FF // 4


def _swiglu_of(ab_ref):
    return _mx(f_swiglu(ab_ref[0].astype(F32), ab_ref[1].astype(F32))[0])


def _gated_spec(rows, index):
    return pl.BlockSpec((2, None, rows, SLAB), index)


def slab_nt(a, w6, t0, n, l, *, out_dtype, tm=1024, name):
    S, K = a.shape
    tm = _fit(S, tm)

    def body(a_ref, w_ref, o_ref):
        o_ref[...] = lax.dot_general(_mx(a_ref[...]), _mx(w_ref[...]), NT, preferred_element_type=F32).astype(o_ref.dtype)

    return pl.pallas_call(
        body, name=name, grid=(S // tm, n),
        in_specs=[pl.BlockSpec((tm, K), lambda i, s: (i, 0)),
                  pl.BlockSpec((None, None, None, SLAB, K), lambda i, s: (s % 4, t0 + s // 4, l, 0, 0))],
        out_specs=pl.BlockSpec((None, tm, SLAB), lambda i, s: (s, i, 0)),
        out_shape=jax.ShapeDtypeStruct((n, S, SLAB), out_dtype),
        compiler_params=_params(("parallel", "parallel"), VMEM_LIMIT),
    )(a, w6)


def slab_nt_swiglu_bwd(d, w6, t0, l, ab, *, tm=1024, name):
    S, K = d.shape
    tm = _fit(S, tm)

    def body(d_ref, w_ref, ab_ref, o_ref):
        dz = lax.dot_general(_mx(d_ref[...]), _mx(w_ref[...]), NT, preferred_element_type=F32)
        _, pullback = jax.vjp(lambda g, u: f_swiglu(g, u)[0], ab_ref[0].astype(F32), ab_ref[1].astype(F32))
        dg, du = pullback(_mx_round(dz))
        o_ref[0] = dg.astype(o_ref.dtype)
        o_ref[1] = du.astype(o_ref.dtype)

    return pl.pallas_call(
        body, name=name, grid=(S // tm, 4),
        in_specs=[pl.BlockSpec((tm, K), lambda i, s: (i, 0)),
                  pl.BlockSpec((None, None, None, SLAB, K), lambda i, s: (s, t0, l, 0, 0)),
                  _gated_spec(tm, lambda i, s: (0, s, i, 0))],
        out_specs=_gated_spec(tm, lambda i, s: (0, s, i, 0)),
        out_shape=jax.ShapeDtypeStruct(ab.shape, MXU_DTYPE),
        compiler_params=_params(("parallel", "parallel"), VMEM_LIMIT),
    )(d, w6, ab)


def slab_nn(a, w6, t0, l, *, gated=False, tail=None, tm=1024, name):
    n, S, _ = a.shape[-3:]
    K = w6.shape[-1]
    tm = _fit(S, tm)
    kind = tail[0] if tail else None
    extra = list(tail[1:]) if tail else []

    def body(a_ref, w_ref, *rest):
        i, s = pl.program_id(0), pl.program_id(1)
        acc = rest[-1]
        lhs = _swiglu_of(a_ref) if gated else _mx(a_ref[...])
        p = jnp.dot(lhs, _mx(w_ref[...]), preferred_element_type=F32)

        @pl.when(s == 0)
        def _():
            acc[...] = p

        @pl.when(s > 0)
        def _():
            acc[...] += p

        @pl.when(s == n - 1)
        def _():
            r = acc[...]
            if kind is None:
                rest[0][...] = r
            elif kind == "half_post":
                x_ref, g_ref, o_ref, x2_ref = rest[:4]
                o_ref[...] = r
                x2_ref[...] = f_half_post(x_ref[...], r, g_ref[...])[0]
            else:
                x_ref, g_ref, d_ref, dx_ref, dg_ref = rest[:5]
                _, pullback = jax.vjp(lambda xv, gv: f_rms(xv, gv)[0], x_ref[...], g_ref[...])
                dxn, dg = pullback(r)
                dx_ref[...] = dxn + d_ref[...]

                @pl.when(i == 0)
                def _():
                    dg_ref[...] = dg

                @pl.when(i > 0)
                def _():
                    dg_ref[...] += dg

    rows = pl.BlockSpec((tm, K), lambda i, s: (i, 0))
    gain = pl.BlockSpec((1, K), lambda i, s: (0, 0))
    tail_in = {None: [], "half_post": [rows, gain], "rms_bwd": [rows, gain, rows]}[kind]
    full, vec = jax.ShapeDtypeStruct((S, K), F32), jax.ShapeDtypeStruct((1, K), F32)
    out_specs, out_shape = {None: (rows, full), "half_post": ([rows, rows], [full, full]),
                            "rms_bwd": ([rows, gain], [full, vec])}[kind]
    return pl.pallas_call(
        body, name=name, grid=(S // tm, n),
        in_specs=[_gated_spec(tm, lambda i, s: (0, s, i, 0)) if gated else
                  pl.BlockSpec((None, tm, SLAB), lambda i, s: (s, i, 0)),
                  pl.BlockSpec((None, None, None, SLAB, K), lambda i, s: (s % 4, t0 + s // 4, l, 0, 0))] + tail_in,
        out_specs=out_specs, out_shape=out_shape, scratch_shapes=[pltpu.VMEM((tm, K), F32)],
        compiler_params=_params(("arbitrary" if kind == "rms_bwd" else "parallel", "arbitrary"), VMEM_LIMIT),
    )(a, w6, *extra)


def slab_tn(a, b, g6, t0, l, *, gated=False, tk=1024, name):
    n, S, _ = a.shape[-3:]
    K = b.shape[1]
    tk = _fit(S, tk)
    nk = S // tk

    def body(a_ref, b_ref, g_ref, o_ref, acc):
        k = pl.program_id(1)
        lhs = _swiglu_of(a_ref) if gated else _mx(a_ref[...])
        p = lax.dot_general(lhs, _mx(b_ref[...]), TN, preferred_element_type=F32)

        @pl.when(k == 0)
        def _():
            acc[...] = p

        @pl.when(k > 0)
        def _():
            acc[...] += p

        @pl.when(k == nk - 1)
        def _():
            o_ref[...] = acc[...]

    return pl.pallas_call(
        body, name=name, grid=(n, nk),
        in_specs=[_gated_spec(tk, lambda s, k: (0, s, k, 0)) if gated else
                  pl.BlockSpec((None, tk, SLAB), lambda s, k: (s, k, 0)),
                  pl.BlockSpec((tk, K), lambda s, k: (k, 0)), pl.BlockSpec(memory_space=pl.ANY)],
        out_specs=pl.BlockSpec((None, None, None, SLAB, K), lambda s, k: (s % 4, t0 + s // 4, l, 0, 0)),
        out_shape=jax.ShapeDtypeStruct(g6.shape, g6.dtype), scratch_shapes=[pltpu.VMEM((SLAB, K), F32)],
        input_output_aliases={2: 0},
        compiler_params=_params(("parallel", "arbitrary"), VMEM_LIMIT),
    )(a, b, g6)


def _row_arg(arg):
    if isinstance(arg, tuple):
        return arg
    return arg, arg.shape[1], 0


def _row_specs(rows, n):
    arrs, specs, avals = [], [], []
    for arg in rows:
        arr, width, cb = _row_arg(arg)
        rb = arr.shape[0] // n
        arrs.append(arr)
        specs.append(pl.BlockSpec((rb, width), lambda i, cb=cb: (i, cb)))
        avals.append(jax.ShapeDtypeStruct((rb, width), F32))
    return arrs, specs, avals


def _par_specs(pars):
    specs = [pl.BlockSpec(p.shape, lambda i, nd=p.ndim: (0,) * nd) for p in pars]
    avals = [jax.ShapeDtypeStruct(p.shape, F32) for p in pars]
    return specs, avals


def rowwise(f, rows, pars, out_dtypes, *, tm, name):
    n = _row_arg(rows[0])[0].shape[0] // tm
    arrs, rspecs, ravals = _row_specs(rows, n)
    pspecs, pavals = _par_specs(pars)
    oavals = jax.eval_shape(f, *ravals, *pavals)
    nin = len(arrs) + len(pars)

    def body(*refs):
        outs = f(*[r[...].astype(F32) for r in refs[:nin]])
        for o_ref, o in zip(refs[nin:], outs):
            o_ref[...] = o.astype(o_ref.dtype)

    return pl.pallas_call(
        body, name=name, grid=(n,), in_specs=rspecs + pspecs,
        out_specs=[pl.BlockSpec(o.shape, lambda i: (i, 0)) for o in oavals],
        out_shape=[jax.ShapeDtypeStruct((n * o.shape[0], o.shape[1]), dt) for o, dt in zip(oavals, out_dtypes)],
        compiler_params=_params(("parallel",), VMEM_LIMIT),
    )(*arrs, *pars)


def rowwise_vjp(f, rows, pars, cots, row_grad, par_grad, d_dtypes, adds=None, *, tm, name):
    n = _row_arg(rows[0])[0].shape[0] // tm
    arrs, rspecs, ravals = _row_specs(rows, n)
    pspecs, pavals = _par_specs(pars)
    carrs, cspecs, _ = _row_specs(cots, n)
    adds = adds or [None] * len(row_grad)
    add_arrs = [a for a in adds if a is not None]
    _, aspecs, _ = _row_specs(add_arrs, n)
    nr, npar, nc, na = len(arrs), len(pars), len(carrs), len(add_arrs)
    diff = list(row_grad) + [nr + j for j in par_grad]
    ngr = len(row_grad)

    def body(*refs):
        ins = [r[...].astype(F32) for r in refs[:nr + npar]]
        cs = tuple(r[...].astype(F32) for r in refs[nr + npar:nr + npar + nc])
        add_refs = list(refs[nr + npar + nc:nr + npar + nc + na])
        outs = refs[nr + npar + nc + na:]

        def g(*d):
            full = list(ins)
            for idx, val in zip(diff, d):
                full[idx] = val
            return f(*full)

        _, pullback = jax.vjp(g, *[ins[idx] for idx in diff])
        ds = pullback(cs)
        for k in range(ngr):
            d = ds[k]
            if adds[k] is not None:
                d = d + add_refs.pop(0)[...].astype(F32)
            outs[k][...] = d.astype(outs[k].dtype)
        i = pl.program_id(0)
        for k in range(ngr, len(diff)):
            @pl.when(i == 0)
            def _(k=k):
                outs[k][...] = ds[k]

            @pl.when(i > 0)
            def _(k=k):
                outs[k][...] += ds[k]

    out_specs = [pl.BlockSpec(ravals[i].shape, lambda i: (i, 0)) for i in row_grad] + [pspecs[j] for j in par_grad]
    out_shape = [jax.ShapeDtypeStruct((arrs[i].shape[0], ravals[i].shape[1]), dt) for i, dt in zip(row_grad, d_dtypes)]
    out_shape += [jax.ShapeDtypeStruct(pars[j].shape, F32) for j in par_grad]
    return pl.pallas_call(
        body, name=name, grid=(n,), in_specs=rspecs + pspecs + cspecs + aspecs, out_specs=out_specs, out_shape=out_shape,
        compiler_params=_params(("arbitrary",), VMEM_LIMIT),
    )(*arrs, *pars, *carrs, *add_arrs)


def _logsig(x):
    return jnp.minimum(x, 0.0) - jnp.log(1.0 + jnp.exp(-jnp.abs(x)))


def _sigmoid(x):
    return 1.0 / (1.0 + jnp.exp(-x))


def _silu(x):
    return x * _sigmoid(x)


def _rms(x, g):
    return x * lax.rsqrt(jnp.mean(x * x, axis=-1, keepdims=True) + NORM_EPS) * g


def f_rms(x, g):
    return (_rms(x, g),)


def f_swiglu(a, b):
    return (_silu(a) * b,)


def f_half_post(x, f, g):
    return (x + 0.5 * _rms(f, g),)


def f_post(x, m, g):
    return (x + _rms(m, g),)


def f_glu(a, g):
    return (a * _sigmoid(g),)


def f_conv_ln(y, lg, lb):
    mu = jnp.mean(y, axis=-1, keepdims=True)
    var = jnp.mean(jnp.square(y - mu), axis=-1, keepdims=True)
    return (_silu((y - mu) * lax.rsqrt(var + NORM_EPS) * lg + lb),)


def f_merge(g0, g1, g2, b0, b1, b2):
    return (_sigmoid(g0) * b0 + _sigmoid(g1) * b1 + _sigmoid(g2) * b2,)


def f_gla_post(o, r, g):
    w = o.shape[1]
    hv = w // 4
    i = lax.broadcasted_iota(jnp.int32, (w, w), 0) // hv
    j = lax.broadcasted_iota(jnp.int32, (w, w), 1) // hv
    avg = jnp.where(i == j, 1.0 / hv, 0.0).astype(F32)
    ms = jnp.dot(o * o, avg, precision=HIGHEST, preferred_element_type=F32)
    return (o * lax.rsqrt(ms + NORM_EPS) * g * _silu(r),)


def f_gla_pre(q, k, lr, wa, ba):
    tm, width = q.shape
    nc = tm // CHUNK
    pre = jnp.dot(_mx_round(lr), _mx_round(wa), precision=HIGHEST, preferred_element_type=F32) + ba
    la = (_logsig(pre) / GLA_TAU).reshape(nc, CHUNK, width)
    i = lax.broadcasted_iota(jnp.int32, (nc, CHUNK, CHUNK), 1)
    j = lax.broadcasted_iota(jnp.int32, (nc, CHUNK, CHUNK), 2)
    later = jnp.where(j > i, 1.0, 0.0).astype(F32)
    to_end = lax.dot_general(later, la, (((2,), (1,)), ((0,), (0,))), precision=HIGHEST, preferred_element_type=F32)
    lam = jnp.exp(jnp.sum(la, axis=1))
    return q * GLA_SCALE, k * jnp.exp(to_end.reshape(tm, width)), lam


def _split_dot(x, u):
    hi = _mx(x)
    lo = _mx(x - hi.astype(F32))
    return jnp.dot(hi, u, preferred_element_type=F32) + jnp.dot(lo, u, preferred_element_type=F32)


def _tri(tq, tk):
    row = lax.broadcasted_iota(jnp.int32, (tq, tk), 0)
    col = lax.broadcasted_iota(jnp.int32, (tq, tk), 1)
    return row, col


def _sb_tile(qh, kh, valid, suf, run):
    z = lax.dot_general(qh, kh, NT, preferred_element_type=F32)
    lp = jnp.minimum(z, 0.0) - jnp.log(1.0 + jnp.exp(-jnp.abs(z)))
    lk = lp - z if valid is None else jnp.where(valid, lp - z, 0.0)
    inc = _split_dot(lk, suf)
    a = jnp.exp(lp + (inc - lk + run))
    if valid is not None:
        a = jnp.where(valid, a, 0.0)
    return lp, a, run + inc[:, 0:1]


def _sticks_left(run0, run1):
    return (jnp.maximum(jnp.max(run0), jnp.max(run1)) > -SB_CUTOFF).astype(jnp.int32)


def sb_attn_fwd(qkv, *, tq, name):
    S = qkv.shape[0]
    tk = tq
    pairs = BRANCH // 128

    def body(q_ref, k_ref, v_ref, o_ref):
        i = pl.program_id(1)
        row, col = _tri(tq, tk)
        suf = _mx(row >= col)
        q = q_ref[...] * SB_SCALE

        def tile(j, carry, valid):
            ks = pl.multiple_of((i - j) * tk, tk)
            kb = k_ref[pl.ds(ks, tk), :]
            vb = v_ref[pl.ds(ks, tk), :]
            new = []
            for h in range(2):
                acc, run = carry[h]
                sl = slice(64 * h, 64 * h + 64)
                _, a, run = _sb_tile(q[:, sl], kb[:, sl], valid, suf, run)
                acc = acc + jnp.dot(_mx(a), vb[:, sl], preferred_element_type=F32)
                new.append((acc, run))
            return _sticks_left(new[0][1], new[1][1]), tuple(new)

        def step(state):
            j, _, carry = state
            return (j + 1, *tile(j, carry, None))

        zero = (jnp.zeros((tq, 64), F32), jnp.zeros((tq, 1), F32))
        first = tile(0, (zero, zero), col < row)
        _, _, res = lax.while_loop(lambda s: (s[0] <= i) & (s[1] > 0), step, (jnp.int32(1), *first))
        o_ref[...] = jnp.concatenate([res[0][0], res[1][0]], axis=1).astype(o_ref.dtype)

    return pl.pallas_call(
        body, name=name, grid=(pairs, S // tq),
        in_specs=[pl.BlockSpec((tq, 128), lambda p, i: (i, p)),
                  pl.BlockSpec((S, 128), lambda p, i: (0, pairs + p)),
                  pl.BlockSpec((S, 128), lambda p, i: (0, 2 * pairs + p))],
        out_specs=pl.BlockSpec((tq, 128), lambda p, i: (i, p)),
        out_shape=jax.ShapeDtypeStruct((S, BRANCH), MXU_DTYPE),
        compiler_params=_params(("parallel", "parallel"), VMEM_LIMIT),
    )(qkv, qkv, qkv)


def sb_attn_bwd(qkv, do, *, tq, name):
    S = qkv.shape[0]
    tk = tq
    nq = S // tq
    pairs = BRANCH // 128

    def body(q_ref, k_ref, v_ref, do_ref, dq_ref, dk_ref, dv_ref, g_sc, b_sc):
        i = pl.program_id(1)

        @pl.when(i == 0)
        def _():
            dk_ref[...] = jnp.zeros_like(dk_ref)
            dv_ref[...] = jnp.zeros_like(dv_ref)

        row, col = _tri(tq, tk)
        suf = _mx(row >= col)
        pre = _mx(row <= col)
        q = q_ref[...] * SB_SCALE
        dout = do_ref[...]

        def tile1(j, carry, valid):
            kblk = i - j
            ks = pl.multiple_of(kblk * tk, tk)
            kb = k_ref[pl.ds(ks, tk), :]
            vb = v_ref[pl.ds(ks, tk), :]
            runs, dvs = [], []
            for h in range(2):
                sl = slice(64 * h, 64 * h + 64)
                lp, a, run = _sb_tile(q[:, sl], kb[:, sl], valid, suf, carry[h])
                da = lax.dot_general(dout[:, sl], vb[:, sl], NT, preferred_element_type=F32)
                beta = jnp.exp(lp)
                g_sc[h, kblk] = (a * da).astype(g_sc.dtype)
                b_sc[h, kblk] = (beta if valid is None else jnp.where(valid, beta, 0.0)).astype(b_sc.dtype)
                dvs.append(lax.dot_general(_mx(a), dout[:, sl], TN, preferred_element_type=F32))
                runs.append(run)
            dv_ref[pl.ds(ks, tk), :] += jnp.concatenate(dvs, axis=1)
            return _sticks_left(runs[0], runs[1]), tuple(runs)

        def sweep1(state):
            j, _, carry = state
            return (j + 1, *tile1(j, carry, None))

        first = tile1(0, (jnp.zeros((tq, 1), F32), jnp.zeros((tq, 1), F32)), col < row)
        tiles, _, _ = lax.while_loop(lambda s: (s[0] <= i) & (s[1] > 0), sweep1, (jnp.int32(1), *first))

        def sweep2(kblk, carry):
            ks = pl.multiple_of(kblk * tk, tk)
            kb = k_ref[pl.ds(ks, tk), :]
            new, dks = [], []
            for h in range(2):
                dq, run = carry[h]
                sl = slice(64 * h, 64 * h + 64)
                g = g_sc[h, kblk]
                beta = b_sc[h, kblk].astype(F32)
                inc = jnp.dot(g, pre, preferred_element_type=F32)
                g = g.astype(F32)
                dz = _mx(g - beta * (inc + run))
                dq = dq + jnp.dot(dz, kb[:, sl], preferred_element_type=F32)
                dks.append(lax.dot_general(dz, q[:, sl], TN, preferred_element_type=F32))
                new.append((dq, run + inc[:, tk - 1:tk]))
            dk_ref[pl.ds(ks, tk), :] += jnp.concatenate(dks, axis=1)
            return tuple(new)

        zero = (jnp.zeros((tq, 64), F32), jnp.zeros((tq, 1), F32))
        res = lax.fori_loop(i + 1 - tiles, i + 1, sweep2, (zero, zero))
        dq_ref[...] = (jnp.concatenate([res[0][0], res[1][0]], axis=1) * SB_SCALE).astype(dq_ref.dtype)

    return pl.pallas_call(
        body, name=name, grid=(pairs, nq),
        in_specs=[pl.BlockSpec((tq, 128), lambda p, i: (i, p)),
                  pl.BlockSpec((S, 128), lambda p, i: (0, pairs + p)),
                  pl.BlockSpec((S, 128), lambda p, i: (0, 2 * pairs + p)),
                  pl.BlockSpec((tq, 128), lambda p, i: (i, p))],
        out_specs=[pl.BlockSpec((tq, 128), lambda p, i: (i, p)),
                   pl.BlockSpec((S, 128), lambda p, i: (0, p)),
                   pl.BlockSpec((S, 128), lambda p, i: (0, p))],
        out_shape=[jax.ShapeDtypeStruct((S, BRANCH), MXU_DTYPE), jax.ShapeDtypeStruct((S, BRANCH), F32),
                   jax.ShapeDtypeStruct((S, BRANCH), F32)],
        scratch_shapes=[pltpu.VMEM((2, nq, tq, tk), MXU_DTYPE), pltpu.VMEM((2, nq, tq, tk), MXU_DTYPE)],
        compiler_params=_params(("parallel", "arbitrary"), VMEM_LIMIT),
    )(qkv, qkv, qkv, do)


def conv_fwd(u, w, b, *, tm, name):
    S, C = u.shape
    hb = tm // CONV_PAD

    def body(u_ref, halo_ref, w_ref, b_ref, y_ref, buf):
        i = pl.program_id(0)
        buf[pl.ds(CONV_PAD, tm), :] = u_ref[...]
        buf[pl.ds(0, CONV_PAD), :] = jnp.where(i > 0, halo_ref[...], 0.0)
        ts = min(tm, CONV_SUB)
        for r in range(0, tm, ts):
            acc = jnp.broadcast_to(b_ref[...], (ts, 128))
            for j in range(CONV_WIDTH):
                acc = acc + buf[pl.ds(r + CONV_PAD - (CONV_WIDTH - 1) + j, ts), :] * w_ref[pl.ds(j, 1), :]
            y_ref[pl.ds(r, ts), :] = acc

    return pl.pallas_call(
        body, name=name, grid=(S // tm, C // 128),
        in_specs=[pl.BlockSpec((tm, 128), lambda i, c: (i, c)),
                  pl.BlockSpec((CONV_PAD, 128), lambda i, c: (jnp.maximum(i * hb - 1, 0), c)),
                  pl.BlockSpec((CONV_PAD, 128), lambda i, c: (0, c)),
                  pl.BlockSpec((1, 128), lambda i, c: (0, c))],
        out_specs=pl.BlockSpec((tm, 128), lambda i, c: (i, c)),
        out_shape=jax.ShapeDtypeStruct((S, C), F32),
        scratch_shapes=[pltpu.VMEM((tm + CONV_PAD, 128), F32)],
        compiler_params=_params(("parallel", "parallel")),
    )(u, u, w, b)


def conv_bwd(dy, u, w, *, tm, name):
    S, C = u.shape
    hb = tm // CONV_PAD
    n = S // tm

    def body(dy_ref, dyn_ref, u_ref, up_ref, w_ref, du_ref, dw_ref, bufy, bufu):
        i = pl.program_id(1)
        dyv = dy_ref[...]
        bufy[pl.ds(0, tm), :] = dyv
        bufy[pl.ds(tm, CONV_PAD), :] = jnp.where(i < n - 1, dyn_ref[...], 0.0)
        bufu[pl.ds(CONV_PAD, tm), :] = u_ref[...]
        bufu[pl.ds(0, CONV_PAD), :] = jnp.where(i > 0, up_ref[...], 0.0)

        @pl.when(i == 0)
        def _():
            dw_ref[...] = jnp.zeros_like(dw_ref)

        ts = min(tm, CONV_SUB)
        for r in range(0, tm, ts):
            dys = dy_ref[pl.ds(r, ts), :]
            acc = jnp.zeros((ts, 128), F32)
            for j in range(CONV_WIDTH):
                acc = acc + bufy[pl.ds(r + CONV_WIDTH - 1 - j, ts), :] * w_ref[pl.ds(j, 1), :]
                shifted = bufu[pl.ds(r + CONV_PAD - (CONV_WIDTH - 1) + j, ts), :]
                dw_ref[pl.ds(j, 1), :] += jnp.sum(dys * shifted, axis=0, keepdims=True)
            du_ref[pl.ds(r, ts), :] = acc
            dw_ref[pl.ds(CONV_WIDTH, 1), :] += jnp.sum(dys, axis=0, keepdims=True)

    return pl.pallas_call(
        body, name=name, grid=(C // 128, n),
        in_specs=[pl.BlockSpec((tm, 128), lambda c, i: (i, c)),
                  pl.BlockSpec((CONV_PAD, 128), lambda c, i: (jnp.minimum((i + 1) * hb, n * hb - 1), c)),
                  pl.BlockSpec((tm, 128), lambda c, i: (i, c)),
                  pl.BlockSpec((CONV_PAD, 128), lambda c, i: (jnp.maximum(i * hb - 1, 0), c)),
                  pl.BlockSpec((CONV_PAD, 128), lambda c, i: (0, c))],
        out_specs=[pl.BlockSpec((tm, 128), lambda c, i: (i, c)), pl.BlockSpec((CONV_PAD, 128), lambda c, i: (0, c))],
        out_shape=[jax.ShapeDtypeStruct((S, C), F32), jax.ShapeDtypeStruct((CONV_PAD, C), F32)],
        scratch_shapes=[pltpu.VMEM((tm + CONV_PAD, 128), F32), pltpu.VMEM((tm + CONV_PAD, 128), F32)],
        compiler_params=_params(("parallel", "arbitrary")),
    )(dy, dy, u, u, w)


GLA_HEADS, GLA_DK, GLA_DV = 4, 64, 128


def gla_scan_fwd(qs, kd, gl, lam, *, tm, name):
    S = qs.shape[0]
    nc = tm // CHUNK

    def body(qs_ref, kd_ref, v_ref, lam_ref, o_ref, st_ref, state):
        @pl.when(pl.program_id(0) == 0)
        def _():
            state[...] = jnp.zeros_like(state)

        for c in range(nc):
            rows = pl.ds(c * CHUNK, CHUNK)
            q, k, v = _mx(qs_ref[rows, :]), _mx(kd_ref[rows, :]), _mx(v_ref[rows, :])
            upd = [lax.dot_general(v[:, h * GLA_DV:(h + 1) * GLA_DV], k[:, h * GLA_DK:(h + 1) * GLA_DK], TN,
                                   preferred_element_type=F32) for h in range(GLA_HEADS)]
            st = state[...] * lam_ref[pl.ds(c, 1), :] + jnp.concatenate(upd, axis=1)
            state[...] = st
            st_ref[c] = st
            stm = _mx(st)
            o = [lax.dot_general(q[:, h * GLA_DK:(h + 1) * GLA_DK], stm[:, h * GLA_DK:(h + 1) * GLA_DK], NT,
                                 preferred_element_type=F32) for h in range(GLA_HEADS)]
            o_ref[rows, :] = jnp.concatenate(o, axis=1)

    dk_all = GLA_HEADS * GLA_DK
    dv_all = GLA_HEADS * GLA_DV
    return pl.pallas_call(
        body, name=name, grid=(S // tm,),
        in_specs=[pl.BlockSpec((tm, dk_all), lambda i: (i, 0)), pl.BlockSpec((tm, dk_all), lambda i: (i, 0)),
                  pl.BlockSpec((tm, dv_all), lambda i: (i, 1)), pl.BlockSpec((nc, dk_all), lambda i: (i, 0))],
        out_specs=[pl.BlockSpec((tm, dv_all), lambda i: (i, 0)), pl.BlockSpec((nc, GLA_DV, dk_all), lambda i: (i, 0, 0))],
        out_shape=[jax.ShapeDtypeStruct((S, dv_all), F32), jax.ShapeDtypeStruct((S // CHUNK, GLA_DV, dk_all), F32)],
        scratch_shapes=[pltpu.VMEM((GLA_DV, dk_all), F32)],
        compiler_params=_params(("arbitrary",)),
    )(qs, kd, gl, lam)


def gla_scan_bwd(do, qs, kd, gl, lam, states, *, tm, name):
    S = qs.shape[0]
    nc = tm // CHUNK
    n = S // tm
    dk_all = GLA_HEADS * GLA_DK
    dv_all = GLA_HEADS * GLA_DV

    def body(do_ref, qs_ref, kd_ref, v_ref, lam_ref, st_ref, prev_ref, dqs_ref, dkd_ref, dv_ref, dlam_ref, carry):
        i = pl.program_id(0)

        @pl.when(i == 0)
        def _():
            carry[...] = jnp.zeros_like(carry)

        for c in reversed(range(nc)):
            rows = pl.ds(c * CHUNK, CHUNK)
            q, k, v, d = _mx(qs_ref[rows, :]), _mx(kd_ref[rows, :]), _mx(v_ref[rows, :]), _mx(do_ref[rows, :])
            st = _mx(st_ref[c])
            before = st_ref[c - 1] if c > 0 else jnp.where(i < n - 1, prev_ref[0], 0.0)
            outer = [lax.dot_general(d[:, h * GLA_DV:(h + 1) * GLA_DV], q[:, h * GLA_DK:(h + 1) * GLA_DK], TN,
                                     preferred_element_type=F32) for h in range(GLA_HEADS)]
            dst = carry[...] + jnp.concatenate(outer, axis=1)
            dstm = _mx(dst)
            dq, dkk, dvv = [], [], []
            for h in range(GLA_HEADS):
                ksl = slice(h * GLA_DK, (h + 1) * GLA_DK)
                vsl = slice(h * GLA_DV, (h + 1) * GLA_DV)
                dq.append(jnp.dot(d[:, vsl], st[:, ksl], preferred_element_type=F32))
                dkk.append(jnp.dot(v[:, vsl], dstm[:, ksl], preferred_element_type=F32))
                dvv.append(lax.dot_general(k[:, ksl], dstm[:, ksl], NT, preferred_element_type=F32))
            dqs_ref[rows, :] = jnp.concatenate(dq, axis=1)
            dkd_ref[rows, :] = jnp.concatenate(dkk, axis=1)
            dv_ref[rows, :] = jnp.concatenate(dvv, axis=1)
            dlam_ref[pl.ds(c, 1), :] = jnp.sum(dst * before, axis=0, keepdims=True)
            carry[...] = dst * lam_ref[pl.ds(c, 1), :]

    rev = lambda i: n - 1 - i
    return pl.pallas_call(
        body, name=name, grid=(n,),
        in_specs=[pl.BlockSpec((tm, dv_all), lambda i: (rev(i), 0)), pl.BlockSpec((tm, dk_all), lambda i: (rev(i), 0)),
                  pl.BlockSpec((tm, dk_all), lambda i: (rev(i), 0)), pl.BlockSpec((tm, dv_all), lambda i: (rev(i), 1)),
                  pl.BlockSpec((nc, dk_all), lambda i: (rev(i), 0)),
                  pl.BlockSpec((nc, GLA_DV, dk_all), lambda i: (rev(i), 0, 0)),
                  pl.BlockSpec((1, GLA_DV, dk_all), lambda i: (jnp.maximum(rev(i) * nc - 1, 0), 0, 0))],
        out_specs=[pl.BlockSpec((tm, dk_all), lambda i: (rev(i), 0)), pl.BlockSpec((tm, dk_all), lambda i: (rev(i), 0)),
                   pl.BlockSpec((tm, dv_all), lambda i: (rev(i), 0)), pl.BlockSpec((nc, dk_all), lambda i: (rev(i), 0))],
        out_shape=[jax.ShapeDtypeStruct((S, dk_all), F32), jax.ShapeDtypeStruct((S, dk_all), F32),
                   jax.ShapeDtypeStruct((S, dv_all), F32), jax.ShapeDtypeStruct((S // CHUNK, dk_all), F32)],
        scratch_shapes=[pltpu.VMEM((GLA_DV, dk_all), F32)],
        compiler_params=_params(("arbitrary",)),
    )(do, qs, kd, gl, lam, states, states)


def loss_head(y, target, *, tm, name):
    S, D = y.shape

    def body(y_ref, t_ref, dy_ref, sq_ref):
        err = y_ref[...] - t_ref[...]
        dy_ref[...] = err * (1.0 / D)
        part = jnp.sum(err * err, axis=0, keepdims=True)

        @pl.when(pl.program_id(0) == 0)
        def _():
            sq_ref[...] = part

        @pl.when(pl.program_id(0) > 0)
        def _():
            sq_ref[...] += part

    return pl.pallas_call(
        body, name=name, grid=(S // tm,),
        in_specs=[pl.BlockSpec((tm, D), lambda i: (i, 0)), pl.BlockSpec((tm, D), lambda i: (i, 0))],
        out_specs=[pl.BlockSpec((tm, D), lambda i: (i, 0)), pl.BlockSpec((1, D), lambda i: (0, 0))],
        out_shape=[jax.ShapeDtypeStruct((S, D), F32), jax.ShapeDtypeStruct((1, D), F32)],
        compiler_params=_params(("arbitrary",)),
    )(y, target)


def f_adamw(w, g, m, v):
    m = ADAM_B1 * m + (1.0 - ADAM_B1) * g
    v = ADAM_B2 * v + (1.0 - ADAM_B2) * jnp.square(g)
    m_hat = m / (1.0 - ADAM_B1 ** ADAM_STEP)
    v_hat = v / (1.0 - ADAM_B2 ** ADAM_STEP)
    return -ADAM_LR * (m_hat / (jnp.sqrt(v_hat) + ADAM_EPS) + ADAM_WD * w), m, v


def adamw(w, g, m, v, *, name):
    shape = w.shape
    cols = shape[-1]
    rows = w.size // cols
    tm = rows
    while tm % 16 == 0 and tm * cols * 4 > (1 << 20):
        tm //= 2
    flat = [t.reshape(rows, cols) for t in (w, g, m, v)]
    outs = rowwise(f_adamw, flat, [], [F32, F32, F32], tm=tm, name=name)
    return [o.reshape(shape) for o in outs]


def _place():
    x, y, c = lax.axis_index("x"), lax.axis_index("y"), lax.axis_index("c")
    return x, y, c, [(1 - x, y), (x, 1 - y), (1 - x, 1 - y)]


def _my_chip():
    return 2 * lax.axis_index("x") + lax.axis_index("y")


def _any():
    return pl.BlockSpec(memory_space=pl.ANY)


DMA_CHUNKS = 8


def _chunks(rows):
    n = DMA_CHUNKS
    while n > 1 and rows % (n * 16):
        n //= 2
    return [(k * (rows // n), rows // n) for k in range(n)]


def _start_chunked(make, rows):
    for off, size in _chunks(rows):
        make(off, size).start()


def all_gather_chips(v, *, name):
    R, C = v.shape
    H = R // 2
    Q = H // 2
    assert R % 64 == 0, R

    def body(v_ref, o_ref, send, recv, psend, precv, fsend, frecv):
        x, y, c, chips = _place()
        me = 2 * x + y
        ids = [2 * px + py for px, py in chips]
        mine = pl.ds(pl.multiple_of(c * H, 8), H)
        other = pl.ds(pl.multiple_of((1 - c) * H, 8), H)

        def my_rows(off, size):
            return pl.ds(pl.multiple_of(c * H + off, 8), size)

        def cross(j, src_chip, rows):
            return pltpu.make_async_remote_copy(
                src_ref=v_ref.at[rows], dst_ref=o_ref.at[src_chip, rows], send_sem=send.at[j], recv_sem=recv.at[j],
                device_id=(*chips[j], c), device_id_type=MESH)

        def passed(j, off, size):
            rows = my_rows(j * Q + off, size)
            src_chip = ids[1 - j]
            return pltpu.make_async_remote_copy(
                src_ref=o_ref.at[src_chip, rows], dst_ref=o_ref.at[src_chip, rows], send_sem=psend.at[j],
                recv_sem=precv.at[j], device_id=(*chips[j], c), device_id_type=MESH)

        def landed(j):
            rows = my_rows(j * Q, Q)
            return pltpu.make_async_remote_copy(
                src_ref=o_ref.at[ids[2], rows], dst_ref=o_ref.at[ids[2], rows], send_sem=psend.at[j],
                recv_sem=precv.at[j], device_id=(*chips[j], c), device_id_type=MESH)

        def handed(j, rows):
            return pltpu.make_async_remote_copy(
                src_ref=o_ref.at[ids[j], rows], dst_ref=o_ref.at[ids[j], rows], send_sem=fsend.at[j],
                recv_sem=frecv.at[j], device_id=(x, y, 1 - c), device_id_type=MESH)

        for j in range(2):
            _start_chunked(lambda off, size, j=j: cross(j, me, my_rows(off, size)), H)
        for j in (1, 0):
            cross(j, ids[j], mine).wait_recv()
            _start_chunked(functools.partial(passed, 1 - j), Q)
            _start_chunked(lambda off, size, j=j: handed(j, my_rows(off, size)), H)
        for j in range(2):
            landed(j).wait_recv()
        _start_chunked(lambda off, size: handed(2, my_rows(off, size)), H)
        for j in range(3):
            handed(j, other).wait_recv()
        for j in range(2):
            cross(j, me, mine).wait_send()
            passed(j, 0, Q).wait_send()
        for j in range(3):
            handed(j, mine).wait_send()

    got = pl.pallas_call(
        body, name=name, in_specs=[_any()], out_specs=_any(), out_shape=jax.ShapeDtypeStruct((4, R, C), v.dtype),
        scratch_shapes=[pltpu.SemaphoreType.DMA((2,)), pltpu.SemaphoreType.DMA((2,)), pltpu.SemaphoreType.DMA((2,)),
                        pltpu.SemaphoreType.DMA((2,)), pltpu.SemaphoreType.DMA((3,)), pltpu.SemaphoreType.DMA((3,))],
    )(v)
    return lax.dynamic_update_slice(got, v[None], (_my_chip(), 0, 0))


def swap_halves(g, *, name):
    n, R, C = g.shape
    H = R // 2

    def body(g_ref, theirs_ref, send, recv):
        x, y, c, _ = _place()

        def give(q, off, size):
            return pltpu.make_async_remote_copy(
                src_ref=g_ref.at[q, pl.ds(pl.multiple_of((1 - c) * H + off, 8), size)],
                dst_ref=theirs_ref.at[q, pl.ds(off, size)], send_sem=send.at[q], recv_sem=recv.at[q],
                device_id=(x, y, 1 - c), device_id_type=MESH)

        for q in range(n):
            _start_chunked(functools.partial(give, q), H)
        for q in range(n):
            give(q, 0, H).wait()

    return pl.pallas_call(
        body, name=name, in_specs=[_any()], out_specs=_any(), out_shape=jax.ShapeDtypeStruct((n, H, C), g.dtype),
        scratch_shapes=[pltpu.SemaphoreType.DMA((n,)), pltpu.SemaphoreType.DMA((n,))],
    )(g)


def exchange_pieces(p, *, swap=None, name):
    H = p.shape[1]
    n2, R2 = (swap.shape[0], swap.shape[1]) if swap is not None else (0, 0)
    H2 = R2 // 2

    def body(*refs):
        if swap is None:
            p_ref, b_ref, send, recv = refs
        else:
            p_ref, g_ref, b_ref, t_ref, send, recv, ssend, srecv = refs
        x, y, c, chips = _place()
        me = 2 * x + y
        ids = [2 * px + py for px, py in chips]

        def cross(j, piece, slot, off, size):
            return pltpu.make_async_remote_copy(
                src_ref=p_ref.at[piece, pl.ds(off, size)], dst_ref=b_ref.at[slot, pl.ds(off, size)], send_sem=send.at[j],
                recv_sem=recv.at[j], device_id=(*chips[j], c), device_id_type=MESH)

        def give(q, off, size):
            return pltpu.make_async_remote_copy(
                src_ref=g_ref.at[q, pl.ds(pl.multiple_of((1 - c) * H2 + off, 8), size)],
                dst_ref=t_ref.at[q, pl.ds(off, size)], send_sem=ssend.at[q], recv_sem=srecv.at[q],
                device_id=(x, y, 1 - c), device_id_type=MESH)

        for j in range(3):
            _start_chunked(functools.partial(cross, j, ids[j], me), H)
        for q in range(n2):
            _start_chunked(functools.partial(give, q), H2)
        for j in range(3):
            cross(j, me, ids[j], 0, H).wait_recv()
        for j in range(3):
            cross(j, ids[j], me, 0, H).wait_send()
        for q in range(n2):
            give(q, 0, H2).wait()

    me = _my_chip()
    if swap is None:
        got = pl.pallas_call(
            body, name=name, in_specs=[_any()], out_specs=_any(), out_shape=jax.ShapeDtypeStruct(p.shape, p.dtype),
            scratch_shapes=[pltpu.SemaphoreType.DMA((3,)), pltpu.SemaphoreType.DMA((3,))],
        )(p)
        return lax.dynamic_update_slice(got, lax.dynamic_slice_in_dim(p, me, 1, axis=0), (me, 0, 0))
    got, theirs = pl.pallas_call(
        body, name=name, in_specs=[_any(), _any()], out_specs=[_any(), _any()],
        out_shape=[jax.ShapeDtypeStruct(p.shape, p.dtype), jax.ShapeDtypeStruct((n2, H2, swap.shape[2]), swap.dtype)],
        scratch_shapes=[pltpu.SemaphoreType.DMA((3,)), pltpu.SemaphoreType.DMA((3,)), pltpu.SemaphoreType.DMA((n2,)),
                        pltpu.SemaphoreType.DMA((n2,))],
    )(p, swap)
    return lax.dynamic_update_slice(got, lax.dynamic_slice_in_dim(p, me, 1, axis=0), (me, 0, 0)), theirs


def join_halves(f, *, name):
    H, C = f.shape

    def body(f_ref, o_ref, send, recv):
        x, y, c, _ = _place()

        def give(off, size):
            return pltpu.make_async_remote_copy(
                src_ref=f_ref.at[pl.ds(off, size)], dst_ref=o_ref.at[pl.ds(pl.multiple_of(c * H + off, 8), size)],
                send_sem=send, recv_sem=recv, device_id=(x, y, 1 - c), device_id_type=MESH)

        _start_chunked(give, H)
        give(0, H).wait()

    both = pl.pallas_call(
        body, name=name, in_specs=[_any()], out_specs=_any(), out_shape=jax.ShapeDtypeStruct((2 * H, C), f.dtype),
        scratch_shapes=[pltpu.SemaphoreType.DMA, pltpu.SemaphoreType.DMA],
    )(f)
    return lax.dynamic_update_slice(both, f, (lax.axis_index("c") * H, 0))


def _row_tile(rows, want):
    tm = want
    while rows % tm:
        tm //= 2
    return tm


def _add_pair(g, theirs, out_dtype, *, name):
    n, R, C = g.shape
    H = R // 2
    tm = _row_tile(H, 512)
    nb = H // tm

    def body(c_ref, g_ref, t_ref, o_ref):
        o_ref[...] = (g_ref[...] + t_ref[...]).astype(o_ref.dtype)

    grid_spec = pltpu.PrefetchScalarGridSpec(
        num_scalar_prefetch=1, grid=(n, nb),
        in_specs=[pl.BlockSpec((None, tm, C), lambda q, i, c: (q, c[0] * nb + i, 0)),
                  pl.BlockSpec((None, tm, C), lambda q, i, c: (q, i, 0))],
        out_specs=pl.BlockSpec((None, tm, C), lambda q, i, c: (q, i, 0)))
    return pl.pallas_call(
        body, name=name, grid_spec=grid_spec, out_shape=jax.ShapeDtypeStruct((n, H, C), out_dtype),
        compiler_params=_params(("parallel", "parallel")),
    )(lax.axis_index("c").astype(jnp.int32).reshape(1), g, theirs)


def _add_stack(b, *, name):
    n, H, C = b.shape
    tm = _row_tile(H, 256)

    def body(b_ref, o_ref):
        s = b_ref[0].astype(F32)
        for k in range(1, n):
            s = s + b_ref[k].astype(F32)
        o_ref[...] = s

    return pl.pallas_call(
        body, name=name, grid=(H // tm,), in_specs=[pl.BlockSpec((n, tm, C), lambda i: (0, i, 0))],
        out_specs=pl.BlockSpec((tm, C), lambda i: (i, 0)), out_shape=jax.ShapeDtypeStruct((H, C), F32),
        compiler_params=_params(("parallel",)),
    )(b)


def reduce_scatter_two(ga, gb, ta, tb):
    pair_a = _add_pair(ga, swap_halves(ga, name=ta + "_swap_halves"), GRAD_WIRE_DTYPE, name=ta + "_add_pair")
    got_a, theirs_b = exchange_pieces(pair_a, swap=gb, name=ta + "_exchange")
    total_a = _add_stack(got_a, name=ta + "_add_chips")
    pair_b = _add_pair(gb, theirs_b, GRAD_WIRE_DTYPE, name=tb + "_add_pair")
    total_b = _add_stack(exchange_pieces(pair_b, name=tb + "_exchange"), name=tb + "_add_chips")
    return join_halves(total_a, name=ta + "_join"), join_halves(total_b, name=tb + "_join")


def reduce_scatter(g, tag):
    pair = _add_pair(g, swap_halves(g, name=tag + "_swap_halves"), GRAD_WIRE_DTYPE, name=tag + "_add_pair")
    total = _add_stack(exchange_pieces(pair, name=tag + "_exchange"), name=tag + "_add_chips")
    return join_halves(total, name=tag + "_join")


FFN = ("ffn1_w_gate", "ffn1_w_up", "ffn2_w_gate", "ffn2_w_up", "ffn1_w_down", "ffn2_w_down")
BIG = ("w_in", "w_branch", "w_out")
SMALL = ("norm_pre", "norm_post", "conv_w", "gla_w_alpha")
REPLICATED = ("conv_b", "conv_ln_g", "conv_ln_b", "gla_b_alpha", "gla_norm_g")
WEIGHTS = ("norm_pre", "norm_post", "ffn1_w_gate", "ffn1_w_up", "ffn1_w_down", "ffn2_w_gate", "ffn2_w_up", "ffn2_w_down",
           "w_in", "conv_w", "conv_b", "conv_ln_g", "conv_ln_b", "gla_w_alpha", "gla_b_alpha", "gla_norm_g", "w_branch",
           "w_out")
SHARD_AXIS = {"ffn1_w_gate": 2, "ffn1_w_up": 2, "ffn1_w_down": 1, "ffn2_w_gate": 2, "ffn2_w_up": 2, "ffn2_w_down": 1,
              "w_in": 2, "w_branch": 3, "w_out": 1, "norm_pre": 2, "norm_post": 2, "conv_w": 2, "gla_w_alpha": 2}


def _size(shape):
    n = 1
    for d in shape:
        n *= d
    return n


def _pack_rows(shape):
    return -(-_size(shape) // (16 * PACK_COLS)) * 16


def _pack(arrs, dtype, row_mult):
    parts, rows = [], 0
    for a in arrs:
        r = _pack_rows(a.shape)
        flat = a.astype(dtype).reshape(-1)
        if r * PACK_COLS != flat.shape[0]:
            flat = jnp.pad(flat, (0, r * PACK_COLS - flat.shape[0]))
        parts.append(flat.reshape(r, PACK_COLS))
        rows += r
    if rows % row_mult:
        parts.append(jnp.zeros((row_mult - rows % row_mult, PACK_COLS), dtype))
    return jnp.concatenate(parts, axis=0)


def _unpack(buf, shapes):
    lead = buf.shape[:-2]
    out, off = [], 0
    for s in shapes:
        r = _pack_rows(s)
        part = lax.slice_in_dim(buf, off, off + r, axis=buf.ndim - 2).reshape(lead + (r * PACK_COLS,))
        if r * PACK_COLS != _size(s):
            part = lax.slice_in_dim(part, 0, _size(s), axis=part.ndim - 1)
        out.append(part.reshape(lead + tuple(s)))
        off += r
    return out


def gather_weights(shards, names, dtype, row_mult, name):
    packed = _pack([shards[k] for k in names], dtype, row_mult)
    parts = _unpack(all_gather_chips(packed, name=name), [shards[k].shape for k in names])
    return {k: jnp.concatenate([p[q] for q in range(4)], axis=SHARD_AXIS[k]) for k, p in zip(names, parts)}


def scatter_grads(grads, shards, g_ffn):
    sharded = BIG + SMALL
    pieces = []
    for q in range(4):
        part = []
        for k in sharded:
            w = shards[k].shape[SHARD_AXIS[k]]
            part.append(lax.slice_in_dim(grads[k], q * w, (q + 1) * w, axis=SHARD_AXIS[k]))
        part += [grads[k] for k in REPLICATED]
        pieces.append(_pack(part, F32, 512))
    ffn_total, total = reduce_scatter_two(g_ffn, jnp.stack(pieces), "rs_ffn", "rs_mix")
    names = sharded + REPLICATED
    return dict(zip(names, _unpack(total, [shards[k].shape for k in names]))), ffn_total


TM = 512
TM_MERGE = 256
TM_CONV = 1024
TM_GLA = 512
TQ = 256

IN_SB, IN_CONV, IN_GLA, IN_LR, IN_GATE = 0, 1536, 2560, 4096, 4112
IN_END = 7184


def layer_weights(full, l):
    w = {}
    win = full["w_in"][l]
    w["in_sb"] = win[:, IN_SB:IN_CONV]
    w["in_conv"] = win[:, IN_CONV:IN_GLA]
    w["in_gla"] = win[:, IN_GLA:IN_LR]
    w["in_lr"] = jnp.pad(win[:, IN_LR:IN_GATE], ((0, 0), (0, GLA_RANK_PAD - GLA_RANK)))
    w["in_gate"] = win[:, IN_GATE:IN_END]
    w["branch"] = [full["w_branch"][l, g] for g in range(3)]
    w["out"] = full["w_out"][l]
    return w


def ffn_fwd(x, gpre, gpost, w6, f, l, tag):
    S = x.shape[0]
    h = rowwise(f_rms, [x], [gpre], [MXU_DTYPE], tm=TM, name=tag + "_pre")[0]
    ab = slab_nt(h, w6, 2 * f, 8, l, out_dtype=MXU_DTYPE, tm=2048, name=tag + "_gu").reshape(2, 4, S, SLAB)
    out, x2 = slab_nn(ab, w6, 4 + f, l, gated=True, tail=("half_post", x, gpost), name=tag + "_down")
    return x2, (x, h, ab, out)


def ffn_bwd(dx2, res, gpre, gpost, w6, g6, f, l, tag):
    x, h, ab, out = res
    S = x.shape[0]
    df, dgpost = rowwise_vjp(f_half_post, [x, out], [gpost], [dx2], [1], [0], [MXU_DTYPE], tm=TM, name=tag + "_post_b")
    dab = slab_nt_swiglu_bwd(df, w6, 4 + f, l, ab, name=tag + "_down_dx").reshape(8, S, SLAB)
    g6 = slab_tn(ab, df, g6, 4 + f, l, gated=True, name=tag + "_down_dw")
    dx, dgpre = slab_nn(dab, w6, 2 * f, l, tail=("rms_bwd", x, gpre, dx2), name=tag + "_gu_dx")
    g6 = slab_tn(dab, h, g6, 2 * f, l, name=tag + "_gu_dw")
    return dx, dgpre, dgpost, g6


def mixer_fwd(x, p, w, tag):
    h = rowwise(f_rms, [x], [p["gpre"]], [MXU_DTYPE], tm=TM, name=tag + "_pre")[0]
    qkv = mm(h, w["in_sb"], out_dtype=MXU_DTYPE, tn=512, name=tag + "_in_sb")
    cv = mm(h, w["in_conv"], name=tag + "_in_conv")
    gl = mm(h, w["in_gla"], out_dtype=MXU_DTYPE, tn=512, name=tag + "_in_gla")
    lr = mm(h, w["in_lr"], out_dtype=MXU_DTYPE, name=tag + "_in_lr")
    gt = mm(h, w["in_gate"], out_dtype=MXU_DTYPE, name=tag + "_in_gate")
    sb = sb_attn_fwd(qkv, tq=TQ, name=tag + "_sb")
    u = rowwise(f_glu, [(cv, BRANCH, 0), (cv, BRANCH, 1)], [], [F32], tm=TM, name=tag + "_glu")[0]
    y = conv_fwd(u, p["conv_w"], p["conv_b"], tm=TM_CONV, name=tag + "_conv")
    cb = rowwise(f_conv_ln, [y], [p["ln_g"], p["ln_b"]], [MXU_DTYPE], tm=TM, name=tag + "_ln")[0]
    qs, kd, lam = rowwise(f_gla_pre, [(gl, 256, 0), (gl, 256, 1), lr], [p["wa"], p["ba"]], [MXU_DTYPE, MXU_DTYPE, F32],
                          tm=TM_GLA, name=tag + "_gla_pre")
    o, states = gla_scan_fwd(qs, kd, gl, lam, tm=TM_GLA, name=tag + "_gla_scan")
    gb = rowwise(f_gla_post, [o, (gl, BRANCH, 2)], [p["gn"]], [MXU_DTYPE], tm=TM, name=tag + "_gla_post")[0]
    branches = [sb, cb, gb]
    bd = [mm(branches[g], w["branch"][g], out_dtype=MXU_DTYPE, name=tag + f"_branch{g}") for g in range(3)]
    merged = rowwise(f_merge, [(gt, D_MODEL, 0), (gt, D_MODEL, 1), (gt, D_MODEL, 2)] + bd, [], [MXU_DTYPE], tm=TM_MERGE,
                     name=tag + "_merge")[0]
    m = mm(merged, w["out"], name=tag + "_out")
    x2 = rowwise(f_post, [x, m], [p["gpost"]], [F32], tm=TM, name=tag + "_post")[0]
    return x2, (x, h, qkv, cv, gl, lr, gt, u, y, qs, kd, lam, o, states, branches, bd, merged, m)


def mixer_bwd(dx2, res, p, w, tag):
    x, h, qkv, cv, gl, lr, gt, u, y, qs, kd, lam, o, states, branches, bd, merged, m = res
    g = {}
    dm, g["gpost"] = rowwise_vjp(f_post, [x, m], [p["gpost"]], [dx2], [1], [0], [MXU_DTYPE], tm=TM, name=tag + "_post_b")
    dmerged = mm(dm, w["out"], tb=True, out_dtype=MXU_DTYPE, name=tag + "_out_dx")
    g["out"] = mm(merged, dm, ta=True, name=tag + "_out_dw")
    gts = [(gt, D_MODEL, 0), (gt, D_MODEL, 1), (gt, D_MODEL, 2)]
    dgate = rowwise_vjp(f_merge, gts + bd, [], [dmerged], [0, 1, 2, 3, 4, 5], [], [MXU_DTYPE] * 6, tm=TM_MERGE,
                        name=tag + "_merge_b")
    dgt = jnp.concatenate(dgate[:3], axis=1)
    dbd = dgate[3:]
    g["branch"] = [mm(branches[k], dbd[k], ta=True, name=tag + f"_branch{k}_dw") for k in range(3)]
    dsb = mm(dbd[0], w["branch"][0], tb=True, out_dtype=MXU_DTYPE, name=tag + "_branch0_dx")
    dq, dk, dv = sb_attn_bwd(qkv, dsb, tq=TQ, name=tag + "_sb_b")
    dqkv = jnp.concatenate([dq, _mx(dk), _mx(dv)], axis=1)
    dcb = mm(dbd[1], w["branch"][1], tb=True, name=tag + "_branch1_dx")
    dy, g["ln_g"], g["ln_b"] = rowwise_vjp(f_conv_ln, [y], [p["ln_g"], p["ln_b"]], [dcb], [0], [0, 1], [F32], tm=TM,
                                           name=tag + "_ln_b")
    du, dwb = conv_bwd(dy, u, p["conv_w"], tm=TM_CONV, name=tag + "_conv_b")
    g["conv_w"], g["conv_b"] = dwb[:CONV_WIDTH], dwb[CONV_WIDTH:CONV_WIDTH + 1]
    dca, dcg = rowwise_vjp(f_glu, [(cv, BRANCH, 0), (cv, BRANCH, 1)], [], [du], [0, 1], [], [MXU_DTYPE, MXU_DTYPE], tm=TM,
                           name=tag + "_glu_b")
    dcv = jnp.concatenate([dca, dcg], axis=1)
    dgb = mm(dbd[2], w["branch"][2], tb=True, name=tag + "_branch2_dx")
    do, dr, g["gn"] = rowwise_vjp(f_gla_post, [o, (gl, BRANCH, 2)], [p["gn"]], [dgb], [0, 1], [0], [F32, MXU_DTYPE], tm=TM,
                                  name=tag + "_gla_post_b")
    dqs, dkd, dgv, dlam = gla_scan_bwd(do, qs, kd, gl, lam, states, tm=TM_GLA, name=tag + "_gla_scan_b")
    dgq, dgk, dlr, g["wa"], g["ba"] = rowwise_vjp(
        f_gla_pre, [(gl, 256, 0), (gl, 256, 1), lr], [p["wa"], p["ba"]], [dqs, dkd, dlam], [0, 1, 2], [0, 1],
        [MXU_DTYPE, MXU_DTYPE, MXU_DTYPE], tm=TM_GLA, name=tag + "_gla_pre_b")
    dgl = jnp.concatenate([dgq, dgk, _mx(dgv), dr], axis=1)
    secs = [("in_sb", dqkv), ("in_conv", dcv), ("in_gla", dgl), ("in_lr", dlr), ("in_gate", dgt)]
    dh = mm_sum_nt([(d, w[k]) for k, d in secs], name=tag + "_in_dx")
    for k, d in secs:
        g[k] = mm(h, d, ta=True, tn=1536, name=tag + "_" + k + "_dw")
    dx, g["gpre"] = rowwise_vjp(f_rms, [x], [p["gpre"]], [dh], [0], [0], [F32], [dx2], tm=TM, name=tag + "_pre_b")
    return dx, g


def kernel(x, norm_pre, norm_post, ffn1_w_gate, ffn1_w_up, ffn1_w_down, ffn2_w_gate, ffn2_w_up, ffn2_w_down, w_in, conv_w, conv_b, conv_ln_g, conv_ln_b, gla_w_alpha, gla_b_alpha, gla_norm_g, w_branch, w_out, loss_target, m_norm_pre, m_norm_post, m_ffn1_w_gate, m_ffn1_w_up, m_ffn1_w_down, m_ffn2_w_gate, m_ffn2_w_up, m_ffn2_w_down, m_w_in, m_conv_w, m_conv_b, m_conv_ln_g, m_conv_ln_b, m_gla_w_alpha, m_gla_b_alpha, m_gla_norm_g, m_w_branch, m_w_out, v_norm_pre, v_norm_post, v_ffn1_w_gate, v_ffn1_w_up, v_ffn1_w_down, v_ffn2_w_gate, v_ffn2_w_up, v_ffn2_w_down, v_w_in, v_conv_w, v_conv_b, v_conv_ln_g, v_conv_ln_b, v_gla_w_alpha, v_gla_b_alpha, v_gla_norm_g, v_w_branch, v_w_out):
    shards = dict(norm_pre=norm_pre, norm_post=norm_post, ffn1_w_gate=ffn1_w_gate, ffn1_w_up=ffn1_w_up,
                  ffn1_w_down=ffn1_w_down, ffn2_w_gate=ffn2_w_gate, ffn2_w_up=ffn2_w_up, ffn2_w_down=ffn2_w_down, w_in=w_in,
                  conv_w=conv_w, conv_b=conv_b, conv_ln_g=conv_ln_g, conv_ln_b=conv_ln_b, gla_w_alpha=gla_w_alpha,
                  gla_b_alpha=gla_b_alpha, gla_norm_g=gla_norm_g, w_branch=w_branch, w_out=w_out)
    mom_m = dict(zip(WEIGHTS, (m_norm_pre, m_norm_post, m_ffn1_w_gate, m_ffn1_w_up, m_ffn1_w_down, m_ffn2_w_gate,
                               m_ffn2_w_up, m_ffn2_w_down, m_w_in, m_conv_w, m_conv_b, m_conv_ln_g, m_conv_ln_b,
                               m_gla_w_alpha, m_gla_b_alpha, m_gla_norm_g, m_w_branch, m_w_out)))
    mom_v = dict(zip(WEIGHTS, (v_norm_pre, v_norm_post, v_ffn1_w_gate, v_ffn1_w_up, v_ffn1_w_down, v_ffn2_w_gate,
                               v_ffn2_w_up, v_ffn2_w_down, v_w_in, v_conv_w, v_conv_b, v_conv_ln_g, v_conv_ln_b,
                               v_gla_w_alpha, v_gla_b_alpha, v_gla_norm_g, v_w_branch, v_w_out)))
    depth = norm_pre.shape[0]
    members = [shards[k] if k.endswith("down") else jnp.swapaxes(shards[k], 1, 2) for k in FFN]
    ffn_rows = jnp.concatenate([_mx(t).reshape(depth * SLAB, D_MODEL) for t in members], axis=0)
    w6 = all_gather_chips(ffn_rows, name="gather_ffn").reshape(4, len(FFN), depth, SLAB, D_MODEL)
    full = gather_weights(shards, BIG, MXU_DTYPE, 256, "gather_big")
    full.update(gather_weights(shards, SMALL, F32, 16, "gather_small"))

    def layer_params(l):
        ffn = [dict(gpre=full["norm_pre"][l, k:k + 1], gpost=full["norm_post"][l, k:k + 1]) for k in (0, 2)]
        mix = dict(gpre=full["norm_pre"][l, 1:2], gpost=full["norm_post"][l, 1:2],
                   conv_w=jnp.pad(full["conv_w"][l], ((0, CONV_PAD - CONV_WIDTH), (0, 0))), conv_b=conv_b[l:l + 1],
                   ln_g=conv_ln_g[l:l + 1], ln_b=conv_ln_b[l:l + 1],
                   wa=jnp.pad(full["gla_w_alpha"][l], ((0, GLA_RANK_PAD - GLA_RANK), (0, 0))), ba=gla_b_alpha[l:l + 1],
                   gn=gla_norm_g[l:l + 1])
        return ffn, mix

    xs = x[0]
    saved = []
    for l in range(depth):
        w = layer_weights(full, l)
        ffn, mix = layer_params(l)
        xs, r1 = ffn_fwd(xs, ffn[0]["gpre"], ffn[0]["gpost"], w6, 0, l, f"l{l}_ffn1")
        xs, r2 = mixer_fwd(xs, mix, w, f"l{l}_mix")
        xs, r3 = ffn_fwd(xs, ffn[1]["gpre"], ffn[1]["gpost"], w6, 1, l, f"l{l}_ffn2")
        saved.append((w, ffn, mix, r1, r2, r3))
    dx, sq = loss_head(xs, loss_target[0], tm=TM, name="loss_head")
    loss = lax.psum(0.5 * jnp.sum(sq) / D_MODEL, ("x", "y", "c"))

    per_layer = []
    g6 = lax.empty(w6.shape, F32)
    for l in reversed(range(depth)):
        w, ffn, mix, r1, r2, r3 = saved[l]
        g = {}
        dx, gpre2, gpost2, g6 = ffn_bwd(dx, r3, ffn[1]["gpre"], ffn[1]["gpost"], w6, g6, 1, l, f"l{l}_ffn2")
        dx, gm = mixer_bwd(dx, r2, mix, w, f"l{l}_mix")
        dx, gpre0, gpost0, g6 = ffn_bwd(dx, r1, ffn[0]["gpre"], ffn[0]["gpost"], w6, g6, 0, l, f"l{l}_ffn1")
        g["norm_pre"] = jnp.concatenate([gpre0, gm["gpre"], gpre2], axis=0)
        g["norm_post"] = jnp.concatenate([gpost0, gm["gpost"], gpost2], axis=0)
        g["w_in"] = jnp.concatenate([gm["in_sb"], gm["in_conv"], gm["in_gla"], gm["in_lr"][:, :GLA_RANK], gm["in_gate"]],
                                    axis=1)
        g["conv_w"], g["conv_b"] = gm["conv_w"], gm["conv_b"][0]
        g["conv_ln_g"], g["conv_ln_b"] = gm["ln_g"][0], gm["ln_b"][0]
        g["gla_w_alpha"], g["gla_b_alpha"], g["gla_norm_g"] = gm["wa"][:GLA_RANK], gm["ba"][0], gm["gn"][0]
        g["w_branch"] = jnp.stack(gm["branch"])
        g["w_out"] = gm["out"]
        per_layer.append(g)
    per_layer.reverse()
    grads = {k: jnp.stack([g[k] for g in per_layer]) for k in BIG + SMALL + REPLICATED}

    grad_w, ffn_sum = scatter_grads(grads, shards, g6.reshape(4, len(FFN) * depth * SLAB, D_MODEL))
    ffn_sum = ffn_sum.reshape(len(FFN), depth, SLAB, D_MODEL)
    for t, k in enumerate(FFN):
        grad_w[k] = ffn_sum[t] if k.endswith("down") else jnp.swapaxes(ffn_sum[t], 1, 2)
    delta, new_m, new_v = {}, {}, {}
    for k in WEIGHTS:
        delta[k], new_m[k], new_v[k] = adamw(shards[k], grad_w[k], mom_m[k], mom_v[k], name="adamw_" + k)
    return (loss, dx[None], *[grad_w[k] for k in WEIGHTS], *[delta[k] for k in WEIGHTS], *[new_m[k] for k in WEIGHTS],
            *[new_v[k] for k in WEIGHTS])
```
